```python
import math
import jax, jax.numpy as jnp
from jax import lax
import numpy as np

D_MODEL = 2048
BATCH = 2
SEQ = 4096
DEPTH = 1

CHUNK = 64
N_MEM = 256
EPS = 1e-6
A_HEADS = 16
A_KV_HEADS = 2
A_HEAD_DIM = 64
WINDOW = 128
WINDOW_CHUNKS = WINDOW // CHUNK
A_WIDTH = A_HEADS * A_HEAD_DIM
B_HEADS = 4
B_HEAD_DIM = 128
B_WIDTH = B_HEADS * B_HEAD_DIM
IDX_HEADS = 4
IDX_DIM = 64
TOPK_MAX = 256
Q_BLOCK = 128
C_HEADS = 4
C_HEAD_DIM = 128
C_WIDTH = C_HEADS * C_HEAD_DIM
MIX_WIDTH = A_WIDTH + B_WIDTH + C_WIDTH
N_BUCKETS = 32
MAX_DISTANCE = 1024
N_BIAS_HEADS = A_HEADS + B_HEADS
SPLIT_SIZES = (A_WIDTH, A_KV_HEADS * A_HEAD_DIM, A_KV_HEADS * A_HEAD_DIM,
               B_WIDTH, B_WIDTH, B_WIDTH,
               IDX_HEADS * IDX_DIM, IDX_DIM, IDX_HEADS,
               C_WIDTH, MIX_WIDTH)
IN_WIDTH = sum(SPLIT_SIZES)

kernel_name = "hybrid_chunk_causal_swa_dsa_memory_block"


def rms_norm(x, g):
    xf = x.astype(jnp.float32)
    y = xf * lax.rsqrt(jnp.mean(xf * xf, axis=-1, keepdims=True) + EPS)
    return (y * g.astype(jnp.float32)).astype(x.dtype)


def t5_bucket(rel):
    nb = N_BUCKETS // 2
    max_exact = nb // 2
    side = jnp.where(rel > 0, nb, 0)
    n = jnp.abs(rel)
    nf = jnp.maximum(n, max_exact).astype(jnp.float32)
    large = max_exact + (jnp.log(nf / max_exact) / math.log(MAX_DISTANCE / max_exact)
                         * (nb - max_exact)).astype(jnp.int32)
    large = jnp.minimum(large, nb - 1)
    return side + jnp.where(n < max_exact, n, large)


def sliding_window_sink_attention(q, k, v, sinks, rel_bias):
    b, s = q.shape[0], q.shape[1]
    nc = s // CHUNK
    grp = A_HEADS // A_KV_HEADS
    pad = WINDOW_CHUNKS * CHUNK
    nk = (WINDOW_CHUNKS + 1) * CHUNK
    qc = q.reshape(b, nc, CHUNK, A_KV_HEADS, grp, A_HEAD_DIM)

    def band(t):
        tp = jnp.pad(t, ((0, 0), (pad, 0), (0, 0), (0, 0)))
        tp = tp.reshape(b, nc + WINDOW_CHUNKS, CHUNK, A_KV_HEADS, A_HEAD_DIM)
        return jnp.concatenate([tp[:, j:j + nc] for j in range(WINDOW_CHUNKS + 1)], axis=2)

    kb, vb = band(k), band(v)
    logits = jnp.einsum('bcqgrd,bckgd->bcgrqk', qc, kb).astype(jnp.float32) * (A_HEAD_DIM ** -0.5)
    rel = jnp.arange(nk)[None, :] - pad - jnp.arange(CHUNK)[:, None]
    bias = rel_bias[t5_bucket(rel)][..., :A_HEADS].astype(jnp.float32)
    bias = jnp.transpose(bias, (2, 0, 1)).reshape(A_KV_HEADS, grp, CHUNK, nk)
    kpos = (jnp.arange(nc)[:, None] - WINDOW_CHUNKS) * CHUNK + jnp.arange(nk)[None, :]
    valid = (kpos >= 0)[:, None, None, None, :]
    logits = jnp.where(valid, logits + bias, -jnp.inf)
    sink = sinks.astype(jnp.float32).reshape(A_KV_HEADS, grp, 1, 1)
    m = jnp.maximum(jnp.max(logits, axis=-1, keepdims=True), sink)
    p = jnp.exp(logits - m)
    denom = jnp.sum(p, axis=-1, keepdims=True) + jnp.exp(sink - m)
    probs = (p / denom).astype(v.dtype)
    out = jnp.einsum('bcgrqk,bckgd->bcqgrd', probs, vb)
    return out.reshape(b, s, A_WIDTH)


def indexed_sparse_attention(q, k, v, iq, ik, iw, rel_bias):
    b, s = q.shape[0], q.shape[1]
    topk = min(TOPK_MAX, s // 4)
    nqb = s // Q_BLOCK
    kchunk = jnp.arange(s) // CHUNK
    gather = jax.vmap(lambda t, i: t[i])

    def to_blocks(t):
        return jnp.moveaxis(t.reshape((b, nqb, Q_BLOCK) + t.shape[2:]), 1, 0)

    qpos_blocks = jnp.arange(s, dtype=jnp.int32).reshape(nqb, Q_BLOCK)

    def block(args):
        qb, iqb, iwb, qpos = args
        qchunk = qpos // CHUNK
        sc = jax.nn.relu(jnp.einsum('bqhd,bsd->bqhs', iqb, ik).astype(jnp.float32) * (IDX_DIM ** -0.5))
        isc = jnp.einsum('bqh,bqhs->bqs', iwb.astype(jnp.float32) * (IDX_HEADS ** -0.5), sc)
        adm = kchunk[None, :] <= qchunk[:, None]
        isc = jnp.where(adm[None], isc, -jnp.inf)
        _, idx = lax.top_k(isc, topk)
        kg = gather(k, idx)
        vg = gather(v, idx)
        logits = jnp.einsum('bqhd,bqkhd->bqhk', qb, kg).astype(jnp.float32) * (B_HEAD_DIM ** -0.5)
        rel = idx - qpos[None, :, None]
        bias = rel_bias[t5_bucket(rel)][..., A_HEADS:A_HEADS + B_HEADS].astype(jnp.float32)
        logits = logits + jnp.moveaxis(bias, -1, 2)
        valid = (idx // CHUNK) <= qchunk[None, :, None]
        logits = jnp.where(valid[:, :, None, :], logits, -jnp.inf)
        probs = jax.nn.softmax(logits, axis=-1).astype(v.dtype)
        return jnp.einsum('bqhk,bqkhd->bqhd', probs, vg)

    out = lax.map(block, (to_blocks(q), to_blocks(iq), to_blocks(iw), qpos_blocks))
    return jnp.moveaxis(out, 0, 1).reshape(b, s, B_WIDTH)


def memory_cross_attention(q, mk, mv):
    b, s = q.shape[0], q.shape[1]
    logits = jnp.einsum('bshd,bmhd->bhsm', q, mk).astype(jnp.float32) * (C_HEAD_DIM ** -0.5)
    probs = jax.nn.softmax(logits, axis=-1).astype(mv.dtype)
    return jnp.einsum('bhsm,bmhd->bshd', probs, mv).reshape(b, s, C_WIDTH)


def setup_inputs(seed: int = 0) -> dict:
    key = jax.random.key(seed)
    ks = jax.random.split(key, 10)
    f32 = jnp.float32
    x = jax.random.normal(ks[0], (BATCH, SEQ, D_MODEL), f32)
    mem = jax.random.normal(ks[1], (BATCH, N_MEM, D_MODEL), f32)
    g_norm = 1.0 + 0.01 * jax.random.normal(ks[2], (DEPTH, D_MODEL), f32)
    w_in = jax.random.normal(ks[3], (DEPTH, D_MODEL, IN_WIDTH), f32) * D_MODEL ** -0.5
    sinks = jax.random.normal(ks[4], (DEPTH, A_HEADS), f32)
    rel_bias = 0.5 * jax.random.normal(ks[5], (N_BUCKETS, N_BIAS_HEADS), f32)
    g_mem = 1.0 + 0.01 * jax.random.normal(ks[6], (DEPTH, D_MODEL), f32)
    w_mem_kv = jax.random.normal(ks[7], (DEPTH, D_MODEL, 2 * C_WIDTH), f32) * D_MODEL ** -0.5
    w_out = jax.random.normal(ks[8], (DEPTH, MIX_WIDTH, D_MODEL), f32) * MIX_WIDTH ** -0.5
    g_final = 1.0 + 0.01 * jax.random.normal(ks[9], (D_MODEL,), f32)
    return {"x": x, "mem": mem, "g_norm": g_norm, "w_in": w_in, "sinks": sinks,
            "rel_bias": rel_bias, "g_mem": g_mem, "w_mem_kv": w_mem_kv,
            "w_out": w_out, "g_final": g_final}


def reference(x, mem, g_norm, w_in, sinks, rel_bias, g_mem, w_mem_kv, w_out, g_final):
    b, s, _ = x.shape
    split_points = [int(p) for p in np.cumsum(SPLIT_SIZES)[:-1]]
    h = x
    for layer in range(DEPTH):
        hn = rms_norm(h, g_norm[layer])
        proj = hn @ w_in[layer]
        (aq, ak, av, bq, bk, bv, iq, ik, iw, cq, gate) = jnp.split(proj, split_points, axis=-1)
        oa = sliding_window_sink_attention(
            aq.reshape(b, s, A_HEADS, A_HEAD_DIM),
            ak.reshape(b, s, A_KV_HEADS, A_HEAD_DIM),
            av.reshape(b, s, A_KV_HEADS, A_HEAD_DIM),
            sinks[layer], rel_bias)
        ob = indexed_sparse_attention(
            bq.reshape(b, s, B_HEADS, B_HEAD_DIM),
            bk.reshape(b, s, B_HEADS, B_HEAD_DIM),
            bv.reshape(b, s, B_HEADS, B_HEAD_DIM),
            iq.reshape(b, s, IDX_HEADS, IDX_DIM), ik, iw, rel_bias)
        mkv = rms_norm(mem, g_mem[layer]) @ w_mem_kv[layer]
        mk, mv = jnp.split(mkv, [C_WIDTH], axis=-1)
        n_mem = mem.shape[1]
        oc = memory_cross_attention(
            cq.reshape(b, s, C_HEADS, C_HEAD_DIM),
            mk.reshape(b, n_mem, C_HEADS, C_HEAD_DIM),
            mv.reshape(b, n_mem, C_HEADS, C_HEAD_DIM))
        y = jnp.concatenate([oa, ob, oc], axis=-1) * jax.nn.silu(gate)
        h = h + y @ w_out[layer]
    return rms_norm(h, g_final)
```

```python
import functools
import math

import numpy as np
import jax
import jax.numpy as jnp
from jax import lax
from jax.experimental import pallas as pl
from jax.experimental.pallas import tpu as pltpu

D_MODEL = 2048
CHUNK = 64
N_MEM = 256
EPS = 1e-6
A_HEADS = 16
A_KV_HEADS = 2
A_HEAD_DIM = 64
WINDOW_CHUNKS = 2
A_WIDTH = A_HEADS * A_HEAD_DIM
B_HEADS = 4
B_HEAD_DIM = 128
B_WIDTH = B_HEADS * B_HEAD_DIM
IDX_HEADS = 4
IDX_DIM = 64
TOPK_MAX = 256
C_HEADS = 4
C_HEAD_DIM = 128
C_WIDTH = C_HEADS * C_HEAD_DIM
MIX_WIDTH = A_WIDTH + B_WIDTH + C_WIDTH
N_BUCKETS = 32
MAX_DISTANCE = 1024
KV_A = A_KV_HEADS * A_HEAD_DIM
SPLIT_SIZES = (A_WIDTH, KV_A, KV_A, B_WIDTH, B_WIDTH, B_WIDTH,
               IDX_HEADS * IDX_DIM, IDX_DIM, IDX_HEADS, C_WIDTH, MIX_WIDTH)

F32 = jnp.float32
BF16 = jnp.bfloat16
LOG2E = math.log2(math.e)
NEG = -1e30
INT_MIN = -(2 ** 31)
LANES = 128

IDX_SEG = 384
COL_GATE = 0
COL_AQ = COL_GATE + MIX_WIDTH
COL_BQ = COL_AQ + A_WIDTH
COL_BK = COL_BQ + B_WIDTH
COL_BV = COL_BK + B_WIDTH
COL_CQ = COL_BV + B_WIDTH
COL_AKV = COL_CQ + C_WIDTH
COL_IDX = COL_AKV + 2 * KV_A
PROJ_USED = COL_IDX + IDX_SEG
PROJ_W = 6144
IK_OFF = IDX_HEADS * IDX_DIM
IW_OFF = IK_OFF + IDX_DIM

TM_IN, TN_IN = 1024, 1024
TQ_A = 128
TQ_B = 256
TK_B = 256
RH_B = 128
NEAR_B = 4
TQ_C = 512
TM_OUT = 512
VMEM_LIMIT = 56 * 1024 * 1024


def _t5_bucket_np(rel):
    nb = N_BUCKETS // 2
    max_exact = nb // 2
    side = np.where(rel > 0, nb, 0)
    n = np.abs(rel)
    nf = np.maximum(n, max_exact).astype(np.float32)
    large = max_exact + (np.log(nf / max_exact) / math.log(MAX_DISTANCE / max_exact)
                         * (nb - max_exact)).astype(np.int32)
    large = np.minimum(large, nb - 1)
    return (side + np.where(n < max_exact, n, large)).astype(np.int32)


def _nt_dot(a, b):
    return lax.dot_general(a, b, (((1,), (1,)), ((), ())), preferred_element_type=F32)


def _bias_table(bucket, rb_ref, col, sub_row=None):
    acc = jnp.zeros(bucket.shape, F32)
    for b in range(N_BUCKETS):
        val = rb_ref[b, col]
        if sub_row is not None:
            val = val - rb_ref[sub_row, col]
        acc = jnp.where(bucket == b, val * LOG2E, acc)
    return acc


def _inproj_kernel(x_ref, g_ref, w_ref, cs_ref, o_ref, hn_ref):
    @pl.when(pl.program_id(1) == 0)
    def _():
        x = x_ref[...]
        ms = jnp.mean(x * x, axis=-1, keepdims=True)
        hn_ref[...] = (x * lax.rsqrt(ms + EPS) * g_ref[...]).astype(BF16)

    acc = jnp.dot(hn_ref[...], w_ref[...], preferred_element_type=F32)
    o_ref[...] = (acc * cs_ref[...]).astype(BF16)


def _inproj(x2d, g, w, cs):
    m = x2d.shape[0]
    return pl.pallas_call(
        _inproj_kernel,
        grid=(m // TM_IN, PROJ_W // TN_IN),
        in_specs=[
            pl.BlockSpec((TM_IN, D_MODEL), lambda i, j: (i, 0)),
            pl.BlockSpec((1, D_MODEL), lambda i, j: (0, 0)),
            pl.BlockSpec((D_MODEL, TN_IN), lambda i, j: (0, j)),
            pl.BlockSpec((1, TN_IN), lambda i, j: (0, j)),
        ],
        out_specs=pl.BlockSpec((TM_IN, TN_IN), lambda i, j: (i, j)),
        out_shape=jax.ShapeDtypeStruct((m, PROJ_W), BF16),
        scratch_shapes=[pltpu.VMEM((TM_IN, D_MODEL), BF16)],
        compiler_params=pltpu.CompilerParams(
            dimension_semantics=("parallel", "arbitrary"), vmem_limit_bytes=VMEM_LIMIT),
        name="inproj",
    )(x2d, g, w, cs)


def _memkv_kernel(m_ref, g_ref, w_ref, o_ref):
    x = m_ref[...]
    ms = jnp.mean(x * x, axis=-1, keepdims=True)
    hn = (x * lax.rsqrt(ms + EPS) * g_ref[...]).astype(BF16)
    o_ref[...] = jnp.dot(hn, w_ref[...], preferred_element_type=F32).astype(BF16)


def _memkv(mem2d, g, w):
    m = mem2d.shape[0]
    return pl.pallas_call(
        _memkv_kernel,
        grid=(m // N_MEM,),
        in_specs=[
            pl.BlockSpec((N_MEM, D_MODEL), lambda i: (i, 0)),
            pl.BlockSpec((1, D_MODEL), lambda i: (0, 0)),
            pl.BlockSpec((D_MODEL, 2 * C_WIDTH), lambda i: (0, 0)),
        ],
        out_specs=pl.BlockSpec((N_MEM, 2 * C_WIDTH), lambda i: (i, 0)),
        out_shape=jax.ShapeDtypeStruct((m, 2 * C_WIDTH), BF16),
        compiler_params=pltpu.CompilerParams(
            dimension_semantics=("arbitrary",), vmem_limit_bytes=VMEM_LIMIT),
        name="memkv",
    )(mem2d, g, w)


def _swa_kernel(q_ref, kvp_ref, kvc_ref, bkt_ref, rb_ref, sink_ref, o_ref, tab_ref):
    i = pl.program_id(1)

    @pl.when((pl.program_id(0) == 0) & (i == 0))
    def _():
        bkt = bkt_ref[...]
        row = lax.broadcasted_iota(jnp.int32, bkt.shape, 0) // CHUNK
        col = lax.broadcasted_iota(jnp.int32, bkt.shape, 1) // CHUNK
        allowed = (col >= row) & (col <= row + WINDOW_CHUNKS)
        for h in range(A_HEADS):
            tab_ref[h] = jnp.where(allowed, _bias_table(bkt, rb_ref, h), NEG)

    col = lax.broadcasted_iota(jnp.int32, (TQ_A, 2 * TQ_A), 1)
    pen = jnp.where(col < TQ_A, jnp.where(i == 0, NEG, 0.0), 0.0)

    outs = []
    for h in range(A_HEADS):
        g = h // (A_HEADS // A_KV_HEADS)
        ks = slice(g * A_HEAD_DIM, (g + 1) * A_HEAD_DIM)
        vs = slice(KV_A + g * A_HEAD_DIM, KV_A + (g + 1) * A_HEAD_DIM)
        qh = q_ref[:, h * A_HEAD_DIM:(h + 1) * A_HEAD_DIM]
        kg = jnp.concatenate([kvp_ref[:, ks], kvc_ref[:, ks]], axis=0)
        vg = jnp.concatenate([kvp_ref[:, vs], kvc_ref[:, vs]], axis=0)
        s = _nt_dot(qh, kg) + tab_ref[h] + pen
        sk = sink_ref[h] * LOG2E
        m = jnp.maximum(jnp.max(s, axis=1, keepdims=True), sk)
        p = jnp.exp2(s - m)
        den = jnp.sum(p, axis=1, keepdims=True) + jnp.exp2(sk - m)
        o = jnp.dot(p.astype(BF16), vg, preferred_element_type=F32)
        outs.append(o / den)
    o_ref[...] = jnp.concatenate(outs, axis=1).astype(BF16)


def _swa(proj, bkt, rel_bias, sinks, batch, seq):
    nt = seq // TQ_A
    return pl.pallas_call(
        _swa_kernel,
        grid=(batch, nt),
        in_specs=[
            pl.BlockSpec((TQ_A, A_WIDTH), lambda b, i: (b * nt + i, COL_AQ // A_WIDTH)),
            pl.BlockSpec((TQ_A, 2 * KV_A),
                         lambda b, i: (b * nt + jnp.maximum(i - 1, 0), COL_AKV // (2 * KV_A))),
            pl.BlockSpec((TQ_A, 2 * KV_A), lambda b, i: (b * nt + i, COL_AKV // (2 * KV_A))),
            pl.BlockSpec((TQ_A, 2 * TQ_A), lambda b, i: (0, 0)),
            pl.BlockSpec(memory_space=pltpu.SMEM),
            pl.BlockSpec(memory_space=pltpu.SMEM),
        ],
        out_specs=pl.BlockSpec((TQ_A, A_WIDTH), lambda b, i: (b * nt + i, 0)),
        out_shape=jax.ShapeDtypeStruct((batch * seq, A_WIDTH), BF16),
        scratch_shapes=[pltpu.VMEM((A_HEADS, TQ_A, 2 * TQ_A), F32)],
        compiler_params=pltpu.CompilerParams(
            dimension_semantics=("arbitrary", "arbitrary"), vmem_limit_bytes=VMEM_LIMIT),
        name="swa",
    )(proj, proj, proj, bkt, rel_bias, sinks)


def _dsa_kernel(iqw_ref, ik_ref, q_ref, k_ref, v_ref, bkt_ref, rb_ref, o_ref,
                keys_ref, mb_ref, tab_ref, tri_ref, wb_ref, u_ref, acc_ref, m_ref,
                *, topk):
    i = pl.program_id(1)
    nkt = i + 1
    n_rh = TQ_B // RH_B
    int32 = jnp.int32

    @pl.when((pl.program_id(0) == 0) & (i == 0))
    def _():
        far_bucket = N_BUCKETS // 2 - 1
        for d in range(NEAR_B):
            for h in range(B_HEADS):
                tab_ref[d, h] = _bias_table(bkt_ref[d], rb_ref, A_HEADS + h, sub_row=far_bucket)
        r = lax.broadcasted_iota(int32, (TK_B, TK_B), 0)
        c = lax.broadcasted_iota(int32, (TK_B, TK_B), 1)
        tri_ref[...] = jnp.where(r < c, 1.0, 0.0).astype(BF16)

    row_chunk = lax.broadcasted_iota(int32, (TQ_B, TK_B), 0) // CHUNK
    col_chunk = lax.broadcasted_iota(int32, (TQ_B, TK_B), 1) // CHUNK
    adm_diag = col_chunk <= row_chunk

    iq = iqw_ref[:, 0:IK_OFF]
    for h in range(IDX_HEADS):
        wcol = iqw_ref[:, IW_OFF + h:IW_OFF + h + 1].astype(F32) * (IDX_HEADS ** -0.5 * IDX_DIM ** -0.5)
        wb_ref[h] = jnp.broadcast_to(wcol, (TQ_B, LANES))

    def score_body(kt, carry):
        ikt = ik_ref[pl.ds(pl.multiple_of(kt * TK_B, TK_B), TK_B), IK_OFF:IW_OFF]
        sc = jnp.zeros((TQ_B, TK_B), F32)
        for h in range(IDX_HEADS):
            x = _nt_dot(iq[:, h * IDX_DIM:(h + 1) * IDX_DIM], ikt)
            w = wb_ref[h]
            sc = sc + jnp.concatenate([w, w], axis=1) * jnp.maximum(x, 0.0)
        bits = pltpu.bitcast(sc, int32)
        sgn = bits >> 31
        keys_ref[kt] = (bits ^ (sgn & 0x7FFFFFFF)) - sgn
        return carry

    lax.fori_loop(0, nkt, score_body, 0)
    keys_ref[i] = jnp.where(adm_diag, keys_ref[i], INT_MIN)

    u_ref[...] = jnp.zeros(u_ref.shape, int32)

    def count_rows(rh, pred):
        def body(kt, acc):
            blk = keys_ref[kt, rh * RH_B:(rh + 1) * RH_B, :]
            return (acc + jnp.where(pred(blk[:, :LANES]), 1, 0)
                    + jnp.where(pred(blk[:, LANES:]), 1, 0))
        acc = lax.fori_loop(0, nkt, body, jnp.zeros((RH_B, LANES), int32))
        return jnp.sum(acc, axis=1, keepdims=True)

    def pass_body(p, carry):
        bit = jnp.left_shift(jnp.int32(1), 31 - p)
        for rh in range(n_rh):
            u = u_ref[rh]
            cand = (u | bit) ^ INT_MIN
            cnt = count_rows(rh, lambda blk: blk >= cand)
            u_ref[rh] = jnp.where(cnt >= topk, u | bit, u)
        return carry

    lax.fori_loop(0, 32, pass_body, 0)

    ones_cnt = jnp.ones((TK_B, LANES), BF16)
    for rh in range(n_rh):
        rows = slice(rh * RH_B, (rh + 1) * RH_B)
        tau = u_ref[rh] ^ INT_MIN
        need = (topk - count_rows(rh, lambda blk: blk > tau)).astype(F32)
        need = jnp.broadcast_to(need, (RH_B, LANES))
        tau2 = jnp.concatenate([tau, tau], axis=1)
        need2 = jnp.concatenate([need, need], axis=1)

        def mask_body(kt, run):
            blk = keys_ref[kt, rows, :]
            eq = blk == tau2
            eqf = jnp.where(eq, 1.0, 0.0).astype(BF16)
            rank = jnp.dot(eqf, tri_ref[...], preferred_element_type=F32)
            rank = rank + jnp.concatenate([run, run], axis=1)
            sel = (blk > tau2) | (eq & (rank < need2))
            mb_ref[kt, rows, :] = jnp.where(sel, 0.0, NEG)
            return run + jnp.dot(eqf, ones_cnt, preferred_element_type=F32)

        lax.fori_loop(0, nkt, mask_body, jnp.zeros((RH_B, LANES), F32))
    mb_ref[i] = jnp.where(adm_diag, mb_ref[i], NEG)

    near_lo = jnp.maximum(i - (NEAR_B - 1), 0)
    ones_v = jnp.ones((TK_B, B_HEAD_DIM), BF16)
    for h in range(B_HEADS):
        hs = slice(h * B_HEAD_DIM, (h + 1) * B_HEAD_DIM)
        qh = q_ref[:, hs]
        m_ref[...] = jnp.full(m_ref.shape, NEG, F32)
        acc_ref[...] = jnp.zeros(acc_ref.shape, F32)

        def att_body(kt, carry, near):
            ksl = pl.ds(pl.multiple_of(kt * TK_B, TK_B), TK_B)
            s = _nt_dot(qh, k_ref[ksl, hs]) + mb_ref[kt]
            if near:
                s = s + tab_ref[i - kt, h]
            m_old = m_ref[...]
            m_new = jnp.maximum(m_old, jnp.max(s, axis=1, keepdims=True))
            alpha = jnp.exp2(m_old - m_new)
            p = jnp.exp2(s - jnp.concatenate([m_new, m_new], axis=1))
            vaug = jnp.concatenate([v_ref[ksl, hs], ones_v], axis=1)
            pv = jnp.dot(p.astype(BF16), vaug, preferred_element_type=F32)
            acc_ref[...] = acc_ref[...] * jnp.concatenate([alpha, alpha], axis=1) + pv
            m_ref[...] = m_new
            return carry

        lax.fori_loop(0, near_lo, functools.partial(att_body, near=False), 0)
        lax.fori_loop(near_lo, nkt, functools.partial(att_body, near=True), 0)
        o_ref[:, hs] = (acc_ref[:, :B_HEAD_DIM] / acc_ref[:, B_HEAD_DIM:]).astype(BF16)


def _dsa(proj, bkt, rel_bias, batch, seq):
    nt = seq // TQ_B
    topk = min(TOPK_MAX, seq // 4)
    return pl.pallas_call(
        functools.partial(_dsa_kernel, topk=topk),
        grid=(batch, nt),
        in_specs=[
            pl.BlockSpec((TQ_B, IDX_SEG), lambda b, i: (b * nt + i, COL_IDX // IDX_SEG)),
            pl.BlockSpec((seq, IDX_SEG), lambda b, i: (b, COL_IDX // IDX_SEG)),
            pl.BlockSpec((TQ_B, B_WIDTH), lambda b, i: (b * nt + i, COL_BQ // B_WIDTH)),
            pl.BlockSpec((seq, B_WIDTH), lambda b, i: (b, COL_BK // B_WIDTH)),
            pl.BlockSpec((seq, B_WIDTH), lambda b, i: (b, COL_BV // B_WIDTH)),
            pl.BlockSpec((NEAR_B, TQ_B, TK_B), lambda b, i: (0, 0, 0)),
            pl.BlockSpec(memory_space=pltpu.SMEM),
        ],
        out_specs=pl.BlockSpec((TQ_B, B_WIDTH), lambda b, i: (b * nt + i, 0)),
        out_shape=jax.ShapeDtypeStruct((batch * seq, B_WIDTH), BF16),
        scratch_shapes=[
            pltpu.VMEM((nt, TQ_B, TK_B), jnp.int32),
            pltpu.VMEM((nt, TQ_B, TK_B), F32),
            pltpu.VMEM((NEAR_B, B_HEADS, TQ_B, TK_B), F32),
            pltpu.VMEM((TK_B, TK_B), BF16),
            pltpu.VMEM((IDX_HEADS, TQ_B, LANES), F32),
            pltpu.VMEM((TQ_B // RH_B, RH_B, LANES), jnp.int32),
            pltpu.VMEM((TQ_B, 2 * B_HEAD_DIM), F32),
            pltpu.VMEM((TQ_B, LANES), F32),
        ],
        compiler_params=pltpu.CompilerParams(
            dimension_semantics=("arbitrary", "arbitrary"), vmem_limit_bytes=VMEM_LIMIT),
        name="dsa",
    )(proj, proj, proj, proj, proj, bkt, rel_bias)


def _memattn_kernel(q_ref, kv_ref, o_ref):
    for h in range(C_HEADS):
        hs = slice(h * C_HEAD_DIM, (h + 1) * C_HEAD_DIM)
        s = _nt_dot(q_ref[:, hs], kv_ref[:, hs])
        m = jnp.max(s, axis=1, keepdims=True)
        p = jnp.exp2(s - m)
        den = jnp.sum(p, axis=1, keepdims=True)
        o = jnp.dot(p.astype(BF16), kv_ref[:, C_WIDTH + h * C_HEAD_DIM:C_WIDTH + (h + 1) * C_HEAD_DIM],
                    preferred_element_type=F32)
        o_ref[:, hs] = (o / den).astype(BF16)


def _memattn(proj, mkv, batch, seq):
    nt = seq // TQ_C
    return pl.pallas_call(
        _memattn_kernel,
        grid=(batch, nt),
        in_specs=[
            pl.BlockSpec((TQ_C, C_WIDTH), lambda b, i: (b * nt + i, COL_CQ // C_WIDTH)),
            pl.BlockSpec((N_MEM, 2 * C_WIDTH), lambda b, i: (b, 0)),
        ],
        out_specs=pl.BlockSpec((TQ_C, C_WIDTH), lambda b, i: (b * nt + i, 0)),
        out_shape=jax.ShapeDtypeStruct((batch * seq, C_WIDTH), BF16),
        compiler_params=pltpu.CompilerParams(
            dimension_semantics=("arbitrary", "arbitrary"), vmem_limit_bytes=VMEM_LIMIT),
        name="memattn",
    )(proj, mkv)


def _outproj_kernel(oa_ref, ob_ref, oc_ref, gate_ref, x_ref, w_ref, g_ref, o_ref):
    gate = gate_ref[...].astype(F32)
    sg = gate / (1.0 + jnp.exp(-gate))
    att = jnp.concatenate([oa_ref[...], ob_ref[...], oc_ref[...]], axis=1).astype(F32)
    y = (att * sg).astype(BF16)
    h = x_ref[...] + jnp.dot(y, w_ref[...], preferred_element_type=F32)
    ms = jnp.mean(h * h, axis=-1, keepdims=True)
    o_ref[...] = h * lax.rsqrt(ms + EPS) * g_ref[...]


def _outproj(oa, ob, oc, proj, x2d, w, g):
    m = x2d.shape[0]
    return pl.pallas_call(
        _outproj_kernel,
        grid=(m // TM_OUT,),
        in_specs=[
            pl.BlockSpec((TM_OUT, A_WIDTH), lambda i: (i, 0)),
            pl.BlockSpec((TM_OUT, B_WIDTH), lambda i: (i, 0)),
            pl.BlockSpec((TM_OUT, C_WIDTH), lambda i: (i, 0)),
            pl.BlockSpec((TM_OUT, MIX_WIDTH), lambda i: (i, COL_GATE // MIX_WIDTH)),
            pl.BlockSpec((TM_OUT, D_MODEL), lambda i: (i, 0)),
            pl.BlockSpec((MIX_WIDTH, D_MODEL), lambda i: (0, 0)),
            pl.BlockSpec((1, D_MODEL), lambda i: (0, 0)),
        ],
        out_specs=pl.BlockSpec((TM_OUT, D_MODEL), lambda i: (i, 0)),
        out_shape=jax.ShapeDtypeStruct((m, D_MODEL), F32),
        compiler_params=pltpu.CompilerParams(
            dimension_semantics=("parallel",), vmem_limit_bytes=VMEM_LIMIT),
        name="outproj",
    )(oa, ob, oc, proj, x2d, w, g)


def _arrange_w_in(w):
    pts = np.cumsum((0,) + SPLIT_SIZES)
    aq, ak, av, bq, bk, bv, iq, ik, iw, cq, gate = (w[:, pts[n]:pts[n + 1]] for n in range(len(SPLIT_SIZES)))
    pad = jnp.zeros((w.shape[0], PROJ_W - COL_IDX - IW_OFF - IDX_HEADS), w.dtype)
    return jnp.concatenate([gate, aq, bq, bk, bv, cq, ak, av, iq, ik, iw, pad], axis=1).astype(BF16)


def _col_scale():
    cs = np.ones((1, PROJ_W), np.float32)
    cs[0, COL_AQ:COL_AQ + A_WIDTH] = A_HEAD_DIM ** -0.5 * LOG2E
    cs[0, COL_BQ:COL_BQ + B_WIDTH] = B_HEAD_DIM ** -0.5 * LOG2E
    cs[0, COL_CQ:COL_CQ + C_WIDTH] = C_HEAD_DIM ** -0.5 * LOG2E
    return jnp.asarray(cs)


def kernel(x, mem, g_norm, w_in, sinks, rel_bias, g_mem, w_mem_kv, w_out, g_final):
    batch, seq, _ = x.shape
    depth = w_in.shape[0]
    r = np.arange(TQ_A)[:, None]
    c = np.arange(2 * TQ_A)[None, :]
    bkt_a = jnp.asarray(_t5_bucket_np(c - TQ_A - r))
    r = np.arange(TQ_B)[None, :, None]
    c = np.arange(TK_B)[None, None, :]
    d = np.arange(NEAR_B)[:, None, None]
    bkt_b = jnp.asarray(_t5_bucket_np(c - r - TK_B * d))
    cs = _col_scale()

    assert depth == 1, "the out-projection kernel fuses the final norm of a single-layer trunk"
    h = x.reshape(batch * seq, D_MODEL)
    mem2d = mem.reshape(batch * N_MEM, D_MODEL)
    proj = _inproj(h, g_norm[0].reshape(1, D_MODEL), _arrange_w_in(w_in[0]), cs)
    mkv = _memkv(mem2d, g_mem[0].reshape(1, D_MODEL), w_mem_kv[0].astype(BF16))
    oa = _swa(proj, bkt_a, rel_bias, sinks[0], batch, seq)
    ob = _dsa(proj, bkt_b, rel_bias, batch, seq)
    oc = _memattn(proj, mkv, batch, seq)
    out = _outproj(oa, ob, oc, proj, h, w_out[0].astype(BF16), g_final.reshape(1, D_MODEL))
    return out.reshape(batch, seq, D_MODEL)
```

```python
import functools
import math

import numpy as np
import jax
import jax.numpy as jnp
from jax import lax
from jax.experimental import pallas as pl
from jax.experimental.pallas import tpu as pltpu

D_MODEL = 2048
CHUNK = 64
N_MEM = 256
EPS = 1e-6
A_HEADS = 16
A_KV_HEADS = 2
A_HEAD_DIM = 64
WINDOW_CHUNKS = 2
A_WIDTH = A_HEADS * A_HEAD_DIM
B_HEADS = 4
B_HEAD_DIM = 128
B_WIDTH = B_HEADS * B_HEAD_DIM
IDX_HEADS = 4
IDX_DIM = 64
TOPK_MAX = 256
C_HEADS = 4
C_HEAD_DIM = 128
C_WIDTH = C_HEADS * C_HEAD_DIM
MIX_WIDTH = A_WIDTH + B_WIDTH + C_WIDTH
N_BUCKETS = 32
MAX_DISTANCE = 1024
KV_A = A_KV_HEADS * A_HEAD_DIM
SPLIT_SIZES = (A_WIDTH, KV_A, KV_A, B_WIDTH, B_WIDTH, B_WIDTH,
               IDX_HEADS * IDX_DIM, IDX_DIM, IDX_HEADS, C_WIDTH, MIX_WIDTH)
IN_WIDTH = sum(SPLIT_SIZES)

F32 = jnp.float32
BF16 = jnp.bfloat16
LOG2E = math.log2(math.e)
NEG = -1e30
INT_MIN = -(2 ** 31)
LANES = 128
SUBLANES = 8

(SRC_AQ, SRC_AK, SRC_AV, SRC_BQ, SRC_BK, SRC_BV,
 SRC_IQ, SRC_IK, SRC_IW, SRC_CQ, SRC_GATE) = (int(c) for c in np.cumsum((0,) + SPLIT_SIZES)[:-1])
COL_GATE = 0
COL_AQ = COL_GATE + MIX_WIDTH
COL_BQ = COL_AQ + A_WIDTH
COL_BK = COL_BQ + B_WIDTH
COL_BV = COL_BK + B_WIDTH
COL_CQ = COL_BV + B_WIDTH
COL_AKV = COL_CQ + C_WIDTH
COL_IQ = COL_AKV + 2 * KV_A
COL_IKW = COL_IQ + IDX_HEADS * IDX_DIM
PROJ_W = 6144
SEGMENTS = ((COL_GATE, SRC_GATE, MIX_WIDTH), (COL_AQ, SRC_AQ, A_WIDTH), (COL_BQ, SRC_BQ, B_WIDTH),
            (COL_BK, SRC_BK, B_WIDTH), (COL_BV, SRC_BV, B_WIDTH), (COL_CQ, SRC_CQ, C_WIDTH),
            (COL_AKV, SRC_AK, 2 * KV_A), (COL_IQ, SRC_IQ, IDX_HEADS * IDX_DIM), (COL_IKW, SRC_IK, 256))
KB_IN = D_MODEL // LANES

TC_PREP = 256
TM_IN, TN_IN = 1024, 1024
TQ_A = 128
TQ_B = 256
TK_B = 256
NEAR_B = 4
TQ_C = 512
TM_OUT = 512
VMEM_LIMIT = 56 * 1024 * 1024


def _t5_bucket_np(rel):
    nb = N_BUCKETS // 2
    max_exact = nb // 2
    side = np.where(rel > 0, nb, 0)
    n = np.abs(rel)
    nf = np.maximum(n, max_exact).astype(np.float32)
    large = max_exact + (np.log(nf / max_exact) / math.log(MAX_DISTANCE / max_exact)
                         * (nb - max_exact)).astype(np.int32)
    large = np.minimum(large, nb - 1)
    return (side + np.where(n < max_exact, n, large)).astype(np.int32)


def _nt_dot(a, b):
    return lax.dot_general(a, b, (((1,), (1,)), ((), ())), preferred_element_type=F32)


def _bias_table(bucket, rb_ref, col, sub_row=None):
    acc = jnp.zeros(bucket.shape, F32)
    for b in range(N_BUCKETS):
        val = rb_ref[b, col]
        if sub_row is not None:
            val = val - rb_ref[sub_row, col]
        acc = jnp.where(bucket == b, val * LOG2E, acc)
    return acc


def _wprep_kernel(src_ref, w_ref, o_ref):
    t = pl.program_id(0)

    @pl.when(src_ref[t] >= 0)
    def _():
        for kb in range(KB_IN):
            o_ref[:, kb * LANES:(kb + 1) * LANES] = w_ref[pl.ds(kb, TC_PREP, stride=KB_IN), :].astype(BF16)

    @pl.when(src_ref[t] < 0)
    def _():
        o_ref[...] = jnp.zeros(o_ref.shape, BF16)


def _wprep(w_t):
    src = np.full((PROJ_W // TC_PREP,), -1, np.int32)
    for dst_col, src_col, width in SEGMENTS:
        for n in range(width // TC_PREP):
            src[dst_col // TC_PREP + n] = src_col + n * TC_PREP
    return pl.pallas_call(
        _wprep_kernel,
        grid_spec=pltpu.PrefetchScalarGridSpec(
            num_scalar_prefetch=1,
            grid=(PROJ_W // TC_PREP,),
            in_specs=[pl.BlockSpec((pl.Element(TC_PREP * KB_IN), pl.Element(LANES)),
                                   lambda t, src: (jnp.maximum(src[t], 0) * KB_IN, 0))],
            out_specs=pl.BlockSpec((TC_PREP, D_MODEL), lambda t, src: (t, 0)),
        ),
        out_shape=jax.ShapeDtypeStruct((PROJ_W, D_MODEL), BF16),
        compiler_params=pltpu.CompilerParams(
            dimension_semantics=("arbitrary",), vmem_limit_bytes=VMEM_LIMIT),
        name="wprep",
    )(jnp.asarray(src), w_t)


def _inproj_kernel(x_ref, g_ref, wt_ref, cs_ref, o_ref, hn_ref):
    @pl.when(pl.program_id(1) == 0)
    def _():
        x = x_ref[...]
        ms = jnp.mean(x * x, axis=-1, keepdims=True)
        hn_ref[...] = (x * lax.rsqrt(ms + EPS) * g_ref[...]).astype(BF16)

    acc = _nt_dot(hn_ref[...], wt_ref[...])
    o_ref[...] = (acc * cs_ref[...]).astype(BF16)


def _inproj(x2d, g, wt, cs):
    m = x2d.shape[0]
    return pl.pallas_call(
        _inproj_kernel,
        grid=(m // TM_IN, PROJ_W // TN_IN),
        in_specs=[
            pl.BlockSpec((TM_IN, D_MODEL), lambda i, j: (i, 0)),
            pl.BlockSpec((1, D_MODEL), lambda i, j: (0, 0)),
            pl.BlockSpec((TN_IN, D_MODEL), lambda i, j: (j, 0)),
            pl.BlockSpec((1, TN_IN), lambda i, j: (0, j)),
        ],
        out_specs=pl.BlockSpec((TM_IN, TN_IN), lambda i, j: (i, j)),
        out_shape=jax.ShapeDtypeStruct((m, PROJ_W), BF16),
        scratch_shapes=[pltpu.VMEM((TM_IN, D_MODEL), BF16)],
        compiler_params=pltpu.CompilerParams(
            dimension_semantics=("parallel", "arbitrary"), vmem_limit_bytes=VMEM_LIMIT),
        name="inproj",
    )(x2d, g, wt, cs)


def _memkv_kernel(m_ref, g_ref, w_ref, o_ref):
    x = m_ref[...]
    ms = jnp.mean(x * x, axis=-1, keepdims=True)
    hn = (x * lax.rsqrt(ms + EPS) * g_ref[...]).astype(BF16)
    o_ref[...] = jnp.dot(hn, w_ref[...].astype(BF16), preferred_element_type=F32).astype(BF16)


def _memkv(mem2d, g, w):
    m = mem2d.shape[0]
    return pl.pallas_call(
        _memkv_kernel,
        grid=(m // N_MEM,),
        in_specs=[
            pl.BlockSpec((N_MEM, D_MODEL), lambda i: (i, 0)),
            pl.BlockSpec((1, D_MODEL), lambda i: (0, 0)),
            pl.BlockSpec((D_MODEL, 2 * C_WIDTH), lambda i: (0, 0)),
        ],
        out_specs=pl.BlockSpec((N_MEM, 2 * C_WIDTH), lambda i: (i, 0)),
        out_shape=jax.ShapeDtypeStruct((m, 2 * C_WIDTH), BF16),
        compiler_params=pltpu.CompilerParams(
            dimension_semantics=("arbitrary",), vmem_limit_bytes=VMEM_LIMIT),
        name="memkv",
    )(mem2d, g, w)


def _swa_kernel(q_ref, kvp_ref, kvc_ref, bkt_ref, rb_ref, sink_ref, o_ref, tab_ref):
    i = pl.program_id(1)

    @pl.when((pl.program_id(0) == 0) & (i == 0))
    def _():
        bkt = bkt_ref[...]
        row = lax.broadcasted_iota(jnp.int32, bkt.shape, 0) // CHUNK
        col = lax.broadcasted_iota(jnp.int32, bkt.shape, 1) // CHUNK
        allowed = (col >= row) & (col <= row + WINDOW_CHUNKS)
        for h in range(A_HEADS):
            tab_ref[h] = jnp.where(allowed, _bias_table(bkt, rb_ref, h), NEG)

    col = lax.broadcasted_iota(jnp.int32, (TQ_A, 2 * TQ_A), 1)
    pen = jnp.where(col < TQ_A, jnp.where(i == 0, NEG, 0.0), 0.0)

    outs = []
    for h in range(A_HEADS):
        g = h // (A_HEADS // A_KV_HEADS)
        ks = slice(g * A_HEAD_DIM, (g + 1) * A_HEAD_DIM)
        vs = slice(KV_A + g * A_HEAD_DIM, KV_A + (g + 1) * A_HEAD_DIM)
        qh = q_ref[:, h * A_HEAD_DIM:(h + 1) * A_HEAD_DIM]
        kg = jnp.concatenate([kvp_ref[:, ks], kvc_ref[:, ks]], axis=0)
        vg = jnp.concatenate([kvp_ref[:, vs], kvc_ref[:, vs]], axis=0)
        s = _nt_dot(qh, kg) + tab_ref[h] + pen
        sk = sink_ref[h] * LOG2E
        m = jnp.maximum(jnp.max(s, axis=1, keepdims=True), sk)
        p = jnp.exp2(s - m)
        den = jnp.sum(p, axis=1, keepdims=True) + jnp.exp2(sk - m)
        o = jnp.dot(p.astype(BF16), vg, preferred_element_type=F32)
        outs.append(o / den)
    o_ref[...] = jnp.concatenate(outs, axis=1).astype(BF16)


def _swa(proj, bkt, rel_bias, sinks, batch, seq):
    nt = seq // TQ_A
    kv_blk = COL_AKV // (2 * KV_A)
    return pl.pallas_call(
        _swa_kernel,
        grid=(batch, nt),
        in_specs=[
            pl.BlockSpec((TQ_A, A_WIDTH), lambda b, i: (b * nt + i, COL_AQ // A_WIDTH)),
            pl.BlockSpec((TQ_A, 2 * KV_A), lambda b, i: (b * nt + jnp.maximum(i - 1, 0), kv_blk)),
            pl.BlockSpec((TQ_A, 2 * KV_A), lambda b, i: (b * nt + i, kv_blk)),
            pl.BlockSpec((TQ_A, 2 * TQ_A), lambda b, i: (0, 0)),
            pl.BlockSpec(memory_space=pltpu.SMEM),
            pl.BlockSpec(memory_space=pltpu.SMEM),
        ],
        out_specs=pl.BlockSpec((TQ_A, A_WIDTH), lambda b, i: (b * nt + i, 0)),
        out_shape=jax.ShapeDtypeStruct((batch * seq, A_WIDTH), BF16),
        scratch_shapes=[pltpu.VMEM((A_HEADS, TQ_A, 2 * TQ_A), F32)],
        compiler_params=pltpu.CompilerParams(
            dimension_semantics=("arbitrary", "arbitrary"), vmem_limit_bytes=VMEM_LIMIT),
        name="swa",
    )(proj, proj, proj, bkt, rel_bias, sinks)


def _dsa_kernel(iq_ref, iwq_ref, ik_ref, q_ref, k_ref, v_ref, bkt_ref, rb_ref, o_ref,
                keys_ref, mb_ref, tab_ref, tri_ref, vt_ref, acc_ref, m_ref, *, topk, seq):
    i = pl.program_id(1)
    nkt = i + 1
    int32 = jnp.int32
    grp = TK_B // SUBLANES

    def hcols(h):
        return slice(h * B_HEAD_DIM, (h + 1) * B_HEAD_DIM)

    def ktile(kt):
        return pl.ds(pl.multiple_of(kt * TK_B, TK_B), TK_B)

    def rows3(a):
        return a.reshape(a.shape[0] // SUBLANES, SUBLANES, TQ_B)

    def all_rows(a, op):
        return jnp.broadcast_to(op(a, axis=0, keepdims=True), a.shape)

    @pl.when((pl.program_id(0) == 0) & (i == 0))
    def _():
        far_bucket = N_BUCKETS // 2 - 1
        for d in range(NEAR_B):
            for h in range(B_HEADS):
                tab_ref[d, h] = _bias_table(bkt_ref[d], rb_ref, A_HEADS + h, sub_row=far_bucket)
        r = lax.broadcasted_iota(int32, (TK_B, TK_B), 0)
        c = lax.broadcasted_iota(int32, (TK_B, TK_B), 1)
        tri_ref[...] = jnp.where(c < r, 1.0, 0.0).astype(BF16)

    @pl.when(i == 0)
    def _():
        r = lax.broadcasted_iota(int32, (B_HEAD_DIM, B_HEAD_DIM), 0)
        c = lax.broadcasted_iota(int32, (B_HEAD_DIM, B_HEAD_DIM), 1)
        eye = jnp.where(r == c, 1.0, 0.0).astype(BF16)

        def body(kt, carry):
            for h in range(B_HEADS):
                vt_ref[kt, h] = _nt_dot(eye, v_ref[ktile(kt), hcols(h)]).astype(BF16)
            return carry

        lax.fori_loop(0, seq // TK_B, body, 0)

    key_chunk = lax.broadcasted_iota(int32, (TK_B, TQ_B), 0) // CHUNK
    qry_chunk = lax.broadcasted_iota(int32, (TK_B, TQ_B), 1) // CHUNK
    adm_diag = key_chunk <= qry_chunk

    r = lax.broadcasted_iota(int32, (IDX_HEADS * SUBLANES, LANES), 0)
    c = lax.broadcasted_iota(int32, (IDX_HEADS * SUBLANES, LANES), 1)
    pick_w = jnp.where(c == IDX_DIM + r // SUBLANES, 1.0, 0.0).astype(BF16)
    w_all = _nt_dot(pick_w, iwq_ref[...]) * (IDX_HEADS ** -0.5 * IDX_DIM ** -0.5)

    def score_body(kt, carry):
        ikt = ik_ref[ktile(kt), 0:IDX_DIM]
        sc = jnp.zeros((grp, SUBLANES, TQ_B), F32)
        for h in range(IDX_HEADS):
            x = _nt_dot(ikt, iq_ref[:, h * IDX_DIM:(h + 1) * IDX_DIM])
            sc = sc + w_all[h * SUBLANES:(h + 1) * SUBLANES][None] * jnp.maximum(rows3(x), 0.0)
        bits = pltpu.bitcast(sc.reshape(TK_B, TQ_B), int32)
        sgn = bits >> 31
        keys_ref[kt] = (bits ^ (sgn & 0x7FFFFFFF)) - sgn
        return carry

    lax.fori_loop(0, nkt, score_body, 0)
    keys_ref[i] = jnp.where(adm_diag, keys_ref[i], INT_MIN)

    def count(pred):
        def body(kt, acc):
            return acc + jnp.sum(jnp.where(pred(rows3(keys_ref[kt])), 1, 0), axis=0)
        acc = lax.fori_loop(0, nkt, body, jnp.zeros((SUBLANES, TQ_B), int32))
        return all_rows(acc, jnp.sum)

    def pass_body(p, u):
        bit = jnp.left_shift(jnp.int32(1), 31 - p)
        cand = ((u | bit) ^ INT_MIN)[None]
        cnt = count(lambda blk: blk >= cand)
        return jnp.where(cnt >= topk, u | bit, u)

    u = lax.fori_loop(0, 32, pass_body, jnp.zeros((SUBLANES, TQ_B), int32))
    tau = (u ^ INT_MIN)[None]

    need = (topk - count(lambda blk: blk > tau)).astype(F32)[None]
    ones_l = jnp.ones((2 * SUBLANES, TK_B), BF16)

    def mask_body(kt, run):
        blk = rows3(keys_ref[kt])
        eq = blk == tau
        eqf = jnp.where(eq, 1.0, 0.0).reshape(TK_B, TQ_B).astype(BF16)
        rank = rows3(jnp.dot(tri_ref[...], eqf, preferred_element_type=F32)) + run[None]
        sel = (blk > tau) | (eq & (rank < need))
        mb_ref[kt] = jnp.where(sel, 0.0, NEG).reshape(TK_B, TQ_B)
        return run + jnp.dot(ones_l, eqf, preferred_element_type=F32)[:SUBLANES]

    lax.fori_loop(0, nkt, mask_body, jnp.zeros((SUBLANES, TQ_B), F32))
    mb_ref[i] = jnp.where(adm_diag, mb_ref[i], NEG)

    m_ref[...] = jnp.full(m_ref.shape, NEG, F32)
    acc_ref[...] = jnp.zeros(acc_ref.shape, F32)
    ones_rows = jnp.ones((B_HEAD_DIM, TK_B), BF16)

    def att_body(kt, carry, near):
        mb = mb_ref[kt]
        for h in range(B_HEADS):
            s = _nt_dot(k_ref[ktile(kt), hcols(h)], q_ref[:, hcols(h)]) + mb
            if near:
                s = s + tab_ref[i - kt, h]
            s = rows3(s)
            m_old = m_ref[h]
            m_new = jnp.maximum(m_old, all_rows(jnp.max(s, axis=0), jnp.max))
            alpha = jnp.exp2(m_old - m_new)
            p = jnp.exp2(s - m_new[None]).reshape(TK_B, TQ_B).astype(BF16)
            vaug = jnp.concatenate([vt_ref[kt, h], ones_rows], axis=0)
            pv = jnp.dot(vaug, p, preferred_element_type=F32)
            acc_ref[h] = (rows3(acc_ref[h]) * alpha[None] + rows3(pv)).reshape(2 * B_HEAD_DIM, TQ_B)
            m_ref[h] = m_new
        return carry

    near_lo = jnp.maximum(i - (NEAR_B - 1), 0)
    lax.fori_loop(0, near_lo, functools.partial(att_body, near=False), 0)
    lax.fori_loop(near_lo, nkt, functools.partial(att_body, near=True), 0)
    for h in range(B_HEADS):
        num = rows3(acc_ref[h, 0:B_HEAD_DIM, :])
        den = acc_ref[h, B_HEAD_DIM:B_HEAD_DIM + SUBLANES, :]
        out_t = (num / den[None]).reshape(B_HEAD_DIM, TQ_B)
        o_ref[:, h * B_HEAD_DIM:(h + 1) * B_HEAD_DIM] = out_t.T.astype(BF16)


def _dsa(proj, bkt, rel_bias, batch, seq):
    nt = seq // TQ_B
    topk = min(TOPK_MAX, seq // 4)

    def q_spec(width, col):
        return pl.BlockSpec((TQ_B, width), lambda b, i: (b * nt + i, col // width))

    def seq_spec(width, col):
        return pl.BlockSpec((seq, width), lambda b, i: (b, col // width))

    return pl.pallas_call(
        functools.partial(_dsa_kernel, topk=topk, seq=seq),
        grid=(batch, nt),
        in_specs=[
            q_spec(IDX_HEADS * IDX_DIM, COL_IQ),
            q_spec(LANES, COL_IKW),
            seq_spec(LANES, COL_IKW),
            q_spec(B_WIDTH, COL_BQ), seq_spec(B_WIDTH, COL_BK), seq_spec(B_WIDTH, COL_BV),
            pl.BlockSpec((NEAR_B, TK_B, TQ_B), lambda b, i: (0, 0, 0)),
            pl.BlockSpec(memory_space=pltpu.SMEM),
        ],
        out_specs=pl.BlockSpec((TQ_B, B_WIDTH), lambda b, i: (b * nt + i, 0)),
        out_shape=jax.ShapeDtypeStruct((batch * seq, B_WIDTH), BF16),
        scratch_shapes=[
            pltpu.VMEM((nt, TK_B, TQ_B), jnp.int32),
            pltpu.VMEM((nt, TK_B, TQ_B), F32),
            pltpu.VMEM((NEAR_B, B_HEADS, TK_B, TQ_B), F32),
            pltpu.VMEM((TK_B, TK_B), BF16),
            pltpu.VMEM((seq // TK_B, B_HEADS, B_HEAD_DIM, TK_B), BF16),
            pltpu.VMEM((B_HEADS, 2 * B_HEAD_DIM, TQ_B), F32),
            pltpu.VMEM((B_HEADS, SUBLANES, TQ_B), F32),
        ],
        compiler_params=pltpu.CompilerParams(
            dimension_semantics=("arbitrary", "arbitrary"), vmem_limit_bytes=VMEM_LIMIT),
        name="dsa",
    )(proj, proj, proj, proj, proj, proj, bkt, rel_bias)


def _memattn_kernel(q_ref, kv_ref, o_ref):
    for h in range(C_HEADS):
        hs = slice(h * C_HEAD_DIM, (h + 1) * C_HEAD_DIM)
        s = _nt_dot(q_ref[:, hs], kv_ref[:, hs])
        m = jnp.max(s, axis=1, keepdims=True)
        p = jnp.exp2(s - m)
        den = jnp.sum(p, axis=1, keepdims=True)
        o = jnp.dot(p.astype(BF16), kv_ref[:, C_WIDTH + h * C_HEAD_DIM:C_WIDTH + (h + 1) * C_HEAD_DIM],
                    preferred_element_type=F32)
        o_ref[:, hs] = (o / den).astype(BF16)


def _memattn(proj, mkv, batch, seq):
    nt = seq // TQ_C
    return pl.pallas_call(
        _memattn_kernel,
        grid=(batch, nt),
        in_specs=[
            pl.BlockSpec((TQ_C, C_WIDTH), lambda b, i: (b * nt + i, COL_CQ // C_WIDTH)),
            pl.BlockSpec((N_MEM, 2 * C_WIDTH), lambda b, i: (b, 0)),
        ],
        out_specs=pl.BlockSpec((TQ_C, C_WIDTH), lambda b, i: (b * nt + i, 0)),
        out_shape=jax.ShapeDtypeStruct((batch * seq, C_WIDTH), BF16),
        compiler_params=pltpu.CompilerParams(
            dimension_semantics=("arbitrary", "arbitrary"), vmem_limit_bytes=VMEM_LIMIT),
        name="memattn",
    )(proj, mkv)


def _outproj_kernel(oa_ref, ob_ref, oc_ref, gate_ref, x_ref, w_ref, g_ref, o_ref):
    gate = gate_ref[...].astype(F32)
    sg = gate / (1.0 + jnp.exp(-gate))
    att = jnp.concatenate([oa_ref[...], ob_ref[...], oc_ref[...]], axis=1).astype(F32)
    y = (att * sg).astype(BF16)
    h = x_ref[...] + jnp.dot(y, w_ref[...], preferred_element_type=F32)
    ms = jnp.mean(h * h, axis=-1, keepdims=True)
    o_ref[...] = h * lax.rsqrt(ms + EPS) * g_ref[...]


def _outproj(oa, ob, oc, proj, x2d, w, g):
    m = x2d.shape[0]
    return pl.pallas_call(
        _outproj_kernel,
        grid=(m // TM_OUT,),
        in_specs=[
            pl.BlockSpec((TM_OUT, A_WIDTH), lambda i: (i, 0)),
            pl.BlockSpec((TM_OUT, B_WIDTH), lambda i: (i, 0)),
            pl.BlockSpec((TM_OUT, C_WIDTH), lambda i: (i, 0)),
            pl.BlockSpec((TM_OUT, MIX_WIDTH), lambda i: (i, COL_GATE // MIX_WIDTH)),
            pl.BlockSpec((TM_OUT, D_MODEL), lambda i: (i, 0)),
            pl.BlockSpec((MIX_WIDTH, D_MODEL), lambda i: (0, 0)),
            pl.BlockSpec((1, D_MODEL), lambda i: (0, 0)),
        ],
        out_specs=pl.BlockSpec((TM_OUT, D_MODEL), lambda i: (i, 0)),
        out_shape=jax.ShapeDtypeStruct((m, D_MODEL), F32),
        compiler_params=pltpu.CompilerParams(
            dimension_semantics=("parallel",), vmem_limit_bytes=VMEM_LIMIT),
        name="outproj",
    )(oa, ob, oc, proj, x2d, w, g)


def _col_scale():
    cs = np.ones((1, PROJ_W), np.float32)
    cs[0, COL_AQ:COL_AQ + A_WIDTH] = A_HEAD_DIM ** -0.5 * LOG2E
    cs[0, COL_BQ:COL_BQ + B_WIDTH] = B_HEAD_DIM ** -0.5 * LOG2E
    cs[0, COL_CQ:COL_CQ + C_WIDTH] = C_HEAD_DIM ** -0.5 * LOG2E
    return jnp.asarray(cs)


def kernel(x, mem, g_norm, w_in, sinks, rel_bias, g_mem, w_mem_kv, w_out, g_final):
    batch, seq, _ = x.shape
    assert w_in.shape[0] == 1, "the out-projection kernel fuses the final norm of a single-layer trunk"
    r = np.arange(TQ_A)[:, None]
    c = np.arange(2 * TQ_A)[None, :]
    bkt_a = jnp.asarray(_t5_bucket_np(c - TQ_A - r))
    kk = np.arange(TK_B)[None, :, None]
    qq = np.arange(TQ_B)[None, None, :]
    d = np.arange(NEAR_B)[:, None, None]
    bkt_b = jnp.asarray(_t5_bucket_np(kk - qq - TK_B * d))

    h = x.reshape(batch * seq, D_MODEL)
    mem2d = mem.reshape(batch * N_MEM, D_MODEL)
    w_t = jnp.transpose(w_in, (2, 0, 1)).reshape(IN_WIDTH * KB_IN, LANES)
    proj = _inproj(h, g_norm[0].reshape(1, D_MODEL), _wprep(w_t), _col_scale())
    mkv = _memkv(mem2d, g_mem[0].reshape(1, D_MODEL), w_mem_kv[0])
    oa = _swa(proj, bkt_a, rel_bias, sinks[0], batch, seq)
    ob = _dsa(proj, bkt_b, rel_bias, batch, seq)
    oc = _memattn(proj, mkv, batch, seq)
    out = _outproj(oa, ob, oc, proj, h, w_out[0].astype(BF16), g_final.reshape(1, D_MODEL))
    return out.reshape(batch, seq, D_MODEL)
```

```python
import functools
import math

import numpy as np
import jax
import jax.numpy as jnp
from jax import lax
from jax.experimental import pallas as pl
from jax.experimental.pallas import tpu as pltpu

D_MODEL = 2048
CHUNK = 64
N_MEM = 256
EPS = 1e-6
A_HEADS = 16
A_KV_HEADS = 2
A_HEAD_DIM = 64
WINDOW_CHUNKS = 2
A_WIDTH = A_HEADS * A_HEAD_DIM
B_HEADS = 4
B_HEAD_DIM = 128
B_WIDTH = B_HEADS * B_HEAD_DIM
IDX_HEADS = 4
IDX_DIM = 64
TOPK_MAX = 256
C_HEADS = 4
C_HEAD_DIM = 128
C_WIDTH = C_HEADS * C_HEAD_DIM
MIX_WIDTH = A_WIDTH + B_WIDTH + C_WIDTH
N_BUCKETS = 32
MAX_DISTANCE = 1024
KV_A = A_KV_HEADS * A_HEAD_DIM
SPLIT_SIZES = (A_WIDTH, KV_A, KV_A, B_WIDTH, B_WIDTH, B_WIDTH,
               IDX_HEADS * IDX_DIM, IDX_DIM, IDX_HEADS, C_WIDTH, MIX_WIDTH)
IN_WIDTH = sum(SPLIT_SIZES)

F32 = jnp.float32
BF16 = jnp.bfloat16
LOG2E = math.log2(math.e)
NEG = -1e30
INT_MIN = -(2 ** 31)
LANES = 128
SUBLANES = 8
PACKED = 16
HALF = 1 << 15

(SRC_AQ, SRC_AK, SRC_AV, SRC_BQ, SRC_BK, SRC_BV,
 SRC_IQ, SRC_IK, SRC_IW, SRC_CQ, SRC_GATE) = (int(c) for c in np.cumsum((0,) + SPLIT_SIZES)[:-1])
COL_GATE = 0
COL_AQ = COL_GATE + MIX_WIDTH
COL_BQ = COL_AQ + A_WIDTH
COL_BK = COL_BQ + B_WIDTH
COL_BV = COL_BK + B_WIDTH
COL_CQ = COL_BV + B_WIDTH
COL_AKV = COL_CQ + C_WIDTH
COL_IQ = COL_AKV + 2 * KV_A
COL_IKW = COL_IQ + IDX_HEADS * IDX_DIM
PROJ_W = 6144
SEGMENTS = ((COL_GATE, SRC_GATE, MIX_WIDTH), (COL_AQ, SRC_AQ, A_WIDTH), (COL_BQ, SRC_BQ, B_WIDTH),
            (COL_BK, SRC_BK, B_WIDTH), (COL_BV, SRC_BV, B_WIDTH), (COL_CQ, SRC_CQ, C_WIDTH),
            (COL_AKV, SRC_AK, 2 * KV_A), (COL_IQ, SRC_IQ, IDX_HEADS * IDX_DIM), (COL_IKW, SRC_IK, 256))
KB_IN = D_MODEL // LANES

TC_PREP = 256
TM_IN, TN_IN = 1024, 1024
TQ_A = 128
TQ_B = 256
TK_B = 256
NEAR_B = 4
TQ_C = 512
TM_OUT = 512
VMEM_LIMIT = 56 * 1024 * 1024


def _t5_bucket_np(rel):
    nb = N_BUCKETS // 2
    max_exact = nb // 2
    side = np.where(rel > 0, nb, 0)
    n = np.abs(rel)
    nf = np.maximum(n, max_exact).astype(np.float32)
    large = max_exact + (np.log(nf / max_exact) / math.log(MAX_DISTANCE / max_exact)
                         * (nb - max_exact)).astype(np.int32)
    large = np.minimum(large, nb - 1)
    return (side + np.where(n < max_exact, n, large)).astype(np.int32)


def _nt_dot(a, b):
    return lax.dot_general(a, b, (((1,), (1,)), ((), ())), preferred_element_type=F32)


def _bias_table(bucket, rb_ref, col, sub_row=None):
    acc = jnp.zeros(bucket.shape, F32)
    for b in range(N_BUCKETS):
        val = rb_ref[b, col]
        if sub_row is not None:
            val = val - rb_ref[sub_row, col]
        acc = jnp.where(bucket == b, val * LOG2E, acc)
    return acc


def _wprep_kernel(src_ref, w_ref, o_ref):
    t = pl.program_id(0)

    @pl.when(src_ref[t] >= 0)
    def _():
        for kb in range(KB_IN):
            o_ref[:, kb * LANES:(kb + 1) * LANES] = w_ref[pl.ds(kb, TC_PREP, stride=KB_IN), :].astype(BF16)

    @pl.when(src_ref[t] < 0)
    def _():
        o_ref[...] = jnp.zeros(o_ref.shape, BF16)


def _wprep(w_t):
    src = np.full((PROJ_W // TC_PREP,), -1, np.int32)
    for dst_col, src_col, width in SEGMENTS:
        for n in range(width // TC_PREP):
            src[dst_col // TC_PREP + n] = src_col + n * TC_PREP
    return pl.pallas_call(
        _wprep_kernel,
        grid_spec=pltpu.PrefetchScalarGridSpec(
            num_scalar_prefetch=1,
            grid=(PROJ_W // TC_PREP,),
            in_specs=[pl.BlockSpec((pl.Element(TC_PREP * KB_IN), pl.Element(LANES)),
                                   lambda t, src: (jnp.maximum(src[t], 0) * KB_IN, 0))],
            out_specs=pl.BlockSpec((TC_PREP, D_MODEL), lambda t, src: (t, 0)),
        ),
        out_shape=jax.ShapeDtypeStruct((PROJ_W, D_MODEL), BF16),
        compiler_params=pltpu.CompilerParams(
            dimension_semantics=("arbitrary",), vmem_limit_bytes=VMEM_LIMIT),
        name="wprep",
    )(jnp.asarray(src), w_t)


def _inproj_kernel(x_ref, g_ref, wt_ref, cs_ref, o_ref, hn_ref):
    @pl.when(pl.program_id(1) == 0)
    def _():
        x = x_ref[...]
        ms = jnp.mean(x * x, axis=-1, keepdims=True)
        hn_ref[...] = (x * lax.rsqrt(ms + EPS) * g_ref[...]).astype(BF16)

    acc = _nt_dot(hn_ref[...], wt_ref[...])
    o_ref[...] = (acc * cs_ref[...]).astype(BF16)


def _inproj(x2d, g, wt, cs):
    m = x2d.shape[0]
    return pl.pallas_call(
        _inproj_kernel,
        grid=(m // TM_IN, PROJ_W // TN_IN),
        in_specs=[
            pl.BlockSpec((TM_IN, D_MODEL), lambda i, j: (i, 0)),
            pl.BlockSpec((1, D_MODEL), lambda i, j: (0, 0)),
            pl.BlockSpec((TN_IN, D_MODEL), lambda i, j: (j, 0)),
            pl.BlockSpec((1, TN_IN), lambda i, j: (0, j)),
        ],
        out_specs=pl.BlockSpec((TM_IN, TN_IN), lambda i, j: (i, j)),
        out_shape=jax.ShapeDtypeStruct((m, PROJ_W), BF16),
        scratch_shapes=[pltpu.VMEM((TM_IN, D_MODEL), BF16)],
        compiler_params=pltpu.CompilerParams(
            dimension_semantics=("parallel", "arbitrary"), vmem_limit_bytes=VMEM_LIMIT),
        name="inproj",
    )(x2d, g, wt, cs)


def _memkv_kernel(m_ref, g_ref, w_ref, o_ref):
    x = m_ref[...]
    ms = jnp.mean(x * x, axis=-1, keepdims=True)
    hn = (x * lax.rsqrt(ms + EPS) * g_ref[...]).astype(BF16)
    o_ref[...] = jnp.dot(hn, w_ref[...].astype(BF16), preferred_element_type=F32).astype(BF16)


def _memkv(mem2d, g, w):
    m = mem2d.shape[0]
    return pl.pallas_call(
        _memkv_kernel,
        grid=(m // N_MEM,),
        in_specs=[
            pl.BlockSpec((N_MEM, D_MODEL), lambda i: (i, 0)),
            pl.BlockSpec((1, D_MODEL), lambda i: (0, 0)),
            pl.BlockSpec((D_MODEL, 2 * C_WIDTH), lambda i: (0, 0)),
        ],
        out_specs=pl.BlockSpec((N_MEM, 2 * C_WIDTH), lambda i: (i, 0)),
        out_shape=jax.ShapeDtypeStruct((m, 2 * C_WIDTH), BF16),
        compiler_params=pltpu.CompilerParams(
            dimension_semantics=("arbitrary",), vmem_limit_bytes=VMEM_LIMIT),
        name="memkv",
    )(mem2d, g, w)


def _swa_kernel(q_ref, kvp_ref, kvc_ref, bkt_ref, rb_ref, sink_ref, o_ref, tab_ref):
    i = pl.program_id(1)

    @pl.when((pl.program_id(0) == 0) & (i == 0))
    def _():
        bkt = bkt_ref[...]
        row = lax.broadcasted_iota(jnp.int32, bkt.shape, 0) // CHUNK
        col = lax.broadcasted_iota(jnp.int32, bkt.shape, 1) // CHUNK
        allowed = (col >= row) & (col <= row + WINDOW_CHUNKS)
        for h in range(A_HEADS):
            tab_ref[h] = jnp.where(allowed, _bias_table(bkt, rb_ref, h), NEG)

    col = lax.broadcasted_iota(jnp.int32, (TQ_A, 2 * TQ_A), 1)
    pen = jnp.where(col < TQ_A, jnp.where(i == 0, NEG, 0.0), 0.0)

    outs = []
    for h in range(A_HEADS):
        g = h // (A_HEADS // A_KV_HEADS)
        ks = slice(g * A_HEAD_DIM, (g + 1) * A_HEAD_DIM)
        vs = slice(KV_A + g * A_HEAD_DIM, KV_A + (g + 1) * A_HEAD_DIM)
        qh = q_ref[:, h * A_HEAD_DIM:(h + 1) * A_HEAD_DIM]
        kg = jnp.concatenate([kvp_ref[:, ks], kvc_ref[:, ks]], axis=0)
        vg = jnp.concatenate([kvp_ref[:, vs], kvc_ref[:, vs]], axis=0)
        s = _nt_dot(qh, kg) + tab_ref[h] + pen
        sk = sink_ref[h] * LOG2E
        m = jnp.maximum(jnp.max(s, axis=1, keepdims=True), sk)
        p = jnp.exp2(s - m)
        den = jnp.sum(p, axis=1, keepdims=True) + jnp.exp2(sk - m)
        o = jnp.dot(p.astype(BF16), vg, preferred_element_type=F32)
        outs.append(o / den)
    o_ref[...] = jnp.concatenate(outs, axis=1).astype(BF16)


def _swa(proj, bkt, rel_bias, sinks, batch, seq):
    nt = seq // TQ_A
    kv_blk = COL_AKV // (2 * KV_A)
    return pl.pallas_call(
        _swa_kernel,
        grid=(batch, nt),
        in_specs=[
            pl.BlockSpec((TQ_A, A_WIDTH), lambda b, i: (b * nt + i, COL_AQ // A_WIDTH)),
            pl.BlockSpec((TQ_A, 2 * KV_A), lambda b, i: (b * nt + jnp.maximum(i - 1, 0), kv_blk)),
            pl.BlockSpec((TQ_A, 2 * KV_A), lambda b, i: (b * nt + i, kv_blk)),
            pl.BlockSpec((TQ_A, 2 * TQ_A), lambda b, i: (0, 0)),
            pl.BlockSpec(memory_space=pltpu.SMEM),
            pl.BlockSpec(memory_space=pltpu.SMEM),
        ],
        out_specs=pl.BlockSpec((TQ_A, A_WIDTH), lambda b, i: (b * nt + i, 0)),
        out_shape=jax.ShapeDtypeStruct((batch * seq, A_WIDTH), BF16),
        scratch_shapes=[pltpu.VMEM((A_HEADS, TQ_A, 2 * TQ_A), F32)],
        compiler_params=pltpu.CompilerParams(
            dimension_semantics=("arbitrary", "arbitrary"), vmem_limit_bytes=VMEM_LIMIT),
        name="swa",
    )(proj, proj, proj, bkt, rel_bias, sinks)


def _dsa_kernel(iq_ref, iwq_ref, ik_ref, q_ref, k_ref, v_ref, bkt_ref, rb_ref, o_ref,
                keys_ref, hi_ref, lo_ref, mb_ref, tab_ref, tri_ref, vt_ref, acc_ref, m_ref, *, topk, seq):
    i = pl.program_id(1)
    nkt = i + 1
    int32, int16 = jnp.int32, jnp.int16
    grp = TK_B // SUBLANES

    def hcols(h):
        return slice(h * B_HEAD_DIM, (h + 1) * B_HEAD_DIM)

    def ktile(kt):
        return pl.ds(pl.multiple_of(kt * TK_B, TK_B), TK_B)

    def rows3(a):
        return a.reshape(a.shape[0] // SUBLANES, SUBLANES, TQ_B)

    def all_rows(a, op):
        return jnp.broadcast_to(op(a, axis=0, keepdims=True), a.shape)

    @pl.when((pl.program_id(0) == 0) & (i == 0))
    def _():
        far_bucket = N_BUCKETS // 2 - 1
        for d in range(NEAR_B):
            for h in range(B_HEADS):
                tab_ref[d, h] = _bias_table(bkt_ref[d], rb_ref, A_HEADS + h, sub_row=far_bucket)
        r = lax.broadcasted_iota(int32, (TK_B, TK_B), 0)
        c = lax.broadcasted_iota(int32, (TK_B, TK_B), 1)
        tri_ref[...] = jnp.where(c < r, 1.0, 0.0).astype(BF16)

    @pl.when(i == 0)
    def _():
        r = lax.broadcasted_iota(int32, (B_HEAD_DIM, B_HEAD_DIM), 0)
        c = lax.broadcasted_iota(int32, (B_HEAD_DIM, B_HEAD_DIM), 1)
        eye = jnp.where(r == c, 1.0, 0.0).astype(BF16)

        def body(kt, carry):
            for h in range(B_HEADS):
                vt_ref[kt, h] = _nt_dot(eye, v_ref[ktile(kt), hcols(h)]).astype(BF16)
            return carry

        lax.fori_loop(0, seq // TK_B, body, 0)

    key_chunk = lax.broadcasted_iota(int32, (TK_B, TQ_B), 0) // CHUNK
    qry_chunk = lax.broadcasted_iota(int32, (TK_B, TQ_B), 1) // CHUNK
    adm_diag = key_chunk <= qry_chunk

    r = lax.broadcasted_iota(int32, (IDX_HEADS * SUBLANES, LANES), 0)
    c = lax.broadcasted_iota(int32, (IDX_HEADS * SUBLANES, LANES), 1)
    pick_w = jnp.where(c == IDX_DIM + r // SUBLANES, 1.0, 0.0).astype(BF16)
    w_all = _nt_dot(pick_w, iwq_ref[...]) * (IDX_HEADS ** -0.5 * IDX_DIM ** -0.5)

    def score_body(kt, carry):
        ikt = ik_ref[ktile(kt), 0:IDX_DIM]
        sc = jnp.zeros((grp, SUBLANES, TQ_B), F32)
        for h in range(IDX_HEADS):
            x = _nt_dot(ikt, iq_ref[:, h * IDX_DIM:(h + 1) * IDX_DIM])
            sc = sc + w_all[h * SUBLANES:(h + 1) * SUBLANES][None] * jnp.maximum(rows3(x), 0.0)
        bits = pltpu.bitcast(sc.reshape(TK_B, TQ_B), int32)
        sgn = bits >> 31
        store_key(kt, (bits ^ (sgn & 0x7FFFFFFF)) - sgn)
        return carry

    def store_key(kt, key):
        keys_ref[kt] = key
        hi_ref[kt] = (key >> 16).astype(int16)
        lo_ref[kt] = ((key & 0xFFFF) - HALF).astype(int16)

    lax.fori_loop(0, nkt, score_body, 0)
    store_key(i, jnp.where(adm_diag, keys_ref[i], INT_MIN))

    def rows3h(a):
        return a.reshape(TK_B // PACKED, PACKED, TQ_B)

    def count16(ref, pred):
        def body(kt, acc):
            blk = rows3h(ref[kt])
            parts = [jnp.where(pred(blk[g]), jnp.ones((), int16), jnp.zeros((), int16))
                     for g in range(TK_B // PACKED)]
            while len(parts) > 1:
                parts = [parts[n] + parts[n + 1] for n in range(0, len(parts), 2)]
            return acc + parts[0]
        acc = lax.fori_loop(0, nkt, body, jnp.zeros((PACKED, TQ_B), int16))
        return all_rows(acc.astype(int32), jnp.sum)

    def search16(ref, target):
        def pass_body(p, u):
            bit = jnp.left_shift(jnp.int32(1), 15 - p)
            cand = ((u | bit) - HALF).astype(int16)
            cnt = count16(ref, lambda blk: blk >= cand)
            return jnp.where(cnt >= target, u | bit, u)
        return lax.fori_loop(0, 16, pass_body, jnp.zeros((PACKED, TQ_B), int32)) - HALF

    tau_hi = search16(hi_ref, topk)
    tau_hi16 = tau_hi.astype(int16)
    above = count16(hi_ref, lambda blk: blk > tau_hi16)

    def bucket_body(kt, carry):
        lo_ref[kt] = jnp.where(rows3h(hi_ref[kt]) == tau_hi16[None], rows3h(lo_ref[kt]),
                               jnp.full((), -HALF, int16)).reshape(TK_B, TQ_B)
        return carry

    lax.fori_loop(0, nkt, bucket_body, 0)
    tau_lo = search16(lo_ref, topk - above)
    tau_lo16 = tau_lo.astype(int16)
    above = above + count16(lo_ref, lambda blk: blk > tau_lo16)
    tau = ((tau_hi << 16) | (tau_lo + HALF))[:SUBLANES][None]

    need = (topk - above).astype(F32)[:SUBLANES][None]
    ones_l = jnp.ones((2 * SUBLANES, TK_B), BF16)

    def mask_body(kt, run):
        blk = rows3(keys_ref[kt])
        eq = blk == tau
        eqf = jnp.where(eq, 1.0, 0.0).reshape(TK_B, TQ_B).astype(BF16)
        rank = rows3(jnp.dot(tri_ref[...], eqf, preferred_element_type=F32)) + run[None]
        sel = (blk > tau) | (eq & (rank < need))
        mb_ref[kt] = jnp.where(sel, 0.0, NEG).reshape(TK_B, TQ_B)
        return run + jnp.dot(ones_l, eqf, preferred_element_type=F32)[:SUBLANES]

    lax.fori_loop(0, nkt, mask_body, jnp.zeros((SUBLANES, TQ_B), F32))
    mb_ref[i] = jnp.where(adm_diag, mb_ref[i], NEG)

    m_ref[...] = jnp.full(m_ref.shape, NEG, F32)
    acc_ref[...] = jnp.zeros(acc_ref.shape, F32)
    ones_rows = jnp.ones((B_HEAD_DIM, TK_B), BF16)

    def att_body(kt, carry, near):
        mb = mb_ref[kt]
        for h in range(B_HEADS):
            s = _nt_dot(k_ref[ktile(kt), hcols(h)], q_ref[:, hcols(h)]) + mb
            if near:
                s = s + tab_ref[i - kt, h]
            s = rows3(s)
            m_old = m_ref[h]
            m_new = jnp.maximum(m_old, all_rows(jnp.max(s, axis=0), jnp.max))
            alpha = jnp.exp2(m_old - m_new)
            p = jnp.exp2(s - m_new[None]).reshape(TK_B, TQ_B).astype(BF16)
            vaug = jnp.concatenate([vt_ref[kt, h], ones_rows], axis=0)
            pv = jnp.dot(vaug, p, preferred_element_type=F32)
            acc_ref[h] = (rows3(acc_ref[h]) * alpha[None] + rows3(pv)).reshape(2 * B_HEAD_DIM, TQ_B)
            m_ref[h] = m_new
        return carry

    near_lo = jnp.maximum(i - (NEAR_B - 1), 0)
    lax.fori_loop(0, near_lo, functools.partial(att_body, near=False), 0)
    lax.fori_loop(near_lo, nkt, functools.partial(att_body, near=True), 0)
    for h in range(B_HEADS):
        num = rows3(acc_ref[h, 0:B_HEAD_DIM, :])
        den = acc_ref[h, B_HEAD_DIM:B_HEAD_DIM + SUBLANES, :]
        out_t = (num / den[None]).reshape(B_HEAD_DIM, TQ_B)
        o_ref[:, h * B_HEAD_DIM:(h + 1) * B_HEAD_DIM] = out_t.T.astype(BF16)


def _dsa(proj, bkt, rel_bias, batch, seq):
    nt = seq // TQ_B
    topk = min(TOPK_MAX, seq // 4)

    def q_spec(width, col):
        return pl.BlockSpec((TQ_B, width), lambda b, i: (b * nt + i, col // width))

    def seq_spec(width, col):
        return pl.BlockSpec((seq, width), lambda b, i: (b, col // width))

    return pl.pallas_call(
        functools.partial(_dsa_kernel, topk=topk, seq=seq),
        grid=(batch, nt),
        in_specs=[
            q_spec(IDX_HEADS * IDX_DIM, COL_IQ),
            q_spec(LANES, COL_IKW),
            seq_spec(LANES, COL_IKW),
            q_spec(B_WIDTH, COL_BQ), seq_spec(B_WIDTH, COL_BK), seq_spec(B_WIDTH, COL_BV),
            pl.BlockSpec((NEAR_B, TK_B, TQ_B), lambda b, i: (0, 0, 0)),
            pl.BlockSpec(memory_space=pltpu.SMEM),
        ],
        out_specs=pl.BlockSpec((TQ_B, B_WIDTH), lambda b, i: (b * nt + i, 0)),
        out_shape=jax.ShapeDtypeStruct((batch * seq, B_WIDTH), BF16),
        scratch_shapes=[
            pltpu.VMEM((nt, TK_B, TQ_B), jnp.int32),
            pltpu.VMEM((nt, TK_B, TQ_B), jnp.int16),
            pltpu.VMEM((nt, TK_B, TQ_B), jnp.int16),
            pltpu.VMEM((nt, TK_B, TQ_B), F32),
            pltpu.VMEM((NEAR_B, B_HEADS, TK_B, TQ_B), F32),
            pltpu.VMEM((TK_B, TK_B), BF16),
            pltpu.VMEM((seq // TK_B, B_HEADS, B_HEAD_DIM, TK_B), BF16),
            pltpu.VMEM((B_HEADS, 2 * B_HEAD_DIM, TQ_B), F32),
            pltpu.VMEM((B_HEADS, SUBLANES, TQ_B), F32),
        ],
        compiler_params=pltpu.CompilerParams(
            dimension_semantics=("arbitrary", "arbitrary"), vmem_limit_bytes=VMEM_LIMIT),
        name="dsa",
    )(proj, proj, proj, proj, proj, proj, bkt, rel_bias)


def _memattn_kernel(q_ref, kv_ref, o_ref):
    for h in range(C_HEADS):
        hs = slice(h * C_HEAD_DIM, (h + 1) * C_HEAD_DIM)
        s = _nt_dot(q_ref[:, hs], kv_ref[:, hs])
        m = jnp.max(s, axis=1, keepdims=True)
        p = jnp.exp2(s - m)
        den = jnp.sum(p, axis=1, keepdims=True)
        o = jnp.dot(p.astype(BF16), kv_ref[:, C_WIDTH + h * C_HEAD_DIM:C_WIDTH + (h + 1) * C_HEAD_DIM],
                    preferred_element_type=F32)
        o_ref[:, hs] = (o / den).astype(BF16)


def _memattn(proj, mkv, batch, seq):
    nt = seq // TQ_C
    return pl.pallas_call(
        _memattn_kernel,
        grid=(batch, nt),
        in_specs=[
            pl.BlockSpec((TQ_C, C_WIDTH), lambda b, i: (b * nt + i, COL_CQ // C_WIDTH)),
            pl.BlockSpec((N_MEM, 2 * C_WIDTH), lambda b, i: (b, 0)),
        ],
        out_specs=pl.BlockSpec((TQ_C, C_WIDTH), lambda b, i: (b * nt + i, 0)),
        out_shape=jax.ShapeDtypeStruct((batch * seq, C_WIDTH), BF16),
        compiler_params=pltpu.CompilerParams(
            dimension_semantics=("arbitrary", "arbitrary"), vmem_limit_bytes=VMEM_LIMIT),
        name="memattn",
    )(proj, mkv)


def _outproj_kernel(oa_ref, ob_ref, oc_ref, gate_ref, x_ref, w_ref, g_ref, o_ref):
    gate = gate_ref[...].astype(F32)
    sg = gate / (1.0 + jnp.exp(-gate))
    att = jnp.concatenate([oa_ref[...], ob_ref[...], oc_ref[...]], axis=1).astype(F32)
    y = (att * sg).astype(BF16)
    h = x_ref[...] + jnp.dot(y, w_ref[...], preferred_element_type=F32)
    ms = jnp.mean(h * h, axis=-1, keepdims=True)
    o_ref[...] = h * lax.rsqrt(ms + EPS) * g_ref[...]


def _outproj(oa, ob, oc, proj, x2d, w, g):
    m = x2d.shape[0]
    return pl.pallas_call(
        _outproj_kernel,
        grid=(m // TM_OUT,),
        in_specs=[
            pl.BlockSpec((TM_OUT, A_WIDTH), lambda i: (i, 0)),
            pl.BlockSpec((TM_OUT, B_WIDTH), lambda i: (i, 0)),
            pl.BlockSpec((TM_OUT, C_WIDTH), lambda i: (i, 0)),
            pl.BlockSpec((TM_OUT, MIX_WIDTH), lambda i: (i, COL_GATE // MIX_WIDTH)),
            pl.BlockSpec((TM_OUT, D_MODEL), lambda i: (i, 0)),
            pl.BlockSpec((MIX_WIDTH, D_MODEL), lambda i: (0, 0)),
            pl.BlockSpec((1, D_MODEL), lambda i: (0, 0)),
        ],
        out_specs=pl.BlockSpec((TM_OUT, D_MODEL), lambda i: (i, 0)),
        out_shape=jax.ShapeDtypeStruct((m, D_MODEL), F32),
        compiler_params=pltpu.CompilerParams(
            dimension_semantics=("parallel",), vmem_limit_bytes=VMEM_LIMIT),
        name="outproj",
    )(oa, ob, oc, proj, x2d, w, g)


def _col_scale():
    cs = np.ones((1, PROJ_W), np.float32)
    cs[0, COL_AQ:COL_AQ + A_WIDTH] = A_HEAD_DIM ** -0.5 * LOG2E
    cs[0, COL_BQ:COL_BQ + B_WIDTH] = B_HEAD_DIM ** -0.5 * LOG2E
    cs[0, COL_CQ:COL_CQ + C_WIDTH] = C_HEAD_DIM ** -0.5 * LOG2E
    return jnp.asarray(cs)


def kernel(x, mem, g_norm, w_in, sinks, rel_bias, g_mem, w_mem_kv, w_out, g_final):
    batch, seq, _ = x.shape
    assert w_in.shape[0] == 1, "the out-projection kernel fuses the final norm of a single-layer trunk"
    r = np.arange(TQ_A)[:, None]
    c = np.arange(2 * TQ_A)[None, :]
    bkt_a = jnp.asarray(_t5_bucket_np(c - TQ_A - r))
    kk = np.arange(TK_B)[None, :, None]
    qq = np.arange(TQ_B)[None, None, :]
    d = np.arange(NEAR_B)[:, None, None]
    bkt_b = jnp.asarray(_t5_bucket_np(kk - qq - TK_B * d))

    h = x.reshape(batch * seq, D_MODEL)
    mem2d = mem.reshape(batch * N_MEM, D_MODEL)
    w_t = jnp.transpose(w_in, (2, 0, 1)).reshape(IN_WIDTH * KB_IN, LANES)
    proj = _inproj(h, g_norm[0].reshape(1, D_MODEL), _wprep(w_t), _col_scale())
    mkv = _memkv(mem2d, g_mem[0].reshape(1, D_MODEL), w_mem_kv[0])
    oa = _swa(proj, bkt_a, rel_bias, sinks[0], batch, seq)
    ob = _dsa(proj, bkt_b, rel_bias, batch, seq)
    oc = _memattn(proj, mkv, batch, seq)
    out = _outproj(oa, ob, oc, proj, h, w_out[0].astype(BF16), g_final.reshape(1, D_MODEL))
    return out.reshape(batch, seq, D_MODEL)
```

```python
import functools
import math

import numpy as np
import jax
import jax.numpy as jnp
from jax import lax
from jax.experimental import pallas as pl
from jax.experimental.pallas import tpu as pltpu

D_MODEL = 2048
CHUNK = 64
N_MEM = 256
EPS = 1e-6
A_HEADS = 16
A_KV_HEADS = 2
A_HEAD_DIM = 64
WINDOW_CHUNKS = 2
A_WIDTH = A_HEADS * A_HEAD_DIM
B_HEADS = 4
B_HEAD_DIM = 128
B_WIDTH = B_HEADS * B_HEAD_DIM
IDX_HEADS = 4
IDX_DIM = 64
TOPK_MAX = 256
C_HEADS = 4
C_HEAD_DIM = 128
C_WIDTH = C_HEADS * C_HEAD_DIM
MIX_WIDTH = A_WIDTH + B_WIDTH + C_WIDTH
N_BUCKETS = 32
MAX_DISTANCE = 1024
KV_A = A_KV_HEADS * A_HEAD_DIM
SPLIT_SIZES = (A_WIDTH, KV_A, KV_A, B_WIDTH, B_WIDTH, B_WIDTH,
               IDX_HEADS * IDX_DIM, IDX_DIM, IDX_HEADS, C_WIDTH, MIX_WIDTH)
IN_WIDTH = sum(SPLIT_SIZES)

F32 = jnp.float32
BF16 = jnp.bfloat16
LOG2E = math.log2(math.e)
NEG = -1e30
INT_MIN = -(2 ** 31)
LANES = 128
SUBLANES = 8
PACKED = 16
HALF = 1 << 15

(SRC_AQ, SRC_AK, SRC_AV, SRC_BQ, SRC_BK, SRC_BV,
 SRC_IQ, SRC_IK, SRC_IW, SRC_CQ, SRC_GATE) = (int(c) for c in np.cumsum((0,) + SPLIT_SIZES)[:-1])
COL_GATE = 0
COL_AQ = COL_GATE + MIX_WIDTH
COL_BQ = COL_AQ + A_WIDTH
COL_BK = COL_BQ + B_WIDTH
COL_BV = COL_BK + B_WIDTH
COL_CQ = COL_BV + B_WIDTH
COL_AKV = COL_CQ + C_WIDTH
COL_IQ = COL_AKV + 2 * KV_A
COL_IKW = COL_IQ + IDX_HEADS * IDX_DIM
PROJ_W = 6144
SEGMENTS = ((COL_GATE, SRC_GATE, MIX_WIDTH), (COL_AQ, SRC_AQ, A_WIDTH), (COL_BQ, SRC_BQ, B_WIDTH),
            (COL_BK, SRC_BK, B_WIDTH), (COL_BV, SRC_BV, B_WIDTH), (COL_CQ, SRC_CQ, C_WIDTH),
            (COL_AKV, SRC_AK, 2 * KV_A), (COL_IQ, SRC_IQ, IDX_HEADS * IDX_DIM), (COL_IKW, SRC_IK, 256))
KB_IN = D_MODEL // LANES

TC_PREP = 256
TM_IN, TN_IN = 1024, 1024
TQ_A = 128
TQ_B = 256
TK_B = 256
NEAR_B = 4
TQ_C = 512
TM_OUT = 512
VMEM_LIMIT = 56 * 1024 * 1024


def _t5_bucket_np(rel):
    nb = N_BUCKETS // 2
    max_exact = nb // 2
    side = np.where(rel > 0, nb, 0)
    n = np.abs(rel)
    nf = np.maximum(n, max_exact).astype(np.float32)
    large = max_exact + (np.log(nf / max_exact) / math.log(MAX_DISTANCE / max_exact)
                         * (nb - max_exact)).astype(np.int32)
    large = np.minimum(large, nb - 1)
    return (side + np.where(n < max_exact, n, large)).astype(np.int32)


def _nt_dot(a, b):
    return lax.dot_general(a, b, (((1,), (1,)), ((), ())), preferred_element_type=F32)


def _bias_table(bucket, rb_ref, col, sub_row=None):
    acc = jnp.zeros(bucket.shape, F32)
    for b in range(N_BUCKETS):
        val = rb_ref[b, col]
        if sub_row is not None:
            val = val - rb_ref[sub_row, col]
        acc = jnp.where(bucket == b, val * LOG2E, acc)
    return acc


def _wprep_kernel(src_ref, w_ref, o_ref):
    t = pl.program_id(0)

    @pl.when(src_ref[t] >= 0)
    def _():
        for kb in range(KB_IN):
            o_ref[:, kb * LANES:(kb + 1) * LANES] = w_ref[pl.ds(kb, TC_PREP, stride=KB_IN), :].astype(BF16)

    @pl.when(src_ref[t] < 0)
    def _():
        o_ref[...] = jnp.zeros(o_ref.shape, BF16)


def _wprep(w_t):
    src = np.full((PROJ_W // TC_PREP,), -1, np.int32)
    for dst_col, src_col, width in SEGMENTS:
        for n in range(width // TC_PREP):
            src[dst_col // TC_PREP + n] = src_col + n * TC_PREP
    return pl.pallas_call(
        _wprep_kernel,
        grid_spec=pltpu.PrefetchScalarGridSpec(
            num_scalar_prefetch=1,
            grid=(PROJ_W // TC_PREP,),
            in_specs=[pl.BlockSpec((pl.Element(TC_PREP * KB_IN), pl.Element(LANES)),
                                   lambda t, src: (jnp.maximum(src[t], 0) * KB_IN, 0))],
            out_specs=pl.BlockSpec((TC_PREP, D_MODEL), lambda t, src: (t, 0)),
        ),
        out_shape=jax.ShapeDtypeStruct((PROJ_W, D_MODEL), BF16),
        compiler_params=pltpu.CompilerParams(
            dimension_semantics=("arbitrary",), vmem_limit_bytes=VMEM_LIMIT),
        name="wprep",
    )(jnp.asarray(src), w_t)


def _inproj_kernel(x_ref, g_ref, wt_ref, cs_ref, o_ref, hn_ref):
    @pl.when(pl.program_id(1) == 0)
    def _():
        x = x_ref[...]
        ms = jnp.mean(x * x, axis=-1, keepdims=True)
        hn_ref[...] = (x * lax.rsqrt(ms + EPS) * g_ref[...]).astype(BF16)

    acc = _nt_dot(hn_ref[...], wt_ref[...])
    o_ref[...] = (acc * cs_ref[...]).astype(BF16)


def _inproj(x2d, g, wt, cs):
    m = x2d.shape[0]
    return pl.pallas_call(
        _inproj_kernel,
        grid=(m // TM_IN, PROJ_W // TN_IN),
        in_specs=[
            pl.BlockSpec((TM_IN, D_MODEL), lambda i, j: (i, 0)),
            pl.BlockSpec((1, D_MODEL), lambda i, j: (0, 0)),
            pl.BlockSpec((TN_IN, D_MODEL), lambda i, j: (j, 0)),
            pl.BlockSpec((1, TN_IN), lambda i, j: (0, j)),
        ],
        out_specs=pl.BlockSpec((TM_IN, TN_IN), lambda i, j: (i, j)),
        out_shape=jax.ShapeDtypeStruct((m, PROJ_W), BF16),
        scratch_shapes=[pltpu.VMEM((TM_IN, D_MODEL), BF16)],
        compiler_params=pltpu.CompilerParams(
            dimension_semantics=("parallel", "arbitrary"), vmem_limit_bytes=VMEM_LIMIT),
        name="inproj",
    )(x2d, g, wt, cs)


def _memkv_kernel(m_ref, g_ref, w_ref, o_ref):
    x = m_ref[...]
    ms = jnp.mean(x * x, axis=-1, keepdims=True)
    hn = (x * lax.rsqrt(ms + EPS) * g_ref[...]).astype(BF16)
    o_ref[...] = jnp.dot(hn, w_ref[...].astype(BF16), preferred_element_type=F32).astype(BF16)


def _memkv(mem2d, g, w):
    m = mem2d.shape[0]
    return pl.pallas_call(
        _memkv_kernel,
        grid=(m // N_MEM,),
        in_specs=[
            pl.BlockSpec((N_MEM, D_MODEL), lambda i: (i, 0)),
            pl.BlockSpec((1, D_MODEL), lambda i: (0, 0)),
            pl.BlockSpec((D_MODEL, 2 * C_WIDTH), lambda i: (0, 0)),
        ],
        out_specs=pl.BlockSpec((N_MEM, 2 * C_WIDTH), lambda i: (i, 0)),
        out_shape=jax.ShapeDtypeStruct((m, 2 * C_WIDTH), BF16),
        compiler_params=pltpu.CompilerParams(
            dimension_semantics=("arbitrary",), vmem_limit_bytes=VMEM_LIMIT),
        name="memkv",
    )(mem2d, g, w)


def _swa_kernel(q_ref, kvp_ref, kvc_ref, bkt_ref, rb_ref, sink_ref, o_ref, tab_ref):
    i = pl.program_id(1)

    @pl.when((pl.program_id(0) == 0) & (i == 0))
    def _():
        bkt = bkt_ref[...]
        row = lax.broadcasted_iota(jnp.int32, bkt.shape, 0) // CHUNK
        col = lax.broadcasted_iota(jnp.int32, bkt.shape, 1) // CHUNK
        allowed = (col >= row) & (col <= row + WINDOW_CHUNKS)
        for h in range(A_HEADS):
            tab_ref[h] = jnp.where(allowed, _bias_table(bkt, rb_ref, h), NEG)

    col = lax.broadcasted_iota(jnp.int32, (TQ_A, 2 * TQ_A), 1)
    pen = jnp.where(col < TQ_A, jnp.where(i == 0, NEG, 0.0), 0.0)

    outs = []
    for h in range(A_HEADS):
        g = h // (A_HEADS // A_KV_HEADS)
        ks = slice(g * A_HEAD_DIM, (g + 1) * A_HEAD_DIM)
        vs = slice(KV_A + g * A_HEAD_DIM, KV_A + (g + 1) * A_HEAD_DIM)
        qh = q_ref[:, h * A_HEAD_DIM:(h + 1) * A_HEAD_DIM]
        kg = jnp.concatenate([kvp_ref[:, ks], kvc_ref[:, ks]], axis=0)
        vg = jnp.concatenate([kvp_ref[:, vs], kvc_ref[:, vs]], axis=0)
        s = _nt_dot(qh, kg) + tab_ref[h] + pen
        sk = sink_ref[h] * LOG2E
        m = jnp.maximum(jnp.max(s, axis=1, keepdims=True), sk)
        p = jnp.exp2(s - m)
        den = jnp.sum(p, axis=1, keepdims=True) + jnp.exp2(sk - m)
        o = jnp.dot(p.astype(BF16), vg, preferred_element_type=F32)
        outs.append(o / den)
    o_ref[...] = jnp.concatenate(outs, axis=1).astype(BF16)


def _swa(proj, bkt, rel_bias, sinks, batch, seq):
    nt = seq // TQ_A
    kv_blk = COL_AKV // (2 * KV_A)
    return pl.pallas_call(
        _swa_kernel,
        grid=(batch, nt),
        in_specs=[
            pl.BlockSpec((TQ_A, A_WIDTH), lambda b, i: (b * nt + i, COL_AQ // A_WIDTH)),
            pl.BlockSpec((TQ_A, 2 * KV_A), lambda b, i: (b * nt + jnp.maximum(i - 1, 0), kv_blk)),
            pl.BlockSpec((TQ_A, 2 * KV_A), lambda b, i: (b * nt + i, kv_blk)),
            pl.BlockSpec((TQ_A, 2 * TQ_A), lambda b, i: (0, 0)),
            pl.BlockSpec(memory_space=pltpu.SMEM),
            pl.BlockSpec(memory_space=pltpu.SMEM),
        ],
        out_specs=pl.BlockSpec((TQ_A, A_WIDTH), lambda b, i: (b * nt + i, 0)),
        out_shape=jax.ShapeDtypeStruct((batch * seq, A_WIDTH), BF16),
        scratch_shapes=[pltpu.VMEM((A_HEADS, TQ_A, 2 * TQ_A), F32)],
        compiler_params=pltpu.CompilerParams(
            dimension_semantics=("arbitrary", "arbitrary"), vmem_limit_bytes=VMEM_LIMIT),
        name="swa",
    )(proj, proj, proj, bkt, rel_bias, sinks)


def _dsa_kernel(iq_ref, iwq_ref, ik_ref, q_ref, k_ref, v_ref, bkt_ref, rb_ref, o_ref,
                keys_ref, hi_ref, lo_ref, mb_ref, tab_ref, tri_ref, vt_ref, acc_ref, m_ref,
                sa_ref, smaxa_ref, sb_ref, smaxb_ref, *, topk, seq):
    i = pl.program_id(1)
    nkt = i + 1
    int32, int16 = jnp.int32, jnp.int16
    grp = TK_B // SUBLANES

    def hcols(h):
        return slice(h * B_HEAD_DIM, (h + 1) * B_HEAD_DIM)

    def ktile(kt):
        return pl.ds(pl.multiple_of(kt * TK_B, TK_B), TK_B)

    def rows3(a):
        return a.reshape(a.shape[0] // SUBLANES, SUBLANES, TQ_B)

    def all_rows(a, op):
        return jnp.broadcast_to(op(a, axis=0, keepdims=True), a.shape)

    @pl.when((pl.program_id(0) == 0) & (i == 0))
    def _():
        far_bucket = N_BUCKETS // 2 - 1
        for d in range(NEAR_B):
            for h in range(B_HEADS):
                tab_ref[d, h] = _bias_table(bkt_ref[d], rb_ref, A_HEADS + h, sub_row=far_bucket)
        tab_ref[NEAR_B] = jnp.zeros(tab_ref.shape[1:], F32)
        r = lax.broadcasted_iota(int32, (TK_B, TK_B), 0)
        c = lax.broadcasted_iota(int32, (TK_B, TK_B), 1)
        tri_ref[...] = jnp.where(c < r, 1.0, 0.0).astype(BF16)

    @pl.when(i == 0)
    def _():
        r = lax.broadcasted_iota(int32, (B_HEAD_DIM, B_HEAD_DIM), 0)
        c = lax.broadcasted_iota(int32, (B_HEAD_DIM, B_HEAD_DIM), 1)
        eye = jnp.where(r == c, 1.0, 0.0).astype(BF16)

        def body(kt, carry):
            for h in range(B_HEADS):
                vt_ref[kt, h] = _nt_dot(eye, v_ref[ktile(kt), hcols(h)]).astype(BF16)
            return carry

        lax.fori_loop(0, seq // TK_B, body, 0)

    key_chunk = lax.broadcasted_iota(int32, (TK_B, TQ_B), 0) // CHUNK
    qry_chunk = lax.broadcasted_iota(int32, (TK_B, TQ_B), 1) // CHUNK
    adm_diag = key_chunk <= qry_chunk

    r = lax.broadcasted_iota(int32, (IDX_HEADS * SUBLANES, LANES), 0)
    c = lax.broadcasted_iota(int32, (IDX_HEADS * SUBLANES, LANES), 1)
    pick_w = jnp.where(c == IDX_DIM + r // SUBLANES, 1.0, 0.0).astype(BF16)
    w_all = _nt_dot(pick_w, iwq_ref[...]) * (IDX_HEADS ** -0.5 * IDX_DIM ** -0.5)

    def for_tiles(n, body):
        def pair(j, carry):
            body(2 * j)
            body(2 * j + 1)
            return carry

        lax.fori_loop(0, n // 2, pair, 0)

        @pl.when(n % 2 == 1)
        def _():
            body(n - 1)

    def score_tile(kt):
        ikt = ik_ref[ktile(kt), 0:IDX_DIM]
        sc = jnp.zeros((grp, SUBLANES, TQ_B), F32)
        for h in range(IDX_HEADS):
            x = _nt_dot(ikt, iq_ref[:, h * IDX_DIM:(h + 1) * IDX_DIM])
            sc = sc + w_all[h * SUBLANES:(h + 1) * SUBLANES][None] * jnp.maximum(rows3(x), 0.0)
        bits = pltpu.bitcast(sc.reshape(TK_B, TQ_B), int32)
        sgn = bits >> 31
        store_key(kt, (bits ^ (sgn & 0x7FFFFFFF)) - sgn)

    def store_key(kt, key):
        keys_ref[kt] = key
        hi_ref[kt] = (key >> 16).astype(int16)
        lo_ref[kt] = ((key & 0xFFFF) - HALF).astype(int16)

    for_tiles(nkt, score_tile)
    store_key(i, jnp.where(adm_diag, keys_ref[i], INT_MIN))

    def rows3h(a):
        return a.reshape(TK_B // PACKED, PACKED, TQ_B)

    def count16(ref, pred):
        def tile_count(kt):
            blk = rows3h(ref[kt])
            parts = [jnp.where(pred(blk[g]), jnp.ones((), int16), jnp.zeros((), int16))
                     for g in range(TK_B // PACKED)]
            while len(parts) > 1:
                parts = [parts[n] + parts[n + 1] for n in range(0, len(parts), 2)]
            return parts[0]

        def pair(j, acc):
            return acc + (tile_count(2 * j) + tile_count(2 * j + 1))

        acc = lax.fori_loop(0, nkt // 2, pair, jnp.zeros((PACKED, TQ_B), int16))
        acc = lax.cond(nkt % 2 == 1, lambda a: a + tile_count(nkt - 1), lambda a: a, acc)
        return all_rows(acc.astype(int32), jnp.sum)

    def search16(ref, target):
        def pass_body(p, u):
            bit = jnp.left_shift(jnp.int32(1), 15 - p)
            cand = ((u | bit) - HALF).astype(int16)
            cnt = count16(ref, lambda blk: blk >= cand)
            return jnp.where(cnt >= target, u | bit, u)
        return lax.fori_loop(0, 16, pass_body, jnp.zeros((PACKED, TQ_B), int32)) - HALF

    tau_hi = search16(hi_ref, topk)
    tau_hi16 = tau_hi.astype(int16)
    above = count16(hi_ref, lambda blk: blk > tau_hi16)

    def bucket_tile(kt):
        lo_ref[kt] = jnp.where(rows3h(hi_ref[kt]) == tau_hi16[None], rows3h(lo_ref[kt]),
                               jnp.full((), -HALF, int16)).reshape(TK_B, TQ_B)

    for_tiles(nkt, bucket_tile)
    tau_lo = search16(lo_ref, topk - above)
    tau_lo16 = tau_lo.astype(int16)
    above = above + count16(lo_ref, lambda blk: blk > tau_lo16)
    tau = ((tau_hi << 16) | (tau_lo + HALF))[:SUBLANES][None]

    need = (topk - above).astype(F32)[:SUBLANES][None]
    ones_l = jnp.ones((2 * SUBLANES, TK_B), BF16)

    def mask_tile(kt, run):
        blk = rows3(keys_ref[kt])
        eq = blk == tau
        eqf = jnp.where(eq, 1.0, 0.0).reshape(TK_B, TQ_B).astype(BF16)
        rank = rows3(jnp.dot(tri_ref[...], eqf, preferred_element_type=F32)) + run[None]
        sel = (blk > tau) | (eq & (rank < need))
        mb_ref[kt] = jnp.where(sel, 0.0, NEG).reshape(TK_B, TQ_B)
        return run + jnp.dot(ones_l, eqf, preferred_element_type=F32)[:SUBLANES]

    run = lax.fori_loop(0, nkt // 2, lambda j, run: mask_tile(2 * j + 1, mask_tile(2 * j, run)),
                        jnp.zeros((SUBLANES, TQ_B), F32))

    @pl.when(nkt % 2 == 1)
    def _():
        mask_tile(nkt - 1, run)

    mb_ref[i] = jnp.where(adm_diag, mb_ref[i], NEG)

    m_ref[...] = jnp.full(m_ref.shape, NEG, F32)
    acc_ref[...] = jnp.zeros(acc_ref.shape, F32)
    ones_rows = jnp.ones((PACKED, TK_B), BF16)

    buf_a, buf_b = (sa_ref, smaxa_ref), (sb_ref, smaxb_ref)

    def logits_stage(kt, buf, near):
        s_ref, smax_ref = buf
        mb = mb_ref[kt]
        for h in range(B_HEADS):
            s = _nt_dot(k_ref[ktile(kt), hcols(h)], q_ref[:, hcols(h)]) + mb
            if near:
                s = s + tab_ref[jnp.minimum(i - kt, NEAR_B), h]
            s_ref[h] = s
            smax_ref[h] = all_rows(jnp.max(rows3(s), axis=0), jnp.max)

    def update_stage(kt, buf):
        s_ref, smax_ref = buf
        for h in range(B_HEADS):
            m_old = m_ref[h]
            m_new = jnp.maximum(m_old, smax_ref[h])
            alpha = jnp.exp2(m_old - m_new)
            p = jnp.exp2(rows3(s_ref[h]) - m_new[None]).reshape(TK_B, TQ_B).astype(BF16)
            vaug = jnp.concatenate([vt_ref[kt, h], ones_rows], axis=0)
            pv = jnp.dot(vaug, p, preferred_element_type=F32)
            acc_ref[h] = (rows3(acc_ref[h]) * alpha[None] + rows3(pv)).reshape(acc_ref.shape[1:])
            m_ref[h] = m_new

    def pair_body(j, carry, near):
        logits_stage(2 * j + 1, buf_b, near)
        update_stage(2 * j, buf_a)
        logits_stage(2 * j + 2, buf_a, near)
        update_stage(2 * j + 1, buf_b)
        return carry

    near_lo = jnp.maximum(i - (NEAR_B - 1), 0)
    far_pairs = jnp.maximum(near_lo - 1, 0) // 2
    logits_stage(0, buf_a, True)
    lax.fori_loop(0, far_pairs, functools.partial(pair_body, near=False), 0)
    lax.fori_loop(far_pairs, (nkt - 1) // 2, functools.partial(pair_body, near=True), 0)

    @pl.when(nkt % 2 == 0)
    def _():
        logits_stage(nkt - 1, buf_b, True)
        update_stage(nkt - 2, buf_a)
        update_stage(nkt - 1, buf_b)

    @pl.when(nkt % 2 == 1)
    def _():
        update_stage(nkt - 1, buf_a)

    for h in range(B_HEADS):
        num = rows3(acc_ref[h, 0:B_HEAD_DIM, :])
        den = acc_ref[h, B_HEAD_DIM:B_HEAD_DIM + SUBLANES, :]
        out_t = (num / den[None]).reshape(B_HEAD_DIM, TQ_B)
        o_ref[:, h * B_HEAD_DIM:(h + 1) * B_HEAD_DIM] = out_t.T.astype(BF16)


def _dsa(proj, bkt, rel_bias, batch, seq):
    nt = seq // TQ_B
    topk = min(TOPK_MAX, seq // 4)

    def q_spec(width, col):
        return pl.BlockSpec((TQ_B, width), lambda b, i: (b * nt + i, col // width))

    def seq_spec(width, col):
        return pl.BlockSpec((seq, width), lambda b, i: (b, col // width))

    return pl.pallas_call(
        functools.partial(_dsa_kernel, topk=topk, seq=seq),
        grid=(batch, nt),
        in_specs=[
            q_spec(IDX_HEADS * IDX_DIM, COL_IQ),
            q_spec(LANES, COL_IKW),
            seq_spec(LANES, COL_IKW),
            q_spec(B_WIDTH, COL_BQ), seq_spec(B_WIDTH, COL_BK), seq_spec(B_WIDTH, COL_BV),
            pl.BlockSpec((NEAR_B, TK_B, TQ_B), lambda b, i: (0, 0, 0)),
            pl.BlockSpec(memory_space=pltpu.SMEM),
        ],
        out_specs=pl.BlockSpec((TQ_B, B_WIDTH), lambda b, i: (b * nt + i, 0)),
        out_shape=jax.ShapeDtypeStruct((batch * seq, B_WIDTH), BF16),
        scratch_shapes=[
            pltpu.VMEM((nt, TK_B, TQ_B), jnp.int32),
            pltpu.VMEM((nt, TK_B, TQ_B), jnp.int16),
            pltpu.VMEM((nt, TK_B, TQ_B), jnp.int16),
            pltpu.VMEM((nt, TK_B, TQ_B), F32),
            pltpu.VMEM((NEAR_B + 1, B_HEADS, TK_B, TQ_B), F32),
            pltpu.VMEM((TK_B, TK_B), BF16),
            pltpu.VMEM((seq // TK_B, B_HEADS, B_HEAD_DIM, TK_B), BF16),
            pltpu.VMEM((B_HEADS, B_HEAD_DIM + PACKED, TQ_B), F32),
            pltpu.VMEM((B_HEADS, SUBLANES, TQ_B), F32),
            pltpu.VMEM((B_HEADS, TK_B, TQ_B), F32),
            pltpu.VMEM((B_HEADS, SUBLANES, TQ_B), F32),
            pltpu.VMEM((B_HEADS, TK_B, TQ_B), F32),
            pltpu.VMEM((B_HEADS, SUBLANES, TQ_B), F32),
        ],
        compiler_params=pltpu.CompilerParams(
            dimension_semantics=("arbitrary", "arbitrary"), vmem_limit_bytes=VMEM_LIMIT),
        name="dsa",
    )(proj, proj, proj, proj, proj, proj, bkt, rel_bias)


def _memattn_kernel(q_ref, kv_ref, o_ref):
    for h in range(C_HEADS):
        hs = slice(h * C_HEAD_DIM, (h + 1) * C_HEAD_DIM)
        s = _nt_dot(q_ref[:, hs], kv_ref[:, hs])
        m = jnp.max(s, axis=1, keepdims=True)
        p = jnp.exp2(s - m)
        den = jnp.sum(p, axis=1, keepdims=True)
        o = jnp.dot(p.astype(BF16), kv_ref[:, C_WIDTH + h * C_HEAD_DIM:C_WIDTH + (h + 1) * C_HEAD_DIM],
                    preferred_element_type=F32)
        o_ref[:, hs] = (o / den).astype(BF16)


def _memattn(proj, mkv, batch, seq):
    nt = seq // TQ_C
    return pl.pallas_call(
        _memattn_kernel,
        grid=(batch, nt),
        in_specs=[
            pl.BlockSpec((TQ_C, C_WIDTH), lambda b, i: (b * nt + i, COL_CQ // C_WIDTH)),
            pl.BlockSpec((N_MEM, 2 * C_WIDTH), lambda b, i: (b, 0)),
        ],
        out_specs=pl.BlockSpec((TQ_C, C_WIDTH), lambda b, i: (b * nt + i, 0)),
        out_shape=jax.ShapeDtypeStruct((batch * seq, C_WIDTH), BF16),
        compiler_params=pltpu.CompilerParams(
            dimension_semantics=("arbitrary", "arbitrary"), vmem_limit_bytes=VMEM_LIMIT),
        name="memattn",
    )(proj, mkv)


def _outproj_kernel(oa_ref, ob_ref, oc_ref, gate_ref, x_ref, w_ref, g_ref, o_ref):
    gate = gate_ref[...].astype(F32)
    sg = gate / (1.0 + jnp.exp(-gate))
    att = jnp.concatenate([oa_ref[...], ob_ref[...], oc_ref[...]], axis=1).astype(F32)
    y = (att * sg).astype(BF16)
    h = x_ref[...] + jnp.dot(y, w_ref[...], preferred_element_type=F32)
    ms = jnp.mean(h * h, axis=-1, keepdims=True)
    o_ref[...] = h * lax.rsqrt(ms + EPS) * g_ref[...]


def _outproj(oa, ob, oc, proj, x2d, w, g):
    m = x2d.shape[0]
    return pl.pallas_call(
        _outproj_kernel,
        grid=(m // TM_OUT,),
        in_specs=[
            pl.BlockSpec((TM_OUT, A_WIDTH), lambda i: (i, 0)),
            pl.BlockSpec((TM_OUT, B_WIDTH), lambda i: (i, 0)),
            pl.BlockSpec((TM_OUT, C_WIDTH), lambda i: (i, 0)),
            pl.BlockSpec((TM_OUT, MIX_WIDTH), lambda i: (i, COL_GATE // MIX_WIDTH)),
            pl.BlockSpec((TM_OUT, D_MODEL), lambda i: (i, 0)),
            pl.BlockSpec((MIX_WIDTH, D_MODEL), lambda i: (0, 0)),
            pl.BlockSpec((1, D_MODEL), lambda i: (0, 0)),
        ],
        out_specs=pl.BlockSpec((TM_OUT, D_MODEL), lambda i: (i, 0)),
        out_shape=jax.ShapeDtypeStruct((m, D_MODEL), F32),
        compiler_params=pltpu.CompilerParams(
            dimension_semantics=("parallel",), vmem_limit_bytes=VMEM_LIMIT),
        name="outproj",
    )(oa, ob, oc, proj, x2d, w, g)


def _col_scale():
    cs = np.ones((1, PROJ_W), np.float32)
    cs[0, COL_AQ:COL_AQ + A_WIDTH] = A_HEAD_DIM ** -0.5 * LOG2E
    cs[0, COL_BQ:COL_BQ + B_WIDTH] = B_HEAD_DIM ** -0.5 * LOG2E
    cs[0, COL_CQ:COL_CQ + C_WIDTH] = C_HEAD_DIM ** -0.5 * LOG2E
    return jnp.asarray(cs)


def kernel(x, mem, g_norm, w_in, sinks, rel_bias, g_mem, w_mem_kv, w_out, g_final):
    batch, seq, _ = x.shape
    assert w_in.shape[0] == 1, "the out-projection kernel fuses the final norm of a single-layer trunk"
    r = np.arange(TQ_A)[:, None]
    c = np.arange(2 * TQ_A)[None, :]
    bkt_a = jnp.asarray(_t5_bucket_np(c - TQ_A - r))
    kk = np.arange(TK_B)[None, :, None]
    qq = np.arange(TQ_B)[None, None, :]
    d = np.arange(NEAR_B)[:, None, None]
    bkt_b = jnp.asarray(_t5_bucket_np(kk - qq - TK_B * d))

    h = x.reshape(batch * seq, D_MODEL)
    mem2d = mem.reshape(batch * N_MEM, D_MODEL)
    w_t = jnp.transpose(w_in, (2, 0, 1)).reshape(IN_WIDTH * KB_IN, LANES)
    proj = _inproj(h, g_norm[0].reshape(1, D_MODEL), _wprep(w_t), _col_scale())
    mkv = _memkv(mem2d, g_mem[0].reshape(1, D_MODEL), w_mem_kv[0])
    oa = _swa(proj, bkt_a, rel_bias, sinks[0], batch, seq)
    ob = _dsa(proj, bkt_b, rel_bias, batch, seq)
    oc = _memattn(proj, mkv, batch, seq)
    out = _outproj(oa, ob, oc, proj, h, w_out[0].astype(BF16), g_final.reshape(1, D_MODEL))
    return out.reshape(batch, seq, D_MODEL)
```

```python
import functools
import math

import numpy as np
import jax
import jax.numpy as jnp
from jax import lax
from jax.experimental import pallas as pl
from jax.experimental.pallas import tpu as pltpu

D_MODEL = 2048
CHUNK = 64
N_MEM = 256
EPS = 1e-6
A_HEADS = 16
A_KV_HEADS = 2
A_HEAD_DIM = 64
WINDOW_CHUNKS = 2
A_WIDTH = A_HEADS * A_HEAD_DIM
B_HEADS = 4
B_HEAD_DIM = 128
B_WIDTH = B_HEADS * B_HEAD_DIM
IDX_HEADS = 4
IDX_DIM = 64
TOPK_MAX = 256
C_HEADS = 4
C_HEAD_DIM = 128
C_WIDTH = C_HEADS * C_HEAD_DIM
MIX_WIDTH = A_WIDTH + B_WIDTH + C_WIDTH
N_BUCKETS = 32
MAX_DISTANCE = 1024
KV_A = A_KV_HEADS * A_HEAD_DIM
SPLIT_SIZES = (A_WIDTH, KV_A, KV_A, B_WIDTH, B_WIDTH, B_WIDTH,
               IDX_HEADS * IDX_DIM, IDX_DIM, IDX_HEADS, C_WIDTH, MIX_WIDTH)
IN_WIDTH = sum(SPLIT_SIZES)

F32 = jnp.float32
BF16 = jnp.bfloat16
LOG2E = math.log2(math.e)
NEG = -1e30
INT_MIN = -(2 ** 31)
LANES = 128
SUBLANES = 8
PACKED = 16
HALF = 1 << 15
KEY_POS_INF = 0x7F800000
KEY_NEG_INF = -KEY_POS_INF
FINE_PASSES = 17

(SRC_AQ, SRC_AK, SRC_AV, SRC_BQ, SRC_BK, SRC_BV,
 SRC_IQ, SRC_IK, SRC_IW, SRC_CQ, SRC_GATE) = (int(c) for c in np.cumsum((0,) + SPLIT_SIZES)[:-1])
COL_GATE = 0
COL_AQ = COL_GATE + MIX_WIDTH
COL_BQ = COL_AQ + A_WIDTH
COL_BK = COL_BQ + B_WIDTH
COL_BV = COL_BK + B_WIDTH
COL_CQ = COL_BV + B_WIDTH
COL_AKV = COL_CQ + C_WIDTH
COL_IQ = COL_AKV + 2 * KV_A
COL_IKW = COL_IQ + IDX_HEADS * IDX_DIM
PROJ_W = 6144
SEGMENTS = ((COL_GATE, SRC_GATE, MIX_WIDTH), (COL_AQ, SRC_AQ, A_WIDTH), (COL_BQ, SRC_BQ, B_WIDTH),
            (COL_BK, SRC_BK, B_WIDTH), (COL_BV, SRC_BV, B_WIDTH), (COL_CQ, SRC_CQ, C_WIDTH),
            (COL_AKV, SRC_AK, 2 * KV_A), (COL_IQ, SRC_IQ, IDX_HEADS * IDX_DIM), (COL_IKW, SRC_IK, 256))
KB_IN = D_MODEL // LANES

TC_PREP = 256
TM_IN, TN_IN = 1024, 1024
TQ_A = 128
TQ_B = 256
TK_B = 256
NEAR_B = 4
TQ_C = 512
TM_OUT = 512
VMEM_LIMIT = 56 * 1024 * 1024


def _t5_bucket_np(rel):
    nb = N_BUCKETS // 2
    max_exact = nb // 2
    side = np.where(rel > 0, nb, 0)
    n = np.abs(rel)
    nf = np.maximum(n, max_exact).astype(np.float32)
    large = max_exact + (np.log(nf / max_exact) / math.log(MAX_DISTANCE / max_exact)
                         * (nb - max_exact)).astype(np.int32)
    large = np.minimum(large, nb - 1)
    return (side + np.where(n < max_exact, n, large)).astype(np.int32)


def _nt_dot(a, b):
    return lax.dot_general(a, b, (((1,), (1,)), ((), ())), preferred_element_type=F32)


def _bias_table(bucket, rb_ref, col, sub_row=None):
    acc = jnp.zeros(bucket.shape, F32)
    for b in range(N_BUCKETS):
        val = rb_ref[b, col]
        if sub_row is not None:
            val = val - rb_ref[sub_row, col]
        acc = jnp.where(bucket == b, val * LOG2E, acc)
    return acc


def _wprep_kernel(src_ref, w_ref, o_ref):
    t = pl.program_id(0)

    @pl.when(src_ref[t] >= 0)
    def _():
        for kb in range(KB_IN):
            o_ref[:, kb * LANES:(kb + 1) * LANES] = w_ref[pl.ds(kb, TC_PREP, stride=KB_IN), :].astype(BF16)

    @pl.when(src_ref[t] < 0)
    def _():
        o_ref[...] = jnp.zeros(o_ref.shape, BF16)


def _wprep(w_t):
    src = np.full((PROJ_W // TC_PREP,), -1, np.int32)
    for dst_col, src_col, width in SEGMENTS:
        for n in range(width // TC_PREP):
            src[dst_col // TC_PREP + n] = src_col + n * TC_PREP
    return pl.pallas_call(
        _wprep_kernel,
        grid_spec=pltpu.PrefetchScalarGridSpec(
            num_scalar_prefetch=1,
            grid=(PROJ_W // TC_PREP,),
            in_specs=[pl.BlockSpec((pl.Element(TC_PREP * KB_IN), pl.Element(LANES)),
                                   lambda t, src: (jnp.maximum(src[t], 0) * KB_IN, 0))],
            out_specs=pl.BlockSpec((TC_PREP, D_MODEL), lambda t, src: (t, 0)),
        ),
        out_shape=jax.ShapeDtypeStruct((PROJ_W, D_MODEL), BF16),
        compiler_params=pltpu.CompilerParams(
            dimension_semantics=("arbitrary",), vmem_limit_bytes=VMEM_LIMIT),
        name="wprep",
    )(jnp.asarray(src), w_t)


def _inproj_kernel(x_ref, g_ref, wt_ref, cs_ref, o_ref, hn_ref):
    @pl.when(pl.program_id(1) == 0)
    def _():
        x = x_ref[...]
        ms = jnp.mean(x * x, axis=-1, keepdims=True)
        hn_ref[...] = (x * lax.rsqrt(ms + EPS) * g_ref[...]).astype(BF16)

    acc = _nt_dot(hn_ref[...], wt_ref[...])
    o_ref[...] = (acc * cs_ref[...]).astype(BF16)


def _inproj(x2d, g, wt, cs):
    m = x2d.shape[0]
    return pl.pallas_call(
        _inproj_kernel,
        grid=(m // TM_IN, PROJ_W // TN_IN),
        in_specs=[
            pl.BlockSpec((TM_IN, D_MODEL), lambda i, j: (i, 0)),
            pl.BlockSpec((1, D_MODEL), lambda i, j: (0, 0)),
            pl.BlockSpec((TN_IN, D_MODEL), lambda i, j: (j, 0)),
            pl.BlockSpec((1, TN_IN), lambda i, j: (0, j)),
        ],
        out_specs=pl.BlockSpec((TM_IN, TN_IN), lambda i, j: (i, j)),
        out_shape=jax.ShapeDtypeStruct((m, PROJ_W), BF16),
        scratch_shapes=[pltpu.VMEM((TM_IN, D_MODEL), BF16)],
        compiler_params=pltpu.CompilerParams(
            dimension_semantics=("parallel", "arbitrary"), vmem_limit_bytes=VMEM_LIMIT),
        name="inproj",
    )(x2d, g, wt, cs)


def _memkv_kernel(m_ref, g_ref, w_ref, o_ref):
    x = m_ref[...]
    ms = jnp.mean(x * x, axis=-1, keepdims=True)
    hn = (x * lax.rsqrt(ms + EPS) * g_ref[...]).astype(BF16)
    o_ref[...] = jnp.dot(hn, w_ref[...].astype(BF16), preferred_element_type=F32).astype(BF16)


def _memkv(mem2d, g, w):
    m = mem2d.shape[0]
    return pl.pallas_call(
        _memkv_kernel,
        grid=(m // N_MEM,),
        in_specs=[
            pl.BlockSpec((N_MEM, D_MODEL), lambda i: (i, 0)),
            pl.BlockSpec((1, D_MODEL), lambda i: (0, 0)),
            pl.BlockSpec((D_MODEL, 2 * C_WIDTH), lambda i: (0, 0)),
        ],
        out_specs=pl.BlockSpec((N_MEM, 2 * C_WIDTH), lambda i: (i, 0)),
        out_shape=jax.ShapeDtypeStruct((m, 2 * C_WIDTH), BF16),
        compiler_params=pltpu.CompilerParams(
            dimension_semantics=("arbitrary",), vmem_limit_bytes=VMEM_LIMIT),
        name="memkv",
    )(mem2d, g, w)


def _swa_kernel(q_ref, kvp_ref, kvc_ref, bkt_ref, rb_ref, sink_ref, o_ref, tab_ref):
    i = pl.program_id(1)

    @pl.when((pl.program_id(0) == 0) & (i == 0))
    def _():
        bkt = bkt_ref[...]
        row = lax.broadcasted_iota(jnp.int32, bkt.shape, 0) // CHUNK
        col = lax.broadcasted_iota(jnp.int32, bkt.shape, 1) // CHUNK
        allowed = (col >= row) & (col <= row + WINDOW_CHUNKS)
        for h in range(A_HEADS):
            tab_ref[h] = jnp.where(allowed, _bias_table(bkt, rb_ref, h), NEG)

    col = lax.broadcasted_iota(jnp.int32, (TQ_A, 2 * TQ_A), 1)
    pen = jnp.where(col < TQ_A, jnp.where(i == 0, NEG, 0.0), 0.0)

    outs = []
    for h in range(A_HEADS):
        g = h // (A_HEADS // A_KV_HEADS)
        ks = slice(g * A_HEAD_DIM, (g + 1) * A_HEAD_DIM)
        vs = slice(KV_A + g * A_HEAD_DIM, KV_A + (g + 1) * A_HEAD_DIM)
        qh = q_ref[:, h * A_HEAD_DIM:(h + 1) * A_HEAD_DIM]
        kg = jnp.concatenate([kvp_ref[:, ks], kvc_ref[:, ks]], axis=0)
        vg = jnp.concatenate([kvp_ref[:, vs], kvc_ref[:, vs]], axis=0)
        s = _nt_dot(qh, kg) + tab_ref[h] + pen
        sk = sink_ref[h] * LOG2E
        m = jnp.maximum(jnp.max(s, axis=1, keepdims=True), sk)
        p = jnp.exp2(s - m)
        den = jnp.sum(p, axis=1, keepdims=True) + jnp.exp2(sk - m)
        o = jnp.dot(p.astype(BF16), vg, preferred_element_type=F32)
        outs.append(o / den)
    o_ref[...] = jnp.concatenate(outs, axis=1).astype(BF16)


def _swa(proj, bkt, rel_bias, sinks, batch, seq):
    nt = seq // TQ_A
    kv_blk = COL_AKV // (2 * KV_A)
    return pl.pallas_call(
        _swa_kernel,
        grid=(batch, nt),
        in_specs=[
            pl.BlockSpec((TQ_A, A_WIDTH), lambda b, i: (b * nt + i, COL_AQ // A_WIDTH)),
            pl.BlockSpec((TQ_A, 2 * KV_A), lambda b, i: (b * nt + jnp.maximum(i - 1, 0), kv_blk)),
            pl.BlockSpec((TQ_A, 2 * KV_A), lambda b, i: (b * nt + i, kv_blk)),
            pl.BlockSpec((TQ_A, 2 * TQ_A), lambda b, i: (0, 0)),
            pl.BlockSpec(memory_space=pltpu.SMEM),
            pl.BlockSpec(memory_space=pltpu.SMEM),
        ],
        out_specs=pl.BlockSpec((TQ_A, A_WIDTH), lambda b, i: (b * nt + i, 0)),
        out_shape=jax.ShapeDtypeStruct((batch * seq, A_WIDTH), BF16),
        scratch_shapes=[pltpu.VMEM((A_HEADS, TQ_A, 2 * TQ_A), F32)],
        compiler_params=pltpu.CompilerParams(
            dimension_semantics=("arbitrary", "arbitrary"), vmem_limit_bytes=VMEM_LIMIT),
        name="swa",
    )(proj, proj, proj, bkt, rel_bias, sinks)


def _dsa_kernel(iq_ref, iwq_ref, ik_ref, q_ref, k_ref, v_ref, bkt_ref, rb_ref, o_ref,
                sc_ref, scb_ref, mb_ref, tab_ref, tri_ref, vt_ref, acc_ref, m_ref,
                sa_ref, smaxa_ref, sb_ref, smaxb_ref, *, topk, seq):
    i = pl.program_id(1)
    nkt = i + 1
    int32, int16 = jnp.int32, jnp.int16
    grp = TK_B // SUBLANES

    def hcols(h):
        return slice(h * B_HEAD_DIM, (h + 1) * B_HEAD_DIM)

    def ktile(kt):
        return pl.ds(pl.multiple_of(kt * TK_B, TK_B), TK_B)

    def rows3(a):
        return a.reshape(a.shape[0] // SUBLANES, SUBLANES, TQ_B)

    def all_rows(a, op):
        return jnp.broadcast_to(op(a, axis=0, keepdims=True), a.shape)

    @pl.when((pl.program_id(0) == 0) & (i == 0))
    def _():
        far_bucket = N_BUCKETS // 2 - 1
        for d in range(NEAR_B):
            for h in range(B_HEADS):
                tab_ref[d, h] = _bias_table(bkt_ref[d], rb_ref, A_HEADS + h, sub_row=far_bucket)
        tab_ref[NEAR_B] = jnp.zeros(tab_ref.shape[1:], F32)
        r = lax.broadcasted_iota(int32, (TK_B, TK_B), 0)
        c = lax.broadcasted_iota(int32, (TK_B, TK_B), 1)
        tri_ref[...] = jnp.where(c < r, 1.0, 0.0).astype(BF16)

    @pl.when(i == 0)
    def _():
        r = lax.broadcasted_iota(int32, (B_HEAD_DIM, B_HEAD_DIM), 0)
        c = lax.broadcasted_iota(int32, (B_HEAD_DIM, B_HEAD_DIM), 1)
        eye = jnp.where(r == c, 1.0, 0.0).astype(BF16)

        def body(kt, carry):
            for h in range(B_HEADS):
                vt_ref[kt, h] = _nt_dot(eye, v_ref[ktile(kt), hcols(h)]).astype(BF16)
            return carry

        lax.fori_loop(0, seq // TK_B, body, 0)

    key_chunk = lax.broadcasted_iota(int32, (TK_B, TQ_B), 0) // CHUNK
    qry_chunk = lax.broadcasted_iota(int32, (TK_B, TQ_B), 1) // CHUNK
    adm_diag = key_chunk <= qry_chunk

    r = lax.broadcasted_iota(int32, (IDX_HEADS * SUBLANES, LANES), 0)
    c = lax.broadcasted_iota(int32, (IDX_HEADS * SUBLANES, LANES), 1)
    pick_w = jnp.where(c == IDX_DIM + r // SUBLANES, 1.0, 0.0).astype(BF16)
    w_all = _nt_dot(pick_w, iwq_ref[...]) * (IDX_HEADS ** -0.5 * IDX_DIM ** -0.5)

    def for_tiles(n, body):
        def pair(j, carry):
            body(2 * j)
            body(2 * j + 1)
            return carry

        lax.fori_loop(0, n // 2, pair, 0)

        @pl.when(n % 2 == 1)
        def _():
            body(n - 1)

    def score_tile(kt):
        ikt = ik_ref[ktile(kt), 0:IDX_DIM]
        sc = jnp.zeros((grp, SUBLANES, TQ_B), F32)
        for h in range(IDX_HEADS):
            x = _nt_dot(ikt, iq_ref[:, h * IDX_DIM:(h + 1) * IDX_DIM])
            sc = sc + w_all[h * SUBLANES:(h + 1) * SUBLANES][None] * jnp.maximum(rows3(x), 0.0)
        store_score(kt, sc.reshape(TK_B, TQ_B))

    def store_score(kt, sc):
        sc_ref[kt] = sc
        scb_ref[kt] = sc.astype(BF16)

    for_tiles(nkt, score_tile)
    store_score(i, jnp.where(adm_diag, sc_ref[i], -jnp.inf))

    def key_to_f32(key):
        return pltpu.bitcast(jnp.where(key >= 0, key, INT_MIN - key), F32)

    def over_tiles(tile_count, zero):
        def pair(j, acc):
            return acc + (tile_count(2 * j) + tile_count(2 * j + 1))
        acc = lax.fori_loop(0, nkt // 2, pair, zero)
        return lax.cond(nkt % 2 == 1, lambda a: a + tile_count(nkt - 1), lambda a: a, acc)

    def tree_sum(parts):
        while len(parts) > 1:
            parts = [parts[n] + parts[n + 1] for n in range(0, len(parts), 2)]
        return parts[0]

    def count_bf16(cand):
        def tile_count(kt):
            blk = scb_ref[kt].reshape(TK_B // PACKED, PACKED, TQ_B)
            return tree_sum([jnp.where(blk[g] >= cand, jnp.ones((), int16), jnp.zeros((), int16))
                             for g in range(TK_B // PACKED)])
        acc = over_tiles(tile_count, jnp.zeros((PACKED, TQ_B), int16))
        return all_rows(acc.astype(int32), jnp.sum)

    def count_f32(pred):
        def tile_count(kt):
            blk = rows3(sc_ref[kt])
            return tree_sum([jnp.where(pred(blk[g]), 1, 0) for g in range(grp)])
        return all_rows(over_tiles(tile_count, jnp.zeros((SUBLANES, TQ_B), int32)), jnp.sum)

    def coarse_pass(p, u):
        bit = jnp.left_shift(jnp.int32(1), 15 - p)
        cand = key_to_f32(((u | bit) - HALF) << 16).astype(BF16)
        return jnp.where(count_bf16(cand) >= topk, u | bit, u)

    u = lax.fori_loop(0, 16, coarse_pass, jnp.zeros((PACKED, TQ_B), int32))
    coarse_key = ((u - HALF) << 16)[:SUBLANES]

    lo = jnp.maximum(coarse_key - HALF, KEY_NEG_INF)
    hi = jnp.minimum(coarse_key, KEY_POS_INF - 2 * HALF) + 2 * HALF

    def fine_pass(p, lohi):
        lo, hi = lohi
        mid = lo + ((hi - lo) >> 1)
        cand = key_to_f32(mid)
        ok = count_f32(lambda blk: blk >= cand) >= topk
        return jnp.where(ok, mid, lo), jnp.where(ok, hi, mid)

    lo, hi = lax.fori_loop(0, FINE_PASSES, fine_pass, (lo, hi))
    tau2d = key_to_f32(lo)
    tau = tau2d[None]

    need = (topk - count_f32(lambda blk: blk > tau2d)).astype(F32)[None]
    ones_l = jnp.ones((2 * SUBLANES, TK_B), BF16)

    def mask_tile(kt, run):
        blk = rows3(sc_ref[kt])
        eq = blk == tau
        eqf = jnp.where(eq, 1.0, 0.0).reshape(TK_B, TQ_B).astype(BF16)
        rank = rows3(jnp.dot(tri_ref[...], eqf, preferred_element_type=F32)) + run[None]
        sel = (blk > tau) | (eq & (rank < need))
        mb_ref[kt] = jnp.where(sel, 0.0, NEG).reshape(TK_B, TQ_B)
        return run + jnp.dot(ones_l, eqf, preferred_element_type=F32)[:SUBLANES]

    run = lax.fori_loop(0, nkt // 2, lambda j, run: mask_tile(2 * j + 1, mask_tile(2 * j, run)),
                        jnp.zeros((SUBLANES, TQ_B), F32))

    @pl.when(nkt % 2 == 1)
    def _():
        mask_tile(nkt - 1, run)

    mb_ref[i] = jnp.where(adm_diag, mb_ref[i], NEG)

    m_ref[...] = jnp.full(m_ref.shape, NEG, F32)
    acc_ref[...] = jnp.zeros(acc_ref.shape, F32)
    ones_rows = jnp.ones((PACKED, TK_B), BF16)

    buf_a, buf_b = (sa_ref, smaxa_ref), (sb_ref, smaxb_ref)

    def logits_stage(kt, buf, near):
        s_ref, smax_ref = buf
        mb = mb_ref[kt]
        for h in range(B_HEADS):
            s = _nt_dot(k_ref[ktile(kt), hcols(h)], q_ref[:, hcols(h)]) + mb
            if near:
                s = s + tab_ref[jnp.minimum(i - kt, NEAR_B), h]
            s_ref[h] = s
            smax_ref[h] = all_rows(jnp.max(rows3(s), axis=0), jnp.max)

    def update_stage(kt, buf):
        s_ref, smax_ref = buf
        for h in range(B_HEADS):
            m_old = m_ref[h]
            m_new = jnp.maximum(m_old, smax_ref[h])
            alpha = jnp.exp2(m_old - m_new)
            p = jnp.exp2(rows3(s_ref[h]) - m_new[None]).reshape(TK_B, TQ_B).astype(BF16)
            vaug = jnp.concatenate([vt_ref[kt, h], ones_rows], axis=0)
            pv = jnp.dot(vaug, p, preferred_element_type=F32)
            acc_ref[h] = (rows3(acc_ref[h]) * alpha[None] + rows3(pv)).reshape(acc_ref.shape[1:])
            m_ref[h] = m_new

    def pair_body(j, carry, near):
        logits_stage(2 * j + 1, buf_b, near)
        update_stage(2 * j, buf_a)
        logits_stage(2 * j + 2, buf_a, near)
        update_stage(2 * j + 1, buf_b)
        return carry

    near_lo = jnp.maximum(i - (NEAR_B - 1), 0)
    far_pairs = jnp.maximum(near_lo - 1, 0) // 2
    logits_stage(0, buf_a, True)
    lax.fori_loop(0, far_pairs, functools.partial(pair_body, near=False), 0)
    lax.fori_loop(far_pairs, (nkt - 1) // 2, functools.partial(pair_body, near=True), 0)

    @pl.when(nkt % 2 == 0)
    def _():
        logits_stage(nkt - 1, buf_b, True)
        update_stage(nkt - 2, buf_a)
        update_stage(nkt - 1, buf_b)

    @pl.when(nkt % 2 == 1)
    def _():
        update_stage(nkt - 1, buf_a)

    for h in range(B_HEADS):
        num = rows3(acc_ref[h, 0:B_HEAD_DIM, :])
        den = acc_ref[h, B_HEAD_DIM:B_HEAD_DIM + SUBLANES, :]
        out_t = (num / den[None]).reshape(B_HEAD_DIM, TQ_B)
        o_ref[:, h * B_HEAD_DIM:(h + 1) * B_HEAD_DIM] = out_t.T.astype(BF16)


def _dsa(proj, bkt, rel_bias, batch, seq):
    nt = seq // TQ_B
    topk = min(TOPK_MAX, seq // 4)

    def q_spec(width, col):
        return pl.BlockSpec((TQ_B, width), lambda b, i: (b * nt + i, col // width))

    def seq_spec(width, col):
        return pl.BlockSpec((seq, width), lambda b, i: (b, col // width))

    return pl.pallas_call(
        functools.partial(_dsa_kernel, topk=topk, seq=seq),
        grid=(batch, nt),
        in_specs=[
            q_spec(IDX_HEADS * IDX_DIM, COL_IQ),
            q_spec(LANES, COL_IKW),
            seq_spec(LANES, COL_IKW),
            q_spec(B_WIDTH, COL_BQ), seq_spec(B_WIDTH, COL_BK), seq_spec(B_WIDTH, COL_BV),
            pl.BlockSpec((NEAR_B, TK_B, TQ_B), lambda b, i: (0, 0, 0)),
            pl.BlockSpec(memory_space=pltpu.SMEM),
        ],
        out_specs=pl.BlockSpec((TQ_B, B_WIDTH), lambda b, i: (b * nt + i, 0)),
        out_shape=jax.ShapeDtypeStruct((batch * seq, B_WIDTH), BF16),
        scratch_shapes=[
            pltpu.VMEM((nt, TK_B, TQ_B), F32),
            pltpu.VMEM((nt, TK_B, TQ_B), BF16),
            pltpu.VMEM((nt, TK_B, TQ_B), F32),
            pltpu.VMEM((NEAR_B + 1, B_HEADS, TK_B, TQ_B), F32),
            pltpu.VMEM((TK_B, TK_B), BF16),
            pltpu.VMEM((seq // TK_B, B_HEADS, B_HEAD_DIM, TK_B), BF16),
            pltpu.VMEM((B_HEADS, B_HEAD_DIM + PACKED, TQ_B), F32),
            pltpu.VMEM((B_HEADS, SUBLANES, TQ_B), F32),
            pltpu.VMEM((B_HEADS, TK_B, TQ_B), F32),
            pltpu.VMEM((B_HEADS, SUBLANES, TQ_B), F32),
            pltpu.VMEM((B_HEADS, TK_B, TQ_B), F32),
            pltpu.VMEM((B_HEADS, SUBLANES, TQ_B), F32),
        ],
        compiler_params=pltpu.CompilerParams(
            dimension_semantics=("arbitrary", "arbitrary"), vmem_limit_bytes=VMEM_LIMIT),
        name="dsa",
    )(proj, proj, proj, proj, proj, proj, bkt, rel_bias)


def _memattn_kernel(q_ref, kv_ref, o_ref):
    for h in range(C_HEADS):
        hs = slice(h * C_HEAD_DIM, (h + 1) * C_HEAD_DIM)
        s = _nt_dot(q_ref[:, hs], kv_ref[:, hs])
        m = jnp.max(s, axis=1, keepdims=True)
        p = jnp.exp2(s - m)
        den = jnp.sum(p, axis=1, keepdims=True)
        o = jnp.dot(p.astype(BF16), kv_ref[:, C_WIDTH + h * C_HEAD_DIM:C_WIDTH + (h + 1) * C_HEAD_DIM],
                    preferred_element_type=F32)
        o_ref[:, hs] = (o / den).astype(BF16)


def _memattn(proj, mkv, batch, seq):
    nt = seq // TQ_C
    return pl.pallas_call(
        _memattn_kernel,
        grid=(batch, nt),
        in_specs=[
            pl.BlockSpec((TQ_C, C_WIDTH), lambda b, i: (b * nt + i, COL_CQ // C_WIDTH)),
            pl.BlockSpec((N_MEM, 2 * C_WIDTH), lambda b, i: (b, 0)),
        ],
        out_specs=pl.BlockSpec((TQ_C, C_WIDTH), lambda b, i: (b * nt + i, 0)),
        out_shape=jax.ShapeDtypeStruct((batch * seq, C_WIDTH), BF16),
        compiler_params=pltpu.CompilerParams(
            dimension_semantics=("arbitrary", "arbitrary"), vmem_limit_bytes=VMEM_LIMIT),
        name="memattn",
    )(proj, mkv)


def _outproj_kernel(oa_ref, ob_ref, oc_ref, gate_ref, x_ref, w_ref, g_ref, o_ref):
    gate = gate_ref[...].astype(F32)
    sg = gate / (1.0 + jnp.exp(-gate))
    att = jnp.concatenate([oa_ref[...], ob_ref[...], oc_ref[...]], axis=1).astype(F32)
    y = (att * sg).astype(BF16)
    h = x_ref[...] + jnp.dot(y, w_ref[...], preferred_element_type=F32)
    ms = jnp.mean(h * h, axis=-1, keepdims=True)
    o_ref[...] = h * lax.rsqrt(ms + EPS) * g_ref[...]


def _outproj(oa, ob, oc, proj, x2d, w, g):
    m = x2d.shape[0]
    return pl.pallas_call(
        _outproj_kernel,
        grid=(m // TM_OUT,),
        in_specs=[
            pl.BlockSpec((TM_OUT, A_WIDTH), lambda i: (i, 0)),
            pl.BlockSpec((TM_OUT, B_WIDTH), lambda i: (i, 0)),
            pl.BlockSpec((TM_OUT, C_WIDTH), lambda i: (i, 0)),
            pl.BlockSpec((TM_OUT, MIX_WIDTH), lambda i: (i, COL_GATE // MIX_WIDTH)),
            pl.BlockSpec((TM_OUT, D_MODEL), lambda i: (i, 0)),
            pl.BlockSpec((MIX_WIDTH, D_MODEL), lambda i: (0, 0)),
            pl.BlockSpec((1, D_MODEL), lambda i: (0, 0)),
        ],
        out_specs=pl.BlockSpec((TM_OUT, D_MODEL), lambda i: (i, 0)),
        out_shape=jax.ShapeDtypeStruct((m, D_MODEL), F32),
        compiler_params=pltpu.CompilerParams(
            dimension_semantics=("parallel",), vmem_limit_bytes=VMEM_LIMIT),
        name="outproj",
    )(oa, ob, oc, proj, x2d, w, g)


def _col_scale():
    cs = np.ones((1, PROJ_W), np.float32)
    cs[0, COL_AQ:COL_AQ + A_WIDTH] = A_HEAD_DIM ** -0.5 * LOG2E
    cs[0, COL_BQ:COL_BQ + B_WIDTH] = B_HEAD_DIM ** -0.5 * LOG2E
    cs[0, COL_CQ:COL_CQ + C_WIDTH] = C_HEAD_DIM ** -0.5 * LOG2E
    return jnp.asarray(cs)


def kernel(x, mem, g_norm, w_in, sinks, rel_bias, g_mem, w_mem_kv, w_out, g_final):
    batch, seq, _ = x.shape
    assert w_in.shape[0] == 1, "the out-projection kernel fuses the final norm of a single-layer trunk"
    r = np.arange(TQ_A)[:, None]
    c = np.arange(2 * TQ_A)[None, :]
    bkt_a = jnp.asarray(_t5_bucket_np(c - TQ_A - r))
    kk = np.arange(TK_B)[None, :, None]
    qq = np.arange(TQ_B)[None, None, :]
    d = np.arange(NEAR_B)[:, None, None]
    bkt_b = jnp.asarray(_t5_bucket_np(kk - qq - TK_B * d))

    h = x.reshape(batch * seq, D_MODEL)
    mem2d = mem.reshape(batch * N_MEM, D_MODEL)
    w_t = jnp.transpose(w_in, (2, 0, 1)).reshape(IN_WIDTH * KB_IN, LANES)
    proj = _inproj(h, g_norm[0].reshape(1, D_MODEL), _wprep(w_t), _col_scale())
    mkv = _memkv(mem2d, g_mem[0].reshape(1, D_MODEL), w_mem_kv[0])
    oa = _swa(proj, bkt_a, rel_bias, sinks[0], batch, seq)
    ob = _dsa(proj, bkt_b, rel_bias, batch, seq)
    oc = _memattn(proj, mkv, batch, seq)
    out = _outproj(oa, ob, oc, proj, h, w_out[0].astype(BF16), g_final.reshape(1, D_MODEL))
    return out.reshape(batch, seq, D_MODEL)
```

```python
import functools
import math

import numpy as np
import jax
import jax.numpy as jnp
from jax import lax
from jax.experimental import pallas as pl
from jax.experimental.pallas import tpu as pltpu

D_MODEL = 2048
CHUNK = 64
N_MEM = 256
EPS = 1e-6
A_HEADS = 16
A_KV_HEADS = 2
A_HEAD_DIM = 64
WINDOW_CHUNKS = 2
A_WIDTH = A_HEADS * A_HEAD_DIM
B_HEADS = 4
B_HEAD_DIM = 128
B_WIDTH = B_HEADS * B_HEAD_DIM
IDX_HEADS = 4
IDX_DIM = 64
TOPK_MAX = 256
C_HEADS = 4
C_HEAD_DIM = 128
C_WIDTH = C_HEADS * C_HEAD_DIM
MIX_WIDTH = A_WIDTH + B_WIDTH + C_WIDTH
N_BUCKETS = 32
MAX_DISTANCE = 1024
KV_A = A_KV_HEADS * A_HEAD_DIM
SPLIT_SIZES = (A_WIDTH, KV_A, KV_A, B_WIDTH, B_WIDTH, B_WIDTH,
               IDX_HEADS * IDX_DIM, IDX_DIM, IDX_HEADS, C_WIDTH, MIX_WIDTH)
IN_WIDTH = sum(SPLIT_SIZES)

F32 = jnp.float32
BF16 = jnp.bfloat16
LOG2E = math.log2(math.e)
NEG = -1e30
INT_MIN = -(2 ** 31)
LANES = 128
SUBLANES = 8
PACKED = 16
HALF = 1 << 15
KEY_POS_INF = 0x7F800000
KEY_NEG_INF = -KEY_POS_INF
FINE_PASSES = 17

(SRC_AQ, SRC_AK, SRC_AV, SRC_BQ, SRC_BK, SRC_BV,
 SRC_IQ, SRC_IK, SRC_IW, SRC_CQ, SRC_GATE) = (int(c) for c in np.cumsum((0,) + SPLIT_SIZES)[:-1])
COL_GATE = 0
COL_AQ = COL_GATE + MIX_WIDTH
COL_BQ = COL_AQ + A_WIDTH
COL_BK = COL_BQ + B_WIDTH
COL_BV = COL_BK + B_WIDTH
COL_CQ = COL_BV + B_WIDTH
COL_AKV = COL_CQ + C_WIDTH
COL_IQ = COL_AKV + 2 * KV_A
COL_IKW = COL_IQ + IDX_HEADS * IDX_DIM
PROJ_W = 6144
SEGMENTS = ((COL_GATE, SRC_GATE, MIX_WIDTH), (COL_AQ, SRC_AQ, A_WIDTH), (COL_BQ, SRC_BQ, B_WIDTH),
            (COL_BK, SRC_BK, B_WIDTH), (COL_BV, SRC_BV, B_WIDTH), (COL_CQ, SRC_CQ, C_WIDTH),
            (COL_AKV, SRC_AK, 2 * KV_A), (COL_IQ, SRC_IQ, IDX_HEADS * IDX_DIM), (COL_IKW, SRC_IK, 256))
KB_IN = D_MODEL // LANES

TC_PREP = 256
TM_IN, TN_IN = 1024, 1024
TQ_A = 128
RB_A = 512
TQ_B = 256
TK_B = 256
NEAR_B = 4
TQ_C = 512
TM_OUT = 512
VMEM_LIMIT = 56 * 1024 * 1024


def _t5_bucket_np(rel):
    nb = N_BUCKETS // 2
    max_exact = nb // 2
    side = np.where(rel > 0, nb, 0)
    n = np.abs(rel)
    nf = np.maximum(n, max_exact).astype(np.float32)
    large = max_exact + (np.log(nf / max_exact) / math.log(MAX_DISTANCE / max_exact)
                         * (nb - max_exact)).astype(np.int32)
    large = np.minimum(large, nb - 1)
    return (side + np.where(n < max_exact, n, large)).astype(np.int32)


def _nt_dot(a, b):
    return lax.dot_general(a, b, (((1,), (1,)), ((), ())), preferred_element_type=F32)


def _bias_table(bucket, rb_ref, col, sub_row=None):
    acc = jnp.zeros(bucket.shape, F32)
    for b in range(N_BUCKETS):
        val = rb_ref[b, col]
        if sub_row is not None:
            val = val - rb_ref[sub_row, col]
        acc = jnp.where(bucket == b, val * LOG2E, acc)
    return acc


def _wprep_kernel(src_ref, w_ref, o_ref):
    t = pl.program_id(0)

    @pl.when(src_ref[t] >= 0)
    def _():
        for kb in range(KB_IN):
            o_ref[:, kb * LANES:(kb + 1) * LANES] = w_ref[pl.ds(kb, TC_PREP, stride=KB_IN), :].astype(BF16)

    @pl.when(src_ref[t] < 0)
    def _():
        o_ref[...] = jnp.zeros(o_ref.shape, BF16)


def _wprep(w_t):
    src = np.full((PROJ_W // TC_PREP,), -1, np.int32)
    for dst_col, src_col, width in SEGMENTS:
        for n in range(width // TC_PREP):
            src[dst_col // TC_PREP + n] = src_col + n * TC_PREP
    return pl.pallas_call(
        _wprep_kernel,
        grid_spec=pltpu.PrefetchScalarGridSpec(
            num_scalar_prefetch=1,
            grid=(PROJ_W // TC_PREP,),
            in_specs=[pl.BlockSpec((pl.Element(TC_PREP * KB_IN), pl.Element(LANES)),
                                   lambda t, src: (jnp.maximum(src[t], 0) * KB_IN, 0))],
            out_specs=pl.BlockSpec((TC_PREP, D_MODEL), lambda t, src: (t, 0)),
        ),
        out_shape=jax.ShapeDtypeStruct((PROJ_W, D_MODEL), BF16),
        compiler_params=pltpu.CompilerParams(
            dimension_semantics=("arbitrary",), vmem_limit_bytes=VMEM_LIMIT),
        name="wprep",
    )(jnp.asarray(src), w_t)


def _inproj_kernel(x_ref, g_ref, wt_ref, cs_ref, o_ref, hn_ref):
    @pl.when(pl.program_id(1) == 0)
    def _():
        x = x_ref[...]
        ms = jnp.mean(x * x, axis=-1, keepdims=True)
        hn_ref[...] = (x * lax.rsqrt(ms + EPS) * g_ref[...]).astype(BF16)

    acc = _nt_dot(hn_ref[...], wt_ref[...])
    o_ref[...] = (acc * cs_ref[...]).astype(BF16)


def _inproj(x2d, g, wt, cs):
    m = x2d.shape[0]
    return pl.pallas_call(
        _inproj_kernel,
        grid=(m // TM_IN, PROJ_W // TN_IN),
        in_specs=[
            pl.BlockSpec((TM_IN, D_MODEL), lambda i, j: (i, 0)),
            pl.BlockSpec((1, D_MODEL), lambda i, j: (0, 0)),
            pl.BlockSpec((TN_IN, D_MODEL), lambda i, j: (j, 0)),
            pl.BlockSpec((1, TN_IN), lambda i, j: (0, j)),
        ],
        out_specs=pl.BlockSpec((TM_IN, TN_IN), lambda i, j: (i, j)),
        out_shape=jax.ShapeDtypeStruct((m, PROJ_W), BF16),
        scratch_shapes=[pltpu.VMEM((TM_IN, D_MODEL), BF16)],
        compiler_params=pltpu.CompilerParams(
            dimension_semantics=("parallel", "arbitrary"), vmem_limit_bytes=VMEM_LIMIT),
        name="inproj",
    )(x2d, g, wt, cs)


def _memkv_kernel(m_ref, g_ref, w_ref, o_ref):
    x = m_ref[...]
    ms = jnp.mean(x * x, axis=-1, keepdims=True)
    hn = (x * lax.rsqrt(ms + EPS) * g_ref[...]).astype(BF16)
    o_ref[...] = jnp.dot(hn, w_ref[...].astype(BF16), preferred_element_type=F32).astype(BF16)


def _memkv(mem2d, g, w):
    m = mem2d.shape[0]
    return pl.pallas_call(
        _memkv_kernel,
        grid=(m // N_MEM,),
        in_specs=[
            pl.BlockSpec((N_MEM, D_MODEL), lambda i: (i, 0)),
            pl.BlockSpec((1, D_MODEL), lambda i: (0, 0)),
            pl.BlockSpec((D_MODEL, 2 * C_WIDTH), lambda i: (0, 0)),
        ],
        out_specs=pl.BlockSpec((N_MEM, 2 * C_WIDTH), lambda i: (i, 0)),
        out_shape=jax.ShapeDtypeStruct((m, 2 * C_WIDTH), BF16),
        compiler_params=pltpu.CompilerParams(
            dimension_semantics=("arbitrary",), vmem_limit_bytes=VMEM_LIMIT),
        name="memkv",
    )(mem2d, g, w)


GRP_A = A_HEADS // A_KV_HEADS
PAIRS_A = GRP_A // 2
COLS_A = GRP_A * TQ_A
KEYS_A = 2 * TQ_A
CHUNK_PAIRS_A = 4


def _swa_kernel(q_ref, kvp_ref, kvc_ref, bkt_ref, rb_ref, sink_ref, o_ref,
                tab_ref, sinkv_ref, kp_ref, vt_ref, s_ref, m_ref, ot_ref):
    i = pl.program_id(1)

    @pl.when((pl.program_id(0) == 0) & (i == 0))
    def _():
        bkt = bkt_ref[...]
        kchunk = lax.broadcasted_iota(jnp.int32, bkt.shape, 0) // CHUNK
        qchunk = lax.broadcasted_iota(jnp.int32, bkt.shape, 1) // CHUNK
        allowed = (kchunk >= qchunk) & (kchunk <= qchunk + WINDOW_CHUNKS)
        has_prev = lax.broadcasted_iota(jnp.int32, bkt.shape, 0) >= TQ_A
        for g in range(A_KV_HEADS):
            for col in range(GRP_A):
                h = g * GRP_A + 2 * (col % PAIRS_A) + col // PAIRS_A
                cols = slice(col * TQ_A, (col + 1) * TQ_A)
                t = jnp.where(allowed, _bias_table(bkt, rb_ref, h), NEG)
                tab_ref[0, g, :, cols] = t
                tab_ref[1, g, :, cols] = jnp.where(has_prev, t, NEG)
                sinkv_ref[g, :, cols] = jnp.full((SUBLANES, TQ_A), sink_ref[h] * LOG2E, F32)

    width = CHUNK_PAIRS_A * TQ_A
    tasks = [(t, g, p0, parity) for t in range(RB_A // TQ_A) for g in range(A_KV_HEADS)
             for p0 in range(0, PAIRS_A, CHUNK_PAIRS_A) for parity in range(2)]

    def prepare(t, g):
        slot = (t * A_KV_HEADS + g) % 2
        if t == 0:
            kv_t = jnp.concatenate([kvp_ref[...], kvc_ref[0:TQ_A, :]], axis=0)
        else:
            kv_t = kvc_ref[(t - 1) * TQ_A:(t + 1) * TQ_A, :]
        kg = kv_t[:, g * A_HEAD_DIM:(g + 1) * A_HEAD_DIM]
        zeros = jnp.zeros((KEYS_A, A_HEAD_DIM), BF16)
        kp_ref[slot, 0] = jnp.concatenate([kg, zeros], axis=1)
        kp_ref[slot, 1] = jnp.concatenate([zeros, kg], axis=1)
        r = lax.broadcasted_iota(jnp.int32, (A_HEAD_DIM, KV_A), 0)
        c = lax.broadcasted_iota(jnp.int32, (A_HEAD_DIM, KV_A), 1)
        pick = jnp.where(c == g * A_HEAD_DIM + r, 1.0, 0.0).astype(BF16)
        vt = _nt_dot(pick, kv_t[:, KV_A:]).astype(BF16)
        vt_ref[slot] = jnp.concatenate([vt, jnp.ones((PACKED, KEYS_A), BF16)], axis=0)

    def chunk_cols(p0, parity):
        lo = (parity * PAIRS_A + p0) * TQ_A
        return slice(lo, lo + width)

    def logits_stage(n):
        t, g, p0, parity = tasks[n]
        if (p0, parity) == (0, 0):
            prepare(t, g)
        slot, buf, cols = (t * A_KV_HEADS + g) % 2, n % 2, chunk_cols(p0, parity)
        q_pairs = jnp.concatenate(
            [q_ref[t * TQ_A:(t + 1) * TQ_A, (g * PAIRS_A + p) * LANES:(g * PAIRS_A + p + 1) * LANES]
             for p in range(p0, p0 + CHUNK_PAIRS_A)], axis=0)
        variant = jnp.where(i == 0, 1, 0) if t == 0 else 0
        s = _nt_dot(kp_ref[slot, parity], q_pairs) + tab_ref[variant, g, :, cols]
        s_ref[buf] = s
        m = jnp.max(s.reshape(KEYS_A // SUBLANES, SUBLANES, width), axis=0)
        m_ref[buf] = jnp.maximum(jnp.broadcast_to(jnp.max(m, axis=0, keepdims=True), m.shape),
                                 sinkv_ref[g, :, cols])

    def update_stage(n):
        t, g, p0, parity = tasks[n]
        slot, buf, cols = (t * A_KV_HEADS + g) % 2, n % 2, chunk_cols(p0, parity)
        m = m_ref[buf]
        s = s_ref[buf].reshape(KEYS_A // SUBLANES, SUBLANES, width)
        p = jnp.exp2(s - m[None]).reshape(KEYS_A, width).astype(BF16)
        pv = jnp.dot(vt_ref[slot], p, preferred_element_type=F32)
        den = pv[A_HEAD_DIM:A_HEAD_DIM + SUBLANES] + jnp.exp2(sinkv_ref[g, :, cols] - m)
        out_t = pv[:A_HEAD_DIM].reshape(A_HEAD_DIM // SUBLANES, SUBLANES, width) / den[None]
        ot_ref[parity] = out_t.reshape(A_HEAD_DIM, width)
        if parity == 1:
            for k in range(CHUNK_PAIRS_A):
                blk = jnp.concatenate([ot_ref[0, :, k * TQ_A:(k + 1) * TQ_A],
                                       ot_ref[1, :, k * TQ_A:(k + 1) * TQ_A]], axis=0)
                lanes = slice((g * PAIRS_A + p0 + k) * LANES, (g * PAIRS_A + p0 + k + 1) * LANES)
                o_ref[t * TQ_A:(t + 1) * TQ_A, lanes] = blk.T.astype(BF16)

    logits_stage(0)
    for n in range(len(tasks)):
        if n + 1 < len(tasks):
            logits_stage(n + 1)
        update_stage(n)


def _swa(proj, bkt, rel_bias, sinks, batch, seq):
    nt = seq // RB_A
    sub = RB_A // TQ_A
    kv_blk = COL_AKV // (2 * KV_A)
    return pl.pallas_call(
        _swa_kernel,
        grid=(batch, nt),
        in_specs=[
            pl.BlockSpec((RB_A, A_WIDTH), lambda b, i: (b * nt + i, COL_AQ // A_WIDTH)),
            pl.BlockSpec((TQ_A, 2 * KV_A), lambda b, i: (jnp.maximum((b * nt + i) * sub - 1, 0), kv_blk)),
            pl.BlockSpec((RB_A, 2 * KV_A), lambda b, i: (b * nt + i, kv_blk)),
            pl.BlockSpec((KEYS_A, TQ_A), lambda b, i: (0, 0)),
            pl.BlockSpec(memory_space=pltpu.SMEM),
            pl.BlockSpec(memory_space=pltpu.SMEM),
        ],
        out_specs=pl.BlockSpec((RB_A, A_WIDTH), lambda b, i: (b * nt + i, 0)),
        out_shape=jax.ShapeDtypeStruct((batch * seq, A_WIDTH), BF16),
        scratch_shapes=[
            pltpu.VMEM((2, A_KV_HEADS, KEYS_A, COLS_A), F32),
            pltpu.VMEM((A_KV_HEADS, SUBLANES, COLS_A), F32),
            pltpu.VMEM((2, 2, KEYS_A, LANES), BF16),
            pltpu.VMEM((2, A_HEAD_DIM + PACKED, KEYS_A), BF16),
            pltpu.VMEM((2, KEYS_A, CHUNK_PAIRS_A * TQ_A), F32),
            pltpu.VMEM((2, SUBLANES, CHUNK_PAIRS_A * TQ_A), F32),
            pltpu.VMEM((2, A_HEAD_DIM, CHUNK_PAIRS_A * TQ_A), F32),
        ],
        compiler_params=pltpu.CompilerParams(
            dimension_semantics=("arbitrary", "arbitrary"), vmem_limit_bytes=VMEM_LIMIT),
        name="swa",
    )(proj, proj, proj, bkt, rel_bias, sinks)


def _dsa_kernel(iq_ref, iwq_ref, ik_ref, q_ref, k_ref, v_ref, bkt_ref, rb_ref, o_ref,
                sc_ref, scb_ref, mb_ref, tab_ref, tri_ref, vt_ref, acc_ref, m_ref,
                sa_ref, smaxa_ref, sb_ref, smaxb_ref, *, topk, seq):
    i = pl.program_id(1)
    nkt = i + 1
    int32, int16 = jnp.int32, jnp.int16
    grp = TK_B // SUBLANES

    def hcols(h):
        return slice(h * B_HEAD_DIM, (h + 1) * B_HEAD_DIM)

    def ktile(kt):
        return pl.ds(pl.multiple_of(kt * TK_B, TK_B), TK_B)

    def rows3(a):
        return a.reshape(a.shape[0] // SUBLANES, SUBLANES, TQ_B)

    def all_rows(a, op):
        return jnp.broadcast_to(op(a, axis=0, keepdims=True), a.shape)

    @pl.when((pl.program_id(0) == 0) & (i == 0))
    def _():
        far_bucket = N_BUCKETS // 2 - 1
        for d in range(NEAR_B):
            for h in range(B_HEADS):
                tab_ref[d, h] = _bias_table(bkt_ref[d], rb_ref, A_HEADS + h, sub_row=far_bucket)
        tab_ref[NEAR_B] = jnp.zeros(tab_ref.shape[1:], F32)
        r = lax.broadcasted_iota(int32, (TK_B, TK_B), 0)
        c = lax.broadcasted_iota(int32, (TK_B, TK_B), 1)
        tri_ref[...] = jnp.where(c < r, 1.0, 0.0).astype(BF16)

    @pl.when(i == 0)
    def _():
        r = lax.broadcasted_iota(int32, (B_HEAD_DIM, B_HEAD_DIM), 0)
        c = lax.broadcasted_iota(int32, (B_HEAD_DIM, B_HEAD_DIM), 1)
        eye = jnp.where(r == c, 1.0, 0.0).astype(BF16)

        def body(kt, carry):
            for h in range(B_HEADS):
                vt_ref[kt, h] = _nt_dot(eye, v_ref[ktile(kt), hcols(h)]).astype(BF16)
            return carry

        lax.fori_loop(0, seq // TK_B, body, 0)

    key_chunk = lax.broadcasted_iota(int32, (TK_B, TQ_B), 0) // CHUNK
    qry_chunk = lax.broadcasted_iota(int32, (TK_B, TQ_B), 1) // CHUNK
    adm_diag = key_chunk <= qry_chunk

    r = lax.broadcasted_iota(int32, (IDX_HEADS * SUBLANES, LANES), 0)
    c = lax.broadcasted_iota(int32, (IDX_HEADS * SUBLANES, LANES), 1)
    pick_w = jnp.where(c == IDX_DIM + r // SUBLANES, 1.0, 0.0).astype(BF16)
    w_all = _nt_dot(pick_w, iwq_ref[...]) * (IDX_HEADS ** -0.5 * IDX_DIM ** -0.5)

    def for_tiles(n, body):
        def pair(j, carry):
            body(2 * j)
            body(2 * j + 1)
            return carry

        lax.fori_loop(0, n // 2, pair, 0)

        @pl.when(n % 2 == 1)
        def _():
            body(n - 1)

    def score_tile(kt):
        ikt = ik_ref[ktile(kt), 0:IDX_DIM]
        sc = jnp.zeros((grp, SUBLANES, TQ_B), F32)
        for h in range(IDX_HEADS):
            x = _nt_dot(ikt, iq_ref[:, h * IDX_DIM:(h + 1) * IDX_DIM])
            sc = sc + w_all[h * SUBLANES:(h + 1) * SUBLANES][None] * jnp.maximum(rows3(x), 0.0)
        store_score(kt, sc.reshape(TK_B, TQ_B))

    def store_score(kt, sc):
        sc_ref[kt] = sc
        scb_ref[kt] = sc.astype(BF16)

    for_tiles(nkt, score_tile)
    store_score(i, jnp.where(adm_diag, sc_ref[i], -jnp.inf))

    def key_to_f32(key):
        return pltpu.bitcast(jnp.where(key >= 0, key, INT_MIN - key), F32)

    def over_tiles(tile_count, zero):
        def pair(j, acc):
            return acc + (tile_count(2 * j) + tile_count(2 * j + 1))
        acc = lax.fori_loop(0, nkt // 2, pair, zero)
        return lax.cond(nkt % 2 == 1, lambda a: a + tile_count(nkt - 1), lambda a: a, acc)

    def tree_sum(parts):
        while len(parts) > 1:
            parts = [parts[n] + parts[n + 1] for n in range(0, len(parts), 2)]
        return parts[0]

    def count_bf16(cand):
        def tile_count(kt):
            blk = scb_ref[kt].reshape(TK_B // PACKED, PACKED, TQ_B)
            return tree_sum([jnp.where(blk[g] >= cand, jnp.ones((), int16), jnp.zeros((), int16))
                             for g in range(TK_B // PACKED)])
        acc = over_tiles(tile_count, jnp.zeros((PACKED, TQ_B), int16))
        return all_rows(acc.astype(int32), jnp.sum)

    def count_f32(pred):
        def tile_count(kt):
            blk = rows3(sc_ref[kt])
            return tree_sum([jnp.where(pred(blk[g]), 1, 0) for g in range(grp)])
        return all_rows(over_tiles(tile_count, jnp.zeros((SUBLANES, TQ_B), int32)), jnp.sum)

    def coarse_pass(p, u):
        bit = jnp.left_shift(jnp.int32(1), 15 - p)
        cand = key_to_f32(((u | bit) - HALF) << 16).astype(BF16)
        return jnp.where(count_bf16(cand) >= topk, u | bit, u)

    u = lax.fori_loop(0, 16, coarse_pass, jnp.zeros((PACKED, TQ_B), int32))
    coarse_key = ((u - HALF) << 16)[:SUBLANES]

    lo = jnp.maximum(coarse_key - HALF, KEY_NEG_INF)
    hi = jnp.minimum(coarse_key, KEY_POS_INF - 2 * HALF) + 2 * HALF

    def fine_pass(p, lohi):
        lo, hi = lohi
        mid = lo + ((hi - lo) >> 1)
        cand = key_to_f32(mid)
        ok = count_f32(lambda blk: blk >= cand) >= topk
        return jnp.where(ok, mid, lo), jnp.where(ok, hi, mid)

    lo, hi = lax.fori_loop(0, FINE_PASSES, fine_pass, (lo, hi))
    tau2d = key_to_f32(lo)
    tau = tau2d[None]

    need = (topk - count_f32(lambda blk: blk > tau2d)).astype(F32)[None]
    ones_l = jnp.ones((2 * SUBLANES, TK_B), BF16)

    def mask_tile(kt, run):
        blk = rows3(sc_ref[kt])
        eq = blk == tau
        eqf = jnp.where(eq, 1.0, 0.0).reshape(TK_B, TQ_B).astype(BF16)
        rank = rows3(jnp.dot(tri_ref[...], eqf, preferred_element_type=F32)) + run[None]
        sel = (blk > tau) | (eq & (rank < need))
        mb_ref[kt] = jnp.where(sel, 0.0, NEG).reshape(TK_B, TQ_B)
        return run + jnp.dot(ones_l, eqf, preferred_element_type=F32)[:SUBLANES]

    run = lax.fori_loop(0, nkt // 2, lambda j, run: mask_tile(2 * j + 1, mask_tile(2 * j, run)),
                        jnp.zeros((SUBLANES, TQ_B), F32))

    @pl.when(nkt % 2 == 1)
    def _():
        mask_tile(nkt - 1, run)

    mb_ref[i] = jnp.where(adm_diag, mb_ref[i], NEG)

    m_ref[...] = jnp.full(m_ref.shape, NEG, F32)
    acc_ref[...] = jnp.zeros(acc_ref.shape, F32)
    ones_rows = jnp.ones((PACKED, TK_B), BF16)

    buf_a, buf_b = (sa_ref, smaxa_ref), (sb_ref, smaxb_ref)

    def logits_stage(kt, buf, near):
        s_ref, smax_ref = buf
        mb = mb_ref[kt]
        for h in range(B_HEADS):
            s = _nt_dot(k_ref[ktile(kt), hcols(h)], q_ref[:, hcols(h)]) + mb
            if near:
                s = s + tab_ref[jnp.minimum(i - kt, NEAR_B), h]
            s_ref[h] = s
            smax_ref[h] = all_rows(jnp.max(rows3(s), axis=0), jnp.max)

    def update_stage(kt, buf):
        s_ref, smax_ref = buf
        for h in range(B_HEADS):
            m_old = m_ref[h]
            m_new = jnp.maximum(m_old, smax_ref[h])
            alpha = jnp.exp2(m_old - m_new)
            p = jnp.exp2(rows3(s_ref[h]) - m_new[None]).reshape(TK_B, TQ_B).astype(BF16)
            vaug = jnp.concatenate([vt_ref[kt, h], ones_rows], axis=0)
            pv = jnp.dot(vaug, p, preferred_element_type=F32)
            acc_ref[h] = (rows3(acc_ref[h]) * alpha[None] + rows3(pv)).reshape(acc_ref.shape[1:])
            m_ref[h] = m_new

    def pair_body(j, carry, near):
        logits_stage(2 * j + 1, buf_b, near)
        update_stage(2 * j, buf_a)
        logits_stage(2 * j + 2, buf_a, near)
        update_stage(2 * j + 1, buf_b)
        return carry

    near_lo = jnp.maximum(i - (NEAR_B - 1), 0)
    far_pairs = jnp.maximum(near_lo - 1, 0) // 2
    logits_stage(0, buf_a, True)
    lax.fori_loop(0, far_pairs, functools.partial(pair_body, near=False), 0)
    lax.fori_loop(far_pairs, (nkt - 1) // 2, functools.partial(pair_body, near=True), 0)

    @pl.when(nkt % 2 == 0)
    def _():
        logits_stage(nkt - 1, buf_b, True)
        update_stage(nkt - 2, buf_a)
        update_stage(nkt - 1, buf_b)

    @pl.when(nkt % 2 == 1)
    def _():
        update_stage(nkt - 1, buf_a)

    for h in range(B_HEADS):
        num = rows3(acc_ref[h, 0:B_HEAD_DIM, :])
        den = acc_ref[h, B_HEAD_DIM:B_HEAD_DIM + SUBLANES, :]
        out_t = (num / den[None]).reshape(B_HEAD_DIM, TQ_B)
        o_ref[:, h * B_HEAD_DIM:(h + 1) * B_HEAD_DIM] = out_t.T.astype(BF16)


def _dsa(proj, bkt, rel_bias, batch, seq):
    nt = seq // TQ_B
    topk = min(TOPK_MAX, seq // 4)

    def q_spec(width, col):
        return pl.BlockSpec((TQ_B, width), lambda b, i: (b * nt + i, col // width))

    def seq_spec(width, col):
        return pl.BlockSpec((seq, width), lambda b, i: (b, col // width))

    return pl.pallas_call(
        functools.partial(_dsa_kernel, topk=topk, seq=seq),
        grid=(batch, nt),
        in_specs=[
            q_spec(IDX_HEADS * IDX_DIM, COL_IQ),
            q_spec(LANES, COL_IKW),
            seq_spec(LANES, COL_IKW),
            q_spec(B_WIDTH, COL_BQ), seq_spec(B_WIDTH, COL_BK), seq_spec(B_WIDTH, COL_BV),
            pl.BlockSpec((NEAR_B, TK_B, TQ_B), lambda b, i: (0, 0, 0)),
            pl.BlockSpec(memory_space=pltpu.SMEM),
        ],
        out_specs=pl.BlockSpec((TQ_B, B_WIDTH), lambda b, i: (b * nt + i, 0)),
        out_shape=jax.ShapeDtypeStruct((batch * seq, B_WIDTH), BF16),
        scratch_shapes=[
            pltpu.VMEM((nt, TK_B, TQ_B), F32),
            pltpu.VMEM((nt, TK_B, TQ_B), BF16),
            pltpu.VMEM((nt, TK_B, TQ_B), F32),
            pltpu.VMEM((NEAR_B + 1, B_HEADS, TK_B, TQ_B), F32),
            pltpu.VMEM((TK_B, TK_B), BF16),
            pltpu.VMEM((seq // TK_B, B_HEADS, B_HEAD_DIM, TK_B), BF16),
            pltpu.VMEM((B_HEADS, B_HEAD_DIM + PACKED, TQ_B), F32),
            pltpu.VMEM((B_HEADS, SUBLANES, TQ_B), F32),
            pltpu.VMEM((B_HEADS, TK_B, TQ_B), F32),
            pltpu.VMEM((B_HEADS, SUBLANES, TQ_B), F32),
            pltpu.VMEM((B_HEADS, TK_B, TQ_B), F32),
            pltpu.VMEM((B_HEADS, SUBLANES, TQ_B), F32),
        ],
        compiler_params=pltpu.CompilerParams(
            dimension_semantics=("arbitrary", "arbitrary"), vmem_limit_bytes=VMEM_LIMIT),
        name="dsa",
    )(proj, proj, proj, proj, proj, proj, bkt, rel_bias)


def _memattn_kernel(q_ref, kv_ref, o_ref):
    for h in range(C_HEADS):
        hs = slice(h * C_HEAD_DIM, (h + 1) * C_HEAD_DIM)
        s = _nt_dot(q_ref[:, hs], kv_ref[:, hs])
        m = jnp.max(s, axis=1, keepdims=True)
        p = jnp.exp2(s - m)
        den = jnp.sum(p, axis=1, keepdims=True)
        o = jnp.dot(p.astype(BF16), kv_ref[:, C_WIDTH + h * C_HEAD_DIM:C_WIDTH + (h + 1) * C_HEAD_DIM],
                    preferred_element_type=F32)
        o_ref[:, hs] = (o / den).astype(BF16)


def _memattn(proj, mkv, batch, seq):
    nt = seq // TQ_C
    return pl.pallas_call(
        _memattn_kernel,
        grid=(batch, nt),
        in_specs=[
            pl.BlockSpec((TQ_C, C_WIDTH), lambda b, i: (b * nt + i, COL_CQ // C_WIDTH)),
            pl.BlockSpec((N_MEM, 2 * C_WIDTH), lambda b, i: (b, 0)),
        ],
        out_specs=pl.BlockSpec((TQ_C, C_WIDTH), lambda b, i: (b * nt + i, 0)),
        out_shape=jax.ShapeDtypeStruct((batch * seq, C_WIDTH), BF16),
        compiler_params=pltpu.CompilerParams(
            dimension_semantics=("arbitrary", "arbitrary"), vmem_limit_bytes=VMEM_LIMIT),
        name="memattn",
    )(proj, mkv)


def _outproj_kernel(oa_ref, ob_ref, oc_ref, gate_ref, x_ref, w_ref, g_ref, o_ref):
    gate = gate_ref[...].astype(F32)
    sg = gate / (1.0 + jnp.exp(-gate))
    att = jnp.concatenate([oa_ref[...], ob_ref[...], oc_ref[...]], axis=1).astype(F32)
    y = (att * sg).astype(BF16)
    h = x_ref[...] + jnp.dot(y, w_ref[...], preferred_element_type=F32)
    ms = jnp.mean(h * h, axis=-1, keepdims=True)
    o_ref[...] = h * lax.rsqrt(ms + EPS) * g_ref[...]


def _outproj(oa, ob, oc, proj, x2d, w, g):
    m = x2d.shape[0]
    return pl.pallas_call(
        _outproj_kernel,
        grid=(m // TM_OUT,),
        in_specs=[
            pl.BlockSpec((TM_OUT, A_WIDTH), lambda i: (i, 0)),
            pl.BlockSpec((TM_OUT, B_WIDTH), lambda i: (i, 0)),
            pl.BlockSpec((TM_OUT, C_WIDTH), lambda i: (i, 0)),
            pl.BlockSpec((TM_OUT, MIX_WIDTH), lambda i: (i, COL_GATE // MIX_WIDTH)),
            pl.BlockSpec((TM_OUT, D_MODEL), lambda i: (i, 0)),
            pl.BlockSpec((MIX_WIDTH, D_MODEL), lambda i: (0, 0)),
            pl.BlockSpec((1, D_MODEL), lambda i: (0, 0)),
        ],
        out_specs=pl.BlockSpec((TM_OUT, D_MODEL), lambda i: (i, 0)),
        out_shape=jax.ShapeDtypeStruct((m, D_MODEL), F32),
        compiler_params=pltpu.CompilerParams(
            dimension_semantics=("parallel",), vmem_limit_bytes=VMEM_LIMIT),
        name="outproj",
    )(oa, ob, oc, proj, x2d, w, g)


def _col_scale():
    cs = np.ones((1, PROJ_W), np.float32)
    cs[0, COL_AQ:COL_AQ + A_WIDTH] = A_HEAD_DIM ** -0.5 * LOG2E
    cs[0, COL_BQ:COL_BQ + B_WIDTH] = B_HEAD_DIM ** -0.5 * LOG2E
    cs[0, COL_CQ:COL_CQ + C_WIDTH] = C_HEAD_DIM ** -0.5 * LOG2E
    return jnp.asarray(cs)


def kernel(x, mem, g_norm, w_in, sinks, rel_bias, g_mem, w_mem_kv, w_out, g_final):
    batch, seq, _ = x.shape
    assert w_in.shape[0] == 1, "the out-projection kernel fuses the final norm of a single-layer trunk"
    kk = np.arange(2 * TQ_A)[:, None]
    qq = np.arange(TQ_A)[None, :]
    bkt_a = jnp.asarray(_t5_bucket_np(kk - TQ_A - qq))
    kk = np.arange(TK_B)[None, :, None]
    qq = np.arange(TQ_B)[None, None, :]
    d = np.arange(NEAR_B)[:, None, None]
    bkt_b = jnp.asarray(_t5_bucket_np(kk - qq - TK_B * d))

    h = x.reshape(batch * seq, D_MODEL)
    mem2d = mem.reshape(batch * N_MEM, D_MODEL)
    w_t = jnp.transpose(w_in, (2, 0, 1)).reshape(IN_WIDTH * KB_IN, LANES)
    proj = _inproj(h, g_norm[0].reshape(1, D_MODEL), _wprep(w_t), _col_scale())
    mkv = _memkv(mem2d, g_mem[0].reshape(1, D_MODEL), w_mem_kv[0])
    oa = _swa(proj, bkt_a, rel_bias, sinks[0], batch, seq)
    ob = _dsa(proj, bkt_b, rel_bias, batch, seq)
    oc = _memattn(proj, mkv, batch, seq)
    out = _outproj(oa, ob, oc, proj, h, w_out[0].astype(BF16), g_final.reshape(1, D_MODEL))
    return out.reshape(batch, seq, D_MODEL)
```

```python
import functools
import math

import numpy as np
import jax
import jax.numpy as jnp
from jax import lax
from jax.experimental import pallas as pl
from jax.experimental.pallas import tpu as pltpu

D_MODEL = 2048
CHUNK = 64
N_MEM = 256
EPS = 1e-6
A_HEADS = 16
A_KV_HEADS = 2
A_HEAD_DIM = 64
WINDOW_CHUNKS = 2
A_WIDTH = A_HEADS * A_HEAD_DIM
B_HEADS = 4
B_HEAD_DIM = 128
B_WIDTH = B_HEADS * B_HEAD_DIM
IDX_HEADS = 4
IDX_DIM = 64
TOPK_MAX = 256
C_HEADS = 4
C_HEAD_DIM = 128
C_WIDTH = C_HEADS * C_HEAD_DIM
MIX_WIDTH = A_WIDTH + B_WIDTH + C_WIDTH
N_BUCKETS = 32
MAX_DISTANCE = 1024
KV_A = A_KV_HEADS * A_HEAD_DIM
SPLIT_SIZES = (A_WIDTH, KV_A, KV_A, B_WIDTH, B_WIDTH, B_WIDTH,
               IDX_HEADS * IDX_DIM, IDX_DIM, IDX_HEADS, C_WIDTH, MIX_WIDTH)
IN_WIDTH = sum(SPLIT_SIZES)

F32 = jnp.float32
BF16 = jnp.bfloat16
LOG2E = math.log2(math.e)
NEG = -1e30
INT_MIN = -(2 ** 31)
LANES = 128
SUBLANES = 8
PACKED = 16
HALF = 1 << 15
KEY_POS_INF = 0x7F800000
KEY_NEG_INF = -KEY_POS_INF
FINE_PASSES = 17

(SRC_AQ, SRC_AK, SRC_AV, SRC_BQ, SRC_BK, SRC_BV,
 SRC_IQ, SRC_IK, SRC_IW, SRC_CQ, SRC_GATE) = (int(c) for c in np.cumsum((0,) + SPLIT_SIZES)[:-1])
COL_GATE = 0
COL_AQ = COL_GATE + MIX_WIDTH
COL_BQ = COL_AQ + A_WIDTH
COL_BK = COL_BQ + B_WIDTH
COL_BV = COL_BK + B_WIDTH
COL_CQ = COL_BV + B_WIDTH
COL_AKV = COL_CQ + C_WIDTH
COL_IQ = COL_AKV + 2 * KV_A
COL_IKW = COL_IQ + IDX_HEADS * IDX_DIM
PROJ_W = 6144
SEGMENTS = ((COL_GATE, SRC_GATE, MIX_WIDTH), (COL_AQ, SRC_AQ, A_WIDTH), (COL_BQ, SRC_BQ, B_WIDTH),
            (COL_BK, SRC_BK, B_WIDTH), (COL_BV, SRC_BV, B_WIDTH), (COL_CQ, SRC_CQ, C_WIDTH),
            (COL_AKV, SRC_AK, 2 * KV_A), (COL_IQ, SRC_IQ, IDX_HEADS * IDX_DIM), (COL_IKW, SRC_IK, 256))
KB_IN = D_MODEL // LANES

TC_PREP = 256
TM_IN, TN_IN = 1024, 1024
TQ_A = 128
RB_A = 512
TQ_B = 256
TK_B = 256
NEAR_B = 4
TM_OUT = 512
SUB_OUT = 256
VMEM_LIMIT = 56 * 1024 * 1024


def _t5_bucket_np(rel):
    nb = N_BUCKETS // 2
    max_exact = nb // 2
    side = np.where(rel > 0, nb, 0)
    n = np.abs(rel)
    nf = np.maximum(n, max_exact).astype(np.float32)
    large = max_exact + (np.log(nf / max_exact) / math.log(MAX_DISTANCE / max_exact)
                         * (nb - max_exact)).astype(np.int32)
    large = np.minimum(large, nb - 1)
    return (side + np.where(n < max_exact, n, large)).astype(np.int32)


def _nt_dot(a, b):
    return lax.dot_general(a, b, (((1,), (1,)), ((), ())), preferred_element_type=F32)


def _bias_table(bucket, rb_ref, col, sub_row=None):
    acc = jnp.zeros(bucket.shape, F32)
    for b in range(N_BUCKETS):
        val = rb_ref[b, col]
        if sub_row is not None:
            val = val - rb_ref[sub_row, col]
        acc = jnp.where(bucket == b, val * LOG2E, acc)
    return acc


def _wprep_kernel(src_ref, w_ref, o_ref):
    t = pl.program_id(0)

    @pl.when(src_ref[t] >= 0)
    def _():
        for kb in range(KB_IN):
            o_ref[:, kb * LANES:(kb + 1) * LANES] = w_ref[pl.ds(kb, TC_PREP, stride=KB_IN), :].astype(BF16)

    @pl.when(src_ref[t] < 0)
    def _():
        o_ref[...] = jnp.zeros(o_ref.shape, BF16)


def _wprep(w_t):
    src = np.full((PROJ_W // TC_PREP,), -1, np.int32)
    for dst_col, src_col, width in SEGMENTS:
        for n in range(width // TC_PREP):
            src[dst_col // TC_PREP + n] = src_col + n * TC_PREP
    return pl.pallas_call(
        _wprep_kernel,
        grid_spec=pltpu.PrefetchScalarGridSpec(
            num_scalar_prefetch=1,
            grid=(PROJ_W // TC_PREP,),
            in_specs=[pl.BlockSpec((pl.Element(TC_PREP * KB_IN), pl.Element(LANES)),
                                   lambda t, src: (jnp.maximum(src[t], 0) * KB_IN, 0))],
            out_specs=pl.BlockSpec((TC_PREP, D_MODEL), lambda t, src: (t, 0)),
        ),
        out_shape=jax.ShapeDtypeStruct((PROJ_W, D_MODEL), BF16),
        compiler_params=pltpu.CompilerParams(
            dimension_semantics=("arbitrary",), vmem_limit_bytes=VMEM_LIMIT),
        name="wprep",
    )(jnp.asarray(src), w_t)


def _inproj_kernel(x_ref, g_ref, wt_ref, cs_ref, o_ref, hn_ref):
    @pl.when(pl.program_id(1) == 0)
    def _():
        x = x_ref[...]
        ms = jnp.mean(x * x, axis=-1, keepdims=True)
        hn_ref[...] = (x * lax.rsqrt(ms + EPS) * g_ref[...]).astype(BF16)

    acc = _nt_dot(hn_ref[...], wt_ref[...])
    o_ref[...] = (acc * cs_ref[...]).astype(BF16)


def _inproj(x2d, g, wt, cs):
    m = x2d.shape[0]
    return pl.pallas_call(
        _inproj_kernel,
        grid=(m // TM_IN, PROJ_W // TN_IN),
        in_specs=[
            pl.BlockSpec((TM_IN, D_MODEL), lambda i, j: (i, 0)),
            pl.BlockSpec((1, D_MODEL), lambda i, j: (0, 0)),
            pl.BlockSpec((TN_IN, D_MODEL), lambda i, j: (j, 0)),
            pl.BlockSpec((1, TN_IN), lambda i, j: (0, j)),
        ],
        out_specs=pl.BlockSpec((TM_IN, TN_IN), lambda i, j: (i, j)),
        out_shape=jax.ShapeDtypeStruct((m, PROJ_W), BF16),
        scratch_shapes=[pltpu.VMEM((TM_IN, D_MODEL), BF16)],
        compiler_params=pltpu.CompilerParams(
            dimension_semantics=("parallel", "arbitrary"), vmem_limit_bytes=VMEM_LIMIT),
        name="inproj",
    )(x2d, g, wt, cs)


def _memkv_kernel(m_ref, g_ref, w_ref, o_ref):
    x = m_ref[...]
    ms = jnp.mean(x * x, axis=-1, keepdims=True)
    hn = (x * lax.rsqrt(ms + EPS) * g_ref[...]).astype(BF16)
    o_ref[...] = jnp.dot(hn, w_ref[...].astype(BF16), preferred_element_type=F32).astype(BF16)


def _memkv(mem2d, g, w):
    m = mem2d.shape[0]
    return pl.pallas_call(
        _memkv_kernel,
        grid=(m // N_MEM,),
        in_specs=[
            pl.BlockSpec((N_MEM, D_MODEL), lambda i: (i, 0)),
            pl.BlockSpec((1, D_MODEL), lambda i: (0, 0)),
            pl.BlockSpec((D_MODEL, 2 * C_WIDTH), lambda i: (0, 0)),
        ],
        out_specs=pl.BlockSpec((N_MEM, 2 * C_WIDTH), lambda i: (i, 0)),
        out_shape=jax.ShapeDtypeStruct((m, 2 * C_WIDTH), BF16),
        compiler_params=pltpu.CompilerParams(
            dimension_semantics=("arbitrary",), vmem_limit_bytes=VMEM_LIMIT),
        name="memkv",
    )(mem2d, g, w)


GRP_A = A_HEADS // A_KV_HEADS
PAIRS_A = GRP_A // 2
COLS_A = GRP_A * TQ_A
KEYS_A = 2 * TQ_A
CHUNK_PAIRS_A = 4


def _swa_kernel(q_ref, kvp_ref, kvc_ref, bkt_ref, rb_ref, sink_ref, o_ref,
                tab_ref, sinkv_ref, kp_ref, vt_ref, s_ref, m_ref, ot_ref):
    i = pl.program_id(1)

    @pl.when((pl.program_id(0) == 0) & (i == 0))
    def _():
        bkt = bkt_ref[...]
        kchunk = lax.broadcasted_iota(jnp.int32, bkt.shape, 0) // CHUNK
        qchunk = lax.broadcasted_iota(jnp.int32, bkt.shape, 1) // CHUNK
        allowed = (kchunk >= qchunk) & (kchunk <= qchunk + WINDOW_CHUNKS)
        has_prev = lax.broadcasted_iota(jnp.int32, bkt.shape, 0) >= TQ_A
        for g in range(A_KV_HEADS):
            for col in range(GRP_A):
                h = g * GRP_A + 2 * (col % PAIRS_A) + col // PAIRS_A
                cols = slice(col * TQ_A, (col + 1) * TQ_A)
                t = jnp.where(allowed, _bias_table(bkt, rb_ref, h), NEG)
                tab_ref[0, g, :, cols] = t
                tab_ref[1, g, :, cols] = jnp.where(has_prev, t, NEG)
                sinkv_ref[g, :, cols] = jnp.full((SUBLANES, TQ_A), sink_ref[h] * LOG2E, F32)

    width = CHUNK_PAIRS_A * TQ_A
    tasks = [(t, g, p0, parity) for t in range(RB_A // TQ_A) for g in range(A_KV_HEADS)
             for p0 in range(0, PAIRS_A, CHUNK_PAIRS_A) for parity in range(2)]

    def prepare(t, g):
        slot = (t * A_KV_HEADS + g) % 2
        if t == 0:
            kv_t = jnp.concatenate([kvp_ref[...], kvc_ref[0:TQ_A, :]], axis=0)
        else:
            kv_t = kvc_ref[(t - 1) * TQ_A:(t + 1) * TQ_A, :]
        kg = kv_t[:, g * A_HEAD_DIM:(g + 1) * A_HEAD_DIM]
        zeros = jnp.zeros((KEYS_A, A_HEAD_DIM), BF16)
        kp_ref[slot, 0] = jnp.concatenate([kg, zeros], axis=1)
        kp_ref[slot, 1] = jnp.concatenate([zeros, kg], axis=1)
        r = lax.broadcasted_iota(jnp.int32, (A_HEAD_DIM, KV_A), 0)
        c = lax.broadcasted_iota(jnp.int32, (A_HEAD_DIM, KV_A), 1)
        pick = jnp.where(c == g * A_HEAD_DIM + r, 1.0, 0.0).astype(BF16)
        vt = _nt_dot(pick, kv_t[:, KV_A:]).astype(BF16)
        vt_ref[slot] = jnp.concatenate([vt, jnp.ones((PACKED, KEYS_A), BF16)], axis=0)

    def chunk_cols(p0, parity):
        lo = (parity * PAIRS_A + p0) * TQ_A
        return slice(lo, lo + width)

    def logits_stage(n):
        t, g, p0, parity = tasks[n]
        if (p0, parity) == (0, 0):
            prepare(t, g)
        slot, buf, cols = (t * A_KV_HEADS + g) % 2, n % 2, chunk_cols(p0, parity)
        q_pairs = jnp.concatenate(
            [q_ref[t * TQ_A:(t + 1) * TQ_A, (g * PAIRS_A + p) * LANES:(g * PAIRS_A + p + 1) * LANES]
             for p in range(p0, p0 + CHUNK_PAIRS_A)], axis=0)
        variant = jnp.where(i == 0, 1, 0) if t == 0 else 0
        s = _nt_dot(kp_ref[slot, parity], q_pairs) + tab_ref[variant, g, :, cols]
        s_ref[buf] = s
        m = jnp.max(s.reshape(KEYS_A // SUBLANES, SUBLANES, width), axis=0)
        m_ref[buf] = jnp.maximum(jnp.broadcast_to(jnp.max(m, axis=0, keepdims=True), m.shape),
                                 sinkv_ref[g, :, cols])

    def update_stage(n):
        t, g, p0, parity = tasks[n]
        slot, buf, cols = (t * A_KV_HEADS + g) % 2, n % 2, chunk_cols(p0, parity)
        m = m_ref[buf]
        s = s_ref[buf].reshape(KEYS_A // SUBLANES, SUBLANES, width)
        p = jnp.exp2(s - m[None]).reshape(KEYS_A, width).astype(BF16)
        pv = jnp.dot(vt_ref[slot], p, preferred_element_type=F32)
        den = pv[A_HEAD_DIM:A_HEAD_DIM + SUBLANES] + jnp.exp2(sinkv_ref[g, :, cols] - m)
        out_t = pv[:A_HEAD_DIM].reshape(A_HEAD_DIM // SUBLANES, SUBLANES, width) / den[None]
        ot_ref[parity] = out_t.reshape(A_HEAD_DIM, width)
        if parity == 1:
            for k in range(CHUNK_PAIRS_A):
                blk = jnp.concatenate([ot_ref[0, :, k * TQ_A:(k + 1) * TQ_A],
                                       ot_ref[1, :, k * TQ_A:(k + 1) * TQ_A]], axis=0)
                lanes = slice((g * PAIRS_A + p0 + k) * LANES, (g * PAIRS_A + p0 + k + 1) * LANES)
                o_ref[t * TQ_A:(t + 1) * TQ_A, lanes] = blk.T.astype(BF16)

    logits_stage(0)
    for n in range(len(tasks)):
        if n + 1 < len(tasks):
            logits_stage(n + 1)
        update_stage(n)


def _swa(proj, bkt, rel_bias, sinks, batch, seq):
    nt = seq // RB_A
    sub = RB_A // TQ_A
    kv_blk = COL_AKV // (2 * KV_A)
    return pl.pallas_call(
        _swa_kernel,
        grid=(batch, nt),
        in_specs=[
            pl.BlockSpec((RB_A, A_WIDTH), lambda b, i: (b * nt + i, COL_AQ // A_WIDTH)),
            pl.BlockSpec((TQ_A, 2 * KV_A), lambda b, i: (jnp.maximum((b * nt + i) * sub - 1, 0), kv_blk)),
            pl.BlockSpec((RB_A, 2 * KV_A), lambda b, i: (b * nt + i, kv_blk)),
            pl.BlockSpec((KEYS_A, TQ_A), lambda b, i: (0, 0)),
            pl.BlockSpec(memory_space=pltpu.SMEM),
            pl.BlockSpec(memory_space=pltpu.SMEM),
        ],
        out_specs=pl.BlockSpec((RB_A, A_WIDTH), lambda b, i: (b * nt + i, 0)),
        out_shape=jax.ShapeDtypeStruct((batch * seq, A_WIDTH), BF16),
        scratch_shapes=[
            pltpu.VMEM((2, A_KV_HEADS, KEYS_A, COLS_A), F32),
            pltpu.VMEM((A_KV_HEADS, SUBLANES, COLS_A), F32),
            pltpu.VMEM((2, 2, KEYS_A, LANES), BF16),
            pltpu.VMEM((2, A_HEAD_DIM + PACKED, KEYS_A), BF16),
            pltpu.VMEM((2, KEYS_A, CHUNK_PAIRS_A * TQ_A), F32),
            pltpu.VMEM((2, SUBLANES, CHUNK_PAIRS_A * TQ_A), F32),
            pltpu.VMEM((2, A_HEAD_DIM, CHUNK_PAIRS_A * TQ_A), F32),
        ],
        compiler_params=pltpu.CompilerParams(
            dimension_semantics=("arbitrary", "arbitrary"), vmem_limit_bytes=VMEM_LIMIT),
        name="swa",
    )(proj, proj, proj, bkt, rel_bias, sinks)


def _dsa_kernel(iq_ref, iwq_ref, ik_ref, q_ref, k_ref, v_ref, bkt_ref, rb_ref, o_ref,
                sc_ref, scb_ref, mb_ref, tab_ref, tri_ref, vt_ref, acc_ref, m_ref,
                sa_ref, smaxa_ref, sb_ref, smaxb_ref, *, topk, seq):
    i = pl.program_id(1)
    nkt = i + 1
    int32, int16 = jnp.int32, jnp.int16
    grp = TK_B // SUBLANES

    def hcols(h):
        return slice(h * B_HEAD_DIM, (h + 1) * B_HEAD_DIM)

    def ktile(kt):
        return pl.ds(pl.multiple_of(kt * TK_B, TK_B), TK_B)

    def rows3(a):
        return a.reshape(a.shape[0] // SUBLANES, SUBLANES, TQ_B)

    def all_rows(a, op):
        return jnp.broadcast_to(op(a, axis=0, keepdims=True), a.shape)

    @pl.when((pl.program_id(0) == 0) & (i == 0))
    def _():
        far_bucket = N_BUCKETS // 2 - 1
        for d in range(NEAR_B):
            for h in range(B_HEADS):
                tab_ref[d, h] = _bias_table(bkt_ref[d], rb_ref, A_HEADS + h, sub_row=far_bucket)
        tab_ref[NEAR_B] = jnp.zeros(tab_ref.shape[1:], F32)
        r = lax.broadcasted_iota(int32, (TK_B, TK_B), 0)
        c = lax.broadcasted_iota(int32, (TK_B, TK_B), 1)
        tri_ref[...] = jnp.where(c < r, 1.0, 0.0).astype(BF16)

    @pl.when(i == 0)
    def _():
        r = lax.broadcasted_iota(int32, (B_HEAD_DIM, B_HEAD_DIM), 0)
        c = lax.broadcasted_iota(int32, (B_HEAD_DIM, B_HEAD_DIM), 1)
        eye = jnp.where(r == c, 1.0, 0.0).astype(BF16)

        def body(kt, carry):
            for h in range(B_HEADS):
                vt_ref[kt, h] = _nt_dot(eye, v_ref[ktile(kt), hcols(h)]).astype(BF16)
            return carry

        lax.fori_loop(0, seq // TK_B, body, 0)

    key_chunk = lax.broadcasted_iota(int32, (TK_B, TQ_B), 0) // CHUNK
    qry_chunk = lax.broadcasted_iota(int32, (TK_B, TQ_B), 1) // CHUNK
    adm_diag = key_chunk <= qry_chunk

    r = lax.broadcasted_iota(int32, (IDX_HEADS * SUBLANES, LANES), 0)
    c = lax.broadcasted_iota(int32, (IDX_HEADS * SUBLANES, LANES), 1)
    pick_w = jnp.where(c == IDX_DIM + r // SUBLANES, 1.0, 0.0).astype(BF16)
    w_all = _nt_dot(pick_w, iwq_ref[...]) * (IDX_HEADS ** -0.5 * IDX_DIM ** -0.5)

    def for_tiles(n, body):
        def pair(j, carry):
            body(2 * j)
            body(2 * j + 1)
            return carry

        lax.fori_loop(0, n // 2, pair, 0)

        @pl.when(n % 2 == 1)
        def _():
            body(n - 1)

    def score_tile(kt):
        ikt = ik_ref[ktile(kt), 0:IDX_DIM]
        sc = jnp.zeros((grp, SUBLANES, TQ_B), F32)
        for h in range(IDX_HEADS):
            x = _nt_dot(ikt, iq_ref[:, h * IDX_DIM:(h + 1) * IDX_DIM])
            sc = sc + w_all[h * SUBLANES:(h + 1) * SUBLANES][None] * jnp.maximum(rows3(x), 0.0)
        store_score(kt, sc.reshape(TK_B, TQ_B))

    def store_score(kt, sc):
        sc_ref[kt] = sc
        scb_ref[kt] = sc.astype(BF16)

    for_tiles(nkt, score_tile)
    store_score(i, jnp.where(adm_diag, sc_ref[i], -jnp.inf))

    def key_to_f32(key):
        return pltpu.bitcast(jnp.where(key >= 0, key, INT_MIN - key), F32)

    def over_tiles(tile_count, zero):
        def pair(j, acc):
            return acc + (tile_count(2 * j) + tile_count(2 * j + 1))
        acc = lax.fori_loop(0, nkt // 2, pair, zero)
        return lax.cond(nkt % 2 == 1, lambda a: a + tile_count(nkt - 1), lambda a: a, acc)

    def tree_sum(parts):
        while len(parts) > 1:
            parts = [parts[n] + parts[n + 1] for n in range(0, len(parts), 2)]
        return parts[0]

    def count_bf16(cand):
        def tile_count(kt):
            blk = scb_ref[kt].reshape(TK_B // PACKED, PACKED, TQ_B)
            return tree_sum([jnp.where(blk[g] >= cand, jnp.ones((), int16), jnp.zeros((), int16))
                             for g in range(TK_B // PACKED)])
        acc = over_tiles(tile_count, jnp.zeros((PACKED, TQ_B), int16))
        return all_rows(acc.astype(int32), jnp.sum)

    def count_f32(pred):
        def tile_count(kt):
            blk = rows3(sc_ref[kt])
            return tree_sum([jnp.where(pred(blk[g]), 1, 0) for g in range(grp)])
        return all_rows(over_tiles(tile_count, jnp.zeros((SUBLANES, TQ_B), int32)), jnp.sum)

    def coarse_pass(p, u):
        bit = jnp.left_shift(jnp.int32(1), 15 - p)
        cand = key_to_f32(((u | bit) - HALF) << 16).astype(BF16)
        return jnp.where(count_bf16(cand) >= topk, u | bit, u)

    u = lax.fori_loop(0, 16, coarse_pass, jnp.zeros((PACKED, TQ_B), int32))
    coarse_key = ((u - HALF) << 16)[:SUBLANES]

    lo = jnp.maximum(coarse_key - HALF, KEY_NEG_INF)
    hi = jnp.minimum(coarse_key, KEY_POS_INF - 2 * HALF) + 2 * HALF

    def fine_pass(p, lohi):
        lo, hi = lohi
        mid = lo + ((hi - lo) >> 1)
        cand = key_to_f32(mid)
        ok = count_f32(lambda blk: blk >= cand) >= topk
        return jnp.where(ok, mid, lo), jnp.where(ok, hi, mid)

    lo, hi = lax.fori_loop(0, FINE_PASSES, fine_pass, (lo, hi))
    tau2d = key_to_f32(lo)
    tau = tau2d[None]

    need = (topk - count_f32(lambda blk: blk > tau2d)).astype(F32)[None]
    ones_l = jnp.ones((2 * SUBLANES, TK_B), BF16)

    def mask_tile(kt, run):
        blk = rows3(sc_ref[kt])
        eq = blk == tau
        eqf = jnp.where(eq, 1.0, 0.0).reshape(TK_B, TQ_B).astype(BF16)
        rank = rows3(jnp.dot(tri_ref[...], eqf, preferred_element_type=F32)) + run[None]
        sel = (blk > tau) | (eq & (rank < need))
        mb_ref[kt] = jnp.where(sel, 0.0, NEG).reshape(TK_B, TQ_B)
        return run + jnp.dot(ones_l, eqf, preferred_element_type=F32)[:SUBLANES]

    run = lax.fori_loop(0, nkt // 2, lambda j, run: mask_tile(2 * j + 1, mask_tile(2 * j, run)),
                        jnp.zeros((SUBLANES, TQ_B), F32))

    @pl.when(nkt % 2 == 1)
    def _():
        mask_tile(nkt - 1, run)

    mb_ref[i] = jnp.where(adm_diag, mb_ref[i], NEG)

    m_ref[...] = jnp.full(m_ref.shape, NEG, F32)
    acc_ref[...] = jnp.zeros(acc_ref.shape, F32)
    ones_rows = jnp.ones((PACKED, TK_B), BF16)

    buf_a, buf_b = (sa_ref, smaxa_ref), (sb_ref, smaxb_ref)

    def logits_stage(kt, buf, near):
        s_ref, smax_ref = buf
        mb = mb_ref[kt]
        for h in range(B_HEADS):
            s = _nt_dot(k_ref[ktile(kt), hcols(h)], q_ref[:, hcols(h)]) + mb
            if near:
                s = s + tab_ref[jnp.minimum(i - kt, NEAR_B), h]
            s_ref[h] = s
            smax_ref[h] = all_rows(jnp.max(rows3(s), axis=0), jnp.max)

    def update_stage(kt, buf):
        s_ref, smax_ref = buf
        for h in range(B_HEADS):
            m_old = m_ref[h]
            m_new = jnp.maximum(m_old, smax_ref[h])
            alpha = jnp.exp2(m_old - m_new)
            p = jnp.exp2(rows3(s_ref[h]) - m_new[None]).reshape(TK_B, TQ_B).astype(BF16)
            vaug = jnp.concatenate([vt_ref[kt, h], ones_rows], axis=0)
            pv = jnp.dot(vaug, p, preferred_element_type=F32)
            acc_ref[h] = (rows3(acc_ref[h]) * alpha[None] + rows3(pv)).reshape(acc_ref.shape[1:])
            m_ref[h] = m_new

    def pair_body(j, carry, near):
        logits_stage(2 * j + 1, buf_b, near)
        update_stage(2 * j, buf_a)
        logits_stage(2 * j + 2, buf_a, near)
        update_stage(2 * j + 1, buf_b)
        return carry

    near_lo = jnp.maximum(i - (NEAR_B - 1), 0)
    far_pairs = jnp.maximum(near_lo - 1, 0) // 2
    logits_stage(0, buf_a, True)
    lax.fori_loop(0, far_pairs, functools.partial(pair_body, near=False), 0)
    lax.fori_loop(far_pairs, (nkt - 1) // 2, functools.partial(pair_body, near=True), 0)

    @pl.when(nkt % 2 == 0)
    def _():
        logits_stage(nkt - 1, buf_b, True)
        update_stage(nkt - 2, buf_a)
        update_stage(nkt - 1, buf_b)

    @pl.when(nkt % 2 == 1)
    def _():
        update_stage(nkt - 1, buf_a)

    for h in range(B_HEADS):
        num = rows3(acc_ref[h, 0:B_HEAD_DIM, :])
        den = acc_ref[h, B_HEAD_DIM:B_HEAD_DIM + SUBLANES, :]
        out_t = (num / den[None]).reshape(B_HEAD_DIM, TQ_B)
        o_ref[:, h * B_HEAD_DIM:(h + 1) * B_HEAD_DIM] = out_t.T.astype(BF16)


def _dsa(proj, bkt, rel_bias, batch, seq):
    nt = seq // TQ_B
    topk = min(TOPK_MAX, seq // 4)

    def q_spec(width, col):
        return pl.BlockSpec((TQ_B, width), lambda b, i: (b * nt + i, col // width))

    def seq_spec(width, col):
        return pl.BlockSpec((seq, width), lambda b, i: (b, col // width))

    return pl.pallas_call(
        functools.partial(_dsa_kernel, topk=topk, seq=seq),
        grid=(batch, nt),
        in_specs=[
            q_spec(IDX_HEADS * IDX_DIM, COL_IQ),
            q_spec(LANES, COL_IKW),
            seq_spec(LANES, COL_IKW),
            q_spec(B_WIDTH, COL_BQ), seq_spec(B_WIDTH, COL_BK), seq_spec(B_WIDTH, COL_BV),
            pl.BlockSpec((NEAR_B, TK_B, TQ_B), lambda b, i: (0, 0, 0)),
            pl.BlockSpec(memory_space=pltpu.SMEM),
        ],
        out_specs=pl.BlockSpec((TQ_B, B_WIDTH), lambda b, i: (b * nt + i, 0)),
        out_shape=jax.ShapeDtypeStruct((batch * seq, B_WIDTH), BF16),
        scratch_shapes=[
            pltpu.VMEM((nt, TK_B, TQ_B), F32),
            pltpu.VMEM((nt, TK_B, TQ_B), BF16),
            pltpu.VMEM((nt, TK_B, TQ_B), F32),
            pltpu.VMEM((NEAR_B + 1, B_HEADS, TK_B, TQ_B), F32),
            pltpu.VMEM((TK_B, TK_B), BF16),
            pltpu.VMEM((seq // TK_B, B_HEADS, B_HEAD_DIM, TK_B), BF16),
            pltpu.VMEM((B_HEADS, B_HEAD_DIM + PACKED, TQ_B), F32),
            pltpu.VMEM((B_HEADS, SUBLANES, TQ_B), F32),
            pltpu.VMEM((B_HEADS, TK_B, TQ_B), F32),
            pltpu.VMEM((B_HEADS, SUBLANES, TQ_B), F32),
            pltpu.VMEM((B_HEADS, TK_B, TQ_B), F32),
            pltpu.VMEM((B_HEADS, SUBLANES, TQ_B), F32),
        ],
        compiler_params=pltpu.CompilerParams(
            dimension_semantics=("arbitrary", "arbitrary"), vmem_limit_bytes=VMEM_LIMIT),
        name="dsa",
    )(proj, proj, proj, proj, proj, proj, bkt, rel_bias)


def _outproj_kernel(oa_ref, ob_ref, cq_ref, mkv_ref, gate_ref, x_ref, w_ref, g_ref, o_ref):
    ones_m = jnp.ones((N_MEM, C_HEAD_DIM), BF16)
    for r in range(TM_OUT // SUB_OUT):
        rows = slice(r * SUB_OUT, (r + 1) * SUB_OUT)
        att = [oa_ref[rows, :].astype(F32), ob_ref[rows, :].astype(F32)]
        for h in range(C_HEADS):
            hs = slice(h * C_HEAD_DIM, (h + 1) * C_HEAD_DIM)
            s = _nt_dot(cq_ref[rows, hs], mkv_ref[:, hs])
            p = jnp.exp2(s - jnp.max(s, axis=1, keepdims=True)).astype(BF16)
            vaug = jnp.concatenate([mkv_ref[:, C_WIDTH + h * C_HEAD_DIM:C_WIDTH + (h + 1) * C_HEAD_DIM], ones_m],
                                   axis=1)
            pv = jnp.dot(p, vaug, preferred_element_type=F32)
            att.append(pv[:, :C_HEAD_DIM] / pv[:, C_HEAD_DIM:])
        gate = gate_ref[rows, :].astype(F32)
        sg = gate / (1.0 + jnp.exp(-gate))
        y = (jnp.concatenate(att, axis=1) * sg).astype(BF16)
        h_new = x_ref[rows, :] + jnp.dot(y, w_ref[...], preferred_element_type=F32)
        ms = jnp.mean(h_new * h_new, axis=-1, keepdims=True)
        o_ref[rows, :] = h_new * lax.rsqrt(ms + EPS) * g_ref[...]


def _outproj(oa, ob, proj, mkv, x2d, w, g, seq):
    m = x2d.shape[0]
    steps_per_seq = seq // TM_OUT
    return pl.pallas_call(
        _outproj_kernel,
        grid=(m // TM_OUT,),
        in_specs=[
            pl.BlockSpec((TM_OUT, A_WIDTH), lambda i: (i, 0)),
            pl.BlockSpec((TM_OUT, B_WIDTH), lambda i: (i, 0)),
            pl.BlockSpec((TM_OUT, C_WIDTH), lambda i: (i, COL_CQ // C_WIDTH)),
            pl.BlockSpec((N_MEM, 2 * C_WIDTH), lambda i: (i // steps_per_seq, 0)),
            pl.BlockSpec((TM_OUT, MIX_WIDTH), lambda i: (i, COL_GATE // MIX_WIDTH)),
            pl.BlockSpec((TM_OUT, D_MODEL), lambda i: (i, 0)),
            pl.BlockSpec((MIX_WIDTH, D_MODEL), lambda i: (0, 0)),
            pl.BlockSpec((1, D_MODEL), lambda i: (0, 0)),
        ],
        out_specs=pl.BlockSpec((TM_OUT, D_MODEL), lambda i: (i, 0)),
        out_shape=jax.ShapeDtypeStruct((m, D_MODEL), F32),
        compiler_params=pltpu.CompilerParams(
            dimension_semantics=("parallel",), vmem_limit_bytes=VMEM_LIMIT),
        name="outproj",
    )(oa, ob, proj, mkv, proj, x2d, w, g)


def _col_scale():
    cs = np.ones((1, PROJ_W), np.float32)
    cs[0, COL_AQ:COL_AQ + A_WIDTH] = A_HEAD_DIM ** -0.5 * LOG2E
    cs[0, COL_BQ:COL_BQ + B_WIDTH] = B_HEAD_DIM ** -0.5 * LOG2E
    cs[0, COL_CQ:COL_CQ + C_WIDTH] = C_HEAD_DIM ** -0.5 * LOG2E
    return jnp.asarray(cs)


def kernel(x, mem, g_norm, w_in, sinks, rel_bias, g_mem, w_mem_kv, w_out, g_final):
    batch, seq, _ = x.shape
    assert w_in.shape[0] == 1, "the out-projection kernel fuses the final norm of a single-layer trunk"
    kk = np.arange(2 * TQ_A)[:, None]
    qq = np.arange(TQ_A)[None, :]
    bkt_a = jnp.asarray(_t5_bucket_np(kk - TQ_A - qq))
    kk = np.arange(TK_B)[None, :, None]
    qq = np.arange(TQ_B)[None, None, :]
    d = np.arange(NEAR_B)[:, None, None]
    bkt_b = jnp.asarray(_t5_bucket_np(kk - qq - TK_B * d))

    h = x.reshape(batch * seq, D_MODEL)
    mem2d = mem.reshape(batch * N_MEM, D_MODEL)
    w_t = jnp.transpose(w_in, (2, 0, 1)).reshape(IN_WIDTH * KB_IN, LANES)
    proj = _inproj(h, g_norm[0].reshape(1, D_MODEL), _wprep(w_t), _col_scale())
    mkv = _memkv(mem2d, g_mem[0].reshape(1, D_MODEL), w_mem_kv[0])
    oa = _swa(proj, bkt_a, rel_bias, sinks[0], batch, seq)
    ob = _dsa(proj, bkt_b, rel_bias, batch, seq)
    out = _outproj(oa, ob, proj, mkv, h, w_out[0].astype(BF16), g_final.reshape(1, D_MODEL), seq)
    return out.reshape(batch, seq, D_MODEL)
```

```python
import functools
import math

import numpy as np
import jax
import jax.numpy as jnp
from jax import lax
from jax.experimental import pallas as pl
from jax.experimental.pallas import tpu as pltpu

D_MODEL = 2048
CHUNK = 64
N_MEM = 256
EPS = 1e-6
A_HEADS = 16
A_KV_HEADS = 2
A_HEAD_DIM = 64
WINDOW_CHUNKS = 2
A_WIDTH = A_HEADS * A_HEAD_DIM
B_HEADS = 4
B_HEAD_DIM = 128
B_WIDTH = B_HEADS * B_HEAD_DIM
IDX_HEADS = 4
IDX_DIM = 64
TOPK_MAX = 256
C_HEADS = 4
C_HEAD_DIM = 128
C_WIDTH = C_HEADS * C_HEAD_DIM
MIX_WIDTH = A_WIDTH + B_WIDTH + C_WIDTH
N_BUCKETS = 32
MAX_DISTANCE = 1024
KV_A = A_KV_HEADS * A_HEAD_DIM
SPLIT_SIZES = (A_WIDTH, KV_A, KV_A, B_WIDTH, B_WIDTH, B_WIDTH,
               IDX_HEADS * IDX_DIM, IDX_DIM, IDX_HEADS, C_WIDTH, MIX_WIDTH)
IN_WIDTH = sum(SPLIT_SIZES)

F32 = jnp.float32
BF16 = jnp.bfloat16
LOG2E = math.log2(math.e)
NEG = -1e30
INT_MIN = -(2 ** 31)
LANES = 128
SUBLANES = 8
PACKED = 16
HALF = 1 << 15
KEY_POS_INF = 0x7F800000
KEY_NEG_INF = -KEY_POS_INF
FINE_PASSES = 17

(SRC_AQ, SRC_AK, SRC_AV, SRC_BQ, SRC_BK, SRC_BV,
 SRC_IQ, SRC_IK, SRC_IW, SRC_CQ, SRC_GATE) = (int(c) for c in np.cumsum((0,) + SPLIT_SIZES)[:-1])
COL_GATE = 0
COL_AQ = COL_GATE + MIX_WIDTH
COL_BQ = COL_AQ + A_WIDTH
COL_BK = COL_BQ + B_WIDTH
COL_BV = COL_BK + B_WIDTH
COL_CQ = COL_BV + B_WIDTH
COL_AKV = COL_CQ + C_WIDTH
COL_IQ = COL_AKV + 2 * KV_A
COL_IKW = COL_IQ + IDX_HEADS * IDX_DIM
PROJ_W = 6144
SEGMENTS = ((COL_GATE, SRC_GATE, MIX_WIDTH), (COL_AQ, SRC_AQ, A_WIDTH), (COL_BQ, SRC_BQ, B_WIDTH),
            (COL_BK, SRC_BK, B_WIDTH), (COL_BV, SRC_BV, B_WIDTH), (COL_CQ, SRC_CQ, C_WIDTH),
            (COL_AKV, SRC_AK, 2 * KV_A), (COL_IQ, SRC_IQ, IDX_HEADS * IDX_DIM), (COL_IKW, SRC_IK, 256))
KB_IN = D_MODEL // LANES

TC_PREP = 256
TM_IN, TN_IN = 1024, 1024
SUB_IN = 256
TQ_A = 128
RB_A = 512
TQ_B = 256
TK_B = 256
NEAR_B = 4
TM_OUT = 512
SUB_OUT = 256
VMEM_LIMIT = 56 * 1024 * 1024


def _t5_bucket_np(rel):
    nb = N_BUCKETS // 2
    max_exact = nb // 2
    side = np.where(rel > 0, nb, 0)
    n = np.abs(rel)
    nf = np.maximum(n, max_exact).astype(np.float32)
    large = max_exact + (np.log(nf / max_exact) / math.log(MAX_DISTANCE / max_exact)
                         * (nb - max_exact)).astype(np.int32)
    large = np.minimum(large, nb - 1)
    return (side + np.where(n < max_exact, n, large)).astype(np.int32)


def _nt_dot(a, b):
    return lax.dot_general(a, b, (((1,), (1,)), ((), ())), preferred_element_type=F32)


def _bias_table(bucket, rb_ref, col, sub_row=None):
    acc = jnp.zeros(bucket.shape, F32)
    for b in range(N_BUCKETS):
        val = rb_ref[b, col]
        if sub_row is not None:
            val = val - rb_ref[sub_row, col]
        acc = jnp.where(bucket == b, val * LOG2E, acc)
    return acc


def _wprep_kernel(src_ref, w_ref, o_ref):
    t = pl.program_id(0)

    @pl.when(src_ref[t] >= 0)
    def _():
        for kb in range(KB_IN):
            o_ref[:, kb * LANES:(kb + 1) * LANES] = w_ref[pl.ds(kb, TC_PREP, stride=KB_IN), :].astype(BF16)

    @pl.when(src_ref[t] < 0)
    def _():
        o_ref[...] = jnp.zeros(o_ref.shape, BF16)


def _wprep(w_t):
    src = np.full((PROJ_W // TC_PREP,), -1, np.int32)
    for dst_col, src_col, width in SEGMENTS:
        for n in range(width // TC_PREP):
            src[dst_col // TC_PREP + n] = src_col + n * TC_PREP
    return pl.pallas_call(
        _wprep_kernel,
        grid_spec=pltpu.PrefetchScalarGridSpec(
            num_scalar_prefetch=1,
            grid=(PROJ_W // TC_PREP,),
            in_specs=[pl.BlockSpec((pl.Element(TC_PREP * KB_IN), pl.Element(LANES)),
                                   lambda t, src: (jnp.maximum(src[t], 0) * KB_IN, 0))],
            out_specs=pl.BlockSpec((TC_PREP, D_MODEL), lambda t, src: (t, 0)),
        ),
        out_shape=jax.ShapeDtypeStruct((PROJ_W, D_MODEL), BF16),
        compiler_params=pltpu.CompilerParams(
            dimension_semantics=("arbitrary",), vmem_limit_bytes=VMEM_LIMIT),
        name="wprep",
    )(jnp.asarray(src), w_t)


def _inproj_kernel(x_ref, g_ref, wt_ref, cs_ref, o_ref, hn_ref):
    j = pl.program_id(1)

    @pl.when(j == 0)
    def _():
        for r in range(TM_IN // SUB_IN):
            rows = slice(r * SUB_IN, (r + 1) * SUB_IN)
            x = x_ref[rows, :]
            ms = jnp.mean(x * x, axis=-1, keepdims=True)
            hn = (x * lax.rsqrt(ms + EPS) * g_ref[...]).astype(BF16)
            hn_ref[rows, :] = hn
            o_ref[rows, :] = (_nt_dot(hn, wt_ref[...]) * cs_ref[...]).astype(BF16)

    @pl.when(j > 0)
    def _():
        o_ref[...] = (_nt_dot(hn_ref[...], wt_ref[...]) * cs_ref[...]).astype(BF16)


def _inproj(x2d, g, wt, cs):
    m = x2d.shape[0]
    return pl.pallas_call(
        _inproj_kernel,
        grid=(m // TM_IN, PROJ_W // TN_IN),
        in_specs=[
            pl.BlockSpec((TM_IN, D_MODEL), lambda i, j: (i, 0)),
            pl.BlockSpec((1, D_MODEL), lambda i, j: (0, 0)),
            pl.BlockSpec((TN_IN, D_MODEL), lambda i, j: (j, 0)),
            pl.BlockSpec((1, TN_IN), lambda i, j: (0, j)),
        ],
        out_specs=pl.BlockSpec((TM_IN, TN_IN), lambda i, j: (i, j)),
        out_shape=jax.ShapeDtypeStruct((m, PROJ_W), BF16),
        scratch_shapes=[pltpu.VMEM((TM_IN, D_MODEL), BF16)],
        compiler_params=pltpu.CompilerParams(
            dimension_semantics=("parallel", "arbitrary"), vmem_limit_bytes=VMEM_LIMIT),
        name="inproj",
    )(x2d, g, wt, cs)


def _memkv_kernel(m_ref, g_ref, w_ref, o_ref):
    x = m_ref[...]
    ms = jnp.mean(x * x, axis=-1, keepdims=True)
    hn = (x * lax.rsqrt(ms + EPS) * g_ref[...]).astype(BF16)
    o_ref[...] = jnp.dot(hn, w_ref[...].astype(BF16), preferred_element_type=F32).astype(BF16)


def _memkv(mem2d, g, w):
    m = mem2d.shape[0]
    return pl.pallas_call(
        _memkv_kernel,
        grid=(m // N_MEM,),
        in_specs=[
            pl.BlockSpec((N_MEM, D_MODEL), lambda i: (i, 0)),
            pl.BlockSpec((1, D_MODEL), lambda i: (0, 0)),
            pl.BlockSpec((D_MODEL, 2 * C_WIDTH), lambda i: (0, 0)),
        ],
        out_specs=pl.BlockSpec((N_MEM, 2 * C_WIDTH), lambda i: (i, 0)),
        out_shape=jax.ShapeDtypeStruct((m, 2 * C_WIDTH), BF16),
        compiler_params=pltpu.CompilerParams(
            dimension_semantics=("arbitrary",), vmem_limit_bytes=VMEM_LIMIT),
        name="memkv",
    )(mem2d, g, w)


GRP_A = A_HEADS // A_KV_HEADS
PAIRS_A = GRP_A // 2
COLS_A = GRP_A * TQ_A
KEYS_A = 2 * TQ_A
CHUNK_PAIRS_A = 4


def _swa_kernel(q_ref, kvp_ref, kvc_ref, bkt_ref, rb_ref, sink_ref, o_ref,
                tab_ref, sinkv_ref, kp_ref, vt_ref, s_ref, m_ref, ot_ref):
    i = pl.program_id(1)

    @pl.when((pl.program_id(0) == 0) & (i == 0))
    def _():
        bkt = bkt_ref[...]
        kchunk = lax.broadcasted_iota(jnp.int32, bkt.shape, 0) // CHUNK
        qchunk = lax.broadcasted_iota(jnp.int32, bkt.shape, 1) // CHUNK
        allowed = (kchunk >= qchunk) & (kchunk <= qchunk + WINDOW_CHUNKS)
        has_prev = lax.broadcasted_iota(jnp.int32, bkt.shape, 0) >= TQ_A
        for g in range(A_KV_HEADS):
            for col in range(GRP_A):
                h = g * GRP_A + 2 * (col % PAIRS_A) + col // PAIRS_A
                cols = slice(col * TQ_A, (col + 1) * TQ_A)
                t = jnp.where(allowed, _bias_table(bkt, rb_ref, h), NEG)
                tab_ref[0, g, :, cols] = t
                tab_ref[1, g, :, cols] = jnp.where(has_prev, t, NEG)
                sinkv_ref[g, :, cols] = jnp.full((SUBLANES, TQ_A), sink_ref[h] * LOG2E, F32)

    width = CHUNK_PAIRS_A * TQ_A
    tasks = [(t, g, p0, parity) for t in range(RB_A // TQ_A) for g in range(A_KV_HEADS)
             for p0 in range(0, PAIRS_A, CHUNK_PAIRS_A) for parity in range(2)]

    def prepare(t, g):
        slot = (t * A_KV_HEADS + g) % 2
        if t == 0:
            kv_t = jnp.concatenate([kvp_ref[...], kvc_ref[0:TQ_A, :]], axis=0)
        else:
            kv_t = kvc_ref[(t - 1) * TQ_A:(t + 1) * TQ_A, :]
        kg = kv_t[:, g * A_HEAD_DIM:(g + 1) * A_HEAD_DIM]
        zeros = jnp.zeros((KEYS_A, A_HEAD_DIM), BF16)
        kp_ref[slot, 0] = jnp.concatenate([kg, zeros], axis=1)
        kp_ref[slot, 1] = jnp.concatenate([zeros, kg], axis=1)
        r = lax.broadcasted_iota(jnp.int32, (A_HEAD_DIM, KV_A), 0)
        c = lax.broadcasted_iota(jnp.int32, (A_HEAD_DIM, KV_A), 1)
        pick = jnp.where(c == g * A_HEAD_DIM + r, 1.0, 0.0).astype(BF16)
        vt = _nt_dot(pick, kv_t[:, KV_A:]).astype(BF16)
        vt_ref[slot] = jnp.concatenate([vt, jnp.ones((PACKED, KEYS_A), BF16)], axis=0)

    def chunk_cols(p0, parity):
        lo = (parity * PAIRS_A + p0) * TQ_A
        return slice(lo, lo + width)

    def logits_stage(n):
        t, g, p0, parity = tasks[n]
        if (p0, parity) == (0, 0):
            prepare(t, g)
        slot, buf, cols = (t * A_KV_HEADS + g) % 2, n % 2, chunk_cols(p0, parity)
        q_pairs = jnp.concatenate(
            [q_ref[t * TQ_A:(t + 1) * TQ_A, (g * PAIRS_A + p) * LANES:(g * PAIRS_A + p + 1) * LANES]
             for p in range(p0, p0 + CHUNK_PAIRS_A)], axis=0)
        variant = jnp.where(i == 0, 1, 0) if t == 0 else 0
        s = _nt_dot(kp_ref[slot, parity], q_pairs) + tab_ref[variant, g, :, cols]
        s_ref[buf] = s
        m = jnp.max(s.reshape(KEYS_A // SUBLANES, SUBLANES, width), axis=0)
        m_ref[buf] = jnp.maximum(jnp.broadcast_to(jnp.max(m, axis=0, keepdims=True), m.shape),
                                 sinkv_ref[g, :, cols])

    def update_stage(n):
        t, g, p0, parity = tasks[n]
        slot, buf, cols = (t * A_KV_HEADS + g) % 2, n % 2, chunk_cols(p0, parity)
        m = m_ref[buf]
        s = s_ref[buf].reshape(KEYS_A // SUBLANES, SUBLANES, width)
        p = jnp.exp2(s - m[None]).reshape(KEYS_A, width).astype(BF16)
        pv = jnp.dot(vt_ref[slot], p, preferred_element_type=F32)
        den = pv[A_HEAD_DIM:A_HEAD_DIM + SUBLANES] + jnp.exp2(sinkv_ref[g, :, cols] - m)
        out_t = pv[:A_HEAD_DIM].reshape(A_HEAD_DIM // SUBLANES, SUBLANES, width) / den[None]
        ot_ref[parity] = out_t.reshape(A_HEAD_DIM, width)
        if parity == 1:
            for k in range(CHUNK_PAIRS_A):
                blk = jnp.concatenate([ot_ref[0, :, k * TQ_A:(k + 1) * TQ_A],
                                       ot_ref[1, :, k * TQ_A:(k + 1) * TQ_A]], axis=0)
                lanes = slice((g * PAIRS_A + p0 + k) * LANES, (g * PAIRS_A + p0 + k + 1) * LANES)
                o_ref[t * TQ_A:(t + 1) * TQ_A, lanes] = blk.T.astype(BF16)

    logits_stage(0)
    for n in range(len(tasks)):
        if n + 1 < len(tasks):
            logits_stage(n + 1)
        update_stage(n)


def _swa(proj, bkt, rel_bias, sinks, batch, seq):
    nt = seq // RB_A
    sub = RB_A // TQ_A
    kv_blk = COL_AKV // (2 * KV_A)
    return pl.pallas_call(
        _swa_kernel,
        grid=(batch, nt),
        in_specs=[
            pl.BlockSpec((RB_A, A_WIDTH), lambda b, i: (b * nt + i, COL_AQ // A_WIDTH)),
            pl.BlockSpec((TQ_A, 2 * KV_A), lambda b, i: (jnp.maximum((b * nt + i) * sub - 1, 0), kv_blk)),
            pl.BlockSpec((RB_A, 2 * KV_A), lambda b, i: (b * nt + i, kv_blk)),
            pl.BlockSpec((KEYS_A, TQ_A), lambda b, i: (0, 0)),
            pl.BlockSpec(memory_space=pltpu.SMEM),
            pl.BlockSpec(memory_space=pltpu.SMEM),
        ],
        out_specs=pl.BlockSpec((RB_A, A_WIDTH), lambda b, i: (b * nt + i, 0)),
        out_shape=jax.ShapeDtypeStruct((batch * seq, A_WIDTH), BF16),
        scratch_shapes=[
            pltpu.VMEM((2, A_KV_HEADS, KEYS_A, COLS_A), F32),
            pltpu.VMEM((A_KV_HEADS, SUBLANES, COLS_A), F32),
            pltpu.VMEM((2, 2, KEYS_A, LANES), BF16),
            pltpu.VMEM((2, A_HEAD_DIM + PACKED, KEYS_A), BF16),
            pltpu.VMEM((2, KEYS_A, CHUNK_PAIRS_A * TQ_A), F32),
            pltpu.VMEM((2, SUBLANES, CHUNK_PAIRS_A * TQ_A), F32),
            pltpu.VMEM((2, A_HEAD_DIM, CHUNK_PAIRS_A * TQ_A), F32),
        ],
        compiler_params=pltpu.CompilerParams(
            dimension_semantics=("arbitrary", "arbitrary"), vmem_limit_bytes=VMEM_LIMIT),
        name="swa",
    )(proj, proj, proj, bkt, rel_bias, sinks)


def _dsa_kernel(iq_ref, iwq_ref, ik_ref, q_ref, k_ref, v_ref, bkt_ref, rb_ref, o_ref,
                sc_ref, scb_ref, mb_ref, tab_ref, tri_ref, vt_ref, acc_ref, m_ref,
                sa_ref, smaxa_ref, sb_ref, smaxb_ref, *, topk, seq):
    i = pl.program_id(1)
    nkt = i + 1
    int32, int16 = jnp.int32, jnp.int16
    grp = TK_B // SUBLANES

    def hcols(h):
        return slice(h * B_HEAD_DIM, (h + 1) * B_HEAD_DIM)

    def ktile(kt):
        return pl.ds(pl.multiple_of(kt * TK_B, TK_B), TK_B)

    def rows3(a):
        return a.reshape(a.shape[0] // SUBLANES, SUBLANES, TQ_B)

    def all_rows(a, op):
        return jnp.broadcast_to(op(a, axis=0, keepdims=True), a.shape)

    @pl.when((pl.program_id(0) == 0) & (i == 0))
    def _():
        far_bucket = N_BUCKETS // 2 - 1
        for d in range(NEAR_B):
            for h in range(B_HEADS):
                tab_ref[d, h] = _bias_table(bkt_ref[d], rb_ref, A_HEADS + h, sub_row=far_bucket)
        tab_ref[NEAR_B] = jnp.zeros(tab_ref.shape[1:], F32)
        r = lax.broadcasted_iota(int32, (TK_B, TK_B), 0)
        c = lax.broadcasted_iota(int32, (TK_B, TK_B), 1)
        tri_ref[...] = jnp.where(c < r, 1.0, 0.0).astype(BF16)

    @pl.when(i == 0)
    def _():
        r = lax.broadcasted_iota(int32, (B_HEAD_DIM, B_HEAD_DIM), 0)
        c = lax.broadcasted_iota(int32, (B_HEAD_DIM, B_HEAD_DIM), 1)
        eye = jnp.where(r == c, 1.0, 0.0).astype(BF16)

        def body(kt, carry):
            for h in range(B_HEADS):
                vt_ref[kt, h] = _nt_dot(eye, v_ref[ktile(kt), hcols(h)]).astype(BF16)
            return carry

        lax.fori_loop(0, seq // TK_B, body, 0)

    key_chunk = lax.broadcasted_iota(int32, (TK_B, TQ_B), 0) // CHUNK
    qry_chunk = lax.broadcasted_iota(int32, (TK_B, TQ_B), 1) // CHUNK
    adm_diag = key_chunk <= qry_chunk

    r = lax.broadcasted_iota(int32, (IDX_HEADS * SUBLANES, LANES), 0)
    c = lax.broadcasted_iota(int32, (IDX_HEADS * SUBLANES, LANES), 1)
    pick_w = jnp.where(c == IDX_DIM + r // SUBLANES, 1.0, 0.0).astype(BF16)
    w_all = _nt_dot(pick_w, iwq_ref[...]) * (IDX_HEADS ** -0.5 * IDX_DIM ** -0.5)

    def for_tiles(n, body):
        def pair(j, carry):
            body(2 * j)
            body(2 * j + 1)
            return carry

        lax.fori_loop(0, n // 2, pair, 0)

        @pl.when(n % 2 == 1)
        def _():
            body(n - 1)

    def score_tile(kt):
        ikt = ik_ref[ktile(kt), 0:IDX_DIM]
        sc = jnp.zeros((grp, SUBLANES, TQ_B), F32)
        for h in range(IDX_HEADS):
            x = _nt_dot(ikt, iq_ref[:, h * IDX_DIM:(h + 1) * IDX_DIM])
            sc = sc + w_all[h * SUBLANES:(h + 1) * SUBLANES][None] * jnp.maximum(rows3(x), 0.0)
        store_score(kt, sc.reshape(TK_B, TQ_B))

    def store_score(kt, sc):
        sc_ref[kt] = sc
        scb_ref[kt] = sc.astype(BF16)

    for_tiles(nkt, score_tile)
    store_score(i, jnp.where(adm_diag, sc_ref[i], -jnp.inf))

    def key_to_f32(key):
        return pltpu.bitcast(jnp.where(key >= 0, key, INT_MIN - key), F32)

    def over_tiles(tile_count, zero):
        def pair(j, acc):
            return acc + (tile_count(2 * j) + tile_count(2 * j + 1))
        acc = lax.fori_loop(0, nkt // 2, pair, zero)
        return lax.cond(nkt % 2 == 1, lambda a: a + tile_count(nkt - 1), lambda a: a, acc)

    def tree_sum(parts):
        while len(parts) > 1:
            parts = [parts[n] + parts[n + 1] for n in range(0, len(parts), 2)]
        return parts[0]

    def count_bf16(cand):
        def tile_count(kt):
            blk = scb_ref[kt].reshape(TK_B // PACKED, PACKED, TQ_B)
            return tree_sum([jnp.where(blk[g] >= cand, jnp.ones((), int16), jnp.zeros((), int16))
                             for g in range(TK_B // PACKED)])
        acc = over_tiles(tile_count, jnp.zeros((PACKED, TQ_B), int16))
        return all_rows(acc.astype(int32), jnp.sum)

    def count_f32(pred):
        def tile_count(kt):
            blk = rows3(sc_ref[kt])
            return tree_sum([jnp.where(pred(blk[g]), 1, 0) for g in range(grp)])
        return all_rows(over_tiles(tile_count, jnp.zeros((SUBLANES, TQ_B), int32)), jnp.sum)

    def coarse_pass(p, u):
        bit = jnp.left_shift(jnp.int32(1), 15 - p)
        cand = key_to_f32(((u | bit) - HALF) << 16).astype(BF16)
        return jnp.where(count_bf16(cand) >= topk, u | bit, u)

    u = lax.fori_loop(0, 16, coarse_pass, jnp.zeros((PACKED, TQ_B), int32))
    coarse_key = ((u - HALF) << 16)[:SUBLANES]

    lo = jnp.maximum(coarse_key - HALF, KEY_NEG_INF)
    hi = jnp.minimum(coarse_key, KEY_POS_INF - 2 * HALF) + 2 * HALF

    def fine_pass(p, lohi):
        lo, hi = lohi
        mid = lo + ((hi - lo) >> 1)
        cand = key_to_f32(mid)
        ok = count_f32(lambda blk: blk >= cand) >= topk
        return jnp.where(ok, mid, lo), jnp.where(ok, hi, mid)

    lo, hi = lax.fori_loop(0, FINE_PASSES, fine_pass, (lo, hi))
    tau2d = key_to_f32(lo)
    tau = tau2d[None]

    need = (topk - count_f32(lambda blk: blk > tau2d)).astype(F32)[None]
    ones_l = jnp.ones((2 * SUBLANES, TK_B), BF16)

    def mask_tile(kt, run):
        blk = rows3(sc_ref[kt])
        eq = blk == tau
        eqf = jnp.where(eq, 1.0, 0.0).reshape(TK_B, TQ_B).astype(BF16)
        rank = rows3(jnp.dot(tri_ref[...], eqf, preferred_element_type=F32)) + run[None]
        sel = (blk > tau) | (eq & (rank < need))
        mb_ref[kt] = jnp.where(sel, 0.0, NEG).reshape(TK_B, TQ_B)
        return run + jnp.dot(ones_l, eqf, preferred_element_type=F32)[:SUBLANES]

    run = lax.fori_loop(0, nkt // 2, lambda j, run: mask_tile(2 * j + 1, mask_tile(2 * j, run)),
                        jnp.zeros((SUBLANES, TQ_B), F32))

    @pl.when(nkt % 2 == 1)
    def _():
        mask_tile(nkt - 1, run)

    mb_ref[i] = jnp.where(adm_diag, mb_ref[i], NEG)

    m_ref[...] = jnp.full(m_ref.shape, NEG, F32)
    acc_ref[...] = jnp.zeros(acc_ref.shape, F32)
    ones_rows = jnp.ones((PACKED, TK_B), BF16)

    buf_a, buf_b = (sa_ref, smaxa_ref), (sb_ref, smaxb_ref)

    def logits_stage(kt, buf, near):
        s_ref, smax_ref = buf
        mb = mb_ref[kt]
        for h in range(B_HEADS):
            s = _nt_dot(k_ref[ktile(kt), hcols(h)], q_ref[:, hcols(h)]) + mb
            if near:
                s = s + tab_ref[jnp.minimum(i - kt, NEAR_B), h]
            s_ref[h] = s
            smax_ref[h] = all_rows(jnp.max(rows3(s), axis=0), jnp.max)

    def update_stage(kt, buf):
        s_ref, smax_ref = buf
        for h in range(B_HEADS):
            m_old = m_ref[h]
            m_new = jnp.maximum(m_old, smax_ref[h])
            alpha = jnp.exp2(m_old - m_new)
            p = jnp.exp2(rows3(s_ref[h]) - m_new[None]).reshape(TK_B, TQ_B).astype(BF16)
            vaug = jnp.concatenate([vt_ref[kt, h], ones_rows], axis=0)
            pv = jnp.dot(vaug, p, preferred_element_type=F32)
            acc_ref[h] = (rows3(acc_ref[h]) * alpha[None] + rows3(pv)).reshape(acc_ref.shape[1:])
            m_ref[h] = m_new

    def pair_body(j, carry, near):
        logits_stage(2 * j + 1, buf_b, near)
        update_stage(2 * j, buf_a)
        logits_stage(2 * j + 2, buf_a, near)
        update_stage(2 * j + 1, buf_b)
        return carry

    near_lo = jnp.maximum(i - (NEAR_B - 1), 0)
    far_pairs = jnp.maximum(near_lo - 1, 0) // 2
    logits_stage(0, buf_a, True)
    lax.fori_loop(0, far_pairs, functools.partial(pair_body, near=False), 0)
    lax.fori_loop(far_pairs, (nkt - 1) // 2, functools.partial(pair_body, near=True), 0)

    @pl.when(nkt % 2 == 0)
    def _():
        logits_stage(nkt - 1, buf_b, True)
        update_stage(nkt - 2, buf_a)
        update_stage(nkt - 1, buf_b)

    @pl.when(nkt % 2 == 1)
    def _():
        update_stage(nkt - 1, buf_a)

    for h in range(B_HEADS):
        num = rows3(acc_ref[h, 0:B_HEAD_DIM, :])
        den = acc_ref[h, B_HEAD_DIM:B_HEAD_DIM + SUBLANES, :]
        out_t = (num / den[None]).reshape(B_HEAD_DIM, TQ_B)
        o_ref[:, h * B_HEAD_DIM:(h + 1) * B_HEAD_DIM] = out_t.T.astype(BF16)


def _dsa(proj, bkt, rel_bias, batch, seq):
    nt = seq // TQ_B
    topk = min(TOPK_MAX, seq // 4)

    def q_spec(width, col):
        return pl.BlockSpec((TQ_B, width), lambda b, i: (b * nt + i, col // width))

    def seq_spec(width, col):
        return pl.BlockSpec((seq, width), lambda b, i: (b, col // width))

    return pl.pallas_call(
        functools.partial(_dsa_kernel, topk=topk, seq=seq),
        grid=(batch, nt),
        in_specs=[
            q_spec(IDX_HEADS * IDX_DIM, COL_IQ),
            q_spec(LANES, COL_IKW),
            seq_spec(LANES, COL_IKW),
            q_spec(B_WIDTH, COL_BQ), seq_spec(B_WIDTH, COL_BK), seq_spec(B_WIDTH, COL_BV),
            pl.BlockSpec((NEAR_B, TK_B, TQ_B), lambda b, i: (0, 0, 0)),
            pl.BlockSpec(memory_space=pltpu.SMEM),
        ],
        out_specs=pl.BlockSpec((TQ_B, B_WIDTH), lambda b, i: (b * nt + i, 0)),
        out_shape=jax.ShapeDtypeStruct((batch * seq, B_WIDTH), BF16),
        scratch_shapes=[
            pltpu.VMEM((nt, TK_B, TQ_B), F32),
            pltpu.VMEM((nt, TK_B, TQ_B), BF16),
            pltpu.VMEM((nt, TK_B, TQ_B), F32),
            pltpu.VMEM((NEAR_B + 1, B_HEADS, TK_B, TQ_B), F32),
            pltpu.VMEM((TK_B, TK_B), BF16),
            pltpu.VMEM((seq // TK_B, B_HEADS, B_HEAD_DIM, TK_B), BF16),
            pltpu.VMEM((B_HEADS, B_HEAD_DIM + PACKED, TQ_B), F32),
            pltpu.VMEM((B_HEADS, SUBLANES, TQ_B), F32),
            pltpu.VMEM((B_HEADS, TK_B, TQ_B), F32),
            pltpu.VMEM((B_HEADS, SUBLANES, TQ_B), F32),
            pltpu.VMEM((B_HEADS, TK_B, TQ_B), F32),
            pltpu.VMEM((B_HEADS, SUBLANES, TQ_B), F32),
        ],
        compiler_params=pltpu.CompilerParams(
            dimension_semantics=("arbitrary", "arbitrary"), vmem_limit_bytes=VMEM_LIMIT),
        name="dsa",
    )(proj, proj, proj, proj, proj, proj, bkt, rel_bias)


def _outproj_kernel(oa_ref, ob_ref, cq_ref, mkv_ref, gate_ref, x_ref, w32_ref, g_ref, o_ref, w_ref):
    @pl.when(pl.program_id(0) == 0)
    def _():
        w_ref[...] = w32_ref[...].astype(BF16)

    ones_m = jnp.ones((N_MEM, C_HEAD_DIM), BF16)
    for r in range(TM_OUT // SUB_OUT):
        rows = slice(r * SUB_OUT, (r + 1) * SUB_OUT)
        att = [oa_ref[rows, :].astype(F32), ob_ref[rows, :].astype(F32)]
        for h in range(C_HEADS):
            hs = slice(h * C_HEAD_DIM, (h + 1) * C_HEAD_DIM)
            s = _nt_dot(cq_ref[rows, hs], mkv_ref[:, hs])
            p = jnp.exp2(s - jnp.max(s, axis=1, keepdims=True)).astype(BF16)
            vaug = jnp.concatenate([mkv_ref[:, C_WIDTH + h * C_HEAD_DIM:C_WIDTH + (h + 1) * C_HEAD_DIM], ones_m],
                                   axis=1)
            pv = jnp.dot(p, vaug, preferred_element_type=F32)
            att.append(pv[:, :C_HEAD_DIM] / pv[:, C_HEAD_DIM:])
        gate = gate_ref[rows, :].astype(F32)
        sg = gate / (1.0 + jnp.exp(-gate))
        y = (jnp.concatenate(att, axis=1) * sg).astype(BF16)
        h_new = x_ref[rows, :] + jnp.dot(y, w_ref[...], preferred_element_type=F32)
        ms = jnp.mean(h_new * h_new, axis=-1, keepdims=True)
        o_ref[rows, :] = h_new * lax.rsqrt(ms + EPS) * g_ref[...]


def _outproj(oa, ob, proj, mkv, x2d, w, g, seq):
    m = x2d.shape[0]
    steps_per_seq = seq // TM_OUT
    return pl.pallas_call(
        _outproj_kernel,
        grid=(m // TM_OUT,),
        in_specs=[
            pl.BlockSpec((TM_OUT, A_WIDTH), lambda i: (i, 0)),
            pl.BlockSpec((TM_OUT, B_WIDTH), lambda i: (i, 0)),
            pl.BlockSpec((TM_OUT, C_WIDTH), lambda i: (i, COL_CQ // C_WIDTH)),
            pl.BlockSpec((N_MEM, 2 * C_WIDTH), lambda i: (i // steps_per_seq, 0)),
            pl.BlockSpec((TM_OUT, MIX_WIDTH), lambda i: (i, COL_GATE // MIX_WIDTH)),
            pl.BlockSpec((TM_OUT, D_MODEL), lambda i: (i, 0)),
            pl.BlockSpec((MIX_WIDTH, D_MODEL), lambda i: (0, 0), pipeline_mode=pl.Buffered(1)),
            pl.BlockSpec((1, D_MODEL), lambda i: (0, 0)),
        ],
        out_specs=pl.BlockSpec((TM_OUT, D_MODEL), lambda i: (i, 0)),
        out_shape=jax.ShapeDtypeStruct((m, D_MODEL), F32),
        scratch_shapes=[pltpu.VMEM((MIX_WIDTH, D_MODEL), BF16)],
        compiler_params=pltpu.CompilerParams(
            dimension_semantics=("arbitrary",), vmem_limit_bytes=VMEM_LIMIT),
        name="outproj",
    )(oa, ob, proj, mkv, proj, x2d, w, g)


def _col_scale():
    cs = np.ones((1, PROJ_W), np.float32)
    cs[0, COL_AQ:COL_AQ + A_WIDTH] = A_HEAD_DIM ** -0.5 * LOG2E
    cs[0, COL_BQ:COL_BQ + B_WIDTH] = B_HEAD_DIM ** -0.5 * LOG2E
    cs[0, COL_CQ:COL_CQ + C_WIDTH] = C_HEAD_DIM ** -0.5 * LOG2E
    return jnp.asarray(cs)


def kernel(x, mem, g_norm, w_in, sinks, rel_bias, g_mem, w_mem_kv, w_out, g_final):
    batch, seq, _ = x.shape
    assert w_in.shape[0] == 1, "the out-projection kernel fuses the final norm of a single-layer trunk"
    kk = np.arange(2 * TQ_A)[:, None]
    qq = np.arange(TQ_A)[None, :]
    bkt_a = jnp.asarray(_t5_bucket_np(kk - TQ_A - qq))
    kk = np.arange(TK_B)[None, :, None]
    qq = np.arange(TQ_B)[None, None, :]
    d = np.arange(NEAR_B)[:, None, None]
    bkt_b = jnp.asarray(_t5_bucket_np(kk - qq - TK_B * d))

    h = x.reshape(batch * seq, D_MODEL)
    mem2d = mem.reshape(batch * N_MEM, D_MODEL)
    w_t = jnp.transpose(w_in, (2, 0, 1)).reshape(IN_WIDTH * KB_IN, LANES)
    proj = _inproj(h, g_norm[0].reshape(1, D_MODEL), _wprep(w_t), _col_scale())
    mkv = _memkv(mem2d, g_mem[0].reshape(1, D_MODEL), w_mem_kv[0])
    oa = _swa(proj, bkt_a, rel_bias, sinks[0], batch, seq)
    ob = _dsa(proj, bkt_b, rel_bias, batch, seq)
    out = _outproj(oa, ob, proj, mkv, h, w_out[0], g_final.reshape(1, D_MODEL), seq)
    return out.reshape(batch, seq, D_MODEL)
```

```python
import functools
import math

import numpy as np
import jax
import jax.numpy as jnp
from jax import lax
from jax.experimental import pallas as pl
from jax.experimental.pallas import tpu as pltpu

D_MODEL = 2048
CHUNK = 64
N_MEM = 256
EPS = 1e-6
A_HEADS = 16
A_KV_HEADS = 2
A_HEAD_DIM = 64
WINDOW_CHUNKS = 2
A_WIDTH = A_HEADS * A_HEAD_DIM
B_HEADS = 4
B_HEAD_DIM = 128
B_WIDTH = B_HEADS * B_HEAD_DIM
IDX_HEADS = 4
IDX_DIM = 64
TOPK_MAX = 256
C_HEADS = 4
C_HEAD_DIM = 128
C_WIDTH = C_HEADS * C_HEAD_DIM
MIX_WIDTH = A_WIDTH + B_WIDTH + C_WIDTH
N_BUCKETS = 32
MAX_DISTANCE = 1024
KV_A = A_KV_HEADS * A_HEAD_DIM
SPLIT_SIZES = (A_WIDTH, KV_A, KV_A, B_WIDTH, B_WIDTH, B_WIDTH,
               IDX_HEADS * IDX_DIM, IDX_DIM, IDX_HEADS, C_WIDTH, MIX_WIDTH)
IN_WIDTH = sum(SPLIT_SIZES)

F32 = jnp.float32
BF16 = jnp.bfloat16
LOG2E = math.log2(math.e)
NEG = -1e30
INT_MIN = -(2 ** 31)
LANES = 128
SUBLANES = 8
PACKED = 16
HALF = 1 << 15
KEY_POS_INF = 0x7F800000
KEY_NEG_INF = -KEY_POS_INF
FINE_PASSES = 17

(SRC_AQ, SRC_AK, SRC_AV, SRC_BQ, SRC_BK, SRC_BV,
 SRC_IQ, SRC_IK, SRC_IW, SRC_CQ, SRC_GATE) = (int(c) for c in np.cumsum((0,) + SPLIT_SIZES)[:-1])
COL_GATE = 0
COL_AQ = COL_GATE + MIX_WIDTH
COL_BQ = COL_AQ + A_WIDTH
COL_BK = COL_BQ + B_WIDTH
COL_BV = COL_BK + B_WIDTH
COL_CQ = COL_BV + B_WIDTH
COL_AKV = COL_CQ + C_WIDTH
COL_IQ = COL_AKV + 2 * KV_A
COL_IKW = COL_IQ + IDX_HEADS * IDX_DIM
PROJ_W = 6144
SEGMENTS = ((COL_GATE, SRC_GATE, MIX_WIDTH), (COL_AQ, SRC_AQ, A_WIDTH), (COL_BQ, SRC_BQ, B_WIDTH),
            (COL_BK, SRC_BK, B_WIDTH), (COL_BV, SRC_BV, B_WIDTH), (COL_CQ, SRC_CQ, C_WIDTH),
            (COL_AKV, SRC_AK, 2 * KV_A), (COL_IQ, SRC_IQ, IDX_HEADS * IDX_DIM), (COL_IKW, SRC_IK, 256))
KB_IN = D_MODEL // LANES

TC_PREP = 256
TM_IN, TN_IN = 1024, 1024
SUB_IN = 256
LAST_USED_IN = COL_IKW + 256 - (PROJ_W - TN_IN)
TQ_A = 128
RB_A = 512
TQ_B = 256
TK_B = 256
NEAR_B = 4
TM_OUT = 512
SUB_OUT = 256
VMEM_LIMIT = 56 * 1024 * 1024


def _t5_bucket_np(rel):
    nb = N_BUCKETS // 2
    max_exact = nb // 2
    side = np.where(rel > 0, nb, 0)
    n = np.abs(rel)
    nf = np.maximum(n, max_exact).astype(np.float32)
    large = max_exact + (np.log(nf / max_exact) / math.log(MAX_DISTANCE / max_exact)
                         * (nb - max_exact)).astype(np.int32)
    large = np.minimum(large, nb - 1)
    return (side + np.where(n < max_exact, n, large)).astype(np.int32)


def _nt_dot(a, b):
    return lax.dot_general(a, b, (((1,), (1,)), ((), ())), preferred_element_type=F32)


def _bias_table(bucket, rb_ref, col, sub_row=None):
    acc = jnp.zeros(bucket.shape, F32)
    for b in range(N_BUCKETS):
        val = rb_ref[b, col]
        if sub_row is not None:
            val = val - rb_ref[sub_row, col]
        acc = jnp.where(bucket == b, val * LOG2E, acc)
    return acc


def _wprep_kernel(src_ref, w_ref, o_ref):
    t = pl.program_id(0)

    @pl.when(src_ref[t] >= 0)
    def _():
        for kb in range(KB_IN):
            o_ref[:, kb * LANES:(kb + 1) * LANES] = w_ref[pl.ds(kb, TC_PREP, stride=KB_IN), :].astype(BF16)

    @pl.when(src_ref[t] < 0)
    def _():
        o_ref[...] = jnp.zeros(o_ref.shape, BF16)


def _wprep(w_t):
    src = np.full((PROJ_W // TC_PREP,), -1, np.int32)
    for dst_col, src_col, width in SEGMENTS:
        for n in range(width // TC_PREP):
            src[dst_col // TC_PREP + n] = src_col + n * TC_PREP
    return pl.pallas_call(
        _wprep_kernel,
        grid_spec=pltpu.PrefetchScalarGridSpec(
            num_scalar_prefetch=1,
            grid=(PROJ_W // TC_PREP,),
            in_specs=[pl.BlockSpec((pl.Element(TC_PREP * KB_IN), pl.Element(LANES)),
                                   lambda t, src: (jnp.maximum(src[t], 0) * KB_IN, 0))],
            out_specs=pl.BlockSpec((TC_PREP, D_MODEL), lambda t, src: (t, 0)),
        ),
        out_shape=jax.ShapeDtypeStruct((PROJ_W, D_MODEL), BF16),
        compiler_params=pltpu.CompilerParams(
            dimension_semantics=("arbitrary",), vmem_limit_bytes=VMEM_LIMIT),
        name="wprep",
    )(jnp.asarray(src), w_t)


def _inproj_kernel(x_ref, g_ref, wt_ref, cs_ref, o_ref, hn_ref):
    j = pl.program_id(1)

    @pl.when(j == 0)
    def _():
        for r in range(TM_IN // SUB_IN):
            rows = slice(r * SUB_IN, (r + 1) * SUB_IN)
            x = x_ref[rows, :]
            ms = jnp.mean(x * x, axis=-1, keepdims=True)
            hn = (x * lax.rsqrt(ms + EPS) * g_ref[...]).astype(BF16)
            hn_ref[rows, :] = hn
            o_ref[rows, :] = (_nt_dot(hn, wt_ref[...]) * cs_ref[...]).astype(BF16)

    last = pl.num_programs(1) - 1

    @pl.when((j > 0) & (j < last))
    def _():
        o_ref[...] = (_nt_dot(hn_ref[...], wt_ref[...]) * cs_ref[...]).astype(BF16)

    @pl.when(j == last)
    def _():
        acc = _nt_dot(hn_ref[...], wt_ref[0:LAST_USED_IN, :])
        o_ref[:, 0:LAST_USED_IN] = (acc * cs_ref[:, 0:LAST_USED_IN]).astype(BF16)
        o_ref[:, LAST_USED_IN:] = jnp.zeros((TM_IN, TN_IN - LAST_USED_IN), BF16)


def _inproj(x2d, g, wt, cs):
    m = x2d.shape[0]
    return pl.pallas_call(
        _inproj_kernel,
        grid=(m // TM_IN, PROJ_W // TN_IN),
        in_specs=[
            pl.BlockSpec((TM_IN, D_MODEL), lambda i, j: (i, 0)),
            pl.BlockSpec((1, D_MODEL), lambda i, j: (0, 0)),
            pl.BlockSpec((TN_IN, D_MODEL), lambda i, j: (j, 0)),
            pl.BlockSpec((1, TN_IN), lambda i, j: (0, j)),
        ],
        out_specs=pl.BlockSpec((TM_IN, TN_IN), lambda i, j: (i, j)),
        out_shape=jax.ShapeDtypeStruct((m, PROJ_W), BF16),
        scratch_shapes=[pltpu.VMEM((TM_IN, D_MODEL), BF16)],
        compiler_params=pltpu.CompilerParams(
            dimension_semantics=("parallel", "arbitrary"), vmem_limit_bytes=VMEM_LIMIT),
        name="inproj",
    )(x2d, g, wt, cs)


def _memkv_kernel(m_ref, g_ref, w_ref, o_ref):
    x = m_ref[...]
    ms = jnp.mean(x * x, axis=-1, keepdims=True)
    hn = (x * lax.rsqrt(ms + EPS) * g_ref[...]).astype(BF16)
    o_ref[...] = jnp.dot(hn, w_ref[...].astype(BF16), preferred_element_type=F32).astype(BF16)


def _memkv(mem2d, g, w):
    m = mem2d.shape[0]
    return pl.pallas_call(
        _memkv_kernel,
        grid=(m // N_MEM,),
        in_specs=[
            pl.BlockSpec((N_MEM, D_MODEL), lambda i: (i, 0)),
            pl.BlockSpec((1, D_MODEL), lambda i: (0, 0)),
            pl.BlockSpec((D_MODEL, 2 * C_WIDTH), lambda i: (0, 0)),
        ],
        out_specs=pl.BlockSpec((N_MEM, 2 * C_WIDTH), lambda i: (i, 0)),
        out_shape=jax.ShapeDtypeStruct((m, 2 * C_WIDTH), BF16),
        compiler_params=pltpu.CompilerParams(
            dimension_semantics=("arbitrary",), vmem_limit_bytes=VMEM_LIMIT),
        name="memkv",
    )(mem2d, g, w)


GRP_A = A_HEADS // A_KV_HEADS
PAIRS_A = GRP_A // 2
COLS_A = GRP_A * TQ_A
KEYS_A = 2 * TQ_A
CHUNK_PAIRS_A = 4


def _swa_kernel(q_ref, kvp_ref, kvc_ref, bkt_ref, rb_ref, sink_ref, o_ref,
                tab_ref, sinkv_ref, kp_ref, vt_ref, s_ref, m_ref, ot_ref):
    i = pl.program_id(1)

    @pl.when((pl.program_id(0) == 0) & (i == 0))
    def _():
        bkt = bkt_ref[...]
        kchunk = lax.broadcasted_iota(jnp.int32, bkt.shape, 0) // CHUNK
        qchunk = lax.broadcasted_iota(jnp.int32, bkt.shape, 1) // CHUNK
        allowed = (kchunk >= qchunk) & (kchunk <= qchunk + WINDOW_CHUNKS)
        has_prev = lax.broadcasted_iota(jnp.int32, bkt.shape, 0) >= TQ_A
        for g in range(A_KV_HEADS):
            for col in range(GRP_A):
                h = g * GRP_A + 2 * (col % PAIRS_A) + col // PAIRS_A
                cols = slice(col * TQ_A, (col + 1) * TQ_A)
                t = jnp.where(allowed, _bias_table(bkt, rb_ref, h), NEG)
                tab_ref[0, g, :, cols] = t
                tab_ref[1, g, :, cols] = jnp.where(has_prev, t, NEG)
                sinkv_ref[g, :, cols] = jnp.full((SUBLANES, TQ_A), sink_ref[h] * LOG2E, F32)

    width = CHUNK_PAIRS_A * TQ_A
    tasks = [(t, g, p0, parity) for t in range(RB_A // TQ_A) for g in range(A_KV_HEADS)
             for p0 in range(0, PAIRS_A, CHUNK_PAIRS_A) for parity in range(2)]

    def prepare(t, g):
        slot = (t * A_KV_HEADS + g) % 2
        if t == 0:
            kv_t = jnp.concatenate([kvp_ref[...], kvc_ref[0:TQ_A, :]], axis=0)
        else:
            kv_t = kvc_ref[(t - 1) * TQ_A:(t + 1) * TQ_A, :]
        kg = kv_t[:, g * A_HEAD_DIM:(g + 1) * A_HEAD_DIM]
        zeros = jnp.zeros((KEYS_A, A_HEAD_DIM), BF16)
        kp_ref[slot, 0] = jnp.concatenate([kg, zeros], axis=1)
        kp_ref[slot, 1] = jnp.concatenate([zeros, kg], axis=1)
        r = lax.broadcasted_iota(jnp.int32, (A_HEAD_DIM, KV_A), 0)
        c = lax.broadcasted_iota(jnp.int32, (A_HEAD_DIM, KV_A), 1)
        pick = jnp.where(c == g * A_HEAD_DIM + r, 1.0, 0.0).astype(BF16)
        vt = _nt_dot(pick, kv_t[:, KV_A:]).astype(BF16)
        vt_ref[slot] = jnp.concatenate([vt, jnp.ones((PACKED, KEYS_A), BF16)], axis=0)

    def chunk_cols(p0, parity):
        lo = (parity * PAIRS_A + p0) * TQ_A
        return slice(lo, lo + width)

    def logits_stage(n):
        t, g, p0, parity = tasks[n]
        if (p0, parity) == (0, 0):
            prepare(t, g)
        slot, buf, cols = (t * A_KV_HEADS + g) % 2, n % 2, chunk_cols(p0, parity)
        q_pairs = jnp.concatenate(
            [q_ref[t * TQ_A:(t + 1) * TQ_A, (g * PAIRS_A + p) * LANES:(g * PAIRS_A + p + 1) * LANES]
             for p in range(p0, p0 + CHUNK_PAIRS_A)], axis=0)
        variant = jnp.where(i == 0, 1, 0) if t == 0 else 0
        s = _nt_dot(kp_ref[slot, parity], q_pairs) + tab_ref[variant, g, :, cols]
        s_ref[buf] = s
        m = jnp.max(s.reshape(KEYS_A // SUBLANES, SUBLANES, width), axis=0)
        m_ref[buf] = jnp.maximum(jnp.broadcast_to(jnp.max(m, axis=0, keepdims=True), m.shape),
                                 sinkv_ref[g, :, cols])

    def update_stage(n):
        t, g, p0, parity = tasks[n]
        slot, buf, cols = (t * A_KV_HEADS + g) % 2, n % 2, chunk_cols(p0, parity)
        m = m_ref[buf]
        s = s_ref[buf].reshape(KEYS_A // SUBLANES, SUBLANES, width)
        p = jnp.exp2(s - m[None]).reshape(KEYS_A, width).astype(BF16)
        pv = jnp.dot(vt_ref[slot], p, preferred_element_type=F32)
        den = pv[A_HEAD_DIM:A_HEAD_DIM + SUBLANES] + jnp.exp2(sinkv_ref[g, :, cols] - m)
        out_t = pv[:A_HEAD_DIM].reshape(A_HEAD_DIM // SUBLANES, SUBLANES, width) / den[None]
        ot_ref[parity] = out_t.reshape(A_HEAD_DIM, width)
        if parity == 1:
            for k in range(CHUNK_PAIRS_A):
                blk = jnp.concatenate([ot_ref[0, :, k * TQ_A:(k + 1) * TQ_A],
                                       ot_ref[1, :, k * TQ_A:(k + 1) * TQ_A]], axis=0)
                lanes = slice((g * PAIRS_A + p0 + k) * LANES, (g * PAIRS_A + p0 + k + 1) * LANES)
                o_ref[t * TQ_A:(t + 1) * TQ_A, lanes] = blk.T.astype(BF16)

    logits_stage(0)
    for n in range(len(tasks)):
        if n + 1 < len(tasks):
            logits_stage(n + 1)
        update_stage(n)


def _swa(proj, bkt, rel_bias, sinks, batch, seq):
    nt = seq // RB_A
    sub = RB_A // TQ_A
    kv_blk = COL_AKV // (2 * KV_A)
    return pl.pallas_call(
        _swa_kernel,
        grid=(batch, nt),
        in_specs=[
            pl.BlockSpec((RB_A, A_WIDTH), lambda b, i: (b * nt + i, COL_AQ // A_WIDTH)),
            pl.BlockSpec((TQ_A, 2 * KV_A), lambda b, i: (jnp.maximum((b * nt + i) * sub - 1, 0), kv_blk)),
            pl.BlockSpec((RB_A, 2 * KV_A), lambda b, i: (b * nt + i, kv_blk)),
            pl.BlockSpec((KEYS_A, TQ_A), lambda b, i: (0, 0)),
            pl.BlockSpec(memory_space=pltpu.SMEM),
            pl.BlockSpec(memory_space=pltpu.SMEM),
        ],
        out_specs=pl.BlockSpec((RB_A, A_WIDTH), lambda b, i: (b * nt + i, 0)),
        out_shape=jax.ShapeDtypeStruct((batch * seq, A_WIDTH), BF16),
        scratch_shapes=[
            pltpu.VMEM((2, A_KV_HEADS, KEYS_A, COLS_A), F32),
            pltpu.VMEM((A_KV_HEADS, SUBLANES, COLS_A), F32),
            pltpu.VMEM((2, 2, KEYS_A, LANES), BF16),
            pltpu.VMEM((2, A_HEAD_DIM + PACKED, KEYS_A), BF16),
            pltpu.VMEM((2, KEYS_A, CHUNK_PAIRS_A * TQ_A), F32),
            pltpu.VMEM((2, SUBLANES, CHUNK_PAIRS_A * TQ_A), F32),
            pltpu.VMEM((2, A_HEAD_DIM, CHUNK_PAIRS_A * TQ_A), F32),
        ],
        compiler_params=pltpu.CompilerParams(
            dimension_semantics=("arbitrary", "arbitrary"), vmem_limit_bytes=VMEM_LIMIT),
        name="swa",
    )(proj, proj, proj, bkt, rel_bias, sinks)


def _dsa_kernel(iq_ref, iwq_ref, ik_ref, q_ref, k_ref, v_ref, bkt_ref, rb_ref, o_ref,
                sc_ref, scb_ref, mb_ref, tab_ref, tri_ref, vt_ref, acc_ref, m_ref,
                sa_ref, smaxa_ref, sb_ref, smaxb_ref, *, topk, seq):
    i = pl.program_id(1)
    nkt = i + 1
    int32, int16 = jnp.int32, jnp.int16
    grp = TK_B // SUBLANES

    def hcols(h):
        return slice(h * B_HEAD_DIM, (h + 1) * B_HEAD_DIM)

    def ktile(kt):
        return pl.ds(pl.multiple_of(kt * TK_B, TK_B), TK_B)

    def rows3(a):
        return a.reshape(a.shape[0] // SUBLANES, SUBLANES, TQ_B)

    def all_rows(a, op):
        return jnp.broadcast_to(op(a, axis=0, keepdims=True), a.shape)

    @pl.when((pl.program_id(0) == 0) & (i == 0))
    def _():
        far_bucket = N_BUCKETS // 2 - 1
        for d in range(NEAR_B):
            for h in range(B_HEADS):
                tab_ref[d, h] = _bias_table(bkt_ref[d], rb_ref, A_HEADS + h, sub_row=far_bucket)
        tab_ref[NEAR_B] = jnp.zeros(tab_ref.shape[1:], F32)
        r = lax.broadcasted_iota(int32, (TK_B, TK_B), 0)
        c = lax.broadcasted_iota(int32, (TK_B, TK_B), 1)
        tri_ref[...] = jnp.where(c < r, 1.0, 0.0).astype(BF16)

    @pl.when(i == 0)
    def _():
        r = lax.broadcasted_iota(int32, (B_HEAD_DIM, B_HEAD_DIM), 0)
        c = lax.broadcasted_iota(int32, (B_HEAD_DIM, B_HEAD_DIM), 1)
        eye = jnp.where(r == c, 1.0, 0.0).astype(BF16)

        def body(kt, carry):
            for h in range(B_HEADS):
                vt_ref[kt, h] = _nt_dot(eye, v_ref[ktile(kt), hcols(h)]).astype(BF16)
            return carry

        lax.fori_loop(0, seq // TK_B, body, 0)

    key_chunk = lax.broadcasted_iota(int32, (TK_B, TQ_B), 0) // CHUNK
    qry_chunk = lax.broadcasted_iota(int32, (TK_B, TQ_B), 1) // CHUNK
    adm_diag = key_chunk <= qry_chunk

    r = lax.broadcasted_iota(int32, (IDX_HEADS * SUBLANES, LANES), 0)
    c = lax.broadcasted_iota(int32, (IDX_HEADS * SUBLANES, LANES), 1)
    pick_w = jnp.where(c == IDX_DIM + r // SUBLANES, 1.0, 0.0).astype(BF16)
    w_all = _nt_dot(pick_w, iwq_ref[...]) * (IDX_HEADS ** -0.5 * IDX_DIM ** -0.5)

    def for_tiles(n, body, unroll=2):
        def group(j, carry):
            for k in range(unroll):
                body(unroll * j + k)
            return carry

        def single(kt, carry):
            body(kt)
            return carry

        lax.fori_loop(0, n // unroll, group, 0)
        lax.fori_loop(n - n % unroll, n, single, 0)

    def score_tile(kt):
        ikt = ik_ref[ktile(kt), 0:IDX_DIM]
        sc = jnp.zeros((grp, SUBLANES, TQ_B), F32)
        for h in range(IDX_HEADS):
            x = _nt_dot(ikt, iq_ref[:, h * IDX_DIM:(h + 1) * IDX_DIM])
            sc = sc + w_all[h * SUBLANES:(h + 1) * SUBLANES][None] * jnp.maximum(rows3(x), 0.0)
        store_score(kt, sc.reshape(TK_B, TQ_B))

    def store_score(kt, sc):
        sc_ref[kt] = sc
        scb_ref[kt] = sc.astype(BF16)

    for_tiles(nkt, score_tile, unroll=4)
    store_score(i, jnp.where(adm_diag, sc_ref[i], -jnp.inf))

    def key_to_f32(key):
        return pltpu.bitcast(jnp.where(key >= 0, key, INT_MIN - key), F32)

    def over_tiles(tile_count, zero):
        def pair(j, acc):
            return acc + (tile_count(2 * j) + tile_count(2 * j + 1))
        acc = lax.fori_loop(0, nkt // 2, pair, zero)
        return lax.cond(nkt % 2 == 1, lambda a: a + tile_count(nkt - 1), lambda a: a, acc)

    def tree_sum(parts):
        while len(parts) > 1:
            parts = [parts[n] + parts[n + 1] for n in range(0, len(parts), 2)]
        return parts[0]

    def count_bf16(cand):
        def tile_count(kt):
            blk = scb_ref[kt].reshape(TK_B // PACKED, PACKED, TQ_B)
            return tree_sum([jnp.where(blk[g] >= cand, jnp.ones((), int16), jnp.zeros((), int16))
                             for g in range(TK_B // PACKED)])
        acc = over_tiles(tile_count, jnp.zeros((PACKED, TQ_B), int16))
        return all_rows(acc.astype(int32), jnp.sum)

    def count_f32(pred):
        def tile_count(kt):
            blk = rows3(sc_ref[kt])
            return tree_sum([jnp.where(pred(blk[g]), 1, 0) for g in range(grp)])
        return all_rows(over_tiles(tile_count, jnp.zeros((SUBLANES, TQ_B), int32)), jnp.sum)

    def coarse_pass(p, u):
        bit = jnp.left_shift(jnp.int32(1), 15 - p)
        cand = key_to_f32(((u | bit) - HALF) << 16).astype(BF16)
        return jnp.where(count_bf16(cand) >= topk, u | bit, u)

    u = lax.fori_loop(0, 16, coarse_pass, jnp.zeros((PACKED, TQ_B), int32))
    coarse_key = ((u - HALF) << 16)[:SUBLANES]

    lo = jnp.maximum(coarse_key - HALF, KEY_NEG_INF)
    hi = jnp.minimum(coarse_key, KEY_POS_INF - 2 * HALF) + 2 * HALF

    def fine_pass(p, lohi):
        lo, hi = lohi
        mid = lo + ((hi - lo) >> 1)
        cand = key_to_f32(mid)
        ok = count_f32(lambda blk: blk >= cand) >= topk
        return jnp.where(ok, mid, lo), jnp.where(ok, hi, mid)

    lo, hi = lax.fori_loop(0, FINE_PASSES, fine_pass, (lo, hi))
    tau2d = key_to_f32(lo)
    tau = tau2d[None]

    need = (topk - count_f32(lambda blk: blk > tau2d)).astype(F32)[None]
    ones_l = jnp.ones((2 * SUBLANES, TK_B), BF16)

    def mask_tile(kt, run):
        blk = rows3(sc_ref[kt])
        eq = blk == tau
        eqf = jnp.where(eq, 1.0, 0.0).reshape(TK_B, TQ_B).astype(BF16)
        rank = rows3(jnp.dot(tri_ref[...], eqf, preferred_element_type=F32)) + run[None]
        sel = (blk > tau) | (eq & (rank < need))
        mb_ref[kt] = jnp.where(sel, 0.0, NEG).reshape(TK_B, TQ_B)
        return run + jnp.dot(ones_l, eqf, preferred_element_type=F32)[:SUBLANES]

    def mask_group(j, run):
        for k in range(4):
            run = mask_tile(4 * j + k, run)
        return run

    run = lax.fori_loop(0, nkt // 4, mask_group, jnp.zeros((SUBLANES, TQ_B), F32))
    lax.fori_loop(nkt - nkt % 4, nkt, mask_tile, run)

    mb_ref[i] = jnp.where(adm_diag, mb_ref[i], NEG)

    m_ref[...] = jnp.full(m_ref.shape, NEG, F32)
    acc_ref[...] = jnp.zeros(acc_ref.shape, F32)
    ones_rows = jnp.ones((PACKED, TK_B), BF16)

    buf_a, buf_b = (sa_ref, smaxa_ref), (sb_ref, smaxb_ref)

    def logits_stage(kt, buf, near):
        s_ref, smax_ref = buf
        mb = mb_ref[kt]
        for h in range(B_HEADS):
            s = _nt_dot(k_ref[ktile(kt), hcols(h)], q_ref[:, hcols(h)]) + mb
            if near:
                s = s + tab_ref[jnp.minimum(i - kt, NEAR_B), h]
            s_ref[h] = s
            smax_ref[h] = all_rows(jnp.max(rows3(s), axis=0), jnp.max)

    def update_stage(kt, buf):
        s_ref, smax_ref = buf
        for h in range(B_HEADS):
            m_old = m_ref[h]
            m_new = jnp.maximum(m_old, smax_ref[h])
            alpha = jnp.exp2(m_old - m_new)
            p = jnp.exp2(rows3(s_ref[h]) - m_new[None]).reshape(TK_B, TQ_B).astype(BF16)
            vaug = jnp.concatenate([vt_ref[kt, h], ones_rows], axis=0)
            pv = jnp.dot(vaug, p, preferred_element_type=F32)
            acc_ref[h] = (rows3(acc_ref[h]) * alpha[None] + rows3(pv)).reshape(acc_ref.shape[1:])
            m_ref[h] = m_new

    def pair_body(j, carry, near):
        logits_stage(2 * j + 1, buf_b, near)
        update_stage(2 * j, buf_a)
        logits_stage(2 * j + 2, buf_a, near)
        update_stage(2 * j + 1, buf_b)
        return carry

    near_lo = jnp.maximum(i - (NEAR_B - 1), 0)
    far_pairs = jnp.maximum(near_lo - 1, 0) // 2
    logits_stage(0, buf_a, True)
    lax.fori_loop(0, far_pairs, functools.partial(pair_body, near=False), 0)
    lax.fori_loop(far_pairs, (nkt - 1) // 2, functools.partial(pair_body, near=True), 0)

    @pl.when(nkt % 2 == 0)
    def _():
        logits_stage(nkt - 1, buf_b, True)
        update_stage(nkt - 2, buf_a)
        update_stage(nkt - 1, buf_b)

    @pl.when(nkt % 2 == 1)
    def _():
        update_stage(nkt - 1, buf_a)

    for h in range(B_HEADS):
        num = rows3(acc_ref[h, 0:B_HEAD_DIM, :])
        den = acc_ref[h, B_HEAD_DIM:B_HEAD_DIM + SUBLANES, :]
        out_t = (num / den[None]).reshape(B_HEAD_DIM, TQ_B)
        o_ref[:, h * B_HEAD_DIM:(h + 1) * B_HEAD_DIM] = out_t.T.astype(BF16)


def _dsa(proj, bkt, rel_bias, batch, seq):
    nt = seq // TQ_B
    topk = min(TOPK_MAX, seq // 4)

    def q_spec(width, col):
        return pl.BlockSpec((TQ_B, width), lambda b, i: (b * nt + i, col // width))

    def seq_spec(width, col):
        return pl.BlockSpec((seq, width), lambda b, i: (b, col // width))

    return pl.pallas_call(
        functools.partial(_dsa_kernel, topk=topk, seq=seq),
        grid=(batch, nt),
        in_specs=[
            q_spec(IDX_HEADS * IDX_DIM, COL_IQ),
            q_spec(LANES, COL_IKW),
            seq_spec(LANES, COL_IKW),
            q_spec(B_WIDTH, COL_BQ), seq_spec(B_WIDTH, COL_BK), seq_spec(B_WIDTH, COL_BV),
            pl.BlockSpec((NEAR_B, TK_B, TQ_B), lambda b, i: (0, 0, 0)),
            pl.BlockSpec(memory_space=pltpu.SMEM),
        ],
        out_specs=pl.BlockSpec((TQ_B, B_WIDTH), lambda b, i: (b * nt + i, 0)),
        out_shape=jax.ShapeDtypeStruct((batch * seq, B_WIDTH), BF16),
        scratch_shapes=[
            pltpu.VMEM((nt, TK_B, TQ_B), F32),
            pltpu.VMEM((nt, TK_B, TQ_B), BF16),
            pltpu.VMEM((nt, TK_B, TQ_B), F32),
            pltpu.VMEM((NEAR_B + 1, B_HEADS, TK_B, TQ_B), F32),
            pltpu.VMEM((TK_B, TK_B), BF16),
            pltpu.VMEM((seq // TK_B, B_HEADS, B_HEAD_DIM, TK_B), BF16),
            pltpu.VMEM((B_HEADS, B_HEAD_DIM + PACKED, TQ_B), F32),
            pltpu.VMEM((B_HEADS, SUBLANES, TQ_B), F32),
            pltpu.VMEM((B_HEADS, TK_B, TQ_B), F32),
            pltpu.VMEM((B_HEADS, SUBLANES, TQ_B), F32),
            pltpu.VMEM((B_HEADS, TK_B, TQ_B), F32),
            pltpu.VMEM((B_HEADS, SUBLANES, TQ_B), F32),
        ],
        compiler_params=pltpu.CompilerParams(
            dimension_semantics=("arbitrary", "arbitrary"), vmem_limit_bytes=VMEM_LIMIT),
        name="dsa",
    )(proj, proj, proj, proj, proj, proj, bkt, rel_bias)


def _outproj_kernel(oa_ref, ob_ref, cq_ref, mkv_ref, gate_ref, x_ref, w32_ref, g_ref, o_ref, w_ref):
    @pl.when(pl.program_id(0) == 0)
    def _():
        w_ref[...] = w32_ref[...].astype(BF16)

    ones_m = jnp.ones((N_MEM, C_HEAD_DIM), BF16)
    for r in range(TM_OUT // SUB_OUT):
        rows = slice(r * SUB_OUT, (r + 1) * SUB_OUT)
        att = [oa_ref[rows, :].astype(F32), ob_ref[rows, :].astype(F32)]
        for h in range(C_HEADS):
            hs = slice(h * C_HEAD_DIM, (h + 1) * C_HEAD_DIM)
            s = _nt_dot(cq_ref[rows, hs], mkv_ref[:, hs])
            p = jnp.exp2(s - jnp.max(s, axis=1, keepdims=True)).astype(BF16)
            vaug = jnp.concatenate([mkv_ref[:, C_WIDTH + h * C_HEAD_DIM:C_WIDTH + (h + 1) * C_HEAD_DIM], ones_m],
                                   axis=1)
            pv = jnp.dot(p, vaug, preferred_element_type=F32)
            att.append(pv[:, :C_HEAD_DIM] / pv[:, C_HEAD_DIM:])
        gate = gate_ref[rows, :].astype(F32)
        sg = gate / (1.0 + jnp.exp(-gate))
        y = (jnp.concatenate(att, axis=1) * sg).astype(BF16)
        h_new = x_ref[rows, :] + jnp.dot(y, w_ref[...], preferred_element_type=F32)
        ms = jnp.mean(h_new * h_new, axis=-1, keepdims=True)
        o_ref[rows, :] = h_new * lax.rsqrt(ms + EPS) * g_ref[...]


def _outproj(oa, ob, proj, mkv, x2d, w, g, seq):
    m = x2d.shape[0]
    steps_per_seq = seq // TM_OUT
    return pl.pallas_call(
        _outproj_kernel,
        grid=(m // TM_OUT,),
        in_specs=[
            pl.BlockSpec((TM_OUT, A_WIDTH), lambda i: (i, 0)),
            pl.BlockSpec((TM_OUT, B_WIDTH), lambda i: (i, 0)),
            pl.BlockSpec((TM_OUT, C_WIDTH), lambda i: (i, COL_CQ // C_WIDTH)),
            pl.BlockSpec((N_MEM, 2 * C_WIDTH), lambda i: (i // steps_per_seq, 0)),
            pl.BlockSpec((TM_OUT, MIX_WIDTH), lambda i: (i, COL_GATE // MIX_WIDTH)),
            pl.BlockSpec((TM_OUT, D_MODEL), lambda i: (i, 0)),
            pl.BlockSpec((MIX_WIDTH, D_MODEL), lambda i: (0, 0), pipeline_mode=pl.Buffered(1)),
            pl.BlockSpec((1, D_MODEL), lambda i: (0, 0)),
        ],
        out_specs=pl.BlockSpec((TM_OUT, D_MODEL), lambda i: (i, 0)),
        out_shape=jax.ShapeDtypeStruct((m, D_MODEL), F32),
        scratch_shapes=[pltpu.VMEM((MIX_WIDTH, D_MODEL), BF16)],
        compiler_params=pltpu.CompilerParams(
            dimension_semantics=("arbitrary",), vmem_limit_bytes=VMEM_LIMIT),
        name="outproj",
    )(oa, ob, proj, mkv, proj, x2d, w, g)


def _col_scale():
    cs = np.ones((1, PROJ_W), np.float32)
    cs[0, COL_AQ:COL_AQ + A_WIDTH] = A_HEAD_DIM ** -0.5 * LOG2E
    cs[0, COL_BQ:COL_BQ + B_WIDTH] = B_HEAD_DIM ** -0.5 * LOG2E
    cs[0, COL_CQ:COL_CQ + C_WIDTH] = C_HEAD_DIM ** -0.5 * LOG2E
    return jnp.asarray(cs)


def kernel(x, mem, g_norm, w_in, sinks, rel_bias, g_mem, w_mem_kv, w_out, g_final):
    batch, seq, _ = x.shape
    assert w_in.shape[0] == 1, "the out-projection kernel fuses the final norm of a single-layer trunk"
    kk = np.arange(2 * TQ_A)[:, None]
    qq = np.arange(TQ_A)[None, :]
    bkt_a = jnp.asarray(_t5_bucket_np(kk - TQ_A - qq))
    kk = np.arange(TK_B)[None, :, None]
    qq = np.arange(TQ_B)[None, None, :]
    d = np.arange(NEAR_B)[:, None, None]
    bkt_b = jnp.asarray(_t5_bucket_np(kk - qq - TK_B * d))

    h = x.reshape(batch * seq, D_MODEL)
    mem2d = mem.reshape(batch * N_MEM, D_MODEL)
    w_t = jnp.transpose(w_in, (2, 0, 1)).reshape(IN_WIDTH * KB_IN, LANES)
    proj = _inproj(h, g_norm[0].reshape(1, D_MODEL), _wprep(w_t), _col_scale())
    mkv = _memkv(mem2d, g_mem[0].reshape(1, D_MODEL), w_mem_kv[0])
    oa = _swa(proj, bkt_a, rel_bias, sinks[0], batch, seq)
    ob = _dsa(proj, bkt_b, rel_bias, batch, seq)
    out = _outproj(oa, ob, proj, mkv, h, w_out[0], g_final.reshape(1, D_MODEL), seq)
    return out.reshape(batch, seq, D_MODEL)
```

```python
import functools
import math

import numpy as np
import jax
import jax.numpy as jnp
from jax import lax
from jax.experimental import pallas as pl
from jax.experimental.pallas import tpu as pltpu

D_MODEL = 2048
CHUNK = 64
N_MEM = 256
EPS = 1e-6
A_HEADS = 16
A_KV_HEADS = 2
A_HEAD_DIM = 64
WINDOW_CHUNKS = 2
A_WIDTH = A_HEADS * A_HEAD_DIM
B_HEADS = 4
B_HEAD_DIM = 128
B_WIDTH = B_HEADS * B_HEAD_DIM
IDX_HEADS = 4
IDX_DIM = 64
TOPK_MAX = 256
C_HEADS = 4
C_HEAD_DIM = 128
C_WIDTH = C_HEADS * C_HEAD_DIM
MIX_WIDTH = A_WIDTH + B_WIDTH + C_WIDTH
N_BUCKETS = 32
MAX_DISTANCE = 1024
KV_A = A_KV_HEADS * A_HEAD_DIM
SPLIT_SIZES = (A_WIDTH, KV_A, KV_A, B_WIDTH, B_WIDTH, B_WIDTH,
               IDX_HEADS * IDX_DIM, IDX_DIM, IDX_HEADS, C_WIDTH, MIX_WIDTH)
IN_WIDTH = sum(SPLIT_SIZES)

F32 = jnp.float32
BF16 = jnp.bfloat16
LOG2E = math.log2(math.e)
NEG = -1e30
INT_MIN = -(2 ** 31)
LANES = 128
SUBLANES = 8
PACKED = 16
HALF = 1 << 15
KEY_POS_INF = 0x7F800000
KEY_NEG_INF = -KEY_POS_INF
FINE_PASSES = 17
PRE_PASSES = 2
F32_MANT_BITS = 23
F32_EXP_BIAS = 127

(SRC_AQ, SRC_AK, SRC_AV, SRC_BQ, SRC_BK, SRC_BV,
 SRC_IQ, SRC_IK, SRC_IW, SRC_CQ, SRC_GATE) = (int(c) for c in np.cumsum((0,) + SPLIT_SIZES)[:-1])
COL_GATE = 0
COL_AQ = COL_GATE + MIX_WIDTH
COL_BQ = COL_AQ + A_WIDTH
COL_BK = COL_BQ + B_WIDTH
COL_BV = COL_BK + B_WIDTH
COL_CQ = COL_BV + B_WIDTH
COL_AKV = COL_CQ + C_WIDTH
COL_IQ = COL_AKV + 2 * KV_A
COL_IKW = COL_IQ + IDX_HEADS * IDX_DIM
PROJ_W = 6144
SEGMENTS = ((COL_GATE, SRC_GATE, MIX_WIDTH), (COL_AQ, SRC_AQ, A_WIDTH), (COL_BQ, SRC_BQ, B_WIDTH),
            (COL_BK, SRC_BK, B_WIDTH), (COL_BV, SRC_BV, B_WIDTH), (COL_CQ, SRC_CQ, C_WIDTH),
            (COL_AKV, SRC_AK, 2 * KV_A), (COL_IQ, SRC_IQ, IDX_HEADS * IDX_DIM), (COL_IKW, SRC_IK, 256))
KB_IN = D_MODEL // LANES

TC_PREP = 256
TM_IN, TN_IN = 1024, 1024
SUB_IN = 256
TQ_A = 128
RB_A = 512
TQ_B = 256
TK_B = 256
NEAR_B = 4
TM_OUT = 512
SUB_OUT = 256
VMEM_LIMIT = 56 * 1024 * 1024


def _t5_bucket_np(rel):
    nb = N_BUCKETS // 2
    max_exact = nb // 2
    side = np.where(rel > 0, nb, 0)
    n = np.abs(rel)
    nf = np.maximum(n, max_exact).astype(np.float32)
    large = max_exact + (np.log(nf / max_exact) / math.log(MAX_DISTANCE / max_exact)
                         * (nb - max_exact)).astype(np.int32)
    large = np.minimum(large, nb - 1)
    return (side + np.where(n < max_exact, n, large)).astype(np.int32)


def _nt_dot(a, b):
    return lax.dot_general(a, b, (((1,), (1,)), ((), ())), preferred_element_type=F32)


def _bias_table(bucket, rb_ref, col, sub_row=None):
    acc = jnp.zeros(bucket.shape, F32)
    for b in range(N_BUCKETS):
        val = rb_ref[b, col]
        if sub_row is not None:
            val = val - rb_ref[sub_row, col]
        acc = jnp.where(bucket == b, val * LOG2E, acc)
    return acc


def _wprep_kernel(src_ref, w_ref, o_ref):
    t = pl.program_id(0)

    @pl.when(src_ref[t] >= 0)
    def _():
        for kb in range(KB_IN):
            o_ref[:, kb * LANES:(kb + 1) * LANES] = w_ref[pl.ds(kb, TC_PREP, stride=KB_IN), :].astype(BF16)

    @pl.when(src_ref[t] < 0)
    def _():
        o_ref[...] = jnp.zeros(o_ref.shape, BF16)


def _wprep(w_t):
    src = np.full((PROJ_W // TC_PREP,), -1, np.int32)
    for dst_col, src_col, width in SEGMENTS:
        for n in range(width // TC_PREP):
            src[dst_col // TC_PREP + n] = src_col + n * TC_PREP
    return pl.pallas_call(
        _wprep_kernel,
        grid_spec=pltpu.PrefetchScalarGridSpec(
            num_scalar_prefetch=1,
            grid=(PROJ_W // TC_PREP,),
            in_specs=[pl.BlockSpec((pl.Element(TC_PREP * KB_IN), pl.Element(LANES)),
                                   lambda t, src: (jnp.maximum(src[t], 0) * KB_IN, 0))],
            out_specs=pl.BlockSpec((TC_PREP, D_MODEL), lambda t, src: (t, 0)),
        ),
        out_shape=jax.ShapeDtypeStruct((PROJ_W, D_MODEL), BF16),
        compiler_params=pltpu.CompilerParams(
            dimension_semantics=("arbitrary",), vmem_limit_bytes=VMEM_LIMIT),
        name="wprep",
    )(jnp.asarray(src), w_t)


def _inproj_kernel(x_ref, g_ref, wt_ref, cs_ref, o_ref, hn_ref):
    j = pl.program_id(1)

    @pl.when(j == 0)
    def _():
        for r in range(TM_IN // SUB_IN):
            rows = slice(r * SUB_IN, (r + 1) * SUB_IN)
            x = x_ref[rows, :]
            ms = jnp.mean(x * x, axis=-1, keepdims=True)
            hn = (x * lax.rsqrt(ms + EPS) * g_ref[...]).astype(BF16)
            hn_ref[rows, :] = hn
            o_ref[rows, :] = (_nt_dot(hn, wt_ref[...]) * cs_ref[...]).astype(BF16)

    @pl.when(j > 0)
    def _():
        o_ref[...] = (_nt_dot(hn_ref[...], wt_ref[...]) * cs_ref[...]).astype(BF16)


def _inproj(x2d, g, wt, cs):
    m = x2d.shape[0]
    return pl.pallas_call(
        _inproj_kernel,
        grid=(m // TM_IN, PROJ_W // TN_IN),
        in_specs=[
            pl.BlockSpec((TM_IN, D_MODEL), lambda i, j: (i, 0)),
            pl.BlockSpec((1, D_MODEL), lambda i, j: (0, 0)),
            pl.BlockSpec((TN_IN, D_MODEL), lambda i, j: (j, 0)),
            pl.BlockSpec((1, TN_IN), lambda i, j: (0, j)),
        ],
        out_specs=pl.BlockSpec((TM_IN, TN_IN), lambda i, j: (i, j)),
        out_shape=jax.ShapeDtypeStruct((m, PROJ_W), BF16),
        scratch_shapes=[pltpu.VMEM((TM_IN, D_MODEL), BF16)],
        compiler_params=pltpu.CompilerParams(
            dimension_semantics=("parallel", "arbitrary"), vmem_limit_bytes=VMEM_LIMIT),
        name="inproj",
    )(x2d, g, wt, cs)


def _memkv_kernel(m_ref, g_ref, w_ref, o_ref):
    x = m_ref[...]
    ms = jnp.mean(x * x, axis=-1, keepdims=True)
    hn = (x * lax.rsqrt(ms + EPS) * g_ref[...]).astype(BF16)
    o_ref[...] = jnp.dot(hn, w_ref[...].astype(BF16), preferred_element_type=F32).astype(BF16)


def _memkv(mem2d, g, w):
    m = mem2d.shape[0]
    return pl.pallas_call(
        _memkv_kernel,
        grid=(m // N_MEM,),
        in_specs=[
            pl.BlockSpec((N_MEM, D_MODEL), lambda i: (i, 0)),
            pl.BlockSpec((1, D_MODEL), lambda i: (0, 0)),
            pl.BlockSpec((D_MODEL, 2 * C_WIDTH), lambda i: (0, 0)),
        ],
        out_specs=pl.BlockSpec((N_MEM, 2 * C_WIDTH), lambda i: (i, 0)),
        out_shape=jax.ShapeDtypeStruct((m, 2 * C_WIDTH), BF16),
        compiler_params=pltpu.CompilerParams(
            dimension_semantics=("arbitrary",), vmem_limit_bytes=VMEM_LIMIT),
        name="memkv",
    )(mem2d, g, w)


GRP_A = A_HEADS // A_KV_HEADS
PAIRS_A = GRP_A // 2
COLS_A = GRP_A * TQ_A
KEYS_A = 2 * TQ_A
CHUNK_PAIRS_A = 4


def _swa_kernel(q_ref, kvp_ref, kvc_ref, bkt_ref, rb_ref, sink_ref, o_ref,
                tab_ref, sinkv_ref, kp_ref, vt_ref, s_ref, m_ref, ot_ref):
    i = pl.program_id(1)

    @pl.when((pl.program_id(0) == 0) & (i == 0))
    def _():
        bkt = bkt_ref[...]
        kchunk = lax.broadcasted_iota(jnp.int32, bkt.shape, 0) // CHUNK
        qchunk = lax.broadcasted_iota(jnp.int32, bkt.shape, 1) // CHUNK
        allowed = (kchunk >= qchunk) & (kchunk <= qchunk + WINDOW_CHUNKS)
        has_prev = lax.broadcasted_iota(jnp.int32, bkt.shape, 0) >= TQ_A
        for g in range(A_KV_HEADS):
            for col in range(GRP_A):
                h = g * GRP_A + 2 * (col % PAIRS_A) + col // PAIRS_A
                cols = slice(col * TQ_A, (col + 1) * TQ_A)
                t = jnp.where(allowed, _bias_table(bkt, rb_ref, h), NEG)
                tab_ref[0, g, :, cols] = t
                tab_ref[1, g, :, cols] = jnp.where(has_prev, t, NEG)
                sinkv_ref[g, :, cols] = jnp.full((SUBLANES, TQ_A), sink_ref[h] * LOG2E, F32)

    width = CHUNK_PAIRS_A * TQ_A
    tasks = [(t, g, p0, parity) for t in range(RB_A // TQ_A) for g in range(A_KV_HEADS)
             for p0 in range(0, PAIRS_A, CHUNK_PAIRS_A) for parity in range(2)]

    def prepare(t, g):
        slot = (t * A_KV_HEADS + g) % 2
        if t == 0:
            kv_t = jnp.concatenate([kvp_ref[...], kvc_ref[0:TQ_A, :]], axis=0)
        else:
            kv_t = kvc_ref[(t - 1) * TQ_A:(t + 1) * TQ_A, :]
        kg = kv_t[:, g * A_HEAD_DIM:(g + 1) * A_HEAD_DIM]
        zeros = jnp.zeros((KEYS_A, A_HEAD_DIM), BF16)
        kp_ref[slot, 0] = jnp.concatenate([kg, zeros], axis=1)
        kp_ref[slot, 1] = jnp.concatenate([zeros, kg], axis=1)
        r = lax.broadcasted_iota(jnp.int32, (A_HEAD_DIM, KV_A), 0)
        c = lax.broadcasted_iota(jnp.int32, (A_HEAD_DIM, KV_A), 1)
        pick = jnp.where(c == g * A_HEAD_DIM + r, 1.0, 0.0).astype(BF16)
        vt = _nt_dot(pick, kv_t[:, KV_A:]).astype(BF16)
        vt_ref[slot] = jnp.concatenate([vt, jnp.ones((PACKED, KEYS_A), BF16)], axis=0)

    def chunk_cols(p0, parity):
        lo = (parity * PAIRS_A + p0) * TQ_A
        return slice(lo, lo + width)

    def logits_stage(n):
        t, g, p0, parity = tasks[n]
        if (p0, parity) == (0, 0):
            prepare(t, g)
        slot, buf, cols = (t * A_KV_HEADS + g) % 2, n % 2, chunk_cols(p0, parity)
        q_pairs = jnp.concatenate(
            [q_ref[t * TQ_A:(t + 1) * TQ_A, (g * PAIRS_A + p) * LANES:(g * PAIRS_A + p + 1) * LANES]
             for p in range(p0, p0 + CHUNK_PAIRS_A)], axis=0)
        variant = jnp.where(i == 0, 1, 0) if t == 0 else 0
        s = _nt_dot(kp_ref[slot, parity], q_pairs) + tab_ref[variant, g, :, cols]
        s_ref[buf] = s
        m = jnp.max(s.reshape(KEYS_A // SUBLANES, SUBLANES, width), axis=0)
        m_ref[buf] = jnp.maximum(jnp.broadcast_to(jnp.max(m, axis=0, keepdims=True), m.shape),
                                 sinkv_ref[g, :, cols])

    def update_stage(n):
        t, g, p0, parity = tasks[n]
        slot, buf, cols = (t * A_KV_HEADS + g) % 2, n % 2, chunk_cols(p0, parity)
        m = m_ref[buf]
        s = s_ref[buf].reshape(KEYS_A // SUBLANES, SUBLANES, width)
        p = jnp.exp2(s - m[None]).reshape(KEYS_A, width).astype(BF16)
        pv = jnp.dot(vt_ref[slot], p, preferred_element_type=F32)
        den = pv[A_HEAD_DIM:A_HEAD_DIM + SUBLANES] + jnp.exp2(sinkv_ref[g, :, cols] - m)
        out_t = pv[:A_HEAD_DIM].reshape(A_HEAD_DIM // SUBLANES, SUBLANES, width) / den[None]
        ot_ref[parity] = out_t.reshape(A_HEAD_DIM, width)
        if parity == 1:
            for k in range(CHUNK_PAIRS_A):
                blk = jnp.concatenate([ot_ref[0, :, k * TQ_A:(k + 1) * TQ_A],
                                       ot_ref[1, :, k * TQ_A:(k + 1) * TQ_A]], axis=0)
                lanes = slice((g * PAIRS_A + p0 + k) * LANES, (g * PAIRS_A + p0 + k + 1) * LANES)
                o_ref[t * TQ_A:(t + 1) * TQ_A, lanes] = blk.T.astype(BF16)

    logits_stage(0)
    for n in range(len(tasks)):
        if n + 1 < len(tasks):
            logits_stage(n + 1)
        update_stage(n)


def _swa(proj, bkt, rel_bias, sinks, batch, seq):
    nt = seq // RB_A
    sub = RB_A // TQ_A
    kv_blk = COL_AKV // (2 * KV_A)
    return pl.pallas_call(
        _swa_kernel,
        grid=(batch, nt),
        in_specs=[
            pl.BlockSpec((RB_A, A_WIDTH), lambda b, i: (b * nt + i, COL_AQ // A_WIDTH)),
            pl.BlockSpec((TQ_A, 2 * KV_A), lambda b, i: (jnp.maximum((b * nt + i) * sub - 1, 0), kv_blk)),
            pl.BlockSpec((RB_A, 2 * KV_A), lambda b, i: (b * nt + i, kv_blk)),
            pl.BlockSpec((KEYS_A, TQ_A), lambda b, i: (0, 0)),
            pl.BlockSpec(memory_space=pltpu.SMEM),
            pl.BlockSpec(memory_space=pltpu.SMEM),
        ],
        out_specs=pl.BlockSpec((RB_A, A_WIDTH), lambda b, i: (b * nt + i, 0)),
        out_shape=jax.ShapeDtypeStruct((batch * seq, A_WIDTH), BF16),
        scratch_shapes=[
            pltpu.VMEM((2, A_KV_HEADS, KEYS_A, COLS_A), F32),
            pltpu.VMEM((A_KV_HEADS, SUBLANES, COLS_A), F32),
            pltpu.VMEM((2, 2, KEYS_A, LANES), BF16),
            pltpu.VMEM((2, A_HEAD_DIM + PACKED, KEYS_A), BF16),
            pltpu.VMEM((2, KEYS_A, CHUNK_PAIRS_A * TQ_A), F32),
            pltpu.VMEM((2, SUBLANES, CHUNK_PAIRS_A * TQ_A), F32),
            pltpu.VMEM((2, A_HEAD_DIM, CHUNK_PAIRS_A * TQ_A), F32),
        ],
        compiler_params=pltpu.CompilerParams(
            dimension_semantics=("arbitrary", "arbitrary"), vmem_limit_bytes=VMEM_LIMIT),
        name="swa",
    )(proj, proj, proj, bkt, rel_bias, sinks)


def _dsa_kernel(iq_ref, iwq_ref, ik_ref, q_ref, k_ref, v_ref, bkt_ref, rb_ref, o_ref,
                sc_ref, scb_ref, d16_ref, mb_ref, tab_ref, tri_ref, vt_ref, acc_ref, m_ref,
                sa_ref, smaxa_ref, sb_ref, smaxb_ref, *, topk, seq):
    i = pl.program_id(1)
    nkt = i + 1
    int32, int16 = jnp.int32, jnp.int16
    grp = TK_B // SUBLANES

    def hcols(h):
        return slice(h * B_HEAD_DIM, (h + 1) * B_HEAD_DIM)

    def ktile(kt):
        return pl.ds(pl.multiple_of(kt * TK_B, TK_B), TK_B)

    def rows3(a):
        return a.reshape(a.shape[0] // SUBLANES, SUBLANES, TQ_B)

    def all_rows(a, op):
        return jnp.broadcast_to(op(a, axis=0, keepdims=True), a.shape)

    @pl.when((pl.program_id(0) == 0) & (i == 0))
    def _():
        far_bucket = N_BUCKETS // 2 - 1
        for d in range(NEAR_B):
            for h in range(B_HEADS):
                tab_ref[d, h] = _bias_table(bkt_ref[d], rb_ref, A_HEADS + h, sub_row=far_bucket)
        tab_ref[NEAR_B] = jnp.zeros(tab_ref.shape[1:], F32)
        r = lax.broadcasted_iota(int32, (TK_B, TK_B), 0)
        c = lax.broadcasted_iota(int32, (TK_B, TK_B), 1)
        tri_ref[...] = jnp.where(c < r, 1.0, 0.0).astype(BF16)

    @pl.when(i == 0)
    def _():
        r = lax.broadcasted_iota(int32, (B_HEAD_DIM, B_HEAD_DIM), 0)
        c = lax.broadcasted_iota(int32, (B_HEAD_DIM, B_HEAD_DIM), 1)
        eye = jnp.where(r == c, 1.0, 0.0).astype(BF16)

        def body(kt, carry):
            for h in range(B_HEADS):
                vt_ref[kt, h] = _nt_dot(eye, v_ref[ktile(kt), hcols(h)]).astype(BF16)
            return carry

        lax.fori_loop(0, seq // TK_B, body, 0)

    key_chunk = lax.broadcasted_iota(int32, (TK_B, TQ_B), 0) // CHUNK
    qry_chunk = lax.broadcasted_iota(int32, (TK_B, TQ_B), 1) // CHUNK
    adm_diag = key_chunk <= qry_chunk

    r = lax.broadcasted_iota(int32, (IDX_HEADS * SUBLANES, LANES), 0)
    c = lax.broadcasted_iota(int32, (IDX_HEADS * SUBLANES, LANES), 1)
    pick_w = jnp.where(c == IDX_DIM + r // SUBLANES, 1.0, 0.0).astype(BF16)
    w_all = _nt_dot(pick_w, iwq_ref[...]) * (IDX_HEADS ** -0.5 * IDX_DIM ** -0.5)

    def for_tiles(n, body, unroll=2):
        def group(j, carry):
            for k in range(unroll):
                body(unroll * j + k)
            return carry

        def single(kt, carry):
            body(kt)
            return carry

        lax.fori_loop(0, n // unroll, group, 0)
        lax.fori_loop(n - n % unroll, n, single, 0)

    def score_tile(kt):
        ikt = ik_ref[ktile(kt), 0:IDX_DIM]
        sc = jnp.zeros((grp, SUBLANES, TQ_B), F32)
        for h in range(IDX_HEADS):
            x = _nt_dot(ikt, iq_ref[:, h * IDX_DIM:(h + 1) * IDX_DIM])
            sc = sc + w_all[h * SUBLANES:(h + 1) * SUBLANES][None] * jnp.maximum(rows3(x), 0.0)
        store_score(kt, sc.reshape(TK_B, TQ_B))

    def store_score(kt, sc):
        sc_ref[kt] = sc
        scb_ref[kt] = sc.astype(BF16)

    for_tiles(nkt, score_tile, unroll=4)
    store_score(i, jnp.where(adm_diag, sc_ref[i], -jnp.inf))

    def key_to_f32(key):
        return pltpu.bitcast(jnp.where(key >= 0, key, INT_MIN - key), F32)

    def over_tiles(tile_count, zero):
        def pair(j, acc):
            return acc + (tile_count(2 * j) + tile_count(2 * j + 1))
        acc = lax.fori_loop(0, nkt // 2, pair, zero)
        return lax.cond(nkt % 2 == 1, lambda a: a + tile_count(nkt - 1), lambda a: a, acc)

    def tree_sum(parts):
        while len(parts) > 1:
            parts = [parts[n] + parts[n + 1] for n in range(0, len(parts), 2)]
        return parts[0]

    def count_bf16(cand):
        def tile_count(kt):
            blk = scb_ref[kt].reshape(TK_B // PACKED, PACKED, TQ_B)
            return tree_sum([jnp.where(blk[g] >= cand, jnp.ones((), int16), jnp.zeros((), int16))
                             for g in range(TK_B // PACKED)])
        acc = over_tiles(tile_count, jnp.zeros((PACKED, TQ_B), int16))
        return all_rows(acc.astype(int32), jnp.sum)

    def count_f32(pred):
        def tile_count(kt):
            blk = rows3(sc_ref[kt])
            return tree_sum([jnp.where(pred(blk[g]), 1, 0) for g in range(grp)])
        return all_rows(over_tiles(tile_count, jnp.zeros((SUBLANES, TQ_B), int32)), jnp.sum)

    def coarse_pass(p, u):
        bit = jnp.left_shift(jnp.int32(1), 15 - p)
        cand = key_to_f32(((u | bit) - HALF) << 16).astype(BF16)
        return jnp.where(count_bf16(cand) >= topk, u | bit, u)

    u = lax.fori_loop(0, 16, coarse_pass, jnp.zeros((PACKED, TQ_B), int32))
    coarse_key = ((u - HALF) << 16)[:SUBLANES]

    lo = jnp.maximum(coarse_key - HALF, KEY_NEG_INF)
    hi = jnp.minimum(coarse_key, KEY_POS_INF - 2 * HALF) + 2 * HALF

    def fine_pass(p, lohi):
        lo, hi = lohi
        mid = lo + ((hi - lo) >> 1)
        cand = key_to_f32(mid)
        ok = count_f32(lambda blk: blk >= cand) >= topk
        return jnp.where(ok, mid, lo), jnp.where(ok, hi, mid)

    lo, hi = lax.fori_loop(0, PRE_PASSES, fine_pass, (lo, hi))

    mag = jnp.minimum(jnp.abs(lo), jnp.abs(hi))
    exp_field = mag >> F32_MANT_BITS
    irregular = ((lo < 0) & (hi > 0)) | (exp_field <= F32_MANT_BITS)
    any_irregular = jnp.max(jnp.where(irregular, 1, 0)) > 0

    def finish_f32(lohi):
        lo, hi = lax.fori_loop(PRE_PASSES, FINE_PASSES, fine_pass, lohi)
        return key_to_f32(lo)

    def finish_int16(lohi):
        lo, hi = lohi
        centre = key_to_f32(lo + ((hi - lo) >> 1))
        ulp = pltpu.bitcast((exp_field - F32_MANT_BITS) << F32_MANT_BITS, F32)
        inv_ulp = pltpu.bitcast((2 * F32_EXP_BIAS + F32_MANT_BITS - exp_field) << F32_MANT_BITS, F32)

        def build(kt):
            d = (rows3(sc_ref[kt]) - centre[None]) * inv_ulp[None]
            d = jnp.clip(d, -HALF, HALF - 1).reshape(TK_B, TQ_B)
            d16_ref[kt] = d.astype(int32).astype(int16)

        for_tiles(nkt, build)

        def count_d16(cand):
            def tile_count(kt):
                blk = d16_ref[kt].reshape(TK_B // PACKED, PACKED, TQ_B)
                return tree_sum([jnp.where(blk[g] >= cand, jnp.ones((), int16), jnp.zeros((), int16))
                                 for g in range(TK_B // PACKED)])
            acc = over_tiles(tile_count, jnp.zeros((PACKED, TQ_B), int16))
            return all_rows(acc.astype(int32), jnp.sum)

        def d_pass(p, u):
            bit = jnp.left_shift(jnp.int32(1), 15 - p)
            cand = ((u | bit) - HALF).astype(int16)
            return jnp.where(count_d16(cand) >= topk, u | bit, u)

        u = lax.fori_loop(0, 16, d_pass, jnp.zeros((PACKED, TQ_B), int32))
        return centre + (u - HALF)[:SUBLANES].astype(F32) * ulp

    tau2d = lax.cond(any_irregular, finish_f32, finish_int16, (lo, hi))
    tau = tau2d[None]

    need = (topk - count_f32(lambda blk: blk > tau2d)).astype(F32)[None]
    ones_l = jnp.ones((2 * SUBLANES, TK_B), BF16)

    def mask_tile(kt, run):
        blk = rows3(sc_ref[kt])
        eq = blk == tau
        eqf = jnp.where(eq, 1.0, 0.0).reshape(TK_B, TQ_B).astype(BF16)
        rank = rows3(jnp.dot(tri_ref[...], eqf, preferred_element_type=F32)) + run[None]
        sel = (blk > tau) | (eq & (rank < need))
        mb_ref[kt] = jnp.where(sel, 0.0, NEG).reshape(TK_B, TQ_B)
        return run + jnp.dot(ones_l, eqf, preferred_element_type=F32)[:SUBLANES]

    def mask_group(j, run):
        for k in range(4):
            run = mask_tile(4 * j + k, run)
        return run

    run = lax.fori_loop(0, nkt // 4, mask_group, jnp.zeros((SUBLANES, TQ_B), F32))
    lax.fori_loop(nkt - nkt % 4, nkt, mask_tile, run)

    mb_ref[i] = jnp.where(adm_diag, mb_ref[i], NEG)

    m_ref[...] = jnp.full(m_ref.shape, NEG, F32)
    acc_ref[...] = jnp.zeros(acc_ref.shape, F32)
    ones_rows = jnp.ones((PACKED, TK_B), BF16)

    buf_a, buf_b = (sa_ref, smaxa_ref), (sb_ref, smaxb_ref)

    def logits_stage(kt, buf, near):
        s_ref, smax_ref = buf
        mb = mb_ref[kt]
        for h in range(B_HEADS):
            s = _nt_dot(k_ref[ktile(kt), hcols(h)], q_ref[:, hcols(h)]) + mb
            if near:
                s = s + tab_ref[jnp.minimum(i - kt, NEAR_B), h]
            s_ref[h] = s
            smax_ref[h] = all_rows(jnp.max(rows3(s), axis=0), jnp.max)

    def update_stage(kt, buf):
        s_ref, smax_ref = buf
        for h in range(B_HEADS):
            m_old = m_ref[h]
            m_new = jnp.maximum(m_old, smax_ref[h])
            alpha = jnp.exp2(m_old - m_new)
            p = jnp.exp2(rows3(s_ref[h]) - m_new[None]).reshape(TK_B, TQ_B).astype(BF16)
            vaug = jnp.concatenate([vt_ref[kt, h], ones_rows], axis=0)
            pv = jnp.dot(vaug, p, preferred_element_type=F32)
            acc_ref[h] = (rows3(acc_ref[h]) * alpha[None] + rows3(pv)).reshape(acc_ref.shape[1:])
            m_ref[h] = m_new

    def pair_body(j, carry, near):
        logits_stage(2 * j + 1, buf_b, near)
        update_stage(2 * j, buf_a)
        logits_stage(2 * j + 2, buf_a, near)
        update_stage(2 * j + 1, buf_b)
        return carry

    near_lo = jnp.maximum(i - (NEAR_B - 1), 0)
    far_pairs = jnp.maximum(near_lo - 1, 0) // 2
    logits_stage(0, buf_a, True)
    lax.fori_loop(0, far_pairs, functools.partial(pair_body, near=False), 0)
    lax.fori_loop(far_pairs, (nkt - 1) // 2, functools.partial(pair_body, near=True), 0)

    @pl.when(nkt % 2 == 0)
    def _():
        logits_stage(nkt - 1, buf_b, True)
        update_stage(nkt - 2, buf_a)
        update_stage(nkt - 1, buf_b)

    @pl.when(nkt % 2 == 1)
    def _():
        update_stage(nkt - 1, buf_a)

    for h in range(B_HEADS):
        num = rows3(acc_ref[h, 0:B_HEAD_DIM, :])
        den = acc_ref[h, B_HEAD_DIM:B_HEAD_DIM + SUBLANES, :]
        out_t = (num / den[None]).reshape(B_HEAD_DIM, TQ_B)
        o_ref[:, h * B_HEAD_DIM:(h + 1) * B_HEAD_DIM] = out_t.T.astype(BF16)


def _dsa(proj, bkt, rel_bias, batch, seq):
    nt = seq // TQ_B
    topk = min(TOPK_MAX, seq // 4)

    def q_spec(width, col):
        return pl.BlockSpec((TQ_B, width), lambda b, i: (b * nt + i, col // width))

    def seq_spec(width, col):
        return pl.BlockSpec((seq, width), lambda b, i: (b, col // width))

    return pl.pallas_call(
        functools.partial(_dsa_kernel, topk=topk, seq=seq),
        grid=(batch, nt),
        in_specs=[
            q_spec(IDX_HEADS * IDX_DIM, COL_IQ),
            q_spec(LANES, COL_IKW),
            seq_spec(LANES, COL_IKW),
            q_spec(B_WIDTH, COL_BQ), seq_spec(B_WIDTH, COL_BK), seq_spec(B_WIDTH, COL_BV),
            pl.BlockSpec((NEAR_B, TK_B, TQ_B), lambda b, i: (0, 0, 0)),
            pl.BlockSpec(memory_space=pltpu.SMEM),
        ],
        out_specs=pl.BlockSpec((TQ_B, B_WIDTH), lambda b, i: (b * nt + i, 0)),
        out_shape=jax.ShapeDtypeStruct((batch * seq, B_WIDTH), BF16),
        scratch_shapes=[
            pltpu.VMEM((nt, TK_B, TQ_B), F32),
            pltpu.VMEM((nt, TK_B, TQ_B), BF16),
            pltpu.VMEM((nt, TK_B, TQ_B), jnp.int16),
            pltpu.VMEM((nt, TK_B, TQ_B), F32),
            pltpu.VMEM((NEAR_B + 1, B_HEADS, TK_B, TQ_B), F32),
            pltpu.VMEM((TK_B, TK_B), BF16),
            pltpu.VMEM((seq // TK_B, B_HEADS, B_HEAD_DIM, TK_B), BF16),
            pltpu.VMEM((B_HEADS, B_HEAD_DIM + PACKED, TQ_B), F32),
            pltpu.VMEM((B_HEADS, SUBLANES, TQ_B), F32),
            pltpu.VMEM((B_HEADS, TK_B, TQ_B), F32),
            pltpu.VMEM((B_HEADS, SUBLANES, TQ_B), F32),
            pltpu.VMEM((B_HEADS, TK_B, TQ_B), F32),
            pltpu.VMEM((B_HEADS, SUBLANES, TQ_B), F32),
        ],
        compiler_params=pltpu.CompilerParams(
            dimension_semantics=("arbitrary", "arbitrary"), vmem_limit_bytes=VMEM_LIMIT),
        name="dsa",
    )(proj, proj, proj, proj, proj, proj, bkt, rel_bias)


def _outproj_kernel(oa_ref, ob_ref, cq_ref, mkv_ref, gate_ref, x_ref, w32_ref, g_ref, o_ref, w_ref):
    @pl.when(pl.program_id(0) == 0)
    def _():
        w_ref[...] = w32_ref[...].astype(BF16)

    ones_m = jnp.ones((N_MEM, C_HEAD_DIM), BF16)
    for r in range(TM_OUT // SUB_OUT):
        rows = slice(r * SUB_OUT, (r + 1) * SUB_OUT)
        att = [oa_ref[rows, :].astype(F32), ob_ref[rows, :].astype(F32)]
        for h in range(C_HEADS):
            hs = slice(h * C_HEAD_DIM, (h + 1) * C_HEAD_DIM)
            s = _nt_dot(cq_ref[rows, hs], mkv_ref[:, hs])
            p = jnp.exp2(s - jnp.max(s, axis=1, keepdims=True)).astype(BF16)
            vaug = jnp.concatenate([mkv_ref[:, C_WIDTH + h * C_HEAD_DIM:C_WIDTH + (h + 1) * C_HEAD_DIM], ones_m],
                                   axis=1)
            pv = jnp.dot(p, vaug, preferred_element_type=F32)
            att.append(pv[:, :C_HEAD_DIM] / pv[:, C_HEAD_DIM:])
        gate = gate_ref[rows, :].astype(F32)
        sg = gate / (1.0 + jnp.exp(-gate))
        y = (jnp.concatenate(att, axis=1) * sg).astype(BF16)
        h_new = x_ref[rows, :] + jnp.dot(y, w_ref[...], preferred_element_type=F32)
        ms = jnp.mean(h_new * h_new, axis=-1, keepdims=True)
        o_ref[rows, :] = h_new * lax.rsqrt(ms + EPS) * g_ref[...]


def _outproj(oa, ob, proj, mkv, x2d, w, g, seq):
    m = x2d.shape[0]
    steps_per_seq = seq // TM_OUT
    return pl.pallas_call(
        _outproj_kernel,
        grid=(m // TM_OUT,),
        in_specs=[
            pl.BlockSpec((TM_OUT, A_WIDTH), lambda i: (i, 0)),
            pl.BlockSpec((TM_OUT, B_WIDTH), lambda i: (i, 0)),
            pl.BlockSpec((TM_OUT, C_WIDTH), lambda i: (i, COL_CQ // C_WIDTH)),
            pl.BlockSpec((N_MEM, 2 * C_WIDTH), lambda i: (i // steps_per_seq, 0)),
            pl.BlockSpec((TM_OUT, MIX_WIDTH), lambda i: (i, COL_GATE // MIX_WIDTH)),
            pl.BlockSpec((TM_OUT, D_MODEL), lambda i: (i, 0)),
            pl.BlockSpec((MIX_WIDTH, D_MODEL), lambda i: (0, 0), pipeline_mode=pl.Buffered(1)),
            pl.BlockSpec((1, D_MODEL), lambda i: (0, 0)),
        ],
        out_specs=pl.BlockSpec((TM_OUT, D_MODEL), lambda i: (i, 0)),
        out_shape=jax.ShapeDtypeStruct((m, D_MODEL), F32),
        scratch_shapes=[pltpu.VMEM((MIX_WIDTH, D_MODEL), BF16)],
        compiler_params=pltpu.CompilerParams(
            dimension_semantics=("arbitrary",), vmem_limit_bytes=VMEM_LIMIT),
        name="outproj",
    )(oa, ob, proj, mkv, proj, x2d, w, g)


def _col_scale():
    cs = np.ones((1, PROJ_W), np.float32)
    cs[0, COL_AQ:COL_AQ + A_WIDTH] = A_HEAD_DIM ** -0.5 * LOG2E
    cs[0, COL_BQ:COL_BQ + B_WIDTH] = B_HEAD_DIM ** -0.5 * LOG2E
    cs[0, COL_CQ:COL_CQ + C_WIDTH] = C_HEAD_DIM ** -0.5 * LOG2E
    return jnp.asarray(cs)


def kernel(x, mem, g_norm, w_in, sinks, rel_bias, g_mem, w_mem_kv, w_out, g_final):
    batch, seq, _ = x.shape
    assert w_in.shape[0] == 1, "the out-projection kernel fuses the final norm of a single-layer trunk"
    kk = np.arange(2 * TQ_A)[:, None]
    qq = np.arange(TQ_A)[None, :]
    bkt_a = jnp.asarray(_t5_bucket_np(kk - TQ_A - qq))
    kk = np.arange(TK_B)[None, :, None]
    qq = np.arange(TQ_B)[None, None, :]
    d = np.arange(NEAR_B)[:, None, None]
    bkt_b = jnp.asarray(_t5_bucket_np(kk - qq - TK_B * d))

    h = x.reshape(batch * seq, D_MODEL)
    mem2d = mem.reshape(batch * N_MEM, D_MODEL)
    w_t = jnp.transpose(w_in, (2, 0, 1)).reshape(IN_WIDTH * KB_IN, LANES)
    proj = _inproj(h, g_norm[0].reshape(1, D_MODEL), _wprep(w_t), _col_scale())
    mkv = _memkv(mem2d, g_mem[0].reshape(1, D_MODEL), w_mem_kv[0])
    oa = _swa(proj, bkt_a, rel_bias, sinks[0], batch, seq)
    ob = _dsa(proj, bkt_b, rel_bias, batch, seq)
    out = _outproj(oa, ob, proj, mkv, h, w_out[0], g_final.reshape(1, D_MODEL), seq)
    return out.reshape(batch, seq, D_MODEL)
```

```python
import functools
import math

import numpy as np
import jax
import jax.numpy as jnp
from jax import lax
from jax.experimental import pallas as pl
from jax.experimental.pallas import tpu as pltpu

D_MODEL = 2048
CHUNK = 64
N_MEM = 256
EPS = 1e-6
A_HEADS = 16
A_KV_HEADS = 2
A_HEAD_DIM = 64
WINDOW_CHUNKS = 2
A_WIDTH = A_HEADS * A_HEAD_DIM
B_HEADS = 4
B_HEAD_DIM = 128
B_WIDTH = B_HEADS * B_HEAD_DIM
IDX_HEADS = 4
IDX_DIM = 64
TOPK_MAX = 256
C_HEADS = 4
C_HEAD_DIM = 128
C_WIDTH = C_HEADS * C_HEAD_DIM
MIX_WIDTH = A_WIDTH + B_WIDTH + C_WIDTH
N_BUCKETS = 32
MAX_DISTANCE = 1024
KV_A = A_KV_HEADS * A_HEAD_DIM
SPLIT_SIZES = (A_WIDTH, KV_A, KV_A, B_WIDTH, B_WIDTH, B_WIDTH,
               IDX_HEADS * IDX_DIM, IDX_DIM, IDX_HEADS, C_WIDTH, MIX_WIDTH)
IN_WIDTH = sum(SPLIT_SIZES)

F32 = jnp.float32
BF16 = jnp.bfloat16
LOG2E = math.log2(math.e)
NEG = -1e30
INT_MIN = -(2 ** 31)
LANES = 128
SUBLANES = 8
PACKED = 16
HALF = 1 << 15
KEY_POS_INF = 0x7F800000
KEY_NEG_INF = -KEY_POS_INF
FINE_PASSES = 17
PRE_PASSES = 2
F32_MANT_BITS = 23
F32_EXP_BIAS = 127

(SRC_AQ, SRC_AK, SRC_AV, SRC_BQ, SRC_BK, SRC_BV,
 SRC_IQ, SRC_IK, SRC_IW, SRC_CQ, SRC_GATE) = (int(c) for c in np.cumsum((0,) + SPLIT_SIZES)[:-1])
COL_GATE = 0
COL_AQ = COL_GATE + MIX_WIDTH
COL_BQ = COL_AQ + A_WIDTH
COL_BK = COL_BQ + B_WIDTH
COL_BV = COL_BK + B_WIDTH
COL_CQ = COL_BV + B_WIDTH
COL_AKV = COL_CQ + C_WIDTH
COL_IQ = COL_AKV + 2 * KV_A
COL_IKW = COL_IQ + IDX_HEADS * IDX_DIM
PROJ_W = 6144
SEGMENTS = ((COL_GATE, SRC_GATE, MIX_WIDTH), (COL_AQ, SRC_AQ, A_WIDTH), (COL_BQ, SRC_BQ, B_WIDTH),
            (COL_BK, SRC_BK, B_WIDTH), (COL_BV, SRC_BV, B_WIDTH), (COL_CQ, SRC_CQ, C_WIDTH),
            (COL_AKV, SRC_AK, 2 * KV_A), (COL_IQ, SRC_IQ, IDX_HEADS * IDX_DIM), (COL_IKW, SRC_IK, 256))
KB_IN = D_MODEL // LANES

TC_PREP = 256
TM_IN, TN_IN = 1024, 1024
SUB_IN = 256
TQ_A = 128
RB_A = 512
TQ_B = 256
TK_B = 256
NEAR_B = 4
TM_OUT = 512
SUB_OUT = 256
VMEM_LIMIT = 56 * 1024 * 1024


def _t5_bucket_np(rel):
    nb = N_BUCKETS // 2
    max_exact = nb // 2
    side = np.where(rel > 0, nb, 0)
    n = np.abs(rel)
    nf = np.maximum(n, max_exact).astype(np.float32)
    large = max_exact + (np.log(nf / max_exact) / math.log(MAX_DISTANCE / max_exact)
                         * (nb - max_exact)).astype(np.int32)
    large = np.minimum(large, nb - 1)
    return (side + np.where(n < max_exact, n, large)).astype(np.int32)


def _nt_dot(a, b):
    return lax.dot_general(a, b, (((1,), (1,)), ((), ())), preferred_element_type=F32)


def _bias_table(bucket, rb_ref, col, sub_row=None):
    acc = jnp.zeros(bucket.shape, F32)
    for b in range(N_BUCKETS):
        val = rb_ref[b, col]
        if sub_row is not None:
            val = val - rb_ref[sub_row, col]
        acc = jnp.where(bucket == b, val * LOG2E, acc)
    return acc


def _wprep_kernel(src_ref, w_ref, o_ref):
    t = pl.program_id(0)

    @pl.when(src_ref[t] >= 0)
    def _():
        for kb in range(KB_IN):
            o_ref[:, kb * LANES:(kb + 1) * LANES] = w_ref[pl.ds(kb, TC_PREP, stride=KB_IN), :].astype(BF16)

    @pl.when(src_ref[t] < 0)
    def _():
        o_ref[...] = jnp.zeros(o_ref.shape, BF16)


def _wprep(w_t):
    src = np.full((PROJ_W // TC_PREP,), -1, np.int32)
    for dst_col, src_col, width in SEGMENTS:
        for n in range(width // TC_PREP):
            src[dst_col // TC_PREP + n] = src_col + n * TC_PREP
    return pl.pallas_call(
        _wprep_kernel,
        grid_spec=pltpu.PrefetchScalarGridSpec(
            num_scalar_prefetch=1,
            grid=(PROJ_W // TC_PREP,),
            in_specs=[pl.BlockSpec((pl.Element(TC_PREP * KB_IN), pl.Element(LANES)),
                                   lambda t, src: (jnp.maximum(src[t], 0) * KB_IN, 0))],
            out_specs=pl.BlockSpec((TC_PREP, D_MODEL), lambda t, src: (t, 0)),
        ),
        out_shape=jax.ShapeDtypeStruct((PROJ_W, D_MODEL), BF16),
        compiler_params=pltpu.CompilerParams(
            dimension_semantics=("arbitrary",), vmem_limit_bytes=VMEM_LIMIT),
        name="wprep",
    )(jnp.asarray(src), w_t)


def _inproj_kernel(x_ref, g_ref, wt_ref, cs_ref, o_ref, hn_ref):
    j = pl.program_id(1)

    @pl.when(j == 0)
    def _():
        for r in range(TM_IN // SUB_IN):
            rows = slice(r * SUB_IN, (r + 1) * SUB_IN)
            x = x_ref[rows, :]
            ms = jnp.mean(x * x, axis=-1, keepdims=True)
            hn = (x * lax.rsqrt(ms + EPS) * g_ref[...]).astype(BF16)
            hn_ref[rows, :] = hn
            o_ref[rows, :] = (_nt_dot(hn, wt_ref[...]) * cs_ref[...]).astype(BF16)

    @pl.when(j > 0)
    def _():
        o_ref[...] = (_nt_dot(hn_ref[...], wt_ref[...]) * cs_ref[...]).astype(BF16)


def _inproj(x2d, g, wt, cs):
    m = x2d.shape[0]
    return pl.pallas_call(
        _inproj_kernel,
        grid=(m // TM_IN, PROJ_W // TN_IN),
        in_specs=[
            pl.BlockSpec((TM_IN, D_MODEL), lambda i, j: (i, 0)),
            pl.BlockSpec((1, D_MODEL), lambda i, j: (0, 0)),
            pl.BlockSpec((TN_IN, D_MODEL), lambda i, j: (j, 0)),
            pl.BlockSpec((1, TN_IN), lambda i, j: (0, j)),
        ],
        out_specs=pl.BlockSpec((TM_IN, TN_IN), lambda i, j: (i, j)),
        out_shape=jax.ShapeDtypeStruct((m, PROJ_W), BF16),
        scratch_shapes=[pltpu.VMEM((TM_IN, D_MODEL), BF16)],
        compiler_params=pltpu.CompilerParams(
            dimension_semantics=("parallel", "arbitrary"), vmem_limit_bytes=VMEM_LIMIT),
        name="inproj",
    )(x2d, g, wt, cs)


def _memkv_kernel(m_ref, g_ref, w_ref, o_ref):
    x = m_ref[...]
    ms = jnp.mean(x * x, axis=-1, keepdims=True)
    hn = (x * lax.rsqrt(ms + EPS) * g_ref[...]).astype(BF16)
    o_ref[...] = jnp.dot(hn, w_ref[...].astype(BF16), preferred_element_type=F32).astype(BF16)


def _memkv(mem2d, g, w):
    m = mem2d.shape[0]
    return pl.pallas_call(
        _memkv_kernel,
        grid=(m // N_MEM,),
        in_specs=[
            pl.BlockSpec((N_MEM, D_MODEL), lambda i: (i, 0)),
            pl.BlockSpec((1, D_MODEL), lambda i: (0, 0)),
            pl.BlockSpec((D_MODEL, 2 * C_WIDTH), lambda i: (0, 0)),
        ],
        out_specs=pl.BlockSpec((N_MEM, 2 * C_WIDTH), lambda i: (i, 0)),
        out_shape=jax.ShapeDtypeStruct((m, 2 * C_WIDTH), BF16),
        compiler_params=pltpu.CompilerParams(
            dimension_semantics=("arbitrary",), vmem_limit_bytes=VMEM_LIMIT),
        name="memkv",
    )(mem2d, g, w)


GRP_A = A_HEADS // A_KV_HEADS
PAIRS_A = GRP_A // 2
COLS_A = GRP_A * TQ_A
KEYS_A = 2 * TQ_A
CHUNK_PAIRS_A = 4


def _swa_kernel(q_ref, kvp_ref, kvc_ref, bkt_ref, rb_ref, sink_ref, o_ref,
                tab_ref, sinkv_ref, kp_ref, vt_ref, s_ref, m_ref, ot_ref):
    i = pl.program_id(1)

    @pl.when((pl.program_id(0) == 0) & (i == 0))
    def _():
        bkt = bkt_ref[...]
        kchunk = lax.broadcasted_iota(jnp.int32, bkt.shape, 0) // CHUNK
        qchunk = lax.broadcasted_iota(jnp.int32, bkt.shape, 1) // CHUNK
        allowed = (kchunk >= qchunk) & (kchunk <= qchunk + WINDOW_CHUNKS)
        has_prev = lax.broadcasted_iota(jnp.int32, bkt.shape, 0) >= TQ_A
        for g in range(A_KV_HEADS):
            for col in range(GRP_A):
                h = g * GRP_A + 2 * (col % PAIRS_A) + col // PAIRS_A
                cols = slice(col * TQ_A, (col + 1) * TQ_A)
                t = jnp.where(allowed, _bias_table(bkt, rb_ref, h), NEG)
                tab_ref[0, g, :, cols] = t
                tab_ref[1, g, :, cols] = jnp.where(has_prev, t, NEG)
                sinkv_ref[g, :, cols] = jnp.full((SUBLANES, TQ_A), sink_ref[h] * LOG2E, F32)

    width = CHUNK_PAIRS_A * TQ_A
    tasks = [(t, g, p0, parity) for t in range(RB_A // TQ_A) for g in range(A_KV_HEADS)
             for p0 in range(0, PAIRS_A, CHUNK_PAIRS_A) for parity in range(2)]

    def prepare(t, g):
        slot = (t * A_KV_HEADS + g) % 2
        if t == 0:
            kv_t = jnp.concatenate([kvp_ref[...], kvc_ref[0:TQ_A, :]], axis=0)
        else:
            kv_t = kvc_ref[(t - 1) * TQ_A:(t + 1) * TQ_A, :]
        kg = kv_t[:, g * A_HEAD_DIM:(g + 1) * A_HEAD_DIM]
        zeros = jnp.zeros((KEYS_A, A_HEAD_DIM), BF16)
        kp_ref[slot, 0] = jnp.concatenate([kg, zeros], axis=1)
        kp_ref[slot, 1] = jnp.concatenate([zeros, kg], axis=1)
        r = lax.broadcasted_iota(jnp.int32, (A_HEAD_DIM, KV_A), 0)
        c = lax.broadcasted_iota(jnp.int32, (A_HEAD_DIM, KV_A), 1)
        pick = jnp.where(c == g * A_HEAD_DIM + r, 1.0, 0.0).astype(BF16)
        vt = _nt_dot(pick, kv_t[:, KV_A:]).astype(BF16)
        vt_ref[slot] = jnp.concatenate([vt, jnp.ones((PACKED, KEYS_A), BF16)], axis=0)

    def chunk_cols(p0, parity):
        lo = (parity * PAIRS_A + p0) * TQ_A
        return slice(lo, lo + width)

    def logits_stage(n):
        t, g, p0, parity = tasks[n]
        if (p0, parity) == (0, 0):
            prepare(t, g)
        slot, buf, cols = (t * A_KV_HEADS + g) % 2, n % 2, chunk_cols(p0, parity)
        q_pairs = jnp.concatenate(
            [q_ref[t * TQ_A:(t + 1) * TQ_A, (g * PAIRS_A + p) * LANES:(g * PAIRS_A + p + 1) * LANES]
             for p in range(p0, p0 + CHUNK_PAIRS_A)], axis=0)
        variant = jnp.where(i == 0, 1, 0) if t == 0 else 0
        s = _nt_dot(kp_ref[slot, parity], q_pairs) + tab_ref[variant, g, :, cols]
        s_ref[buf] = s
        m = jnp.max(s.reshape(KEYS_A // SUBLANES, SUBLANES, width), axis=0)
        m_ref[buf] = jnp.maximum(jnp.broadcast_to(jnp.max(m, axis=0, keepdims=True), m.shape),
                                 sinkv_ref[g, :, cols])

    def update_stage(n):
        t, g, p0, parity = tasks[n]
        slot, buf, cols = (t * A_KV_HEADS + g) % 2, n % 2, chunk_cols(p0, parity)
        m = m_ref[buf]
        s = s_ref[buf].reshape(KEYS_A // SUBLANES, SUBLANES, width)
        p = jnp.exp2(s - m[None]).reshape(KEYS_A, width).astype(BF16)
        pv = jnp.dot(vt_ref[slot], p, preferred_element_type=F32)
        den = pv[A_HEAD_DIM:A_HEAD_DIM + SUBLANES] + jnp.exp2(sinkv_ref[g, :, cols] - m)
        out_t = pv[:A_HEAD_DIM].reshape(A_HEAD_DIM // SUBLANES, SUBLANES, width) / den[None]
        ot_ref[parity] = out_t.reshape(A_HEAD_DIM, width)
        if parity == 1:
            for k in range(CHUNK_PAIRS_A):
                blk = jnp.concatenate([ot_ref[0, :, k * TQ_A:(k + 1) * TQ_A],
                                       ot_ref[1, :, k * TQ_A:(k + 1) * TQ_A]], axis=0)
                lanes = slice((g * PAIRS_A + p0 + k) * LANES, (g * PAIRS_A + p0 + k + 1) * LANES)
                o_ref[t * TQ_A:(t + 1) * TQ_A, lanes] = blk.T.astype(BF16)

    logits_stage(0)
    for n in range(len(tasks)):
        if n + 1 < len(tasks):
            logits_stage(n + 1)
        update_stage(n)


def _swa(proj, bkt, rel_bias, sinks, batch, seq):
    nt = seq // RB_A
    sub = RB_A // TQ_A
    kv_blk = COL_AKV // (2 * KV_A)
    return pl.pallas_call(
        _swa_kernel,
        grid=(batch, nt),
        in_specs=[
            pl.BlockSpec((RB_A, A_WIDTH), lambda b, i: (b * nt + i, COL_AQ // A_WIDTH)),
            pl.BlockSpec((TQ_A, 2 * KV_A), lambda b, i: (jnp.maximum((b * nt + i) * sub - 1, 0), kv_blk)),
            pl.BlockSpec((RB_A, 2 * KV_A), lambda b, i: (b * nt + i, kv_blk)),
            pl.BlockSpec((KEYS_A, TQ_A), lambda b, i: (0, 0)),
            pl.BlockSpec(memory_space=pltpu.SMEM),
            pl.BlockSpec(memory_space=pltpu.SMEM),
        ],
        out_specs=pl.BlockSpec((RB_A, A_WIDTH), lambda b, i: (b * nt + i, 0)),
        out_shape=jax.ShapeDtypeStruct((batch * seq, A_WIDTH), BF16),
        scratch_shapes=[
            pltpu.VMEM((2, A_KV_HEADS, KEYS_A, COLS_A), F32),
            pltpu.VMEM((A_KV_HEADS, SUBLANES, COLS_A), F32),
            pltpu.VMEM((2, 2, KEYS_A, LANES), BF16),
            pltpu.VMEM((2, A_HEAD_DIM + PACKED, KEYS_A), BF16),
            pltpu.VMEM((2, KEYS_A, CHUNK_PAIRS_A * TQ_A), F32),
            pltpu.VMEM((2, SUBLANES, CHUNK_PAIRS_A * TQ_A), F32),
            pltpu.VMEM((2, A_HEAD_DIM, CHUNK_PAIRS_A * TQ_A), F32),
        ],
        compiler_params=pltpu.CompilerParams(
            dimension_semantics=("arbitrary", "arbitrary"), vmem_limit_bytes=VMEM_LIMIT),
        name="swa",
    )(proj, proj, proj, bkt, rel_bias, sinks)


def _dsa_kernel(iq_ref, iwq_ref, ik_ref, q_ref, k_ref, v_ref, bkt_ref, rb_ref, o_ref,
                sc_ref, scb_ref, d16_ref, mb_ref, tab_ref, tri_ref, vt_ref, acc_ref, m_ref,
                sa_ref, smaxa_ref, sb_ref, smaxb_ref, *, topk, seq):
    i = pl.program_id(1)
    nkt = i + 1
    int32, int16 = jnp.int32, jnp.int16
    grp = TK_B // SUBLANES

    def hcols(h):
        return slice(h * B_HEAD_DIM, (h + 1) * B_HEAD_DIM)

    def ktile(kt):
        return pl.ds(pl.multiple_of(kt * TK_B, TK_B), TK_B)

    def rows3(a):
        return a.reshape(a.shape[0] // SUBLANES, SUBLANES, TQ_B)

    def all_rows(a, op):
        return jnp.broadcast_to(op(a, axis=0, keepdims=True), a.shape)

    @pl.when((pl.program_id(0) == 0) & (i == 0))
    def _():
        far_bucket = N_BUCKETS // 2 - 1
        for d in range(NEAR_B):
            for h in range(B_HEADS):
                tab_ref[d, h] = _bias_table(bkt_ref[d], rb_ref, A_HEADS + h, sub_row=far_bucket)
        tab_ref[NEAR_B] = jnp.zeros(tab_ref.shape[1:], F32)
        r = lax.broadcasted_iota(int32, (TK_B, TK_B), 0)
        c = lax.broadcasted_iota(int32, (TK_B, TK_B), 1)
        tri_ref[...] = jnp.where(c < r, 1.0, 0.0).astype(BF16)

    @pl.when(i == 0)
    def _():
        r = lax.broadcasted_iota(int32, (B_HEAD_DIM, B_HEAD_DIM), 0)
        c = lax.broadcasted_iota(int32, (B_HEAD_DIM, B_HEAD_DIM), 1)
        eye = jnp.where(r == c, 1.0, 0.0).astype(BF16)

        def body(kt, carry):
            for h in range(B_HEADS):
                vt_ref[kt, h] = _nt_dot(eye, v_ref[ktile(kt), hcols(h)]).astype(BF16)
            return carry

        lax.fori_loop(0, seq // TK_B, body, 0)

    key_chunk = lax.broadcasted_iota(int32, (TK_B, TQ_B), 0) // CHUNK
    qry_chunk = lax.broadcasted_iota(int32, (TK_B, TQ_B), 1) // CHUNK
    adm_diag = key_chunk <= qry_chunk

    r = lax.broadcasted_iota(int32, (IDX_HEADS * SUBLANES, LANES), 0)
    c = lax.broadcasted_iota(int32, (IDX_HEADS * SUBLANES, LANES), 1)
    pick_w = jnp.where(c == IDX_DIM + r // SUBLANES, 1.0, 0.0).astype(BF16)
    w_all = _nt_dot(pick_w, iwq_ref[...]) * (IDX_HEADS ** -0.5 * IDX_DIM ** -0.5)

    def for_tiles(n, body, unroll=2):
        def group(j, carry):
            for k in range(unroll):
                body(unroll * j + k)
            return carry

        def single(kt, carry):
            body(kt)
            return carry

        lax.fori_loop(0, n // unroll, group, 0)
        lax.fori_loop(n - n % unroll, n, single, 0)

    def score_tile(kt):
        ikt = ik_ref[ktile(kt), 0:IDX_DIM]
        sc = jnp.zeros((grp, SUBLANES, TQ_B), F32)
        for h in range(IDX_HEADS):
            x = _nt_dot(ikt, iq_ref[:, h * IDX_DIM:(h + 1) * IDX_DIM])
            sc = sc + w_all[h * SUBLANES:(h + 1) * SUBLANES][None] * jnp.maximum(rows3(x), 0.0)
        store_score(kt, sc.reshape(TK_B, TQ_B))

    def store_score(kt, sc):
        sc_ref[kt] = sc
        scb_ref[kt] = sc.astype(BF16)

    for_tiles(nkt, score_tile, unroll=4)
    store_score(i, jnp.where(adm_diag, sc_ref[i], -jnp.inf))

    def key_to_f32(key):
        return pltpu.bitcast(jnp.where(key >= 0, key, INT_MIN - key), F32)

    def over_tiles(tile_count, zero):
        def pair(j, acc):
            return acc + (tile_count(2 * j) + tile_count(2 * j + 1))
        acc = lax.fori_loop(0, nkt // 2, pair, zero)
        return lax.cond(nkt % 2 == 1, lambda a: a + tile_count(nkt - 1), lambda a: a, acc)

    def tree_sum(parts):
        while len(parts) > 1:
            parts = [parts[n] + parts[n + 1] for n in range(0, len(parts), 2)]
        return parts[0]

    def count_bf16(cand):
        def tile_count(kt):
            blk = scb_ref[kt].reshape(TK_B // PACKED, PACKED, TQ_B)
            return tree_sum([jnp.where(blk[g] >= cand, jnp.ones((), int16), jnp.zeros((), int16))
                             for g in range(TK_B // PACKED)])
        acc = over_tiles(tile_count, jnp.zeros((PACKED, TQ_B), int16))
        return all_rows(acc.astype(int32), jnp.sum)

    def count_f32(pred):
        def tile_count(kt):
            blk = rows3(sc_ref[kt])
            return tree_sum([jnp.where(pred(blk[g]), 1, 0) for g in range(grp)])
        return all_rows(over_tiles(tile_count, jnp.zeros((SUBLANES, TQ_B), int32)), jnp.sum)

    def coarse_pass(p, u):
        bit = jnp.left_shift(jnp.int32(1), 15 - p)
        cand = key_to_f32(((u | bit) - HALF) << 16).astype(BF16)
        return jnp.where(count_bf16(cand) >= topk, u | bit, u)

    u = lax.fori_loop(0, 16, coarse_pass, jnp.zeros((PACKED, TQ_B), int32))
    coarse_key = ((u - HALF) << 16)[:SUBLANES]

    lo = jnp.maximum(coarse_key - HALF, KEY_NEG_INF)
    hi = jnp.minimum(coarse_key, KEY_POS_INF - 2 * HALF) + 2 * HALF

    def fine_pass(p, lohi):
        lo, hi = lohi
        mid = lo + ((hi - lo) >> 1)
        cand = key_to_f32(mid)
        ok = count_f32(lambda blk: blk >= cand) >= topk
        return jnp.where(ok, mid, lo), jnp.where(ok, hi, mid)

    lo, hi = lax.fori_loop(0, PRE_PASSES, fine_pass, (lo, hi))

    around_zero = (lo <= 0) & (hi >= 0)
    mag = jnp.minimum(jnp.abs(lo), jnp.abs(hi))
    exp_field = jnp.where(around_zero, F32_MANT_BITS + 1, mag >> F32_MANT_BITS)
    any_irregular = jnp.max(jnp.where(exp_field <= F32_MANT_BITS, 1, 0)) > 0

    def finish_f32(lohi):
        lo, hi = lax.fori_loop(PRE_PASSES, FINE_PASSES, fine_pass, lohi)
        return key_to_f32(lo)

    def finish_int16(lohi):
        lo, hi = lohi
        centre = jnp.where(around_zero, 0.0, key_to_f32(lo + ((hi - lo) >> 1)))
        ulp = pltpu.bitcast((exp_field - F32_MANT_BITS) << F32_MANT_BITS, F32)
        inv_ulp = pltpu.bitcast((2 * F32_EXP_BIAS + F32_MANT_BITS - exp_field) << F32_MANT_BITS, F32)

        def build(kt):
            d = (rows3(sc_ref[kt]) - centre[None]) * inv_ulp[None]
            d = jnp.clip(d, -HALF, HALF - 1).reshape(TK_B, TQ_B)
            d16_ref[kt] = d.astype(int32).astype(int16)

        for_tiles(nkt, build)

        def count_d16(cand):
            def tile_count(kt):
                blk = d16_ref[kt].reshape(TK_B // PACKED, PACKED, TQ_B)
                return tree_sum([jnp.where(blk[g] >= cand, jnp.ones((), int16), jnp.zeros((), int16))
                                 for g in range(TK_B // PACKED)])
            acc = over_tiles(tile_count, jnp.zeros((PACKED, TQ_B), int16))
            return all_rows(acc.astype(int32), jnp.sum)

        def d_pass(p, u):
            bit = jnp.left_shift(jnp.int32(1), 15 - p)
            cand = ((u | bit) - HALF).astype(int16)
            return jnp.where(count_d16(cand) >= topk, u | bit, u)

        u = lax.fori_loop(0, 16, d_pass, jnp.zeros((PACKED, TQ_B), int32))
        return centre + (u - HALF)[:SUBLANES].astype(F32) * ulp

    tau2d = lax.cond(any_irregular, finish_f32, finish_int16, (lo, hi))
    tau = tau2d[None]

    need = (topk - count_f32(lambda blk: blk > tau2d)).astype(F32)[None]
    ones_l = jnp.ones((2 * SUBLANES, TK_B), BF16)

    def mask_tile(kt, run):
        blk = rows3(sc_ref[kt])
        eq = blk == tau
        eqf = jnp.where(eq, 1.0, 0.0).reshape(TK_B, TQ_B).astype(BF16)
        rank = rows3(jnp.dot(tri_ref[...], eqf, preferred_element_type=F32)) + run[None]
        sel = (blk > tau) | (eq & (rank < need))
        mb_ref[kt] = jnp.where(sel, 0.0, NEG).reshape(TK_B, TQ_B)
        return run + jnp.dot(ones_l, eqf, preferred_element_type=F32)[:SUBLANES]

    def mask_group(j, run):
        for k in range(4):
            run = mask_tile(4 * j + k, run)
        return run

    run = lax.fori_loop(0, nkt // 4, mask_group, jnp.zeros((SUBLANES, TQ_B), F32))
    lax.fori_loop(nkt - nkt % 4, nkt, mask_tile, run)

    mb_ref[i] = jnp.where(adm_diag, mb_ref[i], NEG)

    m_ref[...] = jnp.full(m_ref.shape, NEG, F32)
    acc_ref[...] = jnp.zeros(acc_ref.shape, F32)
    ones_rows = jnp.ones((PACKED, TK_B), BF16)

    buf_a, buf_b = (sa_ref, smaxa_ref), (sb_ref, smaxb_ref)

    def logits_stage(kt, buf, near):
        s_ref, smax_ref = buf
        mb = mb_ref[kt]
        for h in range(B_HEADS):
            s = _nt_dot(k_ref[ktile(kt), hcols(h)], q_ref[:, hcols(h)]) + mb
            if near:
                s = s + tab_ref[jnp.minimum(i - kt, NEAR_B), h]
            s_ref[h] = s
            smax_ref[h] = all_rows(jnp.max(rows3(s), axis=0), jnp.max)

    def update_stage(kt, buf):
        s_ref, smax_ref = buf
        for h in range(B_HEADS):
            m_old = m_ref[h]
            m_new = jnp.maximum(m_old, smax_ref[h])
            alpha = jnp.exp2(m_old - m_new)
            p = jnp.exp2(rows3(s_ref[h]) - m_new[None]).reshape(TK_B, TQ_B).astype(BF16)
            vaug = jnp.concatenate([vt_ref[kt, h], ones_rows], axis=0)
            pv = jnp.dot(vaug, p, preferred_element_type=F32)
            acc_ref[h] = (rows3(acc_ref[h]) * alpha[None] + rows3(pv)).reshape(acc_ref.shape[1:])
            m_ref[h] = m_new

    def pair_body(j, carry, near):
        logits_stage(2 * j + 1, buf_b, near)
        update_stage(2 * j, buf_a)
        logits_stage(2 * j + 2, buf_a, near)
        update_stage(2 * j + 1, buf_b)
        return carry

    near_lo = jnp.maximum(i - (NEAR_B - 1), 0)
    far_pairs = jnp.maximum(near_lo - 1, 0) // 2
    logits_stage(0, buf_a, True)
    lax.fori_loop(0, far_pairs, functools.partial(pair_body, near=False), 0)
    lax.fori_loop(far_pairs, (nkt - 1) // 2, functools.partial(pair_body, near=True), 0)

    @pl.when(nkt % 2 == 0)
    def _():
        logits_stage(nkt - 1, buf_b, True)
        update_stage(nkt - 2, buf_a)
        update_stage(nkt - 1, buf_b)

    @pl.when(nkt % 2 == 1)
    def _():
        update_stage(nkt - 1, buf_a)

    for h in range(B_HEADS):
        num = rows3(acc_ref[h, 0:B_HEAD_DIM, :])
        den = acc_ref[h, B_HEAD_DIM:B_HEAD_DIM + SUBLANES, :]
        out_t = (num / den[None]).reshape(B_HEAD_DIM, TQ_B)
        o_ref[:, h * B_HEAD_DIM:(h + 1) * B_HEAD_DIM] = out_t.T.astype(BF16)


def _dsa(proj, bkt, rel_bias, batch, seq):
    nt = seq // TQ_B
    topk = min(TOPK_MAX, seq // 4)

    def q_spec(width, col):
        return pl.BlockSpec((TQ_B, width), lambda b, i: (b * nt + i, col // width))

    def seq_spec(width, col):
        return pl.BlockSpec((seq, width), lambda b, i: (b, col // width))

    return pl.pallas_call(
        functools.partial(_dsa_kernel, topk=topk, seq=seq),
        grid=(batch, nt),
        in_specs=[
            q_spec(IDX_HEADS * IDX_DIM, COL_IQ),
            q_spec(LANES, COL_IKW),
            seq_spec(LANES, COL_IKW),
            q_spec(B_WIDTH, COL_BQ), seq_spec(B_WIDTH, COL_BK), seq_spec(B_WIDTH, COL_BV),
            pl.BlockSpec((NEAR_B, TK_B, TQ_B), lambda b, i: (0, 0, 0)),
            pl.BlockSpec(memory_space=pltpu.SMEM),
        ],
        out_specs=pl.BlockSpec((TQ_B, B_WIDTH), lambda b, i: (b * nt + i, 0)),
        out_shape=jax.ShapeDtypeStruct((batch * seq, B_WIDTH), BF16),
        scratch_shapes=[
            pltpu.VMEM((nt, TK_B, TQ_B), F32),
            pltpu.VMEM((nt, TK_B, TQ_B), BF16),
            pltpu.VMEM((nt, TK_B, TQ_B), jnp.int16),
            pltpu.VMEM((nt, TK_B, TQ_B), F32),
            pltpu.VMEM((NEAR_B + 1, B_HEADS, TK_B, TQ_B), F32),
            pltpu.VMEM((TK_B, TK_B), BF16),
            pltpu.VMEM((seq // TK_B, B_HEADS, B_HEAD_DIM, TK_B), BF16),
            pltpu.VMEM((B_HEADS, B_HEAD_DIM + PACKED, TQ_B), F32),
            pltpu.VMEM((B_HEADS, SUBLANES, TQ_B), F32),
            pltpu.VMEM((B_HEADS, TK_B, TQ_B), F32),
            pltpu.VMEM((B_HEADS, SUBLANES, TQ_B), F32),
            pltpu.VMEM((B_HEADS, TK_B, TQ_B), F32),
            pltpu.VMEM((B_HEADS, SUBLANES, TQ_B), F32),
        ],
        compiler_params=pltpu.CompilerParams(
            dimension_semantics=("arbitrary", "arbitrary"), vmem_limit_bytes=VMEM_LIMIT),
        name="dsa",
    )(proj, proj, proj, proj, proj, proj, bkt, rel_bias)


def _outproj_kernel(oa_ref, ob_ref, cq_ref, mkv_ref, gate_ref, x_ref, w32_ref, g_ref, o_ref, w_ref):
    @pl.when(pl.program_id(0) == 0)
    def _():
        w_ref[...] = w32_ref[...].astype(BF16)

    ones_m = jnp.ones((N_MEM, C_HEAD_DIM), BF16)
    for r in range(TM_OUT // SUB_OUT):
        rows = slice(r * SUB_OUT, (r + 1) * SUB_OUT)
        att = [oa_ref[rows, :].astype(F32), ob_ref[rows, :].astype(F32)]
        for h in range(C_HEADS):
            hs = slice(h * C_HEAD_DIM, (h + 1) * C_HEAD_DIM)
            s = _nt_dot(cq_ref[rows, hs], mkv_ref[:, hs])
            p = jnp.exp2(s - jnp.max(s, axis=1, keepdims=True)).astype(BF16)
            vaug = jnp.concatenate([mkv_ref[:, C_WIDTH + h * C_HEAD_DIM:C_WIDTH + (h + 1) * C_HEAD_DIM], ones_m],
                                   axis=1)
            pv = jnp.dot(p, vaug, preferred_element_type=F32)
            att.append(pv[:, :C_HEAD_DIM] / pv[:, C_HEAD_DIM:])
        gate = gate_ref[rows, :].astype(F32)
        sg = gate / (1.0 + jnp.exp(-gate))
        y = (jnp.concatenate(att, axis=1) * sg).astype(BF16)
        h_new = x_ref[rows, :] + jnp.dot(y, w_ref[...], preferred_element_type=F32)
        ms = jnp.mean(h_new * h_new, axis=-1, keepdims=True)
        o_ref[rows, :] = h_new * lax.rsqrt(ms + EPS) * g_ref[...]


def _outproj(oa, ob, proj, mkv, x2d, w, g, seq):
    m = x2d.shape[0]
    steps_per_seq = seq // TM_OUT
    return pl.pallas_call(
        _outproj_kernel,
        grid=(m // TM_OUT,),
        in_specs=[
            pl.BlockSpec((TM_OUT, A_WIDTH), lambda i: (i, 0)),
            pl.BlockSpec((TM_OUT, B_WIDTH), lambda i: (i, 0)),
            pl.BlockSpec((TM_OUT, C_WIDTH), lambda i: (i, COL_CQ // C_WIDTH)),
            pl.BlockSpec((N_MEM, 2 * C_WIDTH), lambda i: (i // steps_per_seq, 0)),
            pl.BlockSpec((TM_OUT, MIX_WIDTH), lambda i: (i, COL_GATE // MIX_WIDTH)),
            pl.BlockSpec((TM_OUT, D_MODEL), lambda i: (i, 0)),
            pl.BlockSpec((MIX_WIDTH, D_MODEL), lambda i: (0, 0), pipeline_mode=pl.Buffered(1)),
            pl.BlockSpec((1, D_MODEL), lambda i: (0, 0)),
        ],
        out_specs=pl.BlockSpec((TM_OUT, D_MODEL), lambda i: (i, 0)),
        out_shape=jax.ShapeDtypeStruct((m, D_MODEL), F32),
        scratch_shapes=[pltpu.VMEM((MIX_WIDTH, D_MODEL), BF16)],
        compiler_params=pltpu.CompilerParams(
            dimension_semantics=("arbitrary",), vmem_limit_bytes=VMEM_LIMIT),
        name="outproj",
    )(oa, ob, proj, mkv, proj, x2d, w, g)


def _col_scale():
    cs = np.ones((1, PROJ_W), np.float32)
    cs[0, COL_AQ:COL_AQ + A_WIDTH] = A_HEAD_DIM ** -0.5 * LOG2E
    cs[0, COL_BQ:COL_BQ + B_WIDTH] = B_HEAD_DIM ** -0.5 * LOG2E
    cs[0, COL_CQ:COL_CQ + C_WIDTH] = C_HEAD_DIM ** -0.5 * LOG2E
    return jnp.asarray(cs)


def kernel(x, mem, g_norm, w_in, sinks, rel_bias, g_mem, w_mem_kv, w_out, g_final):
    batch, seq, _ = x.shape
    assert w_in.shape[0] == 1, "the out-projection kernel fuses the final norm of a single-layer trunk"
    kk = np.arange(2 * TQ_A)[:, None]
    qq = np.arange(TQ_A)[None, :]
    bkt_a = jnp.asarray(_t5_bucket_np(kk - TQ_A - qq))
    kk = np.arange(TK_B)[None, :, None]
    qq = np.arange(TQ_B)[None, None, :]
    d = np.arange(NEAR_B)[:, None, None]
    bkt_b = jnp.asarray(_t5_bucket_np(kk - qq - TK_B * d))

    h = x.reshape(batch * seq, D_MODEL)
    mem2d = mem.reshape(batch * N_MEM, D_MODEL)
    w_t = jnp.transpose(w_in, (2, 0, 1)).reshape(IN_WIDTH * KB_IN, LANES)
    proj = _inproj(h, g_norm[0].reshape(1, D_MODEL), _wprep(w_t), _col_scale())
    mkv = _memkv(mem2d, g_mem[0].reshape(1, D_MODEL), w_mem_kv[0])
    oa = _swa(proj, bkt_a, rel_bias, sinks[0], batch, seq)
    ob = _dsa(proj, bkt_b, rel_bias, batch, seq)
    out = _outproj(oa, ob, proj, mkv, h, w_out[0], g_final.reshape(1, D_MODEL), seq)
    return out.reshape(batch, seq, D_MODEL)
```

```python
import functools
import math

import numpy as np
import jax
import jax.numpy as jnp
from jax import lax
from jax.experimental import pallas as pl
from jax.experimental.pallas import tpu as pltpu

D_MODEL = 2048
CHUNK = 64
N_MEM = 256
EPS = 1e-6
A_HEADS = 16
A_KV_HEADS = 2
A_HEAD_DIM = 64
WINDOW_CHUNKS = 2
A_WIDTH = A_HEADS * A_HEAD_DIM
B_HEADS = 4
B_HEAD_DIM = 128
B_WIDTH = B_HEADS * B_HEAD_DIM
IDX_HEADS = 4
IDX_DIM = 64
TOPK_MAX = 256
C_HEADS = 4
C_HEAD_DIM = 128
C_WIDTH = C_HEADS * C_HEAD_DIM
MIX_WIDTH = A_WIDTH + B_WIDTH + C_WIDTH
N_BUCKETS = 32
MAX_DISTANCE = 1024
KV_A = A_KV_HEADS * A_HEAD_DIM
SPLIT_SIZES = (A_WIDTH, KV_A, KV_A, B_WIDTH, B_WIDTH, B_WIDTH,
               IDX_HEADS * IDX_DIM, IDX_DIM, IDX_HEADS, C_WIDTH, MIX_WIDTH)
IN_WIDTH = sum(SPLIT_SIZES)

F32 = jnp.float32
BF16 = jnp.bfloat16
LOG2E = math.log2(math.e)
NEG = -1e30
INT_MIN = -(2 ** 31)
LANES = 128
SUBLANES = 8
PACKED = 16
HALF = 1 << 15
KEY_POS_INF = 0x7F800000
KEY_NEG_INF = -KEY_POS_INF
FINE_PASSES = 17
PRE_PASSES = 2
F32_MANT_BITS = 23
F32_EXP_BIAS = 127

(SRC_AQ, SRC_AK, SRC_AV, SRC_BQ, SRC_BK, SRC_BV,
 SRC_IQ, SRC_IK, SRC_IW, SRC_CQ, SRC_GATE) = (int(c) for c in np.cumsum((0,) + SPLIT_SIZES)[:-1])
COL_GATE = 0
COL_AQ = COL_GATE + MIX_WIDTH
COL_BQ = COL_AQ + A_WIDTH
COL_BK = COL_BQ + B_WIDTH
COL_BV = COL_BK + B_WIDTH
COL_CQ = COL_BV + B_WIDTH
COL_AKV = COL_CQ + C_WIDTH
COL_IQ = COL_AKV + 2 * KV_A
COL_IKW = COL_IQ + IDX_HEADS * IDX_DIM
PROJ_W = 6144
SEGMENTS = ((COL_GATE, SRC_GATE, MIX_WIDTH), (COL_AQ, SRC_AQ, A_WIDTH), (COL_BQ, SRC_BQ, B_WIDTH),
            (COL_BK, SRC_BK, B_WIDTH), (COL_BV, SRC_BV, B_WIDTH), (COL_CQ, SRC_CQ, C_WIDTH),
            (COL_AKV, SRC_AK, 2 * KV_A), (COL_IQ, SRC_IQ, IDX_HEADS * IDX_DIM), (COL_IKW, SRC_IK, 256))
KB_IN = D_MODEL // LANES

TC_PREP = 256
TM_IN, TN_IN = 1024, 1024
SUB_IN = 256
TQ_A = 128
RB_A = 512
TQ_B = 256
TK_B = 256
NEAR_B = 4
TM_OUT = 512
SUB_OUT = 256
VMEM_LIMIT = 56 * 1024 * 1024


def _t5_bucket_np(rel):
    nb = N_BUCKETS // 2
    max_exact = nb // 2
    side = np.where(rel > 0, nb, 0)
    n = np.abs(rel)
    nf = np.maximum(n, max_exact).astype(np.float32)
    large = max_exact + (np.log(nf / max_exact) / math.log(MAX_DISTANCE / max_exact)
                         * (nb - max_exact)).astype(np.int32)
    large = np.minimum(large, nb - 1)
    return (side + np.where(n < max_exact, n, large)).astype(np.int32)


def _nt_dot(a, b):
    return lax.dot_general(a, b, (((1,), (1,)), ((), ())), preferred_element_type=F32)


def _bias_table(bucket, rb_ref, col, sub_row=None):
    acc = jnp.zeros(bucket.shape, F32)
    for b in range(N_BUCKETS):
        val = rb_ref[b, col]
        if sub_row is not None:
            val = val - rb_ref[sub_row, col]
        acc = jnp.where(bucket == b, val * LOG2E, acc)
    return acc


def _wprep_kernel(src_ref, w_ref, o_ref):
    t = pl.program_id(0)

    @pl.when(src_ref[t] >= 0)
    def _():
        for kb in range(KB_IN):
            o_ref[:, kb * LANES:(kb + 1) * LANES] = w_ref[pl.ds(kb, TC_PREP, stride=KB_IN), :].astype(BF16)

    @pl.when(src_ref[t] < 0)
    def _():
        o_ref[...] = jnp.zeros(o_ref.shape, BF16)


def _wprep(w_t):
    src = np.full((PROJ_W // TC_PREP,), -1, np.int32)
    for dst_col, src_col, width in SEGMENTS:
        for n in range(width // TC_PREP):
            src[dst_col // TC_PREP + n] = src_col + n * TC_PREP
    return pl.pallas_call(
        _wprep_kernel,
        grid_spec=pltpu.PrefetchScalarGridSpec(
            num_scalar_prefetch=1,
            grid=(PROJ_W // TC_PREP,),
            in_specs=[pl.BlockSpec((pl.Element(TC_PREP * KB_IN), pl.Element(LANES)),
                                   lambda t, src: (jnp.maximum(src[t], 0) * KB_IN, 0))],
            out_specs=pl.BlockSpec((TC_PREP, D_MODEL), lambda t, src: (t, 0)),
        ),
        out_shape=jax.ShapeDtypeStruct((PROJ_W, D_MODEL), BF16),
        compiler_params=pltpu.CompilerParams(
            dimension_semantics=("arbitrary",), vmem_limit_bytes=VMEM_LIMIT),
        name="wprep",
    )(jnp.asarray(src), w_t)


def _inproj_kernel(x_ref, g_ref, wt_ref, cs_ref, o_ref, hn_ref):
    j = pl.program_id(1)

    @pl.when(j == 0)
    def _():
        for r in range(TM_IN // SUB_IN):
            rows = slice(r * SUB_IN, (r + 1) * SUB_IN)
            x = x_ref[rows, :]
            ms = jnp.mean(x * x, axis=-1, keepdims=True)
            hn = (x * lax.rsqrt(ms + EPS) * g_ref[...]).astype(BF16)
            hn_ref[rows, :] = hn
            o_ref[rows, :] = (_nt_dot(hn, wt_ref[...]) * cs_ref[...]).astype(BF16)

    @pl.when(j > 0)
    def _():
        o_ref[...] = (_nt_dot(hn_ref[...], wt_ref[...]) * cs_ref[...]).astype(BF16)


def _inproj(x2d, g, wt, cs):
    m = x2d.shape[0]
    return pl.pallas_call(
        _inproj_kernel,
        grid=(m // TM_IN, PROJ_W // TN_IN),
        in_specs=[
            pl.BlockSpec((TM_IN, D_MODEL), lambda i, j: (i, 0)),
            pl.BlockSpec((1, D_MODEL), lambda i, j: (0, 0)),
            pl.BlockSpec((TN_IN, D_MODEL), lambda i, j: (j, 0)),
            pl.BlockSpec((1, TN_IN), lambda i, j: (0, j)),
        ],
        out_specs=pl.BlockSpec((TM_IN, TN_IN), lambda i, j: (i, j)),
        out_shape=jax.ShapeDtypeStruct((m, PROJ_W), BF16),
        scratch_shapes=[pltpu.VMEM((TM_IN, D_MODEL), BF16)],
        compiler_params=pltpu.CompilerParams(
            dimension_semantics=("parallel", "arbitrary"), vmem_limit_bytes=VMEM_LIMIT),
        name="inproj",
    )(x2d, g, wt, cs)


def _memkv_kernel(m_ref, g_ref, w_ref, o_ref):
    x = m_ref[...]
    ms = jnp.mean(x * x, axis=-1, keepdims=True)
    hn = (x * lax.rsqrt(ms + EPS) * g_ref[...]).astype(BF16)
    o_ref[...] = jnp.dot(hn, w_ref[...].astype(BF16), preferred_element_type=F32).astype(BF16)


def _memkv(mem2d, g, w):
    m = mem2d.shape[0]
    return pl.pallas_call(
        _memkv_kernel,
        grid=(m // N_MEM,),
        in_specs=[
            pl.BlockSpec((N_MEM, D_MODEL), lambda i: (i, 0)),
            pl.BlockSpec((1, D_MODEL), lambda i: (0, 0)),
            pl.BlockSpec((D_MODEL, 2 * C_WIDTH), lambda i: (0, 0)),
        ],
        out_specs=pl.BlockSpec((N_MEM, 2 * C_WIDTH), lambda i: (i, 0)),
        out_shape=jax.ShapeDtypeStruct((m, 2 * C_WIDTH), BF16),
        compiler_params=pltpu.CompilerParams(
            dimension_semantics=("arbitrary",), vmem_limit_bytes=VMEM_LIMIT),
        name="memkv",
    )(mem2d, g, w)


GRP_A = A_HEADS // A_KV_HEADS
PAIRS_A = GRP_A // 2
COLS_A = GRP_A * TQ_A
KEYS_A = 2 * TQ_A
CHUNK_PAIRS_A = 4


def _swa_kernel(q_ref, kvp_ref, kvc_ref, bkt_ref, rb_ref, sink_ref, o_ref,
                tab_ref, sinkv_ref, kp_ref, vt_ref, s_ref, m_ref, ot_ref):
    i = pl.program_id(1)

    @pl.when((pl.program_id(0) == 0) & (i == 0))
    def _():
        bkt = bkt_ref[...]
        kchunk = lax.broadcasted_iota(jnp.int32, bkt.shape, 0) // CHUNK
        qchunk = lax.broadcasted_iota(jnp.int32, bkt.shape, 1) // CHUNK
        allowed = (kchunk >= qchunk) & (kchunk <= qchunk + WINDOW_CHUNKS)
        has_prev = lax.broadcasted_iota(jnp.int32, bkt.shape, 0) >= TQ_A
        for g in range(A_KV_HEADS):
            for col in range(GRP_A):
                h = g * GRP_A + 2 * (col % PAIRS_A) + col // PAIRS_A
                cols = slice(col * TQ_A, (col + 1) * TQ_A)
                t = jnp.where(allowed, _bias_table(bkt, rb_ref, h), NEG)
                tab_ref[0, g, :, cols] = t
                tab_ref[1, g, :, cols] = jnp.where(has_prev, t, NEG)
                sinkv_ref[g, :, cols] = jnp.full((SUBLANES, TQ_A), sink_ref[h] * LOG2E, F32)

    width = CHUNK_PAIRS_A * TQ_A
    tasks = [(t, g, p0, parity) for t in range(RB_A // TQ_A) for g in range(A_KV_HEADS)
             for p0 in range(0, PAIRS_A, CHUNK_PAIRS_A) for parity in range(2)]

    def prepare(t, g):
        slot = (t * A_KV_HEADS + g) % 2
        if t == 0:
            kv_t = jnp.concatenate([kvp_ref[...], kvc_ref[0:TQ_A, :]], axis=0)
        else:
            kv_t = kvc_ref[(t - 1) * TQ_A:(t + 1) * TQ_A, :]
        kg = kv_t[:, g * A_HEAD_DIM:(g + 1) * A_HEAD_DIM]
        zeros = jnp.zeros((KEYS_A, A_HEAD_DIM), BF16)
        kp_ref[slot, 0] = jnp.concatenate([kg, zeros], axis=1)
        kp_ref[slot, 1] = jnp.concatenate([zeros, kg], axis=1)
        r = lax.broadcasted_iota(jnp.int32, (A_HEAD_DIM, KV_A), 0)
        c = lax.broadcasted_iota(jnp.int32, (A_HEAD_DIM, KV_A), 1)
        pick = jnp.where(c == g * A_HEAD_DIM + r, 1.0, 0.0).astype(BF16)
        vt = _nt_dot(pick, kv_t[:, KV_A:]).astype(BF16)
        vt_ref[slot] = jnp.concatenate([vt, jnp.ones((PACKED, KEYS_A), BF16)], axis=0)

    def chunk_cols(p0, parity):
        lo = (parity * PAIRS_A + p0) * TQ_A
        return slice(lo, lo + width)

    def logits_stage(n):
        t, g, p0, parity = tasks[n]
        if (p0, parity) == (0, 0):
            prepare(t, g)
        slot, buf, cols = (t * A_KV_HEADS + g) % 2, n % 2, chunk_cols(p0, parity)
        q_pairs = jnp.concatenate(
            [q_ref[t * TQ_A:(t + 1) * TQ_A, (g * PAIRS_A + p) * LANES:(g * PAIRS_A + p + 1) * LANES]
             for p in range(p0, p0 + CHUNK_PAIRS_A)], axis=0)
        variant = jnp.where(i == 0, 1, 0) if t == 0 else 0
        s = _nt_dot(kp_ref[slot, parity], q_pairs) + tab_ref[variant, g, :, cols]
        s_ref[buf] = s
        m = jnp.max(s.reshape(KEYS_A // SUBLANES, SUBLANES, width), axis=0)
        m_ref[buf] = jnp.maximum(jnp.broadcast_to(jnp.max(m, axis=0, keepdims=True), m.shape),
                                 sinkv_ref[g, :, cols])

    def update_stage(n):
        t, g, p0, parity = tasks[n]
        slot, buf, cols = (t * A_KV_HEADS + g) % 2, n % 2, chunk_cols(p0, parity)
        m = m_ref[buf]
        s = s_ref[buf].reshape(KEYS_A // SUBLANES, SUBLANES, width)
        p = jnp.exp2(s - m[None]).reshape(KEYS_A, width).astype(BF16)
        pv = jnp.dot(vt_ref[slot], p, preferred_element_type=F32)
        den = pv[A_HEAD_DIM:A_HEAD_DIM + SUBLANES] + jnp.exp2(sinkv_ref[g, :, cols] - m)
        out_t = pv[:A_HEAD_DIM].reshape(A_HEAD_DIM // SUBLANES, SUBLANES, width) / den[None]
        ot_ref[parity] = out_t.reshape(A_HEAD_DIM, width)
        if parity == 1:
            for k in range(CHUNK_PAIRS_A):
                blk = jnp.concatenate([ot_ref[0, :, k * TQ_A:(k + 1) * TQ_A],
                                       ot_ref[1, :, k * TQ_A:(k + 1) * TQ_A]], axis=0)
                lanes = slice((g * PAIRS_A + p0 + k) * LANES, (g * PAIRS_A + p0 + k + 1) * LANES)
                o_ref[t * TQ_A:(t + 1) * TQ_A, lanes] = blk.T.astype(BF16)

    logits_stage(0)
    for n in range(len(tasks)):
        if n + 1 < len(tasks):
            logits_stage(n + 1)
        update_stage(n)


def _swa(proj, bkt, rel_bias, sinks, batch, seq):
    nt = seq // RB_A
    sub = RB_A // TQ_A
    kv_blk = COL_AKV // (2 * KV_A)
    return pl.pallas_call(
        _swa_kernel,
        grid=(batch, nt),
        in_specs=[
            pl.BlockSpec((RB_A, A_WIDTH), lambda b, i: (b * nt + i, COL_AQ // A_WIDTH)),
            pl.BlockSpec((TQ_A, 2 * KV_A), lambda b, i: (jnp.maximum((b * nt + i) * sub - 1, 0), kv_blk)),
            pl.BlockSpec((RB_A, 2 * KV_A), lambda b, i: (b * nt + i, kv_blk)),
            pl.BlockSpec((KEYS_A, TQ_A), lambda b, i: (0, 0)),
            pl.BlockSpec(memory_space=pltpu.SMEM),
            pl.BlockSpec(memory_space=pltpu.SMEM),
        ],
        out_specs=pl.BlockSpec((RB_A, A_WIDTH), lambda b, i: (b * nt + i, 0)),
        out_shape=jax.ShapeDtypeStruct((batch * seq, A_WIDTH), BF16),
        scratch_shapes=[
            pltpu.VMEM((2, A_KV_HEADS, KEYS_A, COLS_A), F32),
            pltpu.VMEM((A_KV_HEADS, SUBLANES, COLS_A), F32),
            pltpu.VMEM((2, 2, KEYS_A, LANES), BF16),
            pltpu.VMEM((2, A_HEAD_DIM + PACKED, KEYS_A), BF16),
            pltpu.VMEM((2, KEYS_A, CHUNK_PAIRS_A * TQ_A), F32),
            pltpu.VMEM((2, SUBLANES, CHUNK_PAIRS_A * TQ_A), F32),
            pltpu.VMEM((2, A_HEAD_DIM, CHUNK_PAIRS_A * TQ_A), F32),
        ],
        compiler_params=pltpu.CompilerParams(
            dimension_semantics=("arbitrary", "arbitrary"), vmem_limit_bytes=VMEM_LIMIT),
        name="swa",
    )(proj, proj, proj, bkt, rel_bias, sinks)


def _dsa_kernel(iq_ref, iwq_ref, ik_ref, q_ref, k_ref, v_ref, bkt_ref, rb_ref, o_ref,
                sc_ref, scb_ref, d16_ref, mb_ref, tab_ref, tri_ref, vt_ref, acc_ref, m_ref,
                sa_ref, smaxa_ref, sb_ref, smaxb_ref, *, topk, seq):
    i = pl.program_id(1)
    nkt = i + 1
    int32, int16 = jnp.int32, jnp.int16
    grp = TK_B // SUBLANES

    def hcols(h):
        return slice(h * B_HEAD_DIM, (h + 1) * B_HEAD_DIM)

    def ktile(kt):
        return pl.ds(pl.multiple_of(kt * TK_B, TK_B), TK_B)

    def rows3(a):
        return a.reshape(a.shape[0] // SUBLANES, SUBLANES, TQ_B)

    def all_rows(a, op):
        return jnp.broadcast_to(op(a, axis=0, keepdims=True), a.shape)

    @pl.when((pl.program_id(0) == 0) & (i == 0))
    def _():
        far_bucket = N_BUCKETS // 2 - 1
        for d in range(NEAR_B):
            for h in range(B_HEADS):
                tab_ref[d, h] = _bias_table(bkt_ref[d], rb_ref, A_HEADS + h, sub_row=far_bucket)
        tab_ref[NEAR_B] = jnp.zeros(tab_ref.shape[1:], F32)
        r = lax.broadcasted_iota(int32, (TK_B, TK_B), 0)
        c = lax.broadcasted_iota(int32, (TK_B, TK_B), 1)
        tri_ref[...] = jnp.where(c < r, 1.0, 0.0).astype(BF16)

    @pl.when(i == 0)
    def _():
        r = lax.broadcasted_iota(int32, (B_HEAD_DIM, B_HEAD_DIM), 0)
        c = lax.broadcasted_iota(int32, (B_HEAD_DIM, B_HEAD_DIM), 1)
        eye = jnp.where(r == c, 1.0, 0.0).astype(BF16)

        def body(kt, carry):
            for h in range(B_HEADS):
                vt_ref[kt, h] = _nt_dot(eye, v_ref[ktile(kt), hcols(h)]).astype(BF16)
            return carry

        lax.fori_loop(0, seq // TK_B, body, 0)

    key_chunk = lax.broadcasted_iota(int32, (TK_B, TQ_B), 0) // CHUNK
    qry_chunk = lax.broadcasted_iota(int32, (TK_B, TQ_B), 1) // CHUNK
    adm_diag = key_chunk <= qry_chunk

    r = lax.broadcasted_iota(int32, (IDX_HEADS * SUBLANES, LANES), 0)
    c = lax.broadcasted_iota(int32, (IDX_HEADS * SUBLANES, LANES), 1)
    pick_w = jnp.where(c == IDX_DIM + r // SUBLANES, 1.0, 0.0).astype(BF16)
    w_all = _nt_dot(pick_w, iwq_ref[...]) * (IDX_HEADS ** -0.5 * IDX_DIM ** -0.5)

    def for_tiles(n, body, unroll=2):
        def group(j, carry):
            for k in range(unroll):
                body(unroll * j + k)
            return carry

        def single(kt, carry):
            body(kt)
            return carry

        lax.fori_loop(0, n // unroll, group, 0)
        lax.fori_loop(n - n % unroll, n, single, 0)

    def score_tile(kt):
        ikt = ik_ref[ktile(kt), 0:IDX_DIM]
        sc = jnp.zeros((grp, SUBLANES, TQ_B), F32)
        for h in range(IDX_HEADS):
            x = _nt_dot(ikt, iq_ref[:, h * IDX_DIM:(h + 1) * IDX_DIM])
            sc = sc + w_all[h * SUBLANES:(h + 1) * SUBLANES][None] * jnp.maximum(rows3(x), 0.0)
        store_score(kt, sc.reshape(TK_B, TQ_B))

    def store_score(kt, sc):
        sc_ref[kt] = sc
        scb_ref[kt] = sc.astype(BF16)

    for_tiles(nkt, score_tile, unroll=4)
    store_score(i, jnp.where(adm_diag, sc_ref[i], -jnp.inf))

    def key_to_f32(key):
        return pltpu.bitcast(jnp.where(key >= 0, key, INT_MIN - key), F32)

    def over_tiles(tile_count, zero):
        def pair(j, acc):
            return acc + (tile_count(2 * j) + tile_count(2 * j + 1))
        acc = lax.fori_loop(0, nkt // 2, pair, zero)
        return lax.cond(nkt % 2 == 1, lambda a: a + tile_count(nkt - 1), lambda a: a, acc)

    def tree_sum(parts):
        while len(parts) > 1:
            parts = [parts[n] + parts[n + 1] for n in range(0, len(parts), 2)]
        return parts[0]

    def count_bf16(cand):
        def tile_count(kt):
            blk = scb_ref[kt].reshape(TK_B // PACKED, PACKED, TQ_B)
            return tree_sum([jnp.where(blk[g] >= cand, jnp.ones((), int16), jnp.zeros((), int16))
                             for g in range(TK_B // PACKED)])
        acc = over_tiles(tile_count, jnp.zeros((PACKED, TQ_B), int16))
        return all_rows(acc.astype(int32), jnp.sum)

    def count_f32(pred):
        def tile_count(kt):
            blk = rows3(sc_ref[kt])
            return tree_sum([jnp.where(pred(blk[g]), 1, 0) for g in range(grp)])
        return all_rows(over_tiles(tile_count, jnp.zeros((SUBLANES, TQ_B), int32)), jnp.sum)

    def coarse_pass(p, u):
        bit = jnp.left_shift(jnp.int32(1), 15 - p)
        cand = key_to_f32(((u | bit) - HALF) << 16).astype(BF16)
        return jnp.where(count_bf16(cand) >= topk, u | bit, u)

    u = lax.fori_loop(0, 16, coarse_pass, jnp.zeros((PACKED, TQ_B), int32))
    coarse_key = ((u - HALF) << 16)[:SUBLANES]

    lo = jnp.maximum(coarse_key - HALF, KEY_NEG_INF)
    hi = jnp.minimum(coarse_key, KEY_POS_INF - 2 * HALF) + 2 * HALF

    def fine_pass(p, lohi):
        lo, hi = lohi
        mid = lo + ((hi - lo) >> 1)
        cand = key_to_f32(mid)
        ok = count_f32(lambda blk: blk >= cand) >= topk
        return jnp.where(ok, mid, lo), jnp.where(ok, hi, mid)

    lo, hi = lax.fori_loop(0, PRE_PASSES, fine_pass, (lo, hi))

    mag = jnp.minimum(jnp.abs(lo), jnp.abs(hi))
    around_zero = ((lo <= 0) & (hi >= 0)) | (mag >> F32_MANT_BITS == 0)
    exp_field = jnp.where(around_zero, F32_MANT_BITS + 1, mag >> F32_MANT_BITS)
    any_irregular = jnp.max(jnp.where(exp_field <= F32_MANT_BITS, 1, 0)) > 0

    def finish_f32(lohi):
        lo, hi = lax.fori_loop(PRE_PASSES, FINE_PASSES, fine_pass, lohi)
        return key_to_f32(lo)

    def finish_int16(lohi):
        lo, hi = lohi
        centre = jnp.where(around_zero, 0.0, key_to_f32(lo + ((hi - lo) >> 1)))
        ulp = pltpu.bitcast((exp_field - F32_MANT_BITS) << F32_MANT_BITS, F32)
        inv_ulp = pltpu.bitcast((2 * F32_EXP_BIAS + F32_MANT_BITS - exp_field) << F32_MANT_BITS, F32)

        def build(kt):
            d = (rows3(sc_ref[kt]) - centre[None]) * inv_ulp[None]
            d = jnp.clip(d, -HALF, HALF - 1).reshape(TK_B, TQ_B)
            d16_ref[kt] = d.astype(int32).astype(int16)

        for_tiles(nkt, build)

        def count_d16(cand):
            def tile_count(kt):
                blk = d16_ref[kt].reshape(TK_B // PACKED, PACKED, TQ_B)
                return tree_sum([jnp.where(blk[g] >= cand, jnp.ones((), int16), jnp.zeros((), int16))
                                 for g in range(TK_B // PACKED)])
            acc = over_tiles(tile_count, jnp.zeros((PACKED, TQ_B), int16))
            return all_rows(acc.astype(int32), jnp.sum)

        def d_pass(p, u):
            bit = jnp.left_shift(jnp.int32(1), 15 - p)
            cand = ((u | bit) - HALF).astype(int16)
            return jnp.where(count_d16(cand) >= topk, u | bit, u)

        u = lax.fori_loop(0, 16, d_pass, jnp.zeros((PACKED, TQ_B), int32))
        return centre + (u - HALF)[:SUBLANES].astype(F32) * ulp

    tau2d = lax.cond(any_irregular, finish_f32, finish_int16, (lo, hi))
    tau = tau2d[None]

    need = (topk - count_f32(lambda blk: blk > tau2d)).astype(F32)[None]
    ones_l = jnp.ones((2 * SUBLANES, TK_B), BF16)

    def mask_tile(kt, run):
        blk = rows3(sc_ref[kt])
        eq = blk == tau
        eqf = jnp.where(eq, 1.0, 0.0).reshape(TK_B, TQ_B).astype(BF16)
        rank = rows3(jnp.dot(tri_ref[...], eqf, preferred_element_type=F32)) + run[None]
        sel = (blk > tau) | (eq & (rank < need))
        mb_ref[kt] = jnp.where(sel, 0.0, NEG).reshape(TK_B, TQ_B)
        return run + jnp.dot(ones_l, eqf, preferred_element_type=F32)[:SUBLANES]

    def mask_group(j, run):
        for k in range(4):
            run = mask_tile(4 * j + k, run)
        return run

    run = lax.fori_loop(0, nkt // 4, mask_group, jnp.zeros((SUBLANES, TQ_B), F32))
    lax.fori_loop(nkt - nkt % 4, nkt, mask_tile, run)

    mb_ref[i] = jnp.where(adm_diag, mb_ref[i], NEG)

    m_ref[...] = jnp.full(m_ref.shape, NEG, F32)
    acc_ref[...] = jnp.zeros(acc_ref.shape, F32)
    ones_rows = jnp.ones((PACKED, TK_B), BF16)

    buf_a, buf_b = (sa_ref, smaxa_ref), (sb_ref, smaxb_ref)

    def logits_stage(kt, buf, near):
        s_ref, smax_ref = buf
        mb = mb_ref[kt]
        for h in range(B_HEADS):
            s = _nt_dot(k_ref[ktile(kt), hcols(h)], q_ref[:, hcols(h)]) + mb
            if near:
                s = s + tab_ref[jnp.minimum(i - kt, NEAR_B), h]
            s_ref[h] = s
            smax_ref[h] = all_rows(jnp.max(rows3(s), axis=0), jnp.max)

    def update_stage(kt, buf):
        s_ref, smax_ref = buf
        for h in range(B_HEADS):
            m_old = m_ref[h]
            m_new = jnp.maximum(m_old, smax_ref[h])
            alpha = jnp.exp2(m_old - m_new)
            p = jnp.exp2(rows3(s_ref[h]) - m_new[None]).reshape(TK_B, TQ_B).astype(BF16)
            vaug = jnp.concatenate([vt_ref[kt, h], ones_rows], axis=0)
            pv = jnp.dot(vaug, p, preferred_element_type=F32)
            acc_ref[h] = (rows3(acc_ref[h]) * alpha[None] + rows3(pv)).reshape(acc_ref.shape[1:])
            m_ref[h] = m_new

    def pair_body(j, carry, near):
        logits_stage(2 * j + 1, buf_b, near)
        update_stage(2 * j, buf_a)
        logits_stage(2 * j + 2, buf_a, near)
        update_stage(2 * j + 1, buf_b)
        return carry

    near_lo = jnp.maximum(i - (NEAR_B - 1), 0)
    far_pairs = jnp.maximum(near_lo - 1, 0) // 2
    logits_stage(0, buf_a, True)
    lax.fori_loop(0, far_pairs, functools.partial(pair_body, near=False), 0)
    lax.fori_loop(far_pairs, (nkt - 1) // 2, functools.partial(pair_body, near=True), 0)

    @pl.when(nkt % 2 == 0)
    def _():
        logits_stage(nkt - 1, buf_b, True)
        update_stage(nkt - 2, buf_a)
        update_stage(nkt - 1, buf_b)

    @pl.when(nkt % 2 == 1)
    def _():
        update_stage(nkt - 1, buf_a)

    for h in range(B_HEADS):
        num = rows3(acc_ref[h, 0:B_HEAD_DIM, :])
        den = acc_ref[h, B_HEAD_DIM:B_HEAD_DIM + SUBLANES, :]
        out_t = (num / den[None]).reshape(B_HEAD_DIM, TQ_B)
        o_ref[:, h * B_HEAD_DIM:(h + 1) * B_HEAD_DIM] = out_t.T.astype(BF16)


def _dsa(proj, bkt, rel_bias, batch, seq):
    nt = seq // TQ_B
    topk = min(TOPK_MAX, seq // 4)

    def q_spec(width, col):
        return pl.BlockSpec((TQ_B, width), lambda b, i: (b * nt + i, col // width))

    def seq_spec(width, col):
        return pl.BlockSpec((seq, width), lambda b, i: (b, col // width))

    return pl.pallas_call(
        functools.partial(_dsa_kernel, topk=topk, seq=seq),
        grid=(batch, nt),
        in_specs=[
            q_spec(IDX_HEADS * IDX_DIM, COL_IQ),
            q_spec(LANES, COL_IKW),
            seq_spec(LANES, COL_IKW),
            q_spec(B_WIDTH, COL_BQ), seq_spec(B_WIDTH, COL_BK), seq_spec(B_WIDTH, COL_BV),
            pl.BlockSpec((NEAR_B, TK_B, TQ_B), lambda b, i: (0, 0, 0)),
            pl.BlockSpec(memory_space=pltpu.SMEM),
        ],
        out_specs=pl.BlockSpec((TQ_B, B_WIDTH), lambda b, i: (b * nt + i, 0)),
        out_shape=jax.ShapeDtypeStruct((batch * seq, B_WIDTH), BF16),
        scratch_shapes=[
            pltpu.VMEM((nt, TK_B, TQ_B), F32),
            pltpu.VMEM((nt, TK_B, TQ_B), BF16),
            pltpu.VMEM((nt, TK_B, TQ_B), jnp.int16),
            pltpu.VMEM((nt, TK_B, TQ_B), F32),
            pltpu.VMEM((NEAR_B + 1, B_HEADS, TK_B, TQ_B), F32),
            pltpu.VMEM((TK_B, TK_B), BF16),
            pltpu.VMEM((seq // TK_B, B_HEADS, B_HEAD_DIM, TK_B), BF16),
            pltpu.VMEM((B_HEADS, B_HEAD_DIM + PACKED, TQ_B), F32),
            pltpu.VMEM((B_HEADS, SUBLANES, TQ_B), F32),
            pltpu.VMEM((B_HEADS, TK_B, TQ_B), F32),
            pltpu.VMEM((B_HEADS, SUBLANES, TQ_B), F32),
            pltpu.VMEM((B_HEADS, TK_B, TQ_B), F32),
            pltpu.VMEM((B_HEADS, SUBLANES, TQ_B), F32),
        ],
        compiler_params=pltpu.CompilerParams(
            dimension_semantics=("arbitrary", "arbitrary"), vmem_limit_bytes=VMEM_LIMIT),
        name="dsa",
    )(proj, proj, proj, proj, proj, proj, bkt, rel_bias)


def _outproj_kernel(oa_ref, ob_ref, cq_ref, mkv_ref, gate_ref, x_ref, w32_ref, g_ref, o_ref, w_ref):
    @pl.when(pl.program_id(0) == 0)
    def _():
        w_ref[...] = w32_ref[...].astype(BF16)

    ones_m = jnp.ones((N_MEM, C_HEAD_DIM), BF16)
    for r in range(TM_OUT // SUB_OUT):
        rows = slice(r * SUB_OUT, (r + 1) * SUB_OUT)
        att = [oa_ref[rows, :].astype(F32), ob_ref[rows, :].astype(F32)]
        for h in range(C_HEADS):
            hs = slice(h * C_HEAD_DIM, (h + 1) * C_HEAD_DIM)
            s = _nt_dot(cq_ref[rows, hs], mkv_ref[:, hs])
            p = jnp.exp2(s - jnp.max(s, axis=1, keepdims=True)).astype(BF16)
            vaug = jnp.concatenate([mkv_ref[:, C_WIDTH + h * C_HEAD_DIM:C_WIDTH + (h + 1) * C_HEAD_DIM], ones_m],
                                   axis=1)
            pv = jnp.dot(p, vaug, preferred_element_type=F32)
            att.append(pv[:, :C_HEAD_DIM] / pv[:, C_HEAD_DIM:])
        gate = gate_ref[rows, :].astype(F32)
        sg = gate / (1.0 + jnp.exp(-gate))
        y = (jnp.concatenate(att, axis=1) * sg).astype(BF16)
        h_new = x_ref[rows, :] + jnp.dot(y, w_ref[...], preferred_element_type=F32)
        ms = jnp.mean(h_new * h_new, axis=-1, keepdims=True)
        o_ref[rows, :] = h_new * lax.rsqrt(ms + EPS) * g_ref[...]


def _outproj(oa, ob, proj, mkv, x2d, w, g, seq):
    m = x2d.shape[0]
    steps_per_seq = seq // TM_OUT
    return pl.pallas_call(
        _outproj_kernel,
        grid=(m // TM_OUT,),
        in_specs=[
            pl.BlockSpec((TM_OUT, A_WIDTH), lambda i: (i, 0)),
            pl.BlockSpec((TM_OUT, B_WIDTH), lambda i: (i, 0)),
            pl.BlockSpec((TM_OUT, C_WIDTH), lambda i: (i, COL_CQ // C_WIDTH)),
            pl.BlockSpec((N_MEM, 2 * C_WIDTH), lambda i: (i // steps_per_seq, 0)),
            pl.BlockSpec((TM_OUT, MIX_WIDTH), lambda i: (i, COL_GATE // MIX_WIDTH)),
            pl.BlockSpec((TM_OUT, D_MODEL), lambda i: (i, 0)),
            pl.BlockSpec((MIX_WIDTH, D_MODEL), lambda i: (0, 0), pipeline_mode=pl.Buffered(1)),
            pl.BlockSpec((1, D_MODEL), lambda i: (0, 0)),
        ],
        out_specs=pl.BlockSpec((TM_OUT, D_MODEL), lambda i: (i, 0)),
        out_shape=jax.ShapeDtypeStruct((m, D_MODEL), F32),
        scratch_shapes=[pltpu.VMEM((MIX_WIDTH, D_MODEL), BF16)],
        compiler_params=pltpu.CompilerParams(
            dimension_semantics=("arbitrary",), vmem_limit_bytes=VMEM_LIMIT),
        name="outproj",
    )(oa, ob, proj, mkv, proj, x2d, w, g)


def _col_scale():
    cs = np.ones((1, PROJ_W), np.float32)
    cs[0, COL_AQ:COL_AQ + A_WIDTH] = A_HEAD_DIM ** -0.5 * LOG2E
    cs[0, COL_BQ:COL_BQ + B_WIDTH] = B_HEAD_DIM ** -0.5 * LOG2E
    cs[0, COL_CQ:COL_CQ + C_WIDTH] = C_HEAD_DIM ** -0.5 * LOG2E
    return jnp.asarray(cs)


def kernel(x, mem, g_norm, w_in, sinks, rel_bias, g_mem, w_mem_kv, w_out, g_final):
    batch, seq, _ = x.shape
    assert w_in.shape[0] == 1, "the out-projection kernel fuses the final norm of a single-layer trunk"
    kk = np.arange(2 * TQ_A)[:, None]
    qq = np.arange(TQ_A)[None, :]
    bkt_a = jnp.asarray(_t5_bucket_np(kk - TQ_A - qq))
    kk = np.arange(TK_B)[None, :, None]
    qq = np.arange(TQ_B)[None, None, :]
    d = np.arange(NEAR_B)[:, None, None]
    bkt_b = jnp.asarray(_t5_bucket_np(kk - qq - TK_B * d))

    h = x.reshape(batch * seq, D_MODEL)
    mem2d = mem.reshape(batch * N_MEM, D_MODEL)
    w_t = jnp.transpose(w_in, (2, 0, 1)).reshape(IN_WIDTH * KB_IN, LANES)
    proj = _inproj(h, g_norm[0].reshape(1, D_MODEL), _wprep(w_t), _col_scale())
    mkv = _memkv(mem2d, g_mem[0].reshape(1, D_MODEL), w_mem_kv[0])
    oa = _swa(proj, bkt_a, rel_bias, sinks[0], batch, seq)
    ob = _dsa(proj, bkt_b, rel_bias, batch, seq)
    out = _outproj(oa, ob, proj, mkv, h, w_out[0], g_final.reshape(1, D_MODEL), seq)
    return out.reshape(batch, seq, D_MODEL)
```

```python
import functools
import math

import numpy as np
import jax
import jax.numpy as jnp
from jax import lax
from jax.experimental import pallas as pl
from jax.experimental.pallas import tpu as pltpu

D_MODEL = 2048
CHUNK = 64
N_MEM = 256
EPS = 1e-6
A_HEADS = 16
A_KV_HEADS = 2
A_HEAD_DIM = 64
WINDOW_CHUNKS = 2
A_WIDTH = A_HEADS * A_HEAD_DIM
B_HEADS = 4
B_HEAD_DIM = 128
B_WIDTH = B_HEADS * B_HEAD_DIM
IDX_HEADS = 4
IDX_DIM = 64
TOPK_MAX = 256
C_HEADS = 4
C_HEAD_DIM = 128
C_WIDTH = C_HEADS * C_HEAD_DIM
MIX_WIDTH = A_WIDTH + B_WIDTH + C_WIDTH
N_BUCKETS = 32
MAX_DISTANCE = 1024
KV_A = A_KV_HEADS * A_HEAD_DIM
SPLIT_SIZES = (A_WIDTH, KV_A, KV_A, B_WIDTH, B_WIDTH, B_WIDTH,
               IDX_HEADS * IDX_DIM, IDX_DIM, IDX_HEADS, C_WIDTH, MIX_WIDTH)
IN_WIDTH = sum(SPLIT_SIZES)

F32 = jnp.float32
BF16 = jnp.bfloat16
LOG2E = math.log2(math.e)
NEG = -1e30
INT_MIN = -(2 ** 31)
LANES = 128
SUBLANES = 8
PACKED = 16
HALF = 1 << 15
KEY_POS_INF = 0x7F800000
KEY_NEG_INF = -KEY_POS_INF
FINE_PASSES = 17
PRE_PASSES = 2
F32_MANT_BITS = 23
F32_EXP_BIAS = 127

(SRC_AQ, SRC_AK, SRC_AV, SRC_BQ, SRC_BK, SRC_BV,
 SRC_IQ, SRC_IK, SRC_IW, SRC_CQ, SRC_GATE) = (int(c) for c in np.cumsum((0,) + SPLIT_SIZES)[:-1])
COL_GATE = 0
COL_AQ = COL_GATE + MIX_WIDTH
COL_BQ = COL_AQ + A_WIDTH
COL_BK = COL_BQ + B_WIDTH
COL_BV = COL_BK + B_WIDTH
COL_CQ = COL_BV + B_WIDTH
COL_AKV = COL_CQ + C_WIDTH
COL_IQ = COL_AKV + 2 * KV_A
COL_IKW = COL_IQ + IDX_HEADS * IDX_DIM
PROJ_W = 6144
SEGMENTS = ((COL_GATE, SRC_GATE, MIX_WIDTH), (COL_AQ, SRC_AQ, A_WIDTH), (COL_BQ, SRC_BQ, B_WIDTH),
            (COL_BK, SRC_BK, B_WIDTH), (COL_BV, SRC_BV, B_WIDTH), (COL_CQ, SRC_CQ, C_WIDTH),
            (COL_AKV, SRC_AK, 2 * KV_A), (COL_IQ, SRC_IQ, IDX_HEADS * IDX_DIM), (COL_IKW, SRC_IK, 256))
KB_IN = D_MODEL // LANES

TC_PREP = 256
TM_IN, TN_IN = 1024, 1024
SUB_IN = 256
TQ_A = 128
RB_A = 512
TQ_B = 256
TK_B = 256
NEAR_B = 4
TM_OUT = 512
SUB_OUT = 256
VMEM_LIMIT = 56 * 1024 * 1024


def _t5_bucket_np(rel):
    nb = N_BUCKETS // 2
    max_exact = nb // 2
    side = np.where(rel > 0, nb, 0)
    n = np.abs(rel)
    nf = np.maximum(n, max_exact).astype(np.float32)
    large = max_exact + (np.log(nf / max_exact) / math.log(MAX_DISTANCE / max_exact)
                         * (nb - max_exact)).astype(np.int32)
    large = np.minimum(large, nb - 1)
    return (side + np.where(n < max_exact, n, large)).astype(np.int32)


def _nt_dot(a, b):
    return lax.dot_general(a, b, (((1,), (1,)), ((), ())), preferred_element_type=F32)


def _bias_tables(bucket, present, rb_ref, cols, sub_row=None):
    accs = [jnp.zeros(bucket.shape, F32) for _ in cols]
    for b in present:
        hit = bucket == b
        for n, col in enumerate(cols):
            val = rb_ref[b, col]
            if sub_row is not None:
                val = val - rb_ref[sub_row, col]
            accs[n] = jnp.where(hit, val * LOG2E, accs[n])
    return accs


def _wprep_kernel(src_ref, w_ref, o_ref):
    t = pl.program_id(0)

    @pl.when(src_ref[t] >= 0)
    def _():
        for kb in range(KB_IN):
            o_ref[:, kb * LANES:(kb + 1) * LANES] = w_ref[pl.ds(kb, TC_PREP, stride=KB_IN), :].astype(BF16)

    @pl.when(src_ref[t] < 0)
    def _():
        o_ref[...] = jnp.zeros(o_ref.shape, BF16)


def _wprep(w_t):
    src = np.full((PROJ_W // TC_PREP,), -1, np.int32)
    for dst_col, src_col, width in SEGMENTS:
        for n in range(width // TC_PREP):
            src[dst_col // TC_PREP + n] = src_col + n * TC_PREP
    return pl.pallas_call(
        _wprep_kernel,
        grid_spec=pltpu.PrefetchScalarGridSpec(
            num_scalar_prefetch=1,
            grid=(PROJ_W // TC_PREP,),
            in_specs=[pl.BlockSpec((pl.Element(TC_PREP * KB_IN), pl.Element(LANES)),
                                   lambda t, src: (jnp.maximum(src[t], 0) * KB_IN, 0))],
            out_specs=pl.BlockSpec((TC_PREP, D_MODEL), lambda t, src: (t, 0)),
        ),
        out_shape=jax.ShapeDtypeStruct((PROJ_W, D_MODEL), BF16),
        compiler_params=pltpu.CompilerParams(
            dimension_semantics=("arbitrary",), vmem_limit_bytes=VMEM_LIMIT),
        name="wprep",
    )(jnp.asarray(src), w_t)


def _inproj_kernel(x_ref, g_ref, wt_ref, cs_ref, o_ref, hn_ref):
    j = pl.program_id(1)

    @pl.when(j == 0)
    def _():
        for r in range(TM_IN // SUB_IN):
            rows = slice(r * SUB_IN, (r + 1) * SUB_IN)
            x = x_ref[rows, :]
            ms = jnp.mean(x * x, axis=-1, keepdims=True)
            hn = (x * lax.rsqrt(ms + EPS) * g_ref[...]).astype(BF16)
            hn_ref[rows, :] = hn
            o_ref[rows, :] = (_nt_dot(hn, wt_ref[...]) * cs_ref[...]).astype(BF16)

    @pl.when(j > 0)
    def _():
        o_ref[...] = (_nt_dot(hn_ref[...], wt_ref[...]) * cs_ref[...]).astype(BF16)


def _inproj(x2d, g, wt, cs):
    m = x2d.shape[0]
    return pl.pallas_call(
        _inproj_kernel,
        grid=(m // TM_IN, PROJ_W // TN_IN),
        in_specs=[
            pl.BlockSpec((TM_IN, D_MODEL), lambda i, j: (i, 0)),
            pl.BlockSpec((1, D_MODEL), lambda i, j: (0, 0)),
            pl.BlockSpec((TN_IN, D_MODEL), lambda i, j: (j, 0)),
            pl.BlockSpec((1, TN_IN), lambda i, j: (0, j)),
        ],
        out_specs=pl.BlockSpec((TM_IN, TN_IN), lambda i, j: (i, j)),
        out_shape=jax.ShapeDtypeStruct((m, PROJ_W), BF16),
        scratch_shapes=[pltpu.VMEM((TM_IN, D_MODEL), BF16)],
        compiler_params=pltpu.CompilerParams(
            dimension_semantics=("parallel", "arbitrary"), vmem_limit_bytes=VMEM_LIMIT),
        name="inproj",
    )(x2d, g, wt, cs)


def _memkv_kernel(m_ref, g_ref, w_ref, o_ref):
    x = m_ref[...]
    ms = jnp.mean(x * x, axis=-1, keepdims=True)
    hn = (x * lax.rsqrt(ms + EPS) * g_ref[...]).astype(BF16)
    o_ref[...] = jnp.dot(hn, w_ref[...].astype(BF16), preferred_element_type=F32).astype(BF16)


def _memkv(mem2d, g, w):
    m = mem2d.shape[0]
    return pl.pallas_call(
        _memkv_kernel,
        grid=(m // N_MEM,),
        in_specs=[
            pl.BlockSpec((N_MEM, D_MODEL), lambda i: (i, 0)),
            pl.BlockSpec((1, D_MODEL), lambda i: (0, 0)),
            pl.BlockSpec((D_MODEL, 2 * C_WIDTH), lambda i: (0, 0)),
        ],
        out_specs=pl.BlockSpec((N_MEM, 2 * C_WIDTH), lambda i: (i, 0)),
        out_shape=jax.ShapeDtypeStruct((m, 2 * C_WIDTH), BF16),
        compiler_params=pltpu.CompilerParams(
            dimension_semantics=("arbitrary",), vmem_limit_bytes=VMEM_LIMIT),
        name="memkv",
    )(mem2d, g, w)


GRP_A = A_HEADS // A_KV_HEADS
PAIRS_A = GRP_A // 2
COLS_A = GRP_A * TQ_A
KEYS_A = 2 * TQ_A
CHUNK_PAIRS_A = 4


def _swa_kernel(q_ref, kvp_ref, kvc_ref, bkt_ref, rb_ref, sink_ref, o_ref,
                tab_ref, sinkv_ref, kp_ref, vt_ref, s_ref, m_ref, ot_ref, *, present):
    i = pl.program_id(1)

    @pl.when((pl.program_id(0) == 0) & (i == 0))
    def _():
        bkt = bkt_ref[...]
        kchunk = lax.broadcasted_iota(jnp.int32, bkt.shape, 0) // CHUNK
        qchunk = lax.broadcasted_iota(jnp.int32, bkt.shape, 1) // CHUNK
        allowed = (kchunk >= qchunk) & (kchunk <= qchunk + WINDOW_CHUNKS)
        has_prev = lax.broadcasted_iota(jnp.int32, bkt.shape, 0) >= TQ_A
        for g in range(A_KV_HEADS):
            for c0 in range(0, GRP_A, PAIRS_A):
                heads = [g * GRP_A + 2 * (col % PAIRS_A) + col // PAIRS_A for col in range(c0, c0 + PAIRS_A)]
                for n, t in enumerate(_bias_tables(bkt, present, rb_ref, heads)):
                    cols = slice((c0 + n) * TQ_A, (c0 + n + 1) * TQ_A)
                    t = jnp.where(allowed, t, NEG)
                    tab_ref[0, g, :, cols] = t
                    tab_ref[1, g, :, cols] = jnp.where(has_prev, t, NEG)
                    sinkv_ref[g, :, cols] = jnp.full((SUBLANES, TQ_A), sink_ref[heads[n]] * LOG2E, F32)

    width = CHUNK_PAIRS_A * TQ_A
    tasks = [(t, g, p0, parity) for t in range(RB_A // TQ_A) for g in range(A_KV_HEADS)
             for p0 in range(0, PAIRS_A, CHUNK_PAIRS_A) for parity in range(2)]

    def prepare(t, g):
        slot = (t * A_KV_HEADS + g) % 2
        if t == 0:
            kv_t = jnp.concatenate([kvp_ref[...], kvc_ref[0:TQ_A, :]], axis=0)
        else:
            kv_t = kvc_ref[(t - 1) * TQ_A:(t + 1) * TQ_A, :]
        kg = kv_t[:, g * A_HEAD_DIM:(g + 1) * A_HEAD_DIM]
        zeros = jnp.zeros((KEYS_A, A_HEAD_DIM), BF16)
        kp_ref[slot, 0] = jnp.concatenate([kg, zeros], axis=1)
        kp_ref[slot, 1] = jnp.concatenate([zeros, kg], axis=1)
        r = lax.broadcasted_iota(jnp.int32, (A_HEAD_DIM, KV_A), 0)
        c = lax.broadcasted_iota(jnp.int32, (A_HEAD_DIM, KV_A), 1)
        pick = jnp.where(c == g * A_HEAD_DIM + r, 1.0, 0.0).astype(BF16)
        vt = _nt_dot(pick, kv_t[:, KV_A:]).astype(BF16)
        vt_ref[slot] = jnp.concatenate([vt, jnp.ones((PACKED, KEYS_A), BF16)], axis=0)

    def chunk_cols(p0, parity):
        lo = (parity * PAIRS_A + p0) * TQ_A
        return slice(lo, lo + width)

    def logits_stage(n):
        t, g, p0, parity = tasks[n]
        if (p0, parity) == (0, 0):
            prepare(t, g)
        slot, buf, cols = (t * A_KV_HEADS + g) % 2, n % 2, chunk_cols(p0, parity)
        q_pairs = jnp.concatenate(
            [q_ref[t * TQ_A:(t + 1) * TQ_A, (g * PAIRS_A + p) * LANES:(g * PAIRS_A + p + 1) * LANES]
             for p in range(p0, p0 + CHUNK_PAIRS_A)], axis=0)
        variant = jnp.where(i == 0, 1, 0) if t == 0 else 0
        s = _nt_dot(kp_ref[slot, parity], q_pairs) + tab_ref[variant, g, :, cols]
        s_ref[buf] = s
        m = jnp.max(s.reshape(KEYS_A // SUBLANES, SUBLANES, width), axis=0)
        m_ref[buf] = jnp.maximum(jnp.broadcast_to(jnp.max(m, axis=0, keepdims=True), m.shape),
                                 sinkv_ref[g, :, cols])

    def update_stage(n):
        t, g, p0, parity = tasks[n]
        slot, buf, cols = (t * A_KV_HEADS + g) % 2, n % 2, chunk_cols(p0, parity)
        m = m_ref[buf]
        s = s_ref[buf].reshape(KEYS_A // SUBLANES, SUBLANES, width)
        p = jnp.exp2(s - m[None]).reshape(KEYS_A, width).astype(BF16)
        pv = jnp.dot(vt_ref[slot], p, preferred_element_type=F32)
        den = pv[A_HEAD_DIM:A_HEAD_DIM + SUBLANES] + jnp.exp2(sinkv_ref[g, :, cols] - m)
        out_t = pv[:A_HEAD_DIM].reshape(A_HEAD_DIM // SUBLANES, SUBLANES, width) / den[None]
        ot_ref[parity] = out_t.reshape(A_HEAD_DIM, width)
        if parity == 1:
            for k in range(CHUNK_PAIRS_A):
                blk = jnp.concatenate([ot_ref[0, :, k * TQ_A:(k + 1) * TQ_A],
                                       ot_ref[1, :, k * TQ_A:(k + 1) * TQ_A]], axis=0)
                lanes = slice((g * PAIRS_A + p0 + k) * LANES, (g * PAIRS_A + p0 + k + 1) * LANES)
                o_ref[t * TQ_A:(t + 1) * TQ_A, lanes] = blk.T.astype(BF16)

    logits_stage(0)
    for n in range(len(tasks)):
        if n + 1 < len(tasks):
            logits_stage(n + 1)
        update_stage(n)


def _swa(proj, bkt, rel_bias, sinks, batch, seq):
    nt = seq // RB_A
    sub = RB_A // TQ_A
    kv_blk = COL_AKV // (2 * KV_A)
    return pl.pallas_call(
        functools.partial(_swa_kernel, present=tuple(int(b) for b in np.unique(bkt))),
        grid=(batch, nt),
        in_specs=[
            pl.BlockSpec((RB_A, A_WIDTH), lambda b, i: (b * nt + i, COL_AQ // A_WIDTH)),
            pl.BlockSpec((TQ_A, 2 * KV_A), lambda b, i: (jnp.maximum((b * nt + i) * sub - 1, 0), kv_blk)),
            pl.BlockSpec((RB_A, 2 * KV_A), lambda b, i: (b * nt + i, kv_blk)),
            pl.BlockSpec((KEYS_A, TQ_A), lambda b, i: (0, 0)),
            pl.BlockSpec(memory_space=pltpu.SMEM),
            pl.BlockSpec(memory_space=pltpu.SMEM),
        ],
        out_specs=pl.BlockSpec((RB_A, A_WIDTH), lambda b, i: (b * nt + i, 0)),
        out_shape=jax.ShapeDtypeStruct((batch * seq, A_WIDTH), BF16),
        scratch_shapes=[
            pltpu.VMEM((2, A_KV_HEADS, KEYS_A, COLS_A), F32),
            pltpu.VMEM((A_KV_HEADS, SUBLANES, COLS_A), F32),
            pltpu.VMEM((2, 2, KEYS_A, LANES), BF16),
            pltpu.VMEM((2, A_HEAD_DIM + PACKED, KEYS_A), BF16),
            pltpu.VMEM((2, KEYS_A, CHUNK_PAIRS_A * TQ_A), F32),
            pltpu.VMEM((2, SUBLANES, CHUNK_PAIRS_A * TQ_A), F32),
            pltpu.VMEM((2, A_HEAD_DIM, CHUNK_PAIRS_A * TQ_A), F32),
        ],
        compiler_params=pltpu.CompilerParams(
            dimension_semantics=("arbitrary", "arbitrary"), vmem_limit_bytes=VMEM_LIMIT),
        name="swa",
    )(proj, proj, proj, jnp.asarray(bkt), rel_bias, sinks)


def _dsa_kernel(iq_ref, iwq_ref, ik_ref, q_ref, k_ref, v_ref, bkt_ref, rb_ref, o_ref,
                sc_ref, scb_ref, d16_ref, mb_ref, tab_ref, tri_ref, vt_ref, acc_ref, m_ref,
                sa_ref, smaxa_ref, sb_ref, smaxb_ref, *, topk, seq, present):
    i = pl.program_id(1)
    nkt = i + 1
    int32, int16 = jnp.int32, jnp.int16
    grp = TK_B // SUBLANES

    def hcols(h):
        return slice(h * B_HEAD_DIM, (h + 1) * B_HEAD_DIM)

    def ktile(kt):
        return pl.ds(pl.multiple_of(kt * TK_B, TK_B), TK_B)

    def rows3(a):
        return a.reshape(a.shape[0] // SUBLANES, SUBLANES, TQ_B)

    def all_rows(a, op):
        return jnp.broadcast_to(op(a, axis=0, keepdims=True), a.shape)

    @pl.when((pl.program_id(0) == 0) & (i == 0))
    def _():
        far_bucket = N_BUCKETS // 2 - 1
        for d in range(NEAR_B):
            for h0 in range(0, B_HEADS, 2):
                tabs = _bias_tables(bkt_ref[d], present[d], rb_ref, [A_HEADS + h0, A_HEADS + h0 + 1],
                                    sub_row=far_bucket)
                tab_ref[d, h0] = tabs[0]
                tab_ref[d, h0 + 1] = tabs[1]
        tab_ref[NEAR_B] = jnp.zeros(tab_ref.shape[1:], F32)
        r = lax.broadcasted_iota(int32, (TK_B, TK_B), 0)
        c = lax.broadcasted_iota(int32, (TK_B, TK_B), 1)
        tri_ref[...] = jnp.where(c < r, 1.0, 0.0).astype(BF16)

    @pl.when(i == 0)
    def _():
        r = lax.broadcasted_iota(int32, (B_HEAD_DIM, B_HEAD_DIM), 0)
        c = lax.broadcasted_iota(int32, (B_HEAD_DIM, B_HEAD_DIM), 1)
        eye = jnp.where(r == c, 1.0, 0.0).astype(BF16)

        def body(kt, carry):
            for h in range(B_HEADS):
                vt_ref[kt, h] = _nt_dot(eye, v_ref[ktile(kt), hcols(h)]).astype(BF16)
            return carry

        lax.fori_loop(0, seq // TK_B, body, 0)

    key_chunk = lax.broadcasted_iota(int32, (TK_B, TQ_B), 0) // CHUNK
    qry_chunk = lax.broadcasted_iota(int32, (TK_B, TQ_B), 1) // CHUNK
    adm_diag = key_chunk <= qry_chunk

    r = lax.broadcasted_iota(int32, (IDX_HEADS * SUBLANES, LANES), 0)
    c = lax.broadcasted_iota(int32, (IDX_HEADS * SUBLANES, LANES), 1)
    pick_w = jnp.where(c == IDX_DIM + r // SUBLANES, 1.0, 0.0).astype(BF16)
    w_all = _nt_dot(pick_w, iwq_ref[...]) * (IDX_HEADS ** -0.5 * IDX_DIM ** -0.5)

    def for_tiles(n, body, unroll=2):
        def group(j, carry):
            for k in range(unroll):
                body(unroll * j + k)
            return carry

        def single(kt, carry):
            body(kt)
            return carry

        lax.fori_loop(0, n // unroll, group, 0)
        lax.fori_loop(n - n % unroll, n, single, 0)

    def score_tile(kt):
        ikt = ik_ref[ktile(kt), 0:IDX_DIM]
        sc = jnp.zeros((grp, SUBLANES, TQ_B), F32)
        for h in range(IDX_HEADS):
            x = _nt_dot(ikt, iq_ref[:, h * IDX_DIM:(h + 1) * IDX_DIM])
            sc = sc + w_all[h * SUBLANES:(h + 1) * SUBLANES][None] * jnp.maximum(rows3(x), 0.0)
        store_score(kt, sc.reshape(TK_B, TQ_B))

    def store_score(kt, sc):
        sc_ref[kt] = sc
        scb_ref[kt] = sc.astype(BF16)

    for_tiles(nkt, score_tile, unroll=4)
    store_score(i, jnp.where(adm_diag, sc_ref[i], -jnp.inf))

    def key_to_f32(key):
        return pltpu.bitcast(jnp.where(key >= 0, key, INT_MIN - key), F32)

    def over_tiles(tile_count, zero):
        def pair(j, acc):
            return acc + (tile_count(2 * j) + tile_count(2 * j + 1))
        acc = lax.fori_loop(0, nkt // 2, pair, zero)
        return lax.cond(nkt % 2 == 1, lambda a: a + tile_count(nkt - 1), lambda a: a, acc)

    def tree_sum(parts):
        while len(parts) > 1:
            parts = [parts[n] + parts[n + 1] for n in range(0, len(parts), 2)]
        return parts[0]

    def count_bf16(cand):
        def tile_count(kt):
            blk = scb_ref[kt].reshape(TK_B // PACKED, PACKED, TQ_B)
            return tree_sum([jnp.where(blk[g] >= cand, jnp.ones((), int16), jnp.zeros((), int16))
                             for g in range(TK_B // PACKED)])
        acc = over_tiles(tile_count, jnp.zeros((PACKED, TQ_B), int16))
        return all_rows(acc.astype(int32), jnp.sum)

    def count_f32(pred):
        def tile_count(kt):
            blk = rows3(sc_ref[kt])
            return tree_sum([jnp.where(pred(blk[g]), 1, 0) for g in range(grp)])
        return all_rows(over_tiles(tile_count, jnp.zeros((SUBLANES, TQ_B), int32)), jnp.sum)

    def coarse_pass(p, u):
        bit = jnp.left_shift(jnp.int32(1), 15 - p)
        cand = key_to_f32(((u | bit) - HALF) << 16).astype(BF16)
        return jnp.where(count_bf16(cand) >= topk, u | bit, u)

    u = lax.fori_loop(0, 16, coarse_pass, jnp.zeros((PACKED, TQ_B), int32))
    coarse_key = ((u - HALF) << 16)[:SUBLANES]

    lo = jnp.maximum(coarse_key - HALF, KEY_NEG_INF)
    hi = jnp.minimum(coarse_key, KEY_POS_INF - 2 * HALF) + 2 * HALF

    def fine_pass(p, lohi):
        lo, hi = lohi
        mid = lo + ((hi - lo) >> 1)
        cand = key_to_f32(mid)
        ok = count_f32(lambda blk: blk >= cand) >= topk
        return jnp.where(ok, mid, lo), jnp.where(ok, hi, mid)

    lo, hi = lax.fori_loop(0, PRE_PASSES, fine_pass, (lo, hi))

    mag = jnp.minimum(jnp.abs(lo), jnp.abs(hi))
    around_zero = ((lo <= 0) & (hi >= 0)) | (mag >> F32_MANT_BITS == 0)
    exp_field = jnp.where(around_zero, F32_MANT_BITS + 1, mag >> F32_MANT_BITS)
    any_irregular = jnp.max(jnp.where(exp_field <= F32_MANT_BITS, 1, 0)) > 0

    def finish_f32(lohi):
        lo, hi = lax.fori_loop(PRE_PASSES, FINE_PASSES, fine_pass, lohi)
        return key_to_f32(lo)

    def finish_int16(lohi):
        lo, hi = lohi
        centre = jnp.where(around_zero, 0.0, key_to_f32(lo + ((hi - lo) >> 1)))
        ulp = pltpu.bitcast((exp_field - F32_MANT_BITS) << F32_MANT_BITS, F32)
        inv_ulp = pltpu.bitcast((2 * F32_EXP_BIAS + F32_MANT_BITS - exp_field) << F32_MANT_BITS, F32)

        def build(kt):
            d = (rows3(sc_ref[kt]) - centre[None]) * inv_ulp[None]
            d = jnp.clip(d, -HALF, HALF - 1).reshape(TK_B, TQ_B)
            d16_ref[kt] = d.astype(int32).astype(int16)

        for_tiles(nkt, build)

        def count_d16(cand):
            def tile_count(kt):
                blk = d16_ref[kt].reshape(TK_B // PACKED, PACKED, TQ_B)
                return tree_sum([jnp.where(blk[g] >= cand, jnp.ones((), int16), jnp.zeros((), int16))
                                 for g in range(TK_B // PACKED)])
            acc = over_tiles(tile_count, jnp.zeros((PACKED, TQ_B), int16))
            return all_rows(acc.astype(int32), jnp.sum)

        def d_pass(p, u):
            bit = jnp.left_shift(jnp.int32(1), 15 - p)
            cand = ((u | bit) - HALF).astype(int16)
            return jnp.where(count_d16(cand) >= topk, u | bit, u)

        u = lax.fori_loop(0, 16, d_pass, jnp.zeros((PACKED, TQ_B), int32))
        return centre + (u - HALF)[:SUBLANES].astype(F32) * ulp

    tau2d = lax.cond(any_irregular, finish_f32, finish_int16, (lo, hi))
    tau = tau2d[None]

    need = (topk - count_f32(lambda blk: blk > tau2d)).astype(F32)[None]
    ones_l = jnp.ones((2 * SUBLANES, TK_B), BF16)

    def mask_tile(kt, run):
        blk = rows3(sc_ref[kt])
        eq = blk == tau
        eqf = jnp.where(eq, 1.0, 0.0).reshape(TK_B, TQ_B).astype(BF16)
        rank = rows3(jnp.dot(tri_ref[...], eqf, preferred_element_type=F32)) + run[None]
        sel = (blk > tau) | (eq & (rank < need))
        mb_ref[kt] = jnp.where(sel, 0.0, NEG).reshape(TK_B, TQ_B)
        return run + jnp.dot(ones_l, eqf, preferred_element_type=F32)[:SUBLANES]

    def mask_group(j, run):
        for k in range(4):
            run = mask_tile(4 * j + k, run)
        return run

    run = lax.fori_loop(0, nkt // 4, mask_group, jnp.zeros((SUBLANES, TQ_B), F32))
    lax.fori_loop(nkt - nkt % 4, nkt, mask_tile, run)

    mb_ref[i] = jnp.where(adm_diag, mb_ref[i], NEG)

    m_ref[...] = jnp.full(m_ref.shape, NEG, F32)
    acc_ref[...] = jnp.zeros(acc_ref.shape, F32)
    ones_rows = jnp.ones((PACKED, TK_B), BF16)

    buf_a, buf_b = (sa_ref, smaxa_ref), (sb_ref, smaxb_ref)

    def logits_stage(kt, buf, near):
        s_ref, smax_ref = buf
        mb = mb_ref[kt]
        for h in range(B_HEADS):
            s = _nt_dot(k_ref[ktile(kt), hcols(h)], q_ref[:, hcols(h)]) + mb
            if near:
                s = s + tab_ref[jnp.minimum(i - kt, NEAR_B), h]
            s_ref[h] = s
            smax_ref[h] = all_rows(jnp.max(rows3(s), axis=0), jnp.max)

    def update_stage(kt, buf):
        s_ref, smax_ref = buf
        for h in range(B_HEADS):
            m_old = m_ref[h]
            m_new = jnp.maximum(m_old, smax_ref[h])
            alpha = jnp.exp2(m_old - m_new)
            p = jnp.exp2(rows3(s_ref[h]) - m_new[None]).reshape(TK_B, TQ_B).astype(BF16)
            vaug = jnp.concatenate([vt_ref[kt, h], ones_rows], axis=0)
            pv = jnp.dot(vaug, p, preferred_element_type=F32)
            acc_ref[h] = (rows3(acc_ref[h]) * alpha[None] + rows3(pv)).reshape(acc_ref.shape[1:])
            m_ref[h] = m_new

    def pair_body(j, carry, near):
        logits_stage(2 * j + 1, buf_b, near)
        update_stage(2 * j, buf_a)
        logits_stage(2 * j + 2, buf_a, near)
        update_stage(2 * j + 1, buf_b)
        return carry

    near_lo = jnp.maximum(i - (NEAR_B - 1), 0)
    far_pairs = jnp.maximum(near_lo - 1, 0) // 2
    logits_stage(0, buf_a, True)
    lax.fori_loop(0, far_pairs, functools.partial(pair_body, near=False), 0)
    lax.fori_loop(far_pairs, (nkt - 1) // 2, functools.partial(pair_body, near=True), 0)

    @pl.when(nkt % 2 == 0)
    def _():
        logits_stage(nkt - 1, buf_b, True)
        update_stage(nkt - 2, buf_a)
        update_stage(nkt - 1, buf_b)

    @pl.when(nkt % 2 == 1)
    def _():
        update_stage(nkt - 1, buf_a)

    for h in range(B_HEADS):
        num = rows3(acc_ref[h, 0:B_HEAD_DIM, :])
        den = acc_ref[h, B_HEAD_DIM:B_HEAD_DIM + SUBLANES, :]
        out_t = (num / den[None]).reshape(B_HEAD_DIM, TQ_B)
        o_ref[:, h * B_HEAD_DIM:(h + 1) * B_HEAD_DIM] = out_t.T.astype(BF16)


def _dsa(proj, bkt, rel_bias, batch, seq):
    nt = seq // TQ_B
    topk = min(TOPK_MAX, seq // 4)

    def q_spec(width, col):
        return pl.BlockSpec((TQ_B, width), lambda b, i: (b * nt + i, col // width))

    def seq_spec(width, col):
        return pl.BlockSpec((seq, width), lambda b, i: (b, col // width))

    return pl.pallas_call(
        functools.partial(_dsa_kernel, topk=topk, seq=seq,
                          present=tuple(tuple(int(b) for b in np.unique(t)) for t in bkt)),
        grid=(batch, nt),
        in_specs=[
            q_spec(IDX_HEADS * IDX_DIM, COL_IQ),
            q_spec(LANES, COL_IKW),
            seq_spec(LANES, COL_IKW),
            q_spec(B_WIDTH, COL_BQ), seq_spec(B_WIDTH, COL_BK), seq_spec(B_WIDTH, COL_BV),
            pl.BlockSpec((NEAR_B, TK_B, TQ_B), lambda b, i: (0, 0, 0)),
            pl.BlockSpec(memory_space=pltpu.SMEM),
        ],
        out_specs=pl.BlockSpec((TQ_B, B_WIDTH), lambda b, i: (b * nt + i, 0)),
        out_shape=jax.ShapeDtypeStruct((batch * seq, B_WIDTH), BF16),
        scratch_shapes=[
            pltpu.VMEM((nt, TK_B, TQ_B), F32),
            pltpu.VMEM((nt, TK_B, TQ_B), BF16),
            pltpu.VMEM((nt, TK_B, TQ_B), jnp.int16),
            pltpu.VMEM((nt, TK_B, TQ_B), F32),
            pltpu.VMEM((NEAR_B + 1, B_HEADS, TK_B, TQ_B), F32),
            pltpu.VMEM((TK_B, TK_B), BF16),
            pltpu.VMEM((seq // TK_B, B_HEADS, B_HEAD_DIM, TK_B), BF16),
            pltpu.VMEM((B_HEADS, B_HEAD_DIM + PACKED, TQ_B), F32),
            pltpu.VMEM((B_HEADS, SUBLANES, TQ_B), F32),
            pltpu.VMEM((B_HEADS, TK_B, TQ_B), F32),
            pltpu.VMEM((B_HEADS, SUBLANES, TQ_B), F32),
            pltpu.VMEM((B_HEADS, TK_B, TQ_B), F32),
            pltpu.VMEM((B_HEADS, SUBLANES, TQ_B), F32),
        ],
        compiler_params=pltpu.CompilerParams(
            dimension_semantics=("arbitrary", "arbitrary"), vmem_limit_bytes=VMEM_LIMIT),
        name="dsa",
    )(proj, proj, proj, proj, proj, proj, jnp.asarray(bkt), rel_bias)


def _outproj_kernel(oa_ref, ob_ref, cq_ref, mkv_ref, gate_ref, x_ref, w32_ref, g_ref, o_ref, w_ref):
    @pl.when(pl.program_id(0) == 0)
    def _():
        w_ref[...] = w32_ref[...].astype(BF16)

    ones_m = jnp.ones((N_MEM, C_HEAD_DIM), BF16)
    for r in range(TM_OUT // SUB_OUT):
        rows = slice(r * SUB_OUT, (r + 1) * SUB_OUT)
        att = [oa_ref[rows, :].astype(F32), ob_ref[rows, :].astype(F32)]
        for h in range(C_HEADS):
            hs = slice(h * C_HEAD_DIM, (h + 1) * C_HEAD_DIM)
            s = _nt_dot(cq_ref[rows, hs], mkv_ref[:, hs])
            p = jnp.exp2(s - jnp.max(s, axis=1, keepdims=True)).astype(BF16)
            vaug = jnp.concatenate([mkv_ref[:, C_WIDTH + h * C_HEAD_DIM:C_WIDTH + (h + 1) * C_HEAD_DIM], ones_m],
                                   axis=1)
            pv = jnp.dot(p, vaug, preferred_element_type=F32)
            att.append(pv[:, :C_HEAD_DIM] / pv[:, C_HEAD_DIM:])
        gate = gate_ref[rows, :].astype(F32)
        sg = gate / (1.0 + jnp.exp(-gate))
        y = (jnp.concatenate(att, axis=1) * sg).astype(BF16)
        h_new = x_ref[rows, :] + jnp.dot(y, w_ref[...], preferred_element_type=F32)
        ms = jnp.mean(h_new * h_new, axis=-1, keepdims=True)
        o_ref[rows, :] = h_new * lax.rsqrt(ms + EPS) * g_ref[...]


def _outproj(oa, ob, proj, mkv, x2d, w, g, seq):
    m = x2d.shape[0]
    steps_per_seq = seq // TM_OUT
    return pl.pallas_call(
        _outproj_kernel,
        grid=(m // TM_OUT,),
        in_specs=[
            pl.BlockSpec((TM_OUT, A_WIDTH), lambda i: (i, 0)),
            pl.BlockSpec((TM_OUT, B_WIDTH), lambda i: (i, 0)),
            pl.BlockSpec((TM_OUT, C_WIDTH), lambda i: (i, COL_CQ // C_WIDTH)),
            pl.BlockSpec((N_MEM, 2 * C_WIDTH), lambda i: (i // steps_per_seq, 0)),
            pl.BlockSpec((TM_OUT, MIX_WIDTH), lambda i: (i, COL_GATE // MIX_WIDTH)),
            pl.BlockSpec((TM_OUT, D_MODEL), lambda i: (i, 0)),
            pl.BlockSpec((MIX_WIDTH, D_MODEL), lambda i: (0, 0), pipeline_mode=pl.Buffered(1)),
            pl.BlockSpec((1, D_MODEL), lambda i: (0, 0)),
        ],
        out_specs=pl.BlockSpec((TM_OUT, D_MODEL), lambda i: (i, 0)),
        out_shape=jax.ShapeDtypeStruct((m, D_MODEL), F32),
        scratch_shapes=[pltpu.VMEM((MIX_WIDTH, D_MODEL), BF16)],
        compiler_params=pltpu.CompilerParams(
            dimension_semantics=("arbitrary",), vmem_limit_bytes=VMEM_LIMIT),
        name="outproj",
    )(oa, ob, proj, mkv, proj, x2d, w, g)


def _col_scale():
    cs = np.ones((1, PROJ_W), np.float32)
    cs[0, COL_AQ:COL_AQ + A_WIDTH] = A_HEAD_DIM ** -0.5 * LOG2E
    cs[0, COL_BQ:COL_BQ + B_WIDTH] = B_HEAD_DIM ** -0.5 * LOG2E
    cs[0, COL_CQ:COL_CQ + C_WIDTH] = C_HEAD_DIM ** -0.5 * LOG2E
    return jnp.asarray(cs)


def kernel(x, mem, g_norm, w_in, sinks, rel_bias, g_mem, w_mem_kv, w_out, g_final):
    batch, seq, _ = x.shape
    assert w_in.shape[0] == 1, "the out-projection kernel fuses the final norm of a single-layer trunk"
    kk = np.arange(2 * TQ_A)[:, None]
    qq = np.arange(TQ_A)[None, :]
    bkt_a = _t5_bucket_np(kk - TQ_A - qq)
    kk = np.arange(TK_B)[None, :, None]
    qq = np.arange(TQ_B)[None, None, :]
    d = np.arange(NEAR_B)[:, None, None]
    bkt_b = _t5_bucket_np(kk - qq - TK_B * d)

    h = x.reshape(batch * seq, D_MODEL)
    mem2d = mem.reshape(batch * N_MEM, D_MODEL)
    w_t = jnp.transpose(w_in, (2, 0, 1)).reshape(IN_WIDTH * KB_IN, LANES)
    proj = _inproj(h, g_norm[0].reshape(1, D_MODEL), _wprep(w_t), _col_scale())
    mkv = _memkv(mem2d, g_mem[0].reshape(1, D_MODEL), w_mem_kv[0])
    oa = _swa(proj, bkt_a, rel_bias, sinks[0], batch, seq)
    ob = _dsa(proj, bkt_b, rel_bias, batch, seq)
    out = _outproj(oa, ob, proj, mkv, h, w_out[0], g_final.reshape(1, D_MODEL), seq)
    return out.reshape(batch, seq, D_MODEL)
```

```python
import functools
import math

import numpy as np
import jax
import jax.numpy as jnp
from jax import lax
from jax.experimental import pallas as pl
from jax.experimental.pallas import tpu as pltpu

D_MODEL = 2048
CHUNK = 64
N_MEM = 256
EPS = 1e-6
A_HEADS = 16
A_KV_HEADS = 2
A_HEAD_DIM = 64
WINDOW_CHUNKS = 2
A_WIDTH = A_HEADS * A_HEAD_DIM
B_HEADS = 4
B_HEAD_DIM = 128
B_WIDTH = B_HEADS * B_HEAD_DIM
IDX_HEADS = 4
IDX_DIM = 64
TOPK_MAX = 256
C_HEADS = 4
C_HEAD_DIM = 128
C_WIDTH = C_HEADS * C_HEAD_DIM
MIX_WIDTH = A_WIDTH + B_WIDTH + C_WIDTH
N_BUCKETS = 32
MAX_DISTANCE = 1024
KV_A = A_KV_HEADS * A_HEAD_DIM
SPLIT_SIZES = (A_WIDTH, KV_A, KV_A, B_WIDTH, B_WIDTH, B_WIDTH,
               IDX_HEADS * IDX_DIM, IDX_DIM, IDX_HEADS, C_WIDTH, MIX_WIDTH)
IN_WIDTH = sum(SPLIT_SIZES)

F32 = jnp.float32
BF16 = jnp.bfloat16
LOG2E = math.log2(math.e)
NEG = -1e30
INT_MIN = -(2 ** 31)
LANES = 128
SUBLANES = 8
PACKED = 16
HALF = 1 << 15
KEY_POS_INF = 0x7F800000
KEY_NEG_INF = -KEY_POS_INF
FINE_PASSES = 17
PRE_PASSES = 2
F32_MANT_BITS = 23
F32_EXP_BIAS = 127

(SRC_AQ, SRC_AK, SRC_AV, SRC_BQ, SRC_BK, SRC_BV,
 SRC_IQ, SRC_IK, SRC_IW, SRC_CQ, SRC_GATE) = (int(c) for c in np.cumsum((0,) + SPLIT_SIZES)[:-1])
COL_GATE = 0
COL_AQ = COL_GATE + MIX_WIDTH
COL_BQ = COL_AQ + A_WIDTH
COL_BK = COL_BQ + B_WIDTH
COL_BV = COL_BK + B_WIDTH
COL_CQ = COL_BV + B_WIDTH
COL_AKV = COL_CQ + C_WIDTH
COL_IQ = COL_AKV + 2 * KV_A
COL_IKW = COL_IQ + IDX_HEADS * IDX_DIM
PROJ_W = 6144
SEGMENTS = ((COL_GATE, SRC_GATE, MIX_WIDTH), (COL_AQ, SRC_AQ, A_WIDTH), (COL_BQ, SRC_BQ, B_WIDTH),
            (COL_BK, SRC_BK, B_WIDTH), (COL_BV, SRC_BV, B_WIDTH), (COL_CQ, SRC_CQ, C_WIDTH),
            (COL_AKV, SRC_AK, 2 * KV_A), (COL_IQ, SRC_IQ, IDX_HEADS * IDX_DIM), (COL_IKW, SRC_IK, 256))
KB_IN = D_MODEL // LANES

TC_PREP = 256
TM_IN, TN_IN = 1024, 2048
SUB_IN = 256
TQ_A = 128
RB_A = 512
TQ_B = 256
TK_B = 256
NEAR_B = 4
TM_OUT = 512
SUB_OUT = 256
V7X_VMEM_BYTES = 64 * 1024 * 1024
VMEM_TEMP_PCT = 15


def _vmem_limit(in_specs, operands, out_spec, out_shape, scratch_shapes=()):
    def buffers(spec, dtype):
        if spec.block_shape is None:
            return 0
        elems = math.prod(int(getattr(d, "block_size", d)) for d in spec.block_shape)
        count = spec.pipeline_mode.buffer_count if spec.pipeline_mode is not None else 2
        return elems * jnp.dtype(dtype).itemsize * count

    total = sum(buffers(s, a.dtype) for s, a in zip(in_specs, operands)) + buffers(out_spec, out_shape.dtype)
    total += sum(math.prod(s.shape) * jnp.dtype(s.dtype).itemsize for s in scratch_shapes)
    return min(total * (100 + VMEM_TEMP_PCT) // 100, V7X_VMEM_BYTES * 7 // 8)


def _call(kernel, *, name, grid, semantics, in_specs, operands, out_spec, out_shape, scratch_shapes=()):
    return pl.pallas_call(
        kernel,
        grid=grid,
        in_specs=in_specs,
        out_specs=out_spec,
        out_shape=out_shape,
        scratch_shapes=list(scratch_shapes),
        compiler_params=pltpu.CompilerParams(
            dimension_semantics=semantics,
            vmem_limit_bytes=_vmem_limit(in_specs, operands, out_spec, out_shape, scratch_shapes)),
        name=name,
    )(*operands)


def _t5_bucket_np(rel):
    nb = N_BUCKETS // 2
    max_exact = nb // 2
    side = np.where(rel > 0, nb, 0)
    n = np.abs(rel)
    nf = np.maximum(n, max_exact).astype(np.float32)
    large = max_exact + (np.log(nf / max_exact) / math.log(MAX_DISTANCE / max_exact)
                         * (nb - max_exact)).astype(np.int32)
    large = np.minimum(large, nb - 1)
    return (side + np.where(n < max_exact, n, large)).astype(np.int32)


def _nt_dot(a, b):
    return lax.dot_general(a, b, (((1,), (1,)), ((), ())), preferred_element_type=F32)


def _bias_tables(bucket, present, rb_ref, cols, sub_row=None):
    accs = [jnp.zeros(bucket.shape, F32) for _ in cols]
    for b in present:
        hit = bucket == b
        for n, col in enumerate(cols):
            val = rb_ref[b, col]
            if sub_row is not None:
                val = val - rb_ref[sub_row, col]
            accs[n] = jnp.where(hit, val * LOG2E, accs[n])
    return accs


def _wprep_kernel(src_ref, w_ref, o_ref):
    t = pl.program_id(0)

    @pl.when(src_ref[t] >= 0)
    def _():
        for kb in range(KB_IN):
            o_ref[:, kb * LANES:(kb + 1) * LANES] = w_ref[pl.ds(kb, TC_PREP, stride=KB_IN), :].astype(BF16)

    @pl.when(src_ref[t] < 0)
    def _():
        o_ref[...] = jnp.zeros(o_ref.shape, BF16)


def _wprep(w_t):
    src = np.full((PROJ_W // TC_PREP,), -1, np.int32)
    for dst_col, src_col, width in SEGMENTS:
        for n in range(width // TC_PREP):
            src[dst_col // TC_PREP + n] = src_col + n * TC_PREP
    in_specs = [pl.BlockSpec((pl.Element(TC_PREP * KB_IN), pl.Element(LANES)),
                             lambda t, src: (jnp.maximum(src[t], 0) * KB_IN, 0))]
    out_spec = pl.BlockSpec((TC_PREP, D_MODEL), lambda t, src: (t, 0))
    out_shape = jax.ShapeDtypeStruct((PROJ_W, D_MODEL), BF16)
    return pl.pallas_call(
        _wprep_kernel,
        grid_spec=pltpu.PrefetchScalarGridSpec(
            num_scalar_prefetch=1, grid=(PROJ_W // TC_PREP,), in_specs=in_specs, out_specs=out_spec),
        out_shape=out_shape,
        compiler_params=pltpu.CompilerParams(
            dimension_semantics=("arbitrary",),
            vmem_limit_bytes=_vmem_limit(in_specs, [w_t], out_spec, out_shape)),
        name="wprep",
    )(jnp.asarray(src), w_t)


def _inproj_kernel(x_ref, g_ref, wt_ref, cs_ref, o_ref, hn_ref):
    j = pl.program_id(1)

    @pl.when(j == 0)
    def _():
        for r in range(TM_IN // SUB_IN):
            rows = slice(r * SUB_IN, (r + 1) * SUB_IN)
            x = x_ref[rows, :]
            ms = jnp.mean(x * x, axis=-1, keepdims=True)
            hn = (x * lax.rsqrt(ms + EPS) * g_ref[...]).astype(BF16)
            hn_ref[rows, :] = hn
            o_ref[rows, :] = (_nt_dot(hn, wt_ref[...]) * cs_ref[...]).astype(BF16)

    @pl.when(j > 0)
    def _():
        o_ref[...] = (_nt_dot(hn_ref[...], wt_ref[...]) * cs_ref[...]).astype(BF16)


def _inproj(x2d, g, wt, cs):
    m = x2d.shape[0]
    return _call(
        _inproj_kernel,
        grid=(m // TM_IN, PROJ_W // TN_IN),
        in_specs=[
            pl.BlockSpec((TM_IN, D_MODEL), lambda i, j: (i, 0)),
            pl.BlockSpec((1, D_MODEL), lambda i, j: (0, 0)),
            pl.BlockSpec((TN_IN, D_MODEL), lambda i, j: (j, 0)),
            pl.BlockSpec((1, TN_IN), lambda i, j: (0, j)),
        ],
        out_spec=pl.BlockSpec((TM_IN, TN_IN), lambda i, j: (i, j)),
        out_shape=jax.ShapeDtypeStruct((m, PROJ_W), BF16),
        scratch_shapes=[pltpu.VMEM((TM_IN, D_MODEL), BF16)],
        semantics=("parallel", "arbitrary"),
        name="inproj",
        operands=[x2d, g, wt, cs],
    )


def _memkv_kernel(m_ref, g_ref, w_ref, o_ref):
    x = m_ref[...]
    ms = jnp.mean(x * x, axis=-1, keepdims=True)
    hn = (x * lax.rsqrt(ms + EPS) * g_ref[...]).astype(BF16)
    o_ref[...] = jnp.dot(hn, w_ref[...].astype(BF16), preferred_element_type=F32).astype(BF16)


def _memkv(mem2d, g, w):
    m = mem2d.shape[0]
    return _call(
        _memkv_kernel,
        grid=(m // N_MEM,),
        in_specs=[
            pl.BlockSpec((N_MEM, D_MODEL), lambda i: (i, 0)),
            pl.BlockSpec((1, D_MODEL), lambda i: (0, 0)),
            pl.BlockSpec((D_MODEL, 2 * C_WIDTH), lambda i: (0, 0)),
        ],
        out_spec=pl.BlockSpec((N_MEM, 2 * C_WIDTH), lambda i: (i, 0)),
        out_shape=jax.ShapeDtypeStruct((m, 2 * C_WIDTH), BF16),
        semantics=("arbitrary",),
        name="memkv",
        operands=[mem2d, g, w],
    )


GRP_A = A_HEADS // A_KV_HEADS
PAIRS_A = GRP_A // 2
COLS_A = GRP_A * TQ_A
KEYS_A = 2 * TQ_A
CHUNK_PAIRS_A = 4


def _swa_kernel(q_ref, kvp_ref, kvc_ref, bkt_ref, rb_ref, sink_ref, o_ref,
                tab_ref, sinkv_ref, kp_ref, vt_ref, s_ref, m_ref, ot_ref, *, present):
    i = pl.program_id(1)

    @pl.when((pl.program_id(0) == 0) & (i == 0))
    def _():
        bkt = bkt_ref[...]
        kchunk = lax.broadcasted_iota(jnp.int32, bkt.shape, 0) // CHUNK
        qchunk = lax.broadcasted_iota(jnp.int32, bkt.shape, 1) // CHUNK
        allowed = (kchunk >= qchunk) & (kchunk <= qchunk + WINDOW_CHUNKS)
        has_prev = lax.broadcasted_iota(jnp.int32, bkt.shape, 0) >= TQ_A
        for g in range(A_KV_HEADS):
            for c0 in range(0, GRP_A, PAIRS_A):
                heads = [g * GRP_A + 2 * (col % PAIRS_A) + col // PAIRS_A for col in range(c0, c0 + PAIRS_A)]
                for n, t in enumerate(_bias_tables(bkt, present, rb_ref, heads)):
                    cols = slice((c0 + n) * TQ_A, (c0 + n + 1) * TQ_A)
                    t = jnp.where(allowed, t, NEG)
                    tab_ref[0, g, :, cols] = t
                    tab_ref[1, g, :, cols] = jnp.where(has_prev, t, NEG)
                    sinkv_ref[g, :, cols] = jnp.full((SUBLANES, TQ_A), sink_ref[heads[n]] * LOG2E, F32)

    width = CHUNK_PAIRS_A * TQ_A
    tasks = [(t, g, p0, parity) for t in range(RB_A // TQ_A) for g in range(A_KV_HEADS)
             for p0 in range(0, PAIRS_A, CHUNK_PAIRS_A) for parity in range(2)]

    def prepare(t, g):
        slot = (t * A_KV_HEADS + g) % 2
        if t == 0:
            kv_t = jnp.concatenate([kvp_ref[...], kvc_ref[0:TQ_A, :]], axis=0)
        else:
            kv_t = kvc_ref[(t - 1) * TQ_A:(t + 1) * TQ_A, :]
        kg = kv_t[:, g * A_HEAD_DIM:(g + 1) * A_HEAD_DIM]
        zeros = jnp.zeros((KEYS_A, A_HEAD_DIM), BF16)
        kp_ref[slot, 0] = jnp.concatenate([kg, zeros], axis=1)
        kp_ref[slot, 1] = jnp.concatenate([zeros, kg], axis=1)
        r = lax.broadcasted_iota(jnp.int32, (A_HEAD_DIM, KV_A), 0)
        c = lax.broadcasted_iota(jnp.int32, (A_HEAD_DIM, KV_A), 1)
        pick = jnp.where(c == g * A_HEAD_DIM + r, 1.0, 0.0).astype(BF16)
        vt = _nt_dot(pick, kv_t[:, KV_A:]).astype(BF16)
        vt_ref[slot] = jnp.concatenate([vt, jnp.ones((PACKED, KEYS_A), BF16)], axis=0)

    def chunk_cols(p0, parity):
        lo = (parity * PAIRS_A + p0) * TQ_A
        return slice(lo, lo + width)

    def logits_stage(n):
        t, g, p0, parity = tasks[n]
        if (p0, parity) == (0, 0):
            prepare(t, g)
        slot, buf, cols = (t * A_KV_HEADS + g) % 2, n % 2, chunk_cols(p0, parity)
        q_pairs = jnp.concatenate(
            [q_ref[t * TQ_A:(t + 1) * TQ_A, (g * PAIRS_A + p) * LANES:(g * PAIRS_A + p + 1) * LANES]
             for p in range(p0, p0 + CHUNK_PAIRS_A)], axis=0)
        variant = jnp.where(i == 0, 1, 0) if t == 0 else 0
        s = _nt_dot(kp_ref[slot, parity], q_pairs) + tab_ref[variant, g, :, cols]
        s_ref[buf] = s
        m = jnp.max(s.reshape(KEYS_A // SUBLANES, SUBLANES, width), axis=0)
        m_ref[buf] = jnp.maximum(jnp.broadcast_to(jnp.max(m, axis=0, keepdims=True), m.shape),
                                 sinkv_ref[g, :, cols])

    def update_stage(n):
        t, g, p0, parity = tasks[n]
        slot, buf, cols = (t * A_KV_HEADS + g) % 2, n % 2, chunk_cols(p0, parity)
        m = m_ref[buf]
        s = s_ref[buf].reshape(KEYS_A // SUBLANES, SUBLANES, width)
        p = jnp.exp2(s - m[None]).reshape(KEYS_A, width).astype(BF16)
        pv = jnp.dot(vt_ref[slot], p, preferred_element_type=F32)
        den = pv[A_HEAD_DIM:A_HEAD_DIM + SUBLANES] + jnp.exp2(sinkv_ref[g, :, cols] - m)
        out_t = pv[:A_HEAD_DIM].reshape(A_HEAD_DIM // SUBLANES, SUBLANES, width) / den[None]
        ot_ref[parity] = out_t.reshape(A_HEAD_DIM, width)
        if parity == 1:
            for k in range(CHUNK_PAIRS_A):
                blk = jnp.concatenate([ot_ref[0, :, k * TQ_A:(k + 1) * TQ_A],
                                       ot_ref[1, :, k * TQ_A:(k + 1) * TQ_A]], axis=0)
                lanes = slice((g * PAIRS_A + p0 + k) * LANES, (g * PAIRS_A + p0 + k + 1) * LANES)
                o_ref[t * TQ_A:(t + 1) * TQ_A, lanes] = blk.T.astype(BF16)

    logits_stage(0)
    for n in range(len(tasks)):
        if n + 1 < len(tasks):
            logits_stage(n + 1)
        update_stage(n)


def _swa(proj, bkt, rel_bias, sinks, batch, seq):
    nt = seq // RB_A
    sub = RB_A // TQ_A
    kv_blk = COL_AKV // (2 * KV_A)
    return _call(
        functools.partial(_swa_kernel, present=tuple(int(b) for b in np.unique(bkt))),
        grid=(batch, nt),
        in_specs=[
            pl.BlockSpec((RB_A, A_WIDTH), lambda b, i: (b * nt + i, COL_AQ // A_WIDTH)),
            pl.BlockSpec((TQ_A, 2 * KV_A), lambda b, i: (jnp.maximum((b * nt + i) * sub - 1, 0), kv_blk)),
            pl.BlockSpec((RB_A, 2 * KV_A), lambda b, i: (b * nt + i, kv_blk)),
            pl.BlockSpec((KEYS_A, TQ_A), lambda b, i: (0, 0)),
            pl.BlockSpec(memory_space=pltpu.SMEM),
            pl.BlockSpec(memory_space=pltpu.SMEM),
        ],
        out_spec=pl.BlockSpec((RB_A, A_WIDTH), lambda b, i: (b * nt + i, 0)),
        out_shape=jax.ShapeDtypeStruct((batch * seq, A_WIDTH), BF16),
        scratch_shapes=[
            pltpu.VMEM((2, A_KV_HEADS, KEYS_A, COLS_A), F32),
            pltpu.VMEM((A_KV_HEADS, SUBLANES, COLS_A), F32),
            pltpu.VMEM((2, 2, KEYS_A, LANES), BF16),
            pltpu.VMEM((2, A_HEAD_DIM + PACKED, KEYS_A), BF16),
            pltpu.VMEM((2, KEYS_A, CHUNK_PAIRS_A * TQ_A), F32),
            pltpu.VMEM((2, SUBLANES, CHUNK_PAIRS_A * TQ_A), F32),
            pltpu.VMEM((2, A_HEAD_DIM, CHUNK_PAIRS_A * TQ_A), F32),
        ],
        semantics=("arbitrary", "arbitrary"),
        name="swa",
        operands=[proj, proj, proj, jnp.asarray(bkt), rel_bias, sinks],
    )


def _dsa_kernel(iq_ref, iwq_ref, ik_ref, q_ref, k_ref, v_ref, bkt_ref, rb_ref, o_ref,
                sc_ref, scb_ref, d16_ref, mb_ref, tab_ref, tri_ref, vt_ref, acc_ref, m_ref,
                sa_ref, smaxa_ref, sb_ref, smaxb_ref, *, topk, seq, present):
    i = pl.program_id(1)
    nkt = i + 1
    int32, int16 = jnp.int32, jnp.int16
    grp = TK_B // SUBLANES

    def hcols(h):
        return slice(h * B_HEAD_DIM, (h + 1) * B_HEAD_DIM)

    def ktile(kt):
        return pl.ds(pl.multiple_of(kt * TK_B, TK_B), TK_B)

    def rows3(a):
        return a.reshape(a.shape[0] // SUBLANES, SUBLANES, TQ_B)

    def all_rows(a, op):
        return jnp.broadcast_to(op(a, axis=0, keepdims=True), a.shape)

    @pl.when((pl.program_id(0) == 0) & (i == 0))
    def _():
        far_bucket = N_BUCKETS // 2 - 1
        for d in range(NEAR_B):
            for h0 in range(0, B_HEADS, 2):
                tabs = _bias_tables(bkt_ref[d], present[d], rb_ref, [A_HEADS + h0, A_HEADS + h0 + 1],
                                    sub_row=far_bucket)
                tab_ref[d, h0] = tabs[0]
                tab_ref[d, h0 + 1] = tabs[1]
        tab_ref[NEAR_B] = jnp.zeros(tab_ref.shape[1:], F32)
        r = lax.broadcasted_iota(int32, (TK_B, TK_B), 0)
        c = lax.broadcasted_iota(int32, (TK_B, TK_B), 1)
        tri_ref[...] = jnp.where(c < r, 1.0, 0.0).astype(BF16)

    @pl.when(i == 0)
    def _():
        r = lax.broadcasted_iota(int32, (B_HEAD_DIM, B_HEAD_DIM), 0)
        c = lax.broadcasted_iota(int32, (B_HEAD_DIM, B_HEAD_DIM), 1)
        eye = jnp.where(r == c, 1.0, 0.0).astype(BF16)

        def body(kt, carry):
            for h in range(B_HEADS):
                vt_ref[kt, h] = _nt_dot(eye, v_ref[ktile(kt), hcols(h)]).astype(BF16)
            return carry

        lax.fori_loop(0, seq // TK_B, body, 0)

    key_chunk = lax.broadcasted_iota(int32, (TK_B, TQ_B), 0) // CHUNK
    qry_chunk = lax.broadcasted_iota(int32, (TK_B, TQ_B), 1) // CHUNK
    adm_diag = key_chunk <= qry_chunk

    r = lax.broadcasted_iota(int32, (IDX_HEADS * SUBLANES, LANES), 0)
    c = lax.broadcasted_iota(int32, (IDX_HEADS * SUBLANES, LANES), 1)
    pick_w = jnp.where(c == IDX_DIM + r // SUBLANES, 1.0, 0.0).astype(BF16)
    w_all = _nt_dot(pick_w, iwq_ref[...]) * (IDX_HEADS ** -0.5 * IDX_DIM ** -0.5)

    def for_tiles(n, body, unroll=2):
        def group(j, carry):
            for k in range(unroll):
                body(unroll * j + k)
            return carry

        def single(kt, carry):
            body(kt)
            return carry

        lax.fori_loop(0, n // unroll, group, 0)
        lax.fori_loop(n - n % unroll, n, single, 0)

    def score_tile(kt):
        ikt = ik_ref[ktile(kt), 0:IDX_DIM]
        sc = jnp.zeros((grp, SUBLANES, TQ_B), F32)
        for h in range(IDX_HEADS):
            x = _nt_dot(ikt, iq_ref[:, h * IDX_DIM:(h + 1) * IDX_DIM])
            sc = sc + w_all[h * SUBLANES:(h + 1) * SUBLANES][None] * jnp.maximum(rows3(x), 0.0)
        store_score(kt, sc.reshape(TK_B, TQ_B))

    def store_score(kt, sc):
        sc_ref[kt] = sc
        scb_ref[kt] = sc.astype(BF16)

    for_tiles(nkt, score_tile, unroll=4)
    store_score(i, jnp.where(adm_diag, sc_ref[i], -jnp.inf))

    def key_to_f32(key):
        return pltpu.bitcast(jnp.where(key >= 0, key, INT_MIN - key), F32)

    def over_tiles(tile_count, zero):
        def pair(j, acc):
            return acc + (tile_count(2 * j) + tile_count(2 * j + 1))
        acc = lax.fori_loop(0, nkt // 2, pair, zero)
        return lax.cond(nkt % 2 == 1, lambda a: a + tile_count(nkt - 1), lambda a: a, acc)

    def tree_sum(parts):
        while len(parts) > 1:
            parts = [parts[n] + parts[n + 1] for n in range(0, len(parts), 2)]
        return parts[0]

    def count_bf16(cand):
        def tile_count(kt):
            blk = scb_ref[kt].reshape(TK_B // PACKED, PACKED, TQ_B)
            return tree_sum([jnp.where(blk[g] >= cand, jnp.ones((), int16), jnp.zeros((), int16))
                             for g in range(TK_B // PACKED)])
        acc = over_tiles(tile_count, jnp.zeros((PACKED, TQ_B), int16))
        return all_rows(acc.astype(int32), jnp.sum)

    def count_f32(pred):
        def tile_count(kt):
            blk = rows3(sc_ref[kt])
            return tree_sum([jnp.where(pred(blk[g]), 1, 0) for g in range(grp)])
        return all_rows(over_tiles(tile_count, jnp.zeros((SUBLANES, TQ_B), int32)), jnp.sum)

    def coarse_pass(p, u):
        bit = jnp.left_shift(jnp.int32(1), 15 - p)
        cand = key_to_f32(((u | bit) - HALF) << 16).astype(BF16)
        return jnp.where(count_bf16(cand) >= topk, u | bit, u)

    u = lax.fori_loop(0, 16, coarse_pass, jnp.zeros((PACKED, TQ_B), int32))
    coarse_key = ((u - HALF) << 16)[:SUBLANES]

    lo = jnp.maximum(coarse_key - HALF, KEY_NEG_INF)
    hi = jnp.minimum(coarse_key, KEY_POS_INF - 2 * HALF) + 2 * HALF

    def fine_pass(p, lohi):
        lo, hi = lohi
        mid = lo + ((hi - lo) >> 1)
        cand = key_to_f32(mid)
        ok = count_f32(lambda blk: blk >= cand) >= topk
        return jnp.where(ok, mid, lo), jnp.where(ok, hi, mid)

    lo, hi = lax.fori_loop(0, PRE_PASSES, fine_pass, (lo, hi))

    mag = jnp.minimum(jnp.abs(lo), jnp.abs(hi))
    around_zero = ((lo <= 0) & (hi >= 0)) | (mag >> F32_MANT_BITS == 0)
    exp_field = jnp.where(around_zero, F32_MANT_BITS + 1, mag >> F32_MANT_BITS)
    any_irregular = jnp.max(jnp.where(exp_field <= F32_MANT_BITS, 1, 0)) > 0

    def finish_f32(lohi):
        lo, hi = lax.fori_loop(PRE_PASSES, FINE_PASSES, fine_pass, lohi)
        return key_to_f32(lo)

    def finish_int16(lohi):
        lo, hi = lohi
        centre = jnp.where(around_zero, 0.0, key_to_f32(lo + ((hi - lo) >> 1)))
        ulp = pltpu.bitcast((exp_field - F32_MANT_BITS) << F32_MANT_BITS, F32)
        inv_ulp = pltpu.bitcast((2 * F32_EXP_BIAS + F32_MANT_BITS - exp_field) << F32_MANT_BITS, F32)

        def build(kt):
            d = (rows3(sc_ref[kt]) - centre[None]) * inv_ulp[None]
            d = jnp.clip(d, -HALF, HALF - 1).reshape(TK_B, TQ_B)
            d16_ref[kt] = d.astype(int32).astype(int16)

        for_tiles(nkt, build)

        def count_d16(cand):
            def tile_count(kt):
                blk = d16_ref[kt].reshape(TK_B // PACKED, PACKED, TQ_B)
                return tree_sum([jnp.where(blk[g] >= cand, jnp.ones((), int16), jnp.zeros((), int16))
                                 for g in range(TK_B // PACKED)])
            acc = over_tiles(tile_count, jnp.zeros((PACKED, TQ_B), int16))
            return all_rows(acc.astype(int32), jnp.sum)

        def d_pass(p, u):
            bit = jnp.left_shift(jnp.int32(1), 15 - p)
            cand = ((u | bit) - HALF).astype(int16)
            return jnp.where(count_d16(cand) >= topk, u | bit, u)

        u = lax.fori_loop(0, 16, d_pass, jnp.zeros((PACKED, TQ_B), int32))
        return centre + (u - HALF)[:SUBLANES].astype(F32) * ulp

    tau2d = lax.cond(any_irregular, finish_f32, finish_int16, (lo, hi))
    tau = tau2d[None]

    need = (topk - count_f32(lambda blk: blk > tau2d)).astype(F32)[None]
    ones_l = jnp.ones((2 * SUBLANES, TK_B), BF16)

    def mask_tile(kt, run):
        blk = rows3(sc_ref[kt])
        eq = blk == tau
        eqf = jnp.where(eq, 1.0, 0.0).reshape(TK_B, TQ_B).astype(BF16)
        rank = rows3(jnp.dot(tri_ref[...], eqf, preferred_element_type=F32)) + run[None]
        sel = (blk > tau) | (eq & (rank < need))
        mb_ref[kt] = jnp.where(sel, 0.0, NEG).reshape(TK_B, TQ_B)
        return run + jnp.dot(ones_l, eqf, preferred_element_type=F32)[:SUBLANES]

    def mask_group(j, run):
        for k in range(4):
            run = mask_tile(4 * j + k, run)
        return run

    run = lax.fori_loop(0, nkt // 4, mask_group, jnp.zeros((SUBLANES, TQ_B), F32))
    lax.fori_loop(nkt - nkt % 4, nkt, mask_tile, run)

    mb_ref[i] = jnp.where(adm_diag, mb_ref[i], NEG)

    m_ref[...] = jnp.full(m_ref.shape, NEG, F32)
    acc_ref[...] = jnp.zeros(acc_ref.shape, F32)
    ones_rows = jnp.ones((PACKED, TK_B), BF16)

    buf_a, buf_b = (sa_ref, smaxa_ref), (sb_ref, smaxb_ref)

    def logits_stage(kt, buf, near):
        s_ref, smax_ref = buf
        mb = mb_ref[kt]
        for h in range(B_HEADS):
            s = _nt_dot(k_ref[ktile(kt), hcols(h)], q_ref[:, hcols(h)]) + mb
            if near:
                s = s + tab_ref[jnp.minimum(i - kt, NEAR_B), h]
            s_ref[h] = s
            smax_ref[h] = all_rows(jnp.max(rows3(s), axis=0), jnp.max)

    def update_stage(kt, buf):
        s_ref, smax_ref = buf
        for h in range(B_HEADS):
            m_old = m_ref[h]
            m_new = jnp.maximum(m_old, smax_ref[h])
            alpha = jnp.exp2(m_old - m_new)
            p = jnp.exp2(rows3(s_ref[h]) - m_new[None]).reshape(TK_B, TQ_B).astype(BF16)
            vaug = jnp.concatenate([vt_ref[kt, h], ones_rows], axis=0)
            pv = jnp.dot(vaug, p, preferred_element_type=F32)
            acc_ref[h] = (rows3(acc_ref[h]) * alpha[None] + rows3(pv)).reshape(acc_ref.shape[1:])
            m_ref[h] = m_new

    def pair_body(j, carry, near):
        logits_stage(2 * j + 1, buf_b, near)
        update_stage(2 * j, buf_a)
        logits_stage(2 * j + 2, buf_a, near)
        update_stage(2 * j + 1, buf_b)
        return carry

    near_lo = jnp.maximum(i - (NEAR_B - 1), 0)
    far_pairs = jnp.maximum(near_lo - 1, 0) // 2
    logits_stage(0, buf_a, True)
    lax.fori_loop(0, far_pairs, functools.partial(pair_body, near=False), 0)
    lax.fori_loop(far_pairs, (nkt - 1) // 2, functools.partial(pair_body, near=True), 0)

    @pl.when(nkt % 2 == 0)
    def _():
        logits_stage(nkt - 1, buf_b, True)
        update_stage(nkt - 2, buf_a)
        update_stage(nkt - 1, buf_b)

    @pl.when(nkt % 2 == 1)
    def _():
        update_stage(nkt - 1, buf_a)

    for h in range(B_HEADS):
        num = rows3(acc_ref[h, 0:B_HEAD_DIM, :])
        den = acc_ref[h, B_HEAD_DIM:B_HEAD_DIM + SUBLANES, :]
        out_t = (num / den[None]).reshape(B_HEAD_DIM, TQ_B)
        o_ref[:, h * B_HEAD_DIM:(h + 1) * B_HEAD_DIM] = out_t.T.astype(BF16)


def _dsa(proj, bkt, rel_bias, batch, seq):
    nt = seq // TQ_B
    topk = min(TOPK_MAX, seq // 4)

    def q_spec(width, col):
        return pl.BlockSpec((TQ_B, width), lambda b, i: (b * nt + i, col // width))

    def seq_spec(width, col):
        return pl.BlockSpec((seq, width), lambda b, i: (b, col // width))

    return _call(
        functools.partial(_dsa_kernel, topk=topk, seq=seq,
                          present=tuple(tuple(int(b) for b in np.unique(t)) for t in bkt)),
        grid=(batch, nt),
        in_specs=[
            q_spec(IDX_HEADS * IDX_DIM, COL_IQ),
            q_spec(LANES, COL_IKW),
            seq_spec(LANES, COL_IKW),
            q_spec(B_WIDTH, COL_BQ), seq_spec(B_WIDTH, COL_BK), seq_spec(B_WIDTH, COL_BV),
            pl.BlockSpec((NEAR_B, TK_B, TQ_B), lambda b, i: (0, 0, 0)),
            pl.BlockSpec(memory_space=pltpu.SMEM),
        ],
        out_spec=pl.BlockSpec((TQ_B, B_WIDTH), lambda b, i: (b * nt + i, 0)),
        out_shape=jax.ShapeDtypeStruct((batch * seq, B_WIDTH), BF16),
        scratch_shapes=[
            pltpu.VMEM((nt, TK_B, TQ_B), F32),
            pltpu.VMEM((nt, TK_B, TQ_B), BF16),
            pltpu.VMEM((nt, TK_B, TQ_B), jnp.int16),
            pltpu.VMEM((nt, TK_B, TQ_B), F32),
            pltpu.VMEM((NEAR_B + 1, B_HEADS, TK_B, TQ_B), F32),
            pltpu.VMEM((TK_B, TK_B), BF16),
            pltpu.VMEM((seq // TK_B, B_HEADS, B_HEAD_DIM, TK_B), BF16),
            pltpu.VMEM((B_HEADS, B_HEAD_DIM + PACKED, TQ_B), F32),
            pltpu.VMEM((B_HEADS, SUBLANES, TQ_B), F32),
            pltpu.VMEM((B_HEADS, TK_B, TQ_B), F32),
            pltpu.VMEM((B_HEADS, SUBLANES, TQ_B), F32),
            pltpu.VMEM((B_HEADS, TK_B, TQ_B), F32),
            pltpu.VMEM((B_HEADS, SUBLANES, TQ_B), F32),
        ],
        semantics=("arbitrary", "arbitrary"),
        name="dsa",
        operands=[proj, proj, proj, proj, proj, proj, jnp.asarray(bkt), rel_bias],
    )


def _outproj_kernel(oa_ref, ob_ref, cq_ref, mkv_ref, gate_ref, x_ref, w32_ref, g_ref, o_ref, w_ref):
    @pl.when(pl.program_id(0) == 0)
    def _():
        w_ref[...] = w32_ref[...].astype(BF16)

    ones_m = jnp.ones((N_MEM, C_HEAD_DIM), BF16)
    for r in range(TM_OUT // SUB_OUT):
        rows = slice(r * SUB_OUT, (r + 1) * SUB_OUT)
        att = [oa_ref[rows, :].astype(F32), ob_ref[rows, :].astype(F32)]
        for h in range(C_HEADS):
            hs = slice(h * C_HEAD_DIM, (h + 1) * C_HEAD_DIM)
            s = _nt_dot(cq_ref[rows, hs], mkv_ref[:, hs])
            p = jnp.exp2(s - jnp.max(s, axis=1, keepdims=True)).astype(BF16)
            vaug = jnp.concatenate([mkv_ref[:, C_WIDTH + h * C_HEAD_DIM:C_WIDTH + (h + 1) * C_HEAD_DIM], ones_m],
                                   axis=1)
            pv = jnp.dot(p, vaug, preferred_element_type=F32)
            att.append(pv[:, :C_HEAD_DIM] / pv[:, C_HEAD_DIM:])
        gate = gate_ref[rows, :].astype(F32)
        sg = gate / (1.0 + jnp.exp(-gate))
        y = (jnp.concatenate(att, axis=1) * sg).astype(BF16)
        h_new = x_ref[rows, :] + jnp.dot(y, w_ref[...], preferred_element_type=F32)
        ms = jnp.mean(h_new * h_new, axis=-1, keepdims=True)
        o_ref[rows, :] = h_new * lax.rsqrt(ms + EPS) * g_ref[...]


def _outproj(oa, ob, proj, mkv, x2d, w, g, seq):
    m = x2d.shape[0]
    steps_per_seq = seq // TM_OUT
    return _call(
        _outproj_kernel,
        grid=(m // TM_OUT,),
        in_specs=[
            pl.BlockSpec((TM_OUT, A_WIDTH), lambda i: (i, 0)),
            pl.BlockSpec((TM_OUT, B_WIDTH), lambda i: (i, 0)),
            pl.BlockSpec((TM_OUT, C_WIDTH), lambda i: (i, COL_CQ // C_WIDTH)),
            pl.BlockSpec((N_MEM, 2 * C_WIDTH), lambda i: (i // steps_per_seq, 0)),
            pl.BlockSpec((TM_OUT, MIX_WIDTH), lambda i: (i, COL_GATE // MIX_WIDTH)),
            pl.BlockSpec((TM_OUT, D_MODEL), lambda i: (i, 0)),
            pl.BlockSpec((MIX_WIDTH, D_MODEL), lambda i: (0, 0), pipeline_mode=pl.Buffered(1)),
            pl.BlockSpec((1, D_MODEL), lambda i: (0, 0)),
        ],
        out_spec=pl.BlockSpec((TM_OUT, D_MODEL), lambda i: (i, 0)),
        out_shape=jax.ShapeDtypeStruct((m, D_MODEL), F32),
        scratch_shapes=[pltpu.VMEM((MIX_WIDTH, D_MODEL), BF16)],
        semantics=("arbitrary",),
        name="outproj",
        operands=[oa, ob, proj, mkv, proj, x2d, w, g],
    )


def _col_scale():
    cs = np.ones((1, PROJ_W), np.float32)
    cs[0, COL_AQ:COL_AQ + A_WIDTH] = A_HEAD_DIM ** -0.5 * LOG2E
    cs[0, COL_BQ:COL_BQ + B_WIDTH] = B_HEAD_DIM ** -0.5 * LOG2E
    cs[0, COL_CQ:COL_CQ + C_WIDTH] = C_HEAD_DIM ** -0.5 * LOG2E
    return jnp.asarray(cs)


def kernel(x, mem, g_norm, w_in, sinks, rel_bias, g_mem, w_mem_kv, w_out, g_final):
    batch, seq, _ = x.shape
    assert w_in.shape[0] == 1, "the out-projection kernel fuses the final norm of a single-layer trunk"
    kk = np.arange(2 * TQ_A)[:, None]
    qq = np.arange(TQ_A)[None, :]
    bkt_a = _t5_bucket_np(kk - TQ_A - qq)
    kk = np.arange(TK_B)[None, :, None]
    qq = np.arange(TQ_B)[None, None, :]
    d = np.arange(NEAR_B)[:, None, None]
    bkt_b = _t5_bucket_np(kk - qq - TK_B * d)

    h = x.reshape(batch * seq, D_MODEL)
    mem2d = mem.reshape(batch * N_MEM, D_MODEL)
    w_t = jnp.transpose(w_in, (2, 0, 1)).reshape(IN_WIDTH * KB_IN, LANES)
    proj = _inproj(h, g_norm[0].reshape(1, D_MODEL), _wprep(w_t), _col_scale())
    mkv = _memkv(mem2d, g_mem[0].reshape(1, D_MODEL), w_mem_kv[0])
    oa = _swa(proj, bkt_a, rel_bias, sinks[0], batch, seq)
    ob = _dsa(proj, bkt_b, rel_bias, batch, seq)
    out = _outproj(oa, ob, proj, mkv, h, w_out[0], g_final.reshape(1, D_MODEL), seq)
    return out.reshape(batch, seq, D_MODEL)
```

```python
import functools
import math

import numpy as np
import jax
import jax.numpy as jnp
from jax import lax
from jax.experimental import pallas as pl
from jax.experimental.pallas import tpu as pltpu

D_MODEL = 2048
CHUNK = 64
N_MEM = 256
EPS = 1e-6
A_HEADS = 16
A_KV_HEADS = 2
A_HEAD_DIM = 64
WINDOW_CHUNKS = 2
A_WIDTH = A_HEADS * A_HEAD_DIM
B_HEADS = 4
B_HEAD_DIM = 128
B_WIDTH = B_HEADS * B_HEAD_DIM
IDX_HEADS = 4
IDX_DIM = 64
TOPK_MAX = 256
C_HEADS = 4
C_HEAD_DIM = 128
C_WIDTH = C_HEADS * C_HEAD_DIM
MIX_WIDTH = A_WIDTH + B_WIDTH + C_WIDTH
N_BUCKETS = 32
MAX_DISTANCE = 1024
KV_A = A_KV_HEADS * A_HEAD_DIM
SPLIT_SIZES = (A_WIDTH, KV_A, KV_A, B_WIDTH, B_WIDTH, B_WIDTH,
               IDX_HEADS * IDX_DIM, IDX_DIM, IDX_HEADS, C_WIDTH, MIX_WIDTH)
IN_WIDTH = sum(SPLIT_SIZES)

F32 = jnp.float32
BF16 = jnp.bfloat16
LOG2E = math.log2(math.e)
NEG = -1e30
INT_MIN = -(2 ** 31)
LANES = 128
SUBLANES = 8
PACKED = 16
HALF = 1 << 15
KEY_POS_INF = 0x7F800000
KEY_NEG_INF = -KEY_POS_INF
FINE_PASSES = 17
PRE_PASSES = 2
F32_MANT_BITS = 23
F32_EXP_BIAS = 127

(SRC_AQ, SRC_AK, SRC_AV, SRC_BQ, SRC_BK, SRC_BV,
 SRC_IQ, SRC_IK, SRC_IW, SRC_CQ, SRC_GATE) = (int(c) for c in np.cumsum((0,) + SPLIT_SIZES)[:-1])
COL_GATE = 0
COL_AQ = COL_GATE + MIX_WIDTH
COL_BQ = COL_AQ + A_WIDTH
COL_BK = COL_BQ + B_WIDTH
COL_BV = COL_BK + B_WIDTH
COL_CQ = COL_BV + B_WIDTH
COL_AKV = COL_CQ + C_WIDTH
COL_IQ = COL_AKV + 2 * KV_A
COL_IKW = COL_IQ + IDX_HEADS * IDX_DIM
PROJ_W = 6144
SEGMENTS = ((COL_GATE, SRC_GATE, MIX_WIDTH), (COL_AQ, SRC_AQ, A_WIDTH), (COL_BQ, SRC_BQ, B_WIDTH),
            (COL_BK, SRC_BK, B_WIDTH), (COL_BV, SRC_BV, B_WIDTH), (COL_CQ, SRC_CQ, C_WIDTH),
            (COL_AKV, SRC_AK, 2 * KV_A), (COL_IQ, SRC_IQ, IDX_HEADS * IDX_DIM), (COL_IKW, SRC_IK, 256))
KB_IN = D_MODEL // LANES

TC_PREP = 256
TM_IN, TN_IN = 1024, 2048
SUB_IN = 256
TQ_A = 128
RB_A = 1024
TQ_B = 256
TK_B = 256
NEAR_B = 4
TM_OUT = 512
SUB_OUT = 256
V7X_VMEM_BYTES = 64 * 1024 * 1024
VMEM_TEMP_PCT = 15


def _vmem_limit(in_specs, operands, out_spec, out_shape, scratch_shapes=()):
    def buffers(spec, dtype):
        if spec.block_shape is None:
            return 0
        elems = math.prod(int(getattr(d, "block_size", d)) for d in spec.block_shape)
        count = spec.pipeline_mode.buffer_count if spec.pipeline_mode is not None else 2
        return elems * jnp.dtype(dtype).itemsize * count

    total = sum(buffers(s, a.dtype) for s, a in zip(in_specs, operands)) + buffers(out_spec, out_shape.dtype)
    total += sum(math.prod(s.shape) * jnp.dtype(s.dtype).itemsize for s in scratch_shapes)
    return min(total * (100 + VMEM_TEMP_PCT) // 100, V7X_VMEM_BYTES * 7 // 8)


def _call(kernel, *, name, grid, semantics, in_specs, operands, out_spec, out_shape, scratch_shapes=()):
    return pl.pallas_call(
        kernel,
        grid=grid,
        in_specs=in_specs,
        out_specs=out_spec,
        out_shape=out_shape,
        scratch_shapes=list(scratch_shapes),
        compiler_params=pltpu.CompilerParams(
            dimension_semantics=semantics,
            vmem_limit_bytes=_vmem_limit(in_specs, operands, out_spec, out_shape, scratch_shapes)),
        name=name,
    )(*operands)


def _t5_bucket_np(rel):
    nb = N_BUCKETS // 2
    max_exact = nb // 2
    side = np.where(rel > 0, nb, 0)
    n = np.abs(rel)
    nf = np.maximum(n, max_exact).astype(np.float32)
    large = max_exact + (np.log(nf / max_exact) / math.log(MAX_DISTANCE / max_exact)
                         * (nb - max_exact)).astype(np.int32)
    large = np.minimum(large, nb - 1)
    return (side + np.where(n < max_exact, n, large)).astype(np.int32)


def _nt_dot(a, b):
    return lax.dot_general(a, b, (((1,), (1,)), ((), ())), preferred_element_type=F32)


def _bias_tables(bucket, present, rb_ref, cols, sub_row=None):
    accs = [jnp.zeros(bucket.shape, F32) for _ in cols]
    for b in present:
        hit = bucket == b
        for n, col in enumerate(cols):
            val = rb_ref[b, col]
            if sub_row is not None:
                val = val - rb_ref[sub_row, col]
            accs[n] = jnp.where(hit, val * LOG2E, accs[n])
    return accs


def _wprep_kernel(src_ref, w_ref, o_ref):
    t = pl.program_id(0)

    @pl.when(src_ref[t] >= 0)
    def _():
        for kb in range(KB_IN):
            o_ref[:, kb * LANES:(kb + 1) * LANES] = w_ref[pl.ds(kb, TC_PREP, stride=KB_IN), :].astype(BF16)

    @pl.when(src_ref[t] < 0)
    def _():
        o_ref[...] = jnp.zeros(o_ref.shape, BF16)


def _wprep(w_t):
    src = np.full((PROJ_W // TC_PREP,), -1, np.int32)
    for dst_col, src_col, width in SEGMENTS:
        for n in range(width // TC_PREP):
            src[dst_col // TC_PREP + n] = src_col + n * TC_PREP
    in_specs = [pl.BlockSpec((pl.Element(TC_PREP * KB_IN), pl.Element(LANES)),
                             lambda t, src: (jnp.maximum(src[t], 0) * KB_IN, 0))]
    out_spec = pl.BlockSpec((TC_PREP, D_MODEL), lambda t, src: (t, 0))
    out_shape = jax.ShapeDtypeStruct((PROJ_W, D_MODEL), BF16)
    return pl.pallas_call(
        _wprep_kernel,
        grid_spec=pltpu.PrefetchScalarGridSpec(
            num_scalar_prefetch=1, grid=(PROJ_W // TC_PREP,), in_specs=in_specs, out_specs=out_spec),
        out_shape=out_shape,
        compiler_params=pltpu.CompilerParams(
            dimension_semantics=("arbitrary",),
            vmem_limit_bytes=_vmem_limit(in_specs, [w_t], out_spec, out_shape)),
        name="wprep",
    )(jnp.asarray(src), w_t)


def _inproj_kernel(x_ref, g_ref, wt_ref, cs_ref, o_ref, hn_ref):
    j = pl.program_id(1)

    @pl.when(j == 0)
    def _():
        for r in range(TM_IN // SUB_IN):
            rows = slice(r * SUB_IN, (r + 1) * SUB_IN)
            x = x_ref[rows, :]
            ms = jnp.mean(x * x, axis=-1, keepdims=True)
            hn = (x * lax.rsqrt(ms + EPS) * g_ref[...]).astype(BF16)
            hn_ref[rows, :] = hn
            o_ref[rows, :] = (_nt_dot(hn, wt_ref[...]) * cs_ref[...]).astype(BF16)

    @pl.when(j > 0)
    def _():
        o_ref[...] = (_nt_dot(hn_ref[...], wt_ref[...]) * cs_ref[...]).astype(BF16)


def _inproj(x2d, g, wt, cs):
    m = x2d.shape[0]
    return _call(
        _inproj_kernel,
        grid=(m // TM_IN, PROJ_W // TN_IN),
        in_specs=[
            pl.BlockSpec((TM_IN, D_MODEL), lambda i, j: (i, 0)),
            pl.BlockSpec((1, D_MODEL), lambda i, j: (0, 0)),
            pl.BlockSpec((TN_IN, D_MODEL), lambda i, j: (j, 0)),
            pl.BlockSpec((1, TN_IN), lambda i, j: (0, j)),
        ],
        out_spec=pl.BlockSpec((TM_IN, TN_IN), lambda i, j: (i, j)),
        out_shape=jax.ShapeDtypeStruct((m, PROJ_W), BF16),
        scratch_shapes=[pltpu.VMEM((TM_IN, D_MODEL), BF16)],
        semantics=("parallel", "arbitrary"),
        name="inproj",
        operands=[x2d, g, wt, cs],
    )


def _memkv_kernel(m_ref, g_ref, w_ref, o_ref):
    x = m_ref[...]
    ms = jnp.mean(x * x, axis=-1, keepdims=True)
    hn = (x * lax.rsqrt(ms + EPS) * g_ref[...]).astype(BF16)
    o_ref[...] = jnp.dot(hn, w_ref[...].astype(BF16), preferred_element_type=F32).astype(BF16)


def _memkv(mem2d, g, w):
    m = mem2d.shape[0]
    return _call(
        _memkv_kernel,
        grid=(m // N_MEM,),
        in_specs=[
            pl.BlockSpec((N_MEM, D_MODEL), lambda i: (i, 0)),
            pl.BlockSpec((1, D_MODEL), lambda i: (0, 0)),
            pl.BlockSpec((D_MODEL, 2 * C_WIDTH), lambda i: (0, 0)),
        ],
        out_spec=pl.BlockSpec((N_MEM, 2 * C_WIDTH), lambda i: (i, 0)),
        out_shape=jax.ShapeDtypeStruct((m, 2 * C_WIDTH), BF16),
        semantics=("arbitrary",),
        name="memkv",
        operands=[mem2d, g, w],
    )


GRP_A = A_HEADS // A_KV_HEADS
PAIRS_A = GRP_A // 2
COLS_A = GRP_A * TQ_A
KEYS_A = 2 * TQ_A
CHUNK_PAIRS_A = 4


def _swa_kernel(q_ref, kvp_ref, kvc_ref, bkt_ref, rb_ref, sink_ref, o_ref,
                tab_ref, sinkv_ref, kp_ref, vt_ref, s_ref, m_ref, ot_ref, *, present):
    i = pl.program_id(1)

    @pl.when((pl.program_id(0) == 0) & (i == 0))
    def _():
        bkt = bkt_ref[...]
        kchunk = lax.broadcasted_iota(jnp.int32, bkt.shape, 0) // CHUNK
        qchunk = lax.broadcasted_iota(jnp.int32, bkt.shape, 1) // CHUNK
        allowed = (kchunk >= qchunk) & (kchunk <= qchunk + WINDOW_CHUNKS)
        has_prev = lax.broadcasted_iota(jnp.int32, bkt.shape, 0) >= TQ_A
        for g in range(A_KV_HEADS):
            for c0 in range(0, GRP_A, PAIRS_A):
                heads = [g * GRP_A + 2 * (col % PAIRS_A) + col // PAIRS_A for col in range(c0, c0 + PAIRS_A)]
                for n, t in enumerate(_bias_tables(bkt, present, rb_ref, heads)):
                    cols = slice((c0 + n) * TQ_A, (c0 + n + 1) * TQ_A)
                    t = jnp.where(allowed, t, NEG)
                    tab_ref[0, g, :, cols] = t
                    tab_ref[1, g, :, cols] = jnp.where(has_prev, t, NEG)
                    sinkv_ref[g, :, cols] = jnp.full((SUBLANES, TQ_A), sink_ref[heads[n]] * LOG2E, F32)

    width = CHUNK_PAIRS_A * TQ_A
    tasks = [(t, g, p0, parity) for t in range(RB_A // TQ_A) for g in range(A_KV_HEADS)
             for p0 in range(0, PAIRS_A, CHUNK_PAIRS_A) for parity in range(2)]

    def prepare(t, g):
        slot = (t * A_KV_HEADS + g) % 2
        if t == 0:
            kv_t = jnp.concatenate([kvp_ref[...], kvc_ref[0:TQ_A, :]], axis=0)
        else:
            kv_t = kvc_ref[(t - 1) * TQ_A:(t + 1) * TQ_A, :]
        kg = kv_t[:, g * A_HEAD_DIM:(g + 1) * A_HEAD_DIM]
        zeros = jnp.zeros((KEYS_A, A_HEAD_DIM), BF16)
        kp_ref[slot, 0] = jnp.concatenate([kg, zeros], axis=1)
        kp_ref[slot, 1] = jnp.concatenate([zeros, kg], axis=1)
        r = lax.broadcasted_iota(jnp.int32, (A_HEAD_DIM, KV_A), 0)
        c = lax.broadcasted_iota(jnp.int32, (A_HEAD_DIM, KV_A), 1)
        pick = jnp.where(c == g * A_HEAD_DIM + r, 1.0, 0.0).astype(BF16)
        vt = _nt_dot(pick, kv_t[:, KV_A:]).astype(BF16)
        vt_ref[slot] = jnp.concatenate([vt, jnp.ones((PACKED, KEYS_A), BF16)], axis=0)

    def chunk_cols(p0, parity):
        lo = (parity * PAIRS_A + p0) * TQ_A
        return slice(lo, lo + width)

    def logits_stage(n):
        t, g, p0, parity = tasks[n]
        if (p0, parity) == (0, 0):
            prepare(t, g)
        slot, buf, cols = (t * A_KV_HEADS + g) % 2, n % 2, chunk_cols(p0, parity)
        q_pairs = jnp.concatenate(
            [q_ref[t * TQ_A:(t + 1) * TQ_A, (g * PAIRS_A + p) * LANES:(g * PAIRS_A + p + 1) * LANES]
             for p in range(p0, p0 + CHUNK_PAIRS_A)], axis=0)
        variant = jnp.where(i == 0, 1, 0) if t == 0 else 0
        s = _nt_dot(kp_ref[slot, parity], q_pairs) + tab_ref[variant, g, :, cols]
        s_ref[buf] = s
        m = jnp.max(s.reshape(KEYS_A // SUBLANES, SUBLANES, width), axis=0)
        m_ref[buf] = jnp.maximum(jnp.broadcast_to(jnp.max(m, axis=0, keepdims=True), m.shape),
                                 sinkv_ref[g, :, cols])

    def update_stage(n):
        t, g, p0, parity = tasks[n]
        slot, buf, cols = (t * A_KV_HEADS + g) % 2, n % 2, chunk_cols(p0, parity)
        m = m_ref[buf]
        s = s_ref[buf].reshape(KEYS_A // SUBLANES, SUBLANES, width)
        p = jnp.exp2(s - m[None]).reshape(KEYS_A, width).astype(BF16)
        pv = jnp.dot(vt_ref[slot], p, preferred_element_type=F32)
        den = pv[A_HEAD_DIM:A_HEAD_DIM + SUBLANES] + jnp.exp2(sinkv_ref[g, :, cols] - m)
        out_t = pv[:A_HEAD_DIM].reshape(A_HEAD_DIM // SUBLANES, SUBLANES, width) / den[None]
        ot_ref[parity] = out_t.reshape(A_HEAD_DIM, width)
        if parity == 1:
            for k in range(CHUNK_PAIRS_A):
                blk = jnp.concatenate([ot_ref[0, :, k * TQ_A:(k + 1) * TQ_A],
                                       ot_ref[1, :, k * TQ_A:(k + 1) * TQ_A]], axis=0)
                lanes = slice((g * PAIRS_A + p0 + k) * LANES, (g * PAIRS_A + p0 + k + 1) * LANES)
                o_ref[t * TQ_A:(t + 1) * TQ_A, lanes] = blk.T.astype(BF16)

    logits_stage(0)
    for n in range(len(tasks)):
        if n + 1 < len(tasks):
            logits_stage(n + 1)
        update_stage(n)


def _swa(proj, bkt, rel_bias, sinks, batch, seq):
    nt = seq // RB_A
    sub = RB_A // TQ_A
    kv_blk = COL_AKV // (2 * KV_A)
    return _call(
        functools.partial(_swa_kernel, present=tuple(int(b) for b in np.unique(bkt))),
        grid=(batch, nt),
        in_specs=[
            pl.BlockSpec((RB_A, A_WIDTH), lambda b, i: (b * nt + i, COL_AQ // A_WIDTH)),
            pl.BlockSpec((TQ_A, 2 * KV_A), lambda b, i: (jnp.maximum((b * nt + i) * sub - 1, 0), kv_blk)),
            pl.BlockSpec((RB_A, 2 * KV_A), lambda b, i: (b * nt + i, kv_blk)),
            pl.BlockSpec((KEYS_A, TQ_A), lambda b, i: (0, 0)),
            pl.BlockSpec(memory_space=pltpu.SMEM),
            pl.BlockSpec(memory_space=pltpu.SMEM),
        ],
        out_spec=pl.BlockSpec((RB_A, A_WIDTH), lambda b, i: (b * nt + i, 0)),
        out_shape=jax.ShapeDtypeStruct((batch * seq, A_WIDTH), BF16),
        scratch_shapes=[
            pltpu.VMEM((2, A_KV_HEADS, KEYS_A, COLS_A), F32),
            pltpu.VMEM((A_KV_HEADS, SUBLANES, COLS_A), F32),
            pltpu.VMEM((2, 2, KEYS_A, LANES), BF16),
            pltpu.VMEM((2, A_HEAD_DIM + PACKED, KEYS_A), BF16),
            pltpu.VMEM((2, KEYS_A, CHUNK_PAIRS_A * TQ_A), F32),
            pltpu.VMEM((2, SUBLANES, CHUNK_PAIRS_A * TQ_A), F32),
            pltpu.VMEM((2, A_HEAD_DIM, CHUNK_PAIRS_A * TQ_A), F32),
        ],
        semantics=("arbitrary", "arbitrary"),
        name="swa",
        operands=[proj, proj, proj, jnp.asarray(bkt), rel_bias, sinks],
    )


def _dsa_kernel(iq_ref, iwq_ref, ik_ref, q_ref, k_ref, v_ref, bkt_ref, rb_ref, o_ref,
                sc_ref, scb_ref, d16_ref, mb_ref, tab_ref, tri_ref, vt_ref, acc_ref, m_ref,
                sa_ref, smaxa_ref, sb_ref, smaxb_ref, *, topk, seq, present):
    i = pl.program_id(1)
    nkt = i + 1
    int32, int16 = jnp.int32, jnp.int16
    grp = TK_B // SUBLANES

    def hcols(h):
        return slice(h * B_HEAD_DIM, (h + 1) * B_HEAD_DIM)

    def ktile(kt):
        return pl.ds(pl.multiple_of(kt * TK_B, TK_B), TK_B)

    def rows3(a):
        return a.reshape(a.shape[0] // SUBLANES, SUBLANES, TQ_B)

    def all_rows(a, op):
        return jnp.broadcast_to(op(a, axis=0, keepdims=True), a.shape)

    @pl.when((pl.program_id(0) == 0) & (i == 0))
    def _():
        far_bucket = N_BUCKETS // 2 - 1
        for d in range(NEAR_B):
            for h0 in range(0, B_HEADS, 2):
                tabs = _bias_tables(bkt_ref[d], present[d], rb_ref, [A_HEADS + h0, A_HEADS + h0 + 1],
                                    sub_row=far_bucket)
                tab_ref[d, h0] = tabs[0]
                tab_ref[d, h0 + 1] = tabs[1]
        tab_ref[NEAR_B] = jnp.zeros(tab_ref.shape[1:], F32)
        r = lax.broadcasted_iota(int32, (TK_B, TK_B), 0)
        c = lax.broadcasted_iota(int32, (TK_B, TK_B), 1)
        tri_ref[...] = jnp.where(c < r, 1.0, 0.0).astype(BF16)

    @pl.when(i == 0)
    def _():
        r = lax.broadcasted_iota(int32, (B_HEAD_DIM, B_HEAD_DIM), 0)
        c = lax.broadcasted_iota(int32, (B_HEAD_DIM, B_HEAD_DIM), 1)
        eye = jnp.where(r == c, 1.0, 0.0).astype(BF16)

        def body(kt, carry):
            for h in range(B_HEADS):
                vt_ref[kt, h] = _nt_dot(eye, v_ref[ktile(kt), hcols(h)]).astype(BF16)
            return carry

        lax.fori_loop(0, seq // TK_B, body, 0)

    key_chunk = lax.broadcasted_iota(int32, (TK_B, TQ_B), 0) // CHUNK
    qry_chunk = lax.broadcasted_iota(int32, (TK_B, TQ_B), 1) // CHUNK
    adm_diag = key_chunk <= qry_chunk

    r = lax.broadcasted_iota(int32, (IDX_HEADS * SUBLANES, LANES), 0)
    c = lax.broadcasted_iota(int32, (IDX_HEADS * SUBLANES, LANES), 1)
    pick_w = jnp.where(c == IDX_DIM + r // SUBLANES, 1.0, 0.0).astype(BF16)
    w_all = _nt_dot(pick_w, iwq_ref[...]) * (IDX_HEADS ** -0.5 * IDX_DIM ** -0.5)

    def for_tiles(n, body, unroll=2):
        def group(j, carry):
            for k in range(unroll):
                body(unroll * j + k)
            return carry

        def single(kt, carry):
            body(kt)
            return carry

        lax.fori_loop(0, n // unroll, group, 0)
        lax.fori_loop(n - n % unroll, n, single, 0)

    def score_tile(kt):
        ikt = ik_ref[ktile(kt), 0:IDX_DIM]
        sc = jnp.zeros((grp, SUBLANES, TQ_B), F32)
        for h in range(IDX_HEADS):
            x = _nt_dot(ikt, iq_ref[:, h * IDX_DIM:(h + 1) * IDX_DIM])
            sc = sc + w_all[h * SUBLANES:(h + 1) * SUBLANES][None] * jnp.maximum(rows3(x), 0.0)
        store_score(kt, sc.reshape(TK_B, TQ_B))

    def store_score(kt, sc):
        sc_ref[kt] = sc
        scb_ref[kt] = sc.astype(BF16)

    for_tiles(nkt, score_tile, unroll=4)
    store_score(i, jnp.where(adm_diag, sc_ref[i], -jnp.inf))

    def key_to_f32(key):
        return pltpu.bitcast(jnp.where(key >= 0, key, INT_MIN - key), F32)

    def over_tiles(tile_count, zero):
        def pair(j, acc):
            return acc + (tile_count(2 * j) + tile_count(2 * j + 1))
        acc = lax.fori_loop(0, nkt // 2, pair, zero)
        return lax.cond(nkt % 2 == 1, lambda a: a + tile_count(nkt - 1), lambda a: a, acc)

    def tree_sum(parts):
        while len(parts) > 1:
            parts = [parts[n] + parts[n + 1] for n in range(0, len(parts), 2)]
        return parts[0]

    def count_bf16(cand):
        def tile_count(kt):
            blk = scb_ref[kt].reshape(TK_B // PACKED, PACKED, TQ_B)
            return tree_sum([jnp.where(blk[g] >= cand, jnp.ones((), int16), jnp.zeros((), int16))
                             for g in range(TK_B // PACKED)])
        acc = over_tiles(tile_count, jnp.zeros((PACKED, TQ_B), int16))
        return all_rows(acc.astype(int32), jnp.sum)

    def count_f32(pred):
        def tile_count(kt):
            blk = rows3(sc_ref[kt])
            return tree_sum([jnp.where(pred(blk[g]), 1, 0) for g in range(grp)])
        return all_rows(over_tiles(tile_count, jnp.zeros((SUBLANES, TQ_B), int32)), jnp.sum)

    def coarse_pass(p, u):
        bit = jnp.left_shift(jnp.int32(1), 15 - p)
        cand = key_to_f32(((u | bit) - HALF) << 16).astype(BF16)
        return jnp.where(count_bf16(cand) >= topk, u | bit, u)

    u = lax.fori_loop(0, 16, coarse_pass, jnp.zeros((PACKED, TQ_B), int32))
    coarse_key = ((u - HALF) << 16)[:SUBLANES]

    lo = jnp.maximum(coarse_key - HALF, KEY_NEG_INF)
    hi = jnp.minimum(coarse_key, KEY_POS_INF - 2 * HALF) + 2 * HALF

    def fine_pass(p, lohi):
        lo, hi = lohi
        mid = lo + ((hi - lo) >> 1)
        cand = key_to_f32(mid)
        ok = count_f32(lambda blk: blk >= cand) >= topk
        return jnp.where(ok, mid, lo), jnp.where(ok, hi, mid)

    lo, hi = lax.fori_loop(0, PRE_PASSES, fine_pass, (lo, hi))

    mag = jnp.minimum(jnp.abs(lo), jnp.abs(hi))
    around_zero = ((lo <= 0) & (hi >= 0)) | (mag >> F32_MANT_BITS == 0)
    exp_field = jnp.where(around_zero, F32_MANT_BITS + 1, mag >> F32_MANT_BITS)
    any_irregular = jnp.max(jnp.where(exp_field <= F32_MANT_BITS, 1, 0)) > 0

    def finish_f32(lohi):
        lo, hi = lax.fori_loop(PRE_PASSES, FINE_PASSES, fine_pass, lohi)
        tau_f = key_to_f32(lo)
        return tau_f, count_f32(lambda blk: blk > tau_f)

    def finish_int16(lohi):
        lo, hi = lohi
        centre = jnp.where(around_zero, 0.0, key_to_f32(lo + ((hi - lo) >> 1)))
        ulp = pltpu.bitcast((exp_field - F32_MANT_BITS) << F32_MANT_BITS, F32)
        inv_ulp = pltpu.bitcast((2 * F32_EXP_BIAS + F32_MANT_BITS - exp_field) << F32_MANT_BITS, F32)

        def build(kt):
            d = (rows3(sc_ref[kt]) - centre[None]) * inv_ulp[None]
            d = jnp.clip(d, -HALF, HALF - 1).reshape(TK_B, TQ_B)
            d16_ref[kt] = d.astype(int32).astype(int16)

        for_tiles(nkt, build)

        def count_d16(cand):
            def tile_count(kt):
                blk = d16_ref[kt].reshape(TK_B // PACKED, PACKED, TQ_B)
                return tree_sum([jnp.where(blk[g] >= cand, jnp.ones((), int16), jnp.zeros((), int16))
                                 for g in range(TK_B // PACKED)])
            acc = over_tiles(tile_count, jnp.zeros((PACKED, TQ_B), int16))
            return all_rows(acc.astype(int32), jnp.sum)

        def d_pass(p, u):
            bit = jnp.left_shift(jnp.int32(1), 15 - p)
            cand = ((u | bit) - HALF).astype(int16)
            return jnp.where(count_d16(cand) >= topk, u | bit, u)

        t = lax.fori_loop(0, 16, d_pass, jnp.zeros((PACKED, TQ_B), int32)) - HALF
        above = count_d16((t + 1).astype(int16))
        return centre + t[:SUBLANES].astype(F32) * ulp, above[:SUBLANES]

    tau2d, above = lax.cond(any_irregular, finish_f32, finish_int16, (lo, hi))
    tau = tau2d[None]

    need = (topk - above).astype(F32)[None]
    ones_l = jnp.ones((2 * SUBLANES, TK_B), BF16)

    def mask_tile(kt, run):
        blk = rows3(sc_ref[kt])
        eq = blk == tau
        eqf = jnp.where(eq, 1.0, 0.0).reshape(TK_B, TQ_B).astype(BF16)
        rank = rows3(jnp.dot(tri_ref[...], eqf, preferred_element_type=F32)) + run[None]
        sel = (blk > tau) | (eq & (rank < need))
        mb_ref[kt] = jnp.where(sel, 0.0, NEG).reshape(TK_B, TQ_B)
        return run + jnp.dot(ones_l, eqf, preferred_element_type=F32)[:SUBLANES]

    def mask_group(j, run):
        for k in range(4):
            run = mask_tile(4 * j + k, run)
        return run

    run = lax.fori_loop(0, nkt // 4, mask_group, jnp.zeros((SUBLANES, TQ_B), F32))
    lax.fori_loop(nkt - nkt % 4, nkt, mask_tile, run)

    mb_ref[i] = jnp.where(adm_diag, mb_ref[i], NEG)

    m_ref[...] = jnp.full(m_ref.shape, NEG, F32)
    acc_ref[...] = jnp.zeros(acc_ref.shape, F32)
    ones_rows = jnp.ones((PACKED, TK_B), BF16)

    buf_a, buf_b = (sa_ref, smaxa_ref), (sb_ref, smaxb_ref)

    def logits_stage(kt, buf, near):
        s_ref, smax_ref = buf
        mb = mb_ref[kt]
        for h in range(B_HEADS):
            s = _nt_dot(k_ref[ktile(kt), hcols(h)], q_ref[:, hcols(h)]) + mb
            if near:
                s = s + tab_ref[jnp.minimum(i - kt, NEAR_B), h]
            s_ref[h] = s
            smax_ref[h] = all_rows(jnp.max(rows3(s), axis=0), jnp.max)

    def update_stage(kt, buf):
        s_ref, smax_ref = buf
        for h in range(B_HEADS):
            m_old = m_ref[h]
            m_new = jnp.maximum(m_old, smax_ref[h])
            alpha = jnp.exp2(m_old - m_new)
            p = jnp.exp2(rows3(s_ref[h]) - m_new[None]).reshape(TK_B, TQ_B).astype(BF16)
            vaug = jnp.concatenate([vt_ref[kt, h], ones_rows], axis=0)
            pv = jnp.dot(vaug, p, preferred_element_type=F32)
            acc_ref[h] = (rows3(acc_ref[h]) * alpha[None] + rows3(pv)).reshape(acc_ref.shape[1:])
            m_ref[h] = m_new

    def pair_body(j, carry, near):
        logits_stage(2 * j + 1, buf_b, near)
        update_stage(2 * j, buf_a)
        logits_stage(2 * j + 2, buf_a, near)
        update_stage(2 * j + 1, buf_b)
        return carry

    near_lo = jnp.maximum(i - (NEAR_B - 1), 0)
    far_pairs = jnp.maximum(near_lo - 1, 0) // 2
    logits_stage(0, buf_a, True)
    lax.fori_loop(0, far_pairs, functools.partial(pair_body, near=False), 0)
    lax.fori_loop(far_pairs, (nkt - 1) // 2, functools.partial(pair_body, near=True), 0)

    @pl.when(nkt % 2 == 0)
    def _():
        logits_stage(nkt - 1, buf_b, True)
        update_stage(nkt - 2, buf_a)
        update_stage(nkt - 1, buf_b)

    @pl.when(nkt % 2 == 1)
    def _():
        update_stage(nkt - 1, buf_a)

    for h in range(B_HEADS):
        num = rows3(acc_ref[h, 0:B_HEAD_DIM, :])
        den = acc_ref[h, B_HEAD_DIM:B_HEAD_DIM + SUBLANES, :]
        out_t = (num / den[None]).reshape(B_HEAD_DIM, TQ_B)
        o_ref[:, h * B_HEAD_DIM:(h + 1) * B_HEAD_DIM] = out_t.T.astype(BF16)


def _dsa(proj, bkt, rel_bias, batch, seq):
    nt = seq // TQ_B
    topk = min(TOPK_MAX, seq // 4)

    def q_spec(width, col):
        return pl.BlockSpec((TQ_B, width), lambda b, i: (b * nt + i, col // width))

    def seq_spec(width, col):
        return pl.BlockSpec((seq, width), lambda b, i: (b, col // width))

    return _call(
        functools.partial(_dsa_kernel, topk=topk, seq=seq,
                          present=tuple(tuple(int(b) for b in np.unique(t)) for t in bkt)),
        grid=(batch, nt),
        in_specs=[
            q_spec(IDX_HEADS * IDX_DIM, COL_IQ),
            q_spec(LANES, COL_IKW),
            seq_spec(LANES, COL_IKW),
            q_spec(B_WIDTH, COL_BQ), seq_spec(B_WIDTH, COL_BK), seq_spec(B_WIDTH, COL_BV),
            pl.BlockSpec((NEAR_B, TK_B, TQ_B), lambda b, i: (0, 0, 0)),
            pl.BlockSpec(memory_space=pltpu.SMEM),
        ],
        out_spec=pl.BlockSpec((TQ_B, B_WIDTH), lambda b, i: (b * nt + i, 0)),
        out_shape=jax.ShapeDtypeStruct((batch * seq, B_WIDTH), BF16),
        scratch_shapes=[
            pltpu.VMEM((nt, TK_B, TQ_B), F32),
            pltpu.VMEM((nt, TK_B, TQ_B), BF16),
            pltpu.VMEM((nt, TK_B, TQ_B), jnp.int16),
            pltpu.VMEM((nt, TK_B, TQ_B), F32),
            pltpu.VMEM((NEAR_B + 1, B_HEADS, TK_B, TQ_B), F32),
            pltpu.VMEM((TK_B, TK_B), BF16),
            pltpu.VMEM((seq // TK_B, B_HEADS, B_HEAD_DIM, TK_B), BF16),
            pltpu.VMEM((B_HEADS, B_HEAD_DIM + PACKED, TQ_B), F32),
            pltpu.VMEM((B_HEADS, SUBLANES, TQ_B), F32),
            pltpu.VMEM((B_HEADS, TK_B, TQ_B), F32),
            pltpu.VMEM((B_HEADS, SUBLANES, TQ_B), F32),
            pltpu.VMEM((B_HEADS, TK_B, TQ_B), F32),
            pltpu.VMEM((B_HEADS, SUBLANES, TQ_B), F32),
        ],
        semantics=("arbitrary", "arbitrary"),
        name="dsa",
        operands=[proj, proj, proj, proj, proj, proj, jnp.asarray(bkt), rel_bias],
    )


def _outproj_kernel(oa_ref, ob_ref, cq_ref, mkv_ref, gate_ref, x_ref, w32_ref, g_ref, o_ref, w_ref):
    @pl.when(pl.program_id(0) == 0)
    def _():
        w_ref[...] = w32_ref[...].astype(BF16)

    ones_m = jnp.ones((N_MEM, C_HEAD_DIM), BF16)
    for r in range(TM_OUT // SUB_OUT):
        rows = slice(r * SUB_OUT, (r + 1) * SUB_OUT)
        att = [oa_ref[rows, :].astype(F32), ob_ref[rows, :].astype(F32)]
        for h in range(C_HEADS):
            hs = slice(h * C_HEAD_DIM, (h + 1) * C_HEAD_DIM)
            s = _nt_dot(cq_ref[rows, hs], mkv_ref[:, hs])
            p = jnp.exp2(s - jnp.max(s, axis=1, keepdims=True)).astype(BF16)
            vaug = jnp.concatenate([mkv_ref[:, C_WIDTH + h * C_HEAD_DIM:C_WIDTH + (h + 1) * C_HEAD_DIM], ones_m],
                                   axis=1)
            pv = jnp.dot(p, vaug, preferred_element_type=F32)
            att.append(pv[:, :C_HEAD_DIM] / pv[:, C_HEAD_DIM:])
        gate = gate_ref[rows, :].astype(F32)
        sg = gate / (1.0 + jnp.exp(-gate))
        y = (jnp.concatenate(att, axis=1) * sg).astype(BF16)
        h_new = x_ref[rows, :] + jnp.dot(y, w_ref[...], preferred_element_type=F32)
        ms = jnp.mean(h_new * h_new, axis=-1, keepdims=True)
        o_ref[rows, :] = h_new * lax.rsqrt(ms + EPS) * g_ref[...]


def _outproj(oa, ob, proj, mkv, x2d, w, g, seq):
    m = x2d.shape[0]
    steps_per_seq = seq // TM_OUT
    return _call(
        _outproj_kernel,
        grid=(m // TM_OUT,),
        in_specs=[
            pl.BlockSpec((TM_OUT, A_WIDTH), lambda i: (i, 0)),
            pl.BlockSpec((TM_OUT, B_WIDTH), lambda i: (i, 0)),
            pl.BlockSpec((TM_OUT, C_WIDTH), lambda i: (i, COL_CQ // C_WIDTH)),
            pl.BlockSpec((N_MEM, 2 * C_WIDTH), lambda i: (i // steps_per_seq, 0)),
            pl.BlockSpec((TM_OUT, MIX_WIDTH), lambda i: (i, COL_GATE // MIX_WIDTH)),
            pl.BlockSpec((TM_OUT, D_MODEL), lambda i: (i, 0)),
            pl.BlockSpec((MIX_WIDTH, D_MODEL), lambda i: (0, 0), pipeline_mode=pl.Buffered(1)),
            pl.BlockSpec((1, D_MODEL), lambda i: (0, 0)),
        ],
        out_spec=pl.BlockSpec((TM_OUT, D_MODEL), lambda i: (i, 0)),
        out_shape=jax.ShapeDtypeStruct((m, D_MODEL), F32),
        scratch_shapes=[pltpu.VMEM((MIX_WIDTH, D_MODEL), BF16)],
        semantics=("arbitrary",),
        name="outproj",
        operands=[oa, ob, proj, mkv, proj, x2d, w, g],
    )


def _col_scale():
    cs = np.ones((1, PROJ_W), np.float32)
    cs[0, COL_AQ:COL_AQ + A_WIDTH] = A_HEAD_DIM ** -0.5 * LOG2E
    cs[0, COL_BQ:COL_BQ + B_WIDTH] = B_HEAD_DIM ** -0.5 * LOG2E
    cs[0, COL_CQ:COL_CQ + C_WIDTH] = C_HEAD_DIM ** -0.5 * LOG2E
    return jnp.asarray(cs)


def kernel(x, mem, g_norm, w_in, sinks, rel_bias, g_mem, w_mem_kv, w_out, g_final):
    batch, seq, _ = x.shape
    assert w_in.shape[0] == 1, "the out-projection kernel fuses the final norm of a single-layer trunk"
    kk = np.arange(2 * TQ_A)[:, None]
    qq = np.arange(TQ_A)[None, :]
    bkt_a = _t5_bucket_np(kk - TQ_A - qq)
    kk = np.arange(TK_B)[None, :, None]
    qq = np.arange(TQ_B)[None, None, :]
    d = np.arange(NEAR_B)[:, None, None]
    bkt_b = _t5_bucket_np(kk - qq - TK_B * d)

    h = x.reshape(batch * seq, D_MODEL)
    mem2d = mem.reshape(batch * N_MEM, D_MODEL)
    w_t = jnp.transpose(w_in, (2, 0, 1)).reshape(IN_WIDTH * KB_IN, LANES)
    proj = _inproj(h, g_norm[0].reshape(1, D_MODEL), _wprep(w_t), _col_scale())
    mkv = _memkv(mem2d, g_mem[0].reshape(1, D_MODEL), w_mem_kv[0])
    oa = _swa(proj, bkt_a, rel_bias, sinks[0], batch, seq)
    ob = _dsa(proj, bkt_b, rel_bias, batch, seq)
    out = _outproj(oa, ob, proj, mkv, h, w_out[0], g_final.reshape(1, D_MODEL), seq)
    return out.reshape(batch, seq, D_MODEL)
```

```python
import functools
import math

import numpy as np
import jax
import jax.numpy as jnp
from jax import lax
from jax.experimental import pallas as pl
from jax.experimental.pallas import tpu as pltpu

D_MODEL = 2048
CHUNK = 64
N_MEM = 256
EPS = 1e-6
A_HEADS = 16
A_KV_HEADS = 2
A_HEAD_DIM = 64
WINDOW_CHUNKS = 2
A_WIDTH = A_HEADS * A_HEAD_DIM
B_HEADS = 4
B_HEAD_DIM = 128
B_WIDTH = B_HEADS * B_HEAD_DIM
IDX_HEADS = 4
IDX_DIM = 64
TOPK_MAX = 256
C_HEADS = 4
C_HEAD_DIM = 128
C_WIDTH = C_HEADS * C_HEAD_DIM
MIX_WIDTH = A_WIDTH + B_WIDTH + C_WIDTH
N_BUCKETS = 32
MAX_DISTANCE = 1024
KV_A = A_KV_HEADS * A_HEAD_DIM
SPLIT_SIZES = (A_WIDTH, KV_A, KV_A, B_WIDTH, B_WIDTH, B_WIDTH,
               IDX_HEADS * IDX_DIM, IDX_DIM, IDX_HEADS, C_WIDTH, MIX_WIDTH)
IN_WIDTH = sum(SPLIT_SIZES)

F32 = jnp.float32
BF16 = jnp.bfloat16
LOG2E = math.log2(math.e)
NEG = -1e30
INT_MIN = -(2 ** 31)
LANES = 128
SUBLANES = 8
PACKED = 16
HALF = 1 << 15
KEY_POS_INF = 0x7F800000
KEY_NEG_INF = -KEY_POS_INF
FINE_PASSES = 17
PRE_PASSES = 2
F32_MANT_BITS = 23
F32_EXP_BIAS = 127

(SRC_AQ, SRC_AK, SRC_AV, SRC_BQ, SRC_BK, SRC_BV,
 SRC_IQ, SRC_IK, SRC_IW, SRC_CQ, SRC_GATE) = (int(c) for c in np.cumsum((0,) + SPLIT_SIZES)[:-1])
COL_GATE = 0
COL_AQ = COL_GATE + MIX_WIDTH
COL_BQ = COL_AQ + A_WIDTH
COL_BK = COL_BQ + B_WIDTH
COL_BV = COL_BK + B_WIDTH
COL_CQ = COL_BV + B_WIDTH
COL_AKV = COL_CQ + C_WIDTH
COL_IQ = COL_AKV + 2 * KV_A
COL_IKW = COL_IQ + IDX_HEADS * IDX_DIM
PROJ_W = 6144
SEGMENTS = ((COL_GATE, SRC_GATE, MIX_WIDTH), (COL_AQ, SRC_AQ, A_WIDTH), (COL_BQ, SRC_BQ, B_WIDTH),
            (COL_BK, SRC_BK, B_WIDTH), (COL_BV, SRC_BV, B_WIDTH), (COL_CQ, SRC_CQ, C_WIDTH),
            (COL_AKV, SRC_AK, 2 * KV_A), (COL_IQ, SRC_IQ, IDX_HEADS * IDX_DIM), (COL_IKW, SRC_IK, 256))
KB_IN = D_MODEL // LANES

TC_PREP = 256
TM_IN, TN_IN = 1024, 2048
SUB_IN = 256
TQ_A = 128
RB_A = 1024
TQ_B = 256
TK_B = 256
NEAR_B = 4
TM_OUT = 512
SUB_OUT = 256
V7X_VMEM_BYTES = 64 * 1024 * 1024
VMEM_TEMP_PCT = 15


def _vmem_limit(in_specs, operands, out_spec, out_shape, scratch_shapes=()):
    def buffers(spec, dtype):
        if spec.block_shape is None:
            return 0
        elems = math.prod(int(getattr(d, "block_size", d)) for d in spec.block_shape)
        count = spec.pipeline_mode.buffer_count if spec.pipeline_mode is not None else 2
        return elems * jnp.dtype(dtype).itemsize * count

    total = sum(buffers(s, a.dtype) for s, a in zip(in_specs, operands)) + buffers(out_spec, out_shape.dtype)
    total += sum(math.prod(s.shape) * jnp.dtype(s.dtype).itemsize for s in scratch_shapes)
    return min(total * (100 + VMEM_TEMP_PCT) // 100, V7X_VMEM_BYTES * 7 // 8)


def _call(kernel, *, name, grid, semantics, in_specs, operands, out_spec, out_shape, scratch_shapes=()):
    return pl.pallas_call(
        kernel,
        grid=grid,
        in_specs=in_specs,
        out_specs=out_spec,
        out_shape=out_shape,
        scratch_shapes=list(scratch_shapes),
        compiler_params=pltpu.CompilerParams(
            dimension_semantics=semantics,
            vmem_limit_bytes=_vmem_limit(in_specs, operands, out_spec, out_shape, scratch_shapes)),
        name=name,
    )(*operands)


def _t5_bucket_np(rel):
    nb = N_BUCKETS // 2
    max_exact = nb // 2
    side = np.where(rel > 0, nb, 0)
    n = np.abs(rel)
    nf = np.maximum(n, max_exact).astype(np.float32)
    large = max_exact + (np.log(nf / max_exact) / math.log(MAX_DISTANCE / max_exact)
                         * (nb - max_exact)).astype(np.int32)
    large = np.minimum(large, nb - 1)
    return (side + np.where(n < max_exact, n, large)).astype(np.int32)


def _nt_dot(a, b):
    return lax.dot_general(a, b, (((1,), (1,)), ((), ())), preferred_element_type=F32)


def _bias_tables(bucket, present, rb_ref, cols, sub_row=None):
    accs = [jnp.zeros(bucket.shape, F32) for _ in cols]
    for b in present:
        hit = bucket == b
        for n, col in enumerate(cols):
            val = rb_ref[b, col]
            if sub_row is not None:
                val = val - rb_ref[sub_row, col]
            accs[n] = jnp.where(hit, val * LOG2E, accs[n])
    return accs


def _wprep_kernel(src_ref, w_ref, o_ref):
    t = pl.program_id(0)

    @pl.when(src_ref[t] >= 0)
    def _():
        for kb in range(KB_IN):
            o_ref[:, kb * LANES:(kb + 1) * LANES] = w_ref[pl.ds(kb, TC_PREP, stride=KB_IN), :].astype(BF16)

    @pl.when(src_ref[t] < 0)
    def _():
        o_ref[...] = jnp.zeros(o_ref.shape, BF16)


def _wprep(w_t):
    src = np.full((PROJ_W // TC_PREP,), -1, np.int32)
    for dst_col, src_col, width in SEGMENTS:
        for n in range(width // TC_PREP):
            src[dst_col // TC_PREP + n] = src_col + n * TC_PREP
    in_specs = [pl.BlockSpec((pl.Element(TC_PREP * KB_IN), pl.Element(LANES)),
                             lambda t, src: (jnp.maximum(src[t], 0) * KB_IN, 0))]
    out_spec = pl.BlockSpec((TC_PREP, D_MODEL), lambda t, src: (t, 0))
    out_shape = jax.ShapeDtypeStruct((PROJ_W, D_MODEL), BF16)
    return pl.pallas_call(
        _wprep_kernel,
        grid_spec=pltpu.PrefetchScalarGridSpec(
            num_scalar_prefetch=1, grid=(PROJ_W // TC_PREP,), in_specs=in_specs, out_specs=out_spec),
        out_shape=out_shape,
        compiler_params=pltpu.CompilerParams(
            dimension_semantics=("arbitrary",),
            vmem_limit_bytes=_vmem_limit(in_specs, [w_t], out_spec, out_shape)),
        name="wprep",
    )(jnp.asarray(src), w_t)


def _inproj_kernel(x_ref, g_ref, wt_ref, cs_ref, o_ref, hn_ref):
    assert COL_GATE == 0 and MIX_WIDTH == TN_IN
    j = pl.program_id(1)

    @pl.when(j == 0)
    def _():
        for r in range(TM_IN // SUB_IN):
            rows = slice(r * SUB_IN, (r + 1) * SUB_IN)
            x = x_ref[rows, :]
            ms = jnp.mean(x * x, axis=-1, keepdims=True)
            hn = (x * lax.rsqrt(ms + EPS) * g_ref[...]).astype(BF16)
            hn_ref[rows, :] = hn
            gate = _nt_dot(hn, wt_ref[...])
            o_ref[rows, :] = (gate / (1.0 + jnp.exp(-gate))).astype(BF16)

    @pl.when(j > 0)
    def _():
        o_ref[...] = (_nt_dot(hn_ref[...], wt_ref[...]) * cs_ref[...]).astype(BF16)


def _inproj(x2d, g, wt, cs):
    m = x2d.shape[0]
    return _call(
        _inproj_kernel,
        grid=(m // TM_IN, PROJ_W // TN_IN),
        in_specs=[
            pl.BlockSpec((TM_IN, D_MODEL), lambda i, j: (i, 0)),
            pl.BlockSpec((1, D_MODEL), lambda i, j: (0, 0)),
            pl.BlockSpec((TN_IN, D_MODEL), lambda i, j: (j, 0)),
            pl.BlockSpec((1, TN_IN), lambda i, j: (0, j)),
        ],
        out_spec=pl.BlockSpec((TM_IN, TN_IN), lambda i, j: (i, j)),
        out_shape=jax.ShapeDtypeStruct((m, PROJ_W), BF16),
        scratch_shapes=[pltpu.VMEM((TM_IN, D_MODEL), BF16)],
        semantics=("parallel", "arbitrary"),
        name="inproj",
        operands=[x2d, g, wt, cs],
    )


def _memkv_kernel(m_ref, g_ref, w_ref, o_ref):
    x = m_ref[...]
    ms = jnp.mean(x * x, axis=-1, keepdims=True)
    hn = (x * lax.rsqrt(ms + EPS) * g_ref[...]).astype(BF16)
    o_ref[...] = jnp.dot(hn, w_ref[...].astype(BF16), preferred_element_type=F32).astype(BF16)


def _memkv(mem2d, g, w):
    m = mem2d.shape[0]
    return _call(
        _memkv_kernel,
        grid=(m // N_MEM,),
        in_specs=[
            pl.BlockSpec((N_MEM, D_MODEL), lambda i: (i, 0)),
            pl.BlockSpec((1, D_MODEL), lambda i: (0, 0)),
            pl.BlockSpec((D_MODEL, 2 * C_WIDTH), lambda i: (0, 0)),
        ],
        out_spec=pl.BlockSpec((N_MEM, 2 * C_WIDTH), lambda i: (i, 0)),
        out_shape=jax.ShapeDtypeStruct((m, 2 * C_WIDTH), BF16),
        semantics=("arbitrary",),
        name="memkv",
        operands=[mem2d, g, w],
    )


GRP_A = A_HEADS // A_KV_HEADS
PAIRS_A = GRP_A // 2
COLS_A = GRP_A * TQ_A
KEYS_A = 2 * TQ_A
CHUNK_PAIRS_A = 4


def _swa_kernel(q_ref, kvp_ref, kvc_ref, bkt_ref, rb_ref, sink_ref, o_ref,
                tab_ref, sinkv_ref, kp_ref, vt_ref, s_ref, m_ref, ot_ref, *, present):
    i = pl.program_id(1)

    @pl.when((pl.program_id(0) == 0) & (i == 0))
    def _():
        bkt = bkt_ref[...]
        kchunk = lax.broadcasted_iota(jnp.int32, bkt.shape, 0) // CHUNK
        qchunk = lax.broadcasted_iota(jnp.int32, bkt.shape, 1) // CHUNK
        allowed = (kchunk >= qchunk) & (kchunk <= qchunk + WINDOW_CHUNKS)
        has_prev = lax.broadcasted_iota(jnp.int32, bkt.shape, 0) >= TQ_A
        for g in range(A_KV_HEADS):
            for c0 in range(0, GRP_A, PAIRS_A):
                heads = [g * GRP_A + 2 * (col % PAIRS_A) + col // PAIRS_A for col in range(c0, c0 + PAIRS_A)]
                for n, t in enumerate(_bias_tables(bkt, present, rb_ref, heads)):
                    cols = slice((c0 + n) * TQ_A, (c0 + n + 1) * TQ_A)
                    t = jnp.where(allowed, t, NEG)
                    tab_ref[0, g, :, cols] = t
                    tab_ref[1, g, :, cols] = jnp.where(has_prev, t, NEG)
                    sinkv_ref[g, :, cols] = jnp.full((SUBLANES, TQ_A), sink_ref[heads[n]] * LOG2E, F32)

    width = CHUNK_PAIRS_A * TQ_A
    tasks = [(t, g, p0, parity) for t in range(RB_A // TQ_A) for g in range(A_KV_HEADS)
             for p0 in range(0, PAIRS_A, CHUNK_PAIRS_A) for parity in range(2)]

    def prepare(t, g):
        slot = (t * A_KV_HEADS + g) % 2
        if t == 0:
            kv_t = jnp.concatenate([kvp_ref[...], kvc_ref[0:TQ_A, :]], axis=0)
        else:
            kv_t = kvc_ref[(t - 1) * TQ_A:(t + 1) * TQ_A, :]
        kg = kv_t[:, g * A_HEAD_DIM:(g + 1) * A_HEAD_DIM]
        zeros = jnp.zeros((KEYS_A, A_HEAD_DIM), BF16)
        kp_ref[slot, 0] = jnp.concatenate([kg, zeros], axis=1)
        kp_ref[slot, 1] = jnp.concatenate([zeros, kg], axis=1)
        r = lax.broadcasted_iota(jnp.int32, (A_HEAD_DIM, KV_A), 0)
        c = lax.broadcasted_iota(jnp.int32, (A_HEAD_DIM, KV_A), 1)
        pick = jnp.where(c == g * A_HEAD_DIM + r, 1.0, 0.0).astype(BF16)
        vt = _nt_dot(pick, kv_t[:, KV_A:]).astype(BF16)
        vt_ref[slot] = jnp.concatenate([vt, jnp.ones((PACKED, KEYS_A), BF16)], axis=0)

    def chunk_cols(p0, parity):
        lo = (parity * PAIRS_A + p0) * TQ_A
        return slice(lo, lo + width)

    def logits_stage(n):
        t, g, p0, parity = tasks[n]
        if (p0, parity) == (0, 0):
            prepare(t, g)
        slot, buf, cols = (t * A_KV_HEADS + g) % 2, n % 2, chunk_cols(p0, parity)
        q_pairs = jnp.concatenate(
            [q_ref[t * TQ_A:(t + 1) * TQ_A, (g * PAIRS_A + p) * LANES:(g * PAIRS_A + p + 1) * LANES]
             for p in range(p0, p0 + CHUNK_PAIRS_A)], axis=0)
        variant = jnp.where(i == 0, 1, 0) if t == 0 else 0
        s = _nt_dot(kp_ref[slot, parity], q_pairs) + tab_ref[variant, g, :, cols]
        s_ref[buf] = s
        m = jnp.max(s.reshape(KEYS_A // SUBLANES, SUBLANES, width), axis=0)
        m_ref[buf] = jnp.maximum(jnp.broadcast_to(jnp.max(m, axis=0, keepdims=True), m.shape),
                                 sinkv_ref[g, :, cols])

    def update_stage(n):
        t, g, p0, parity = tasks[n]
        slot, buf, cols = (t * A_KV_HEADS + g) % 2, n % 2, chunk_cols(p0, parity)
        m = m_ref[buf]
        s = s_ref[buf].reshape(KEYS_A // SUBLANES, SUBLANES, width)
        p = jnp.exp2(s - m[None]).reshape(KEYS_A, width).astype(BF16)
        pv = jnp.dot(vt_ref[slot], p, preferred_element_type=F32)
        den = pv[A_HEAD_DIM:A_HEAD_DIM + SUBLANES] + jnp.exp2(sinkv_ref[g, :, cols] - m)
        out_t = pv[:A_HEAD_DIM].reshape(A_HEAD_DIM // SUBLANES, SUBLANES, width) / den[None]
        ot_ref[parity] = out_t.reshape(A_HEAD_DIM, width)
        if parity == 1:
            for k in range(CHUNK_PAIRS_A):
                blk = jnp.concatenate([ot_ref[0, :, k * TQ_A:(k + 1) * TQ_A],
                                       ot_ref[1, :, k * TQ_A:(k + 1) * TQ_A]], axis=0)
                lanes = slice((g * PAIRS_A + p0 + k) * LANES, (g * PAIRS_A + p0 + k + 1) * LANES)
                o_ref[t * TQ_A:(t + 1) * TQ_A, lanes] = blk.T.astype(BF16)

    logits_stage(0)
    for n in range(len(tasks)):
        if n + 1 < len(tasks):
            logits_stage(n + 1)
        update_stage(n)


def _swa(proj, bkt, rel_bias, sinks, batch, seq):
    nt = seq // RB_A
    sub = RB_A // TQ_A
    kv_blk = COL_AKV // (2 * KV_A)
    return _call(
        functools.partial(_swa_kernel, present=tuple(int(b) for b in np.unique(bkt))),
        grid=(batch, nt),
        in_specs=[
            pl.BlockSpec((RB_A, A_WIDTH), lambda b, i: (b * nt + i, COL_AQ // A_WIDTH)),
            pl.BlockSpec((TQ_A, 2 * KV_A), lambda b, i: (jnp.maximum((b * nt + i) * sub - 1, 0), kv_blk)),
            pl.BlockSpec((RB_A, 2 * KV_A), lambda b, i: (b * nt + i, kv_blk)),
            pl.BlockSpec((KEYS_A, TQ_A), lambda b, i: (0, 0)),
            pl.BlockSpec(memory_space=pltpu.SMEM),
            pl.BlockSpec(memory_space=pltpu.SMEM),
        ],
        out_spec=pl.BlockSpec((RB_A, A_WIDTH), lambda b, i: (b * nt + i, 0)),
        out_shape=jax.ShapeDtypeStruct((batch * seq, A_WIDTH), BF16),
        scratch_shapes=[
            pltpu.VMEM((2, A_KV_HEADS, KEYS_A, COLS_A), F32),
            pltpu.VMEM((A_KV_HEADS, SUBLANES, COLS_A), F32),
            pltpu.VMEM((2, 2, KEYS_A, LANES), BF16),
            pltpu.VMEM((2, A_HEAD_DIM + PACKED, KEYS_A), BF16),
            pltpu.VMEM((2, KEYS_A, CHUNK_PAIRS_A * TQ_A), F32),
            pltpu.VMEM((2, SUBLANES, CHUNK_PAIRS_A * TQ_A), F32),
            pltpu.VMEM((2, A_HEAD_DIM, CHUNK_PAIRS_A * TQ_A), F32),
        ],
        semantics=("arbitrary", "arbitrary"),
        name="swa",
        operands=[proj, proj, proj, jnp.asarray(bkt), rel_bias, sinks],
    )


def _dsa_kernel(iq_ref, iwq_ref, ik_ref, q_ref, k_ref, v_ref, bkt_ref, rb_ref, o_ref,
                sc_ref, scb_ref, d16_ref, mb_ref, tab_ref, tri_ref, vt_ref, acc_ref, m_ref,
                sa_ref, smaxa_ref, sb_ref, smaxb_ref, *, topk, seq, present):
    i = pl.program_id(1)
    nkt = i + 1
    int32, int16 = jnp.int32, jnp.int16
    grp = TK_B // SUBLANES

    def hcols(h):
        return slice(h * B_HEAD_DIM, (h + 1) * B_HEAD_DIM)

    def ktile(kt):
        return pl.ds(pl.multiple_of(kt * TK_B, TK_B), TK_B)

    def rows3(a):
        return a.reshape(a.shape[0] // SUBLANES, SUBLANES, TQ_B)

    def all_rows(a, op):
        return jnp.broadcast_to(op(a, axis=0, keepdims=True), a.shape)

    @pl.when((pl.program_id(0) == 0) & (i == 0))
    def _():
        far_bucket = N_BUCKETS // 2 - 1
        for d in range(NEAR_B):
            for h0 in range(0, B_HEADS, 2):
                tabs = _bias_tables(bkt_ref[d], present[d], rb_ref, [A_HEADS + h0, A_HEADS + h0 + 1],
                                    sub_row=far_bucket)
                tab_ref[d, h0] = tabs[0]
                tab_ref[d, h0 + 1] = tabs[1]
        tab_ref[NEAR_B] = jnp.zeros(tab_ref.shape[1:], F32)
        r = lax.broadcasted_iota(int32, (TK_B, TK_B), 0)
        c = lax.broadcasted_iota(int32, (TK_B, TK_B), 1)
        tri_ref[...] = jnp.where(c < r, 1.0, 0.0).astype(BF16)

    @pl.when(i == 0)
    def _():
        r = lax.broadcasted_iota(int32, (B_HEAD_DIM, B_HEAD_DIM), 0)
        c = lax.broadcasted_iota(int32, (B_HEAD_DIM, B_HEAD_DIM), 1)
        eye = jnp.where(r == c, 1.0, 0.0).astype(BF16)

        def body(kt, carry):
            for h in range(B_HEADS):
                vt_ref[kt, h] = _nt_dot(eye, v_ref[ktile(kt), hcols(h)]).astype(BF16)
            return carry

        lax.fori_loop(0, seq // TK_B, body, 0)

    key_chunk = lax.broadcasted_iota(int32, (TK_B, TQ_B), 0) // CHUNK
    qry_chunk = lax.broadcasted_iota(int32, (TK_B, TQ_B), 1) // CHUNK
    adm_diag = key_chunk <= qry_chunk

    r = lax.broadcasted_iota(int32, (IDX_HEADS * SUBLANES, LANES), 0)
    c = lax.broadcasted_iota(int32, (IDX_HEADS * SUBLANES, LANES), 1)
    pick_w = jnp.where(c == IDX_DIM + r // SUBLANES, 1.0, 0.0).astype(BF16)
    w_all = _nt_dot(pick_w, iwq_ref[...]) * (IDX_HEADS ** -0.5 * IDX_DIM ** -0.5)

    def for_tiles(n, body, unroll=2):
        def group(j, carry):
            for k in range(unroll):
                body(unroll * j + k)
            return carry

        def single(kt, carry):
            body(kt)
            return carry

        lax.fori_loop(0, n // unroll, group, 0)
        lax.fori_loop(n - n % unroll, n, single, 0)

    def score_tile(kt):
        ikt = ik_ref[ktile(kt), 0:IDX_DIM]
        sc = jnp.zeros((grp, SUBLANES, TQ_B), F32)
        for h in range(IDX_HEADS):
            x = _nt_dot(ikt, iq_ref[:, h * IDX_DIM:(h + 1) * IDX_DIM])
            sc = sc + w_all[h * SUBLANES:(h + 1) * SUBLANES][None] * jnp.maximum(rows3(x), 0.0)
        store_score(kt, sc.reshape(TK_B, TQ_B))

    def store_score(kt, sc):
        sc_ref[kt] = sc
        scb_ref[kt] = sc.astype(BF16)

    for_tiles(nkt, score_tile, unroll=4)
    store_score(i, jnp.where(adm_diag, sc_ref[i], -jnp.inf))

    def key_to_f32(key):
        return pltpu.bitcast(jnp.where(key >= 0, key, INT_MIN - key), F32)

    def over_tiles(tile_count, zero):
        def pair(j, acc):
            return acc + (tile_count(2 * j) + tile_count(2 * j + 1))
        acc = lax.fori_loop(0, nkt // 2, pair, zero)
        return lax.cond(nkt % 2 == 1, lambda a: a + tile_count(nkt - 1), lambda a: a, acc)

    def tree_sum(parts):
        while len(parts) > 1:
            parts = [parts[n] + parts[n + 1] for n in range(0, len(parts), 2)]
        return parts[0]

    def count_bf16(cand):
        def tile_count(kt):
            blk = scb_ref[kt].reshape(TK_B // PACKED, PACKED, TQ_B)
            return tree_sum([jnp.where(blk[g] >= cand, jnp.ones((), int16), jnp.zeros((), int16))
                             for g in range(TK_B // PACKED)])
        acc = over_tiles(tile_count, jnp.zeros((PACKED, TQ_B), int16))
        return all_rows(acc.astype(int32), jnp.sum)

    def count_f32(pred):
        def tile_count(kt):
            blk = rows3(sc_ref[kt])
            return tree_sum([jnp.where(pred(blk[g]), 1, 0) for g in range(grp)])
        return all_rows(over_tiles(tile_count, jnp.zeros((SUBLANES, TQ_B), int32)), jnp.sum)

    def coarse_pass(p, u):
        bit = jnp.left_shift(jnp.int32(1), 15 - p)
        cand = key_to_f32(((u | bit) - HALF) << 16).astype(BF16)
        return jnp.where(count_bf16(cand) >= topk, u | bit, u)

    u = lax.fori_loop(0, 16, coarse_pass, jnp.zeros((PACKED, TQ_B), int32))
    coarse_key = ((u - HALF) << 16)[:SUBLANES]

    lo = jnp.maximum(coarse_key - HALF, KEY_NEG_INF)
    hi = jnp.minimum(coarse_key, KEY_POS_INF - 2 * HALF) + 2 * HALF

    def fine_pass(p, lohi):
        lo, hi = lohi
        mid = lo + ((hi - lo) >> 1)
        cand = key_to_f32(mid)
        ok = count_f32(lambda blk: blk >= cand) >= topk
        return jnp.where(ok, mid, lo), jnp.where(ok, hi, mid)

    lo, hi = lax.fori_loop(0, PRE_PASSES, fine_pass, (lo, hi))

    mag = jnp.minimum(jnp.abs(lo), jnp.abs(hi))
    around_zero = ((lo <= 0) & (hi >= 0)) | (mag >> F32_MANT_BITS == 0)
    exp_field = jnp.where(around_zero, F32_MANT_BITS + 1, mag >> F32_MANT_BITS)
    any_irregular = jnp.max(jnp.where(exp_field <= F32_MANT_BITS, 1, 0)) > 0

    def finish_f32(lohi):
        lo, hi = lax.fori_loop(PRE_PASSES, FINE_PASSES, fine_pass, lohi)
        tau_f = key_to_f32(lo)
        return tau_f, count_f32(lambda blk: blk > tau_f)

    def finish_int16(lohi):
        lo, hi = lohi
        centre = jnp.where(around_zero, 0.0, key_to_f32(lo + ((hi - lo) >> 1)))
        ulp = pltpu.bitcast((exp_field - F32_MANT_BITS) << F32_MANT_BITS, F32)
        inv_ulp = pltpu.bitcast((2 * F32_EXP_BIAS + F32_MANT_BITS - exp_field) << F32_MANT_BITS, F32)

        def build(kt):
            d = (rows3(sc_ref[kt]) - centre[None]) * inv_ulp[None]
            d = jnp.clip(d, -HALF, HALF - 1).reshape(TK_B, TQ_B)
            d16_ref[kt] = d.astype(int32).astype(int16)

        for_tiles(nkt, build)

        def count_d16(cand):
            def tile_count(kt):
                blk = d16_ref[kt].reshape(TK_B // PACKED, PACKED, TQ_B)
                return tree_sum([jnp.where(blk[g] >= cand, jnp.ones((), int16), jnp.zeros((), int16))
                                 for g in range(TK_B // PACKED)])
            acc = over_tiles(tile_count, jnp.zeros((PACKED, TQ_B), int16))
            return all_rows(acc.astype(int32), jnp.sum)

        def d_pass(p, u):
            bit = jnp.left_shift(jnp.int32(1), 15 - p)
            cand = ((u | bit) - HALF).astype(int16)
            return jnp.where(count_d16(cand) >= topk, u | bit, u)

        t = lax.fori_loop(0, 16, d_pass, jnp.zeros((PACKED, TQ_B), int32)) - HALF
        above = count_d16((t + 1).astype(int16))
        return centre + t[:SUBLANES].astype(F32) * ulp, above[:SUBLANES]

    tau2d, above = lax.cond(any_irregular, finish_f32, finish_int16, (lo, hi))
    tau = tau2d[None]

    need = (topk - above).astype(F32)[None]
    ones_l = jnp.ones((2 * SUBLANES, TK_B), BF16)

    def mask_tile(kt, run):
        blk = rows3(sc_ref[kt])
        eq = blk == tau
        eqf = jnp.where(eq, 1.0, 0.0).reshape(TK_B, TQ_B).astype(BF16)
        rank = rows3(jnp.dot(tri_ref[...], eqf, preferred_element_type=F32)) + run[None]
        sel = (blk > tau) | (eq & (rank < need))
        mb_ref[kt] = jnp.where(sel, 0.0, NEG).reshape(TK_B, TQ_B)
        return run + jnp.dot(ones_l, eqf, preferred_element_type=F32)[:SUBLANES]

    def mask_group(j, run):
        for k in range(4):
            run = mask_tile(4 * j + k, run)
        return run

    run = lax.fori_loop(0, nkt // 4, mask_group, jnp.zeros((SUBLANES, TQ_B), F32))
    lax.fori_loop(nkt - nkt % 4, nkt, mask_tile, run)

    mb_ref[i] = jnp.where(adm_diag, mb_ref[i], NEG)

    m_ref[...] = jnp.full(m_ref.shape, NEG, F32)
    acc_ref[...] = jnp.zeros(acc_ref.shape, F32)
    ones_rows = jnp.ones((PACKED, TK_B), BF16)

    buf_a, buf_b = (sa_ref, smaxa_ref), (sb_ref, smaxb_ref)

    def logits_stage(kt, buf, near):
        s_ref, smax_ref = buf
        mb = mb_ref[kt]
        for h in range(B_HEADS):
            s = _nt_dot(k_ref[ktile(kt), hcols(h)], q_ref[:, hcols(h)]) + mb
            if near:
                s = s + tab_ref[jnp.minimum(i - kt, NEAR_B), h]
            s_ref[h] = s
            smax_ref[h] = all_rows(jnp.max(rows3(s), axis=0), jnp.max)

    def update_stage(kt, buf):
        s_ref, smax_ref = buf
        for h in range(B_HEADS):
            m_old = m_ref[h]
            m_new = jnp.maximum(m_old, smax_ref[h])
            alpha = jnp.exp2(m_old - m_new)
            p = jnp.exp2(rows3(s_ref[h]) - m_new[None]).reshape(TK_B, TQ_B).astype(BF16)
            vaug = jnp.concatenate([vt_ref[kt, h], ones_rows], axis=0)
            pv = jnp.dot(vaug, p, preferred_element_type=F32)
            acc_ref[h] = (rows3(acc_ref[h]) * alpha[None] + rows3(pv)).reshape(acc_ref.shape[1:])
            m_ref[h] = m_new

    def pair_body(j, carry, near):
        logits_stage(2 * j + 1, buf_b, near)
        update_stage(2 * j, buf_a)
        logits_stage(2 * j + 2, buf_a, near)
        update_stage(2 * j + 1, buf_b)
        return carry

    near_lo = jnp.maximum(i - (NEAR_B - 1), 0)
    far_pairs = jnp.maximum(near_lo - 1, 0) // 2
    logits_stage(0, buf_a, True)
    lax.fori_loop(0, far_pairs, functools.partial(pair_body, near=False), 0)
    lax.fori_loop(far_pairs, (nkt - 1) // 2, functools.partial(pair_body, near=True), 0)

    @pl.when(nkt % 2 == 0)
    def _():
        logits_stage(nkt - 1, buf_b, True)
        update_stage(nkt - 2, buf_a)
        update_stage(nkt - 1, buf_b)

    @pl.when(nkt % 2 == 1)
    def _():
        update_stage(nkt - 1, buf_a)

    for h in range(B_HEADS):
        num = rows3(acc_ref[h, 0:B_HEAD_DIM, :])
        den = acc_ref[h, B_HEAD_DIM:B_HEAD_DIM + SUBLANES, :]
        out_t = (num / den[None]).reshape(B_HEAD_DIM, TQ_B)
        o_ref[:, h * B_HEAD_DIM:(h + 1) * B_HEAD_DIM] = out_t.T.astype(BF16)


def _dsa(proj, bkt, rel_bias, batch, seq):
    nt = seq // TQ_B
    topk = min(TOPK_MAX, seq // 4)

    def q_spec(width, col):
        return pl.BlockSpec((TQ_B, width), lambda b, i: (b * nt + i, col // width))

    def seq_spec(width, col):
        return pl.BlockSpec((seq, width), lambda b, i: (b, col // width))

    return _call(
        functools.partial(_dsa_kernel, topk=topk, seq=seq,
                          present=tuple(tuple(int(b) for b in np.unique(t)) for t in bkt)),
        grid=(batch, nt),
        in_specs=[
            q_spec(IDX_HEADS * IDX_DIM, COL_IQ),
            q_spec(LANES, COL_IKW),
            seq_spec(LANES, COL_IKW),
            q_spec(B_WIDTH, COL_BQ), seq_spec(B_WIDTH, COL_BK), seq_spec(B_WIDTH, COL_BV),
            pl.BlockSpec((NEAR_B, TK_B, TQ_B), lambda b, i: (0, 0, 0)),
            pl.BlockSpec(memory_space=pltpu.SMEM),
        ],
        out_spec=pl.BlockSpec((TQ_B, B_WIDTH), lambda b, i: (b * nt + i, 0)),
        out_shape=jax.ShapeDtypeStruct((batch * seq, B_WIDTH), BF16),
        scratch_shapes=[
            pltpu.VMEM((nt, TK_B, TQ_B), F32),
            pltpu.VMEM((nt, TK_B, TQ_B), BF16),
            pltpu.VMEM((nt, TK_B, TQ_B), jnp.int16),
            pltpu.VMEM((nt, TK_B, TQ_B), F32),
            pltpu.VMEM((NEAR_B + 1, B_HEADS, TK_B, TQ_B), F32),
            pltpu.VMEM((TK_B, TK_B), BF16),
            pltpu.VMEM((seq // TK_B, B_HEADS, B_HEAD_DIM, TK_B), BF16),
            pltpu.VMEM((B_HEADS, B_HEAD_DIM + PACKED, TQ_B), F32),
            pltpu.VMEM((B_HEADS, SUBLANES, TQ_B), F32),
            pltpu.VMEM((B_HEADS, TK_B, TQ_B), F32),
            pltpu.VMEM((B_HEADS, SUBLANES, TQ_B), F32),
            pltpu.VMEM((B_HEADS, TK_B, TQ_B), F32),
            pltpu.VMEM((B_HEADS, SUBLANES, TQ_B), F32),
        ],
        semantics=("arbitrary", "arbitrary"),
        name="dsa",
        operands=[proj, proj, proj, proj, proj, proj, jnp.asarray(bkt), rel_bias],
    )


def _outproj_kernel(oa_ref, ob_ref, cq_ref, mkv_ref, gate_ref, x_ref, w32_ref, g_ref, o_ref, w_ref):
    @pl.when(pl.program_id(0) == 0)
    def _():
        w_ref[...] = w32_ref[...].astype(BF16)

    ones_m = jnp.ones((N_MEM, C_HEAD_DIM), BF16)
    for r in range(TM_OUT // SUB_OUT):
        rows = slice(r * SUB_OUT, (r + 1) * SUB_OUT)
        att = [oa_ref[rows, :].astype(F32), ob_ref[rows, :].astype(F32)]
        for h in range(C_HEADS):
            hs = slice(h * C_HEAD_DIM, (h + 1) * C_HEAD_DIM)
            s = _nt_dot(cq_ref[rows, hs], mkv_ref[:, hs])
            p = jnp.exp2(s - jnp.max(s, axis=1, keepdims=True)).astype(BF16)
            vaug = jnp.concatenate([mkv_ref[:, C_WIDTH + h * C_HEAD_DIM:C_WIDTH + (h + 1) * C_HEAD_DIM], ones_m],
                                   axis=1)
            pv = jnp.dot(p, vaug, preferred_element_type=F32)
            att.append(pv[:, :C_HEAD_DIM] / pv[:, C_HEAD_DIM:])
        y = (jnp.concatenate(att, axis=1) * gate_ref[rows, :].astype(F32)).astype(BF16)
        h_new = x_ref[rows, :] + jnp.dot(y, w_ref[...], preferred_element_type=F32)
        ms = jnp.mean(h_new * h_new, axis=-1, keepdims=True)
        o_ref[rows, :] = h_new * lax.rsqrt(ms + EPS) * g_ref[...]


def _outproj(oa, ob, proj, mkv, x2d, w, g, seq):
    m = x2d.shape[0]
    steps_per_seq = seq // TM_OUT
    return _call(
        _outproj_kernel,
        grid=(m // TM_OUT,),
        in_specs=[
            pl.BlockSpec((TM_OUT, A_WIDTH), lambda i: (i, 0)),
            pl.BlockSpec((TM_OUT, B_WIDTH), lambda i: (i, 0)),
            pl.BlockSpec((TM_OUT, C_WIDTH), lambda i: (i, COL_CQ // C_WIDTH)),
            pl.BlockSpec((N_MEM, 2 * C_WIDTH), lambda i: (i // steps_per_seq, 0)),
            pl.BlockSpec((TM_OUT, MIX_WIDTH), lambda i: (i, COL_GATE // MIX_WIDTH)),
            pl.BlockSpec((TM_OUT, D_MODEL), lambda i: (i, 0)),
            pl.BlockSpec((MIX_WIDTH, D_MODEL), lambda i: (0, 0), pipeline_mode=pl.Buffered(1)),
            pl.BlockSpec((1, D_MODEL), lambda i: (0, 0)),
        ],
        out_spec=pl.BlockSpec((TM_OUT, D_MODEL), lambda i: (i, 0)),
        out_shape=jax.ShapeDtypeStruct((m, D_MODEL), F32),
        scratch_shapes=[pltpu.VMEM((MIX_WIDTH, D_MODEL), BF16)],
        semantics=("arbitrary",),
        name="outproj",
        operands=[oa, ob, proj, mkv, proj, x2d, w, g],
    )


def _col_scale():
    cs = np.ones((1, PROJ_W), np.float32)
    cs[0, COL_AQ:COL_AQ + A_WIDTH] = A_HEAD_DIM ** -0.5 * LOG2E
    cs[0, COL_BQ:COL_BQ + B_WIDTH] = B_HEAD_DIM ** -0.5 * LOG2E
    cs[0, COL_CQ:COL_CQ + C_WIDTH] = C_HEAD_DIM ** -0.5 * LOG2E
    return jnp.asarray(cs)


def kernel(x, mem, g_norm, w_in, sinks, rel_bias, g_mem, w_mem_kv, w_out, g_final):
    batch, seq, _ = x.shape
    assert w_in.shape[0] == 1, "the out-projection kernel fuses the final norm of a single-layer trunk"
    kk = np.arange(2 * TQ_A)[:, None]
    qq = np.arange(TQ_A)[None, :]
    bkt_a = _t5_bucket_np(kk - TQ_A - qq)
    kk = np.arange(TK_B)[None, :, None]
    qq = np.arange(TQ_B)[None, None, :]
    d = np.arange(NEAR_B)[:, None, None]
    bkt_b = _t5_bucket_np(kk - qq - TK_B * d)

    h = x.reshape(batch * seq, D_MODEL)
    mem2d = mem.reshape(batch * N_MEM, D_MODEL)
    w_t = jnp.transpose(w_in, (2, 0, 1)).reshape(IN_WIDTH * KB_IN, LANES)
    proj = _inproj(h, g_norm[0].reshape(1, D_MODEL), _wprep(w_t), _col_scale())
    mkv = _memkv(mem2d, g_mem[0].reshape(1, D_MODEL), w_mem_kv[0])
    oa = _swa(proj, bkt_a, rel_bias, sinks[0], batch, seq)
    ob = _dsa(proj, bkt_b, rel_bias, batch, seq)
    out = _outproj(oa, ob, proj, mkv, h, w_out[0], g_final.reshape(1, D_MODEL), seq)
    return out.reshape(batch, seq, D_MODEL)
```

```python
import functools
import math

import numpy as np
import jax
import jax.numpy as jnp
from jax import lax
from jax.experimental import pallas as pl
from jax.experimental.pallas import tpu as pltpu

D_MODEL = 2048
CHUNK = 64
N_MEM = 256
EPS = 1e-6
A_HEADS = 16
A_KV_HEADS = 2
A_HEAD_DIM = 64
WINDOW_CHUNKS = 2
A_WIDTH = A_HEADS * A_HEAD_DIM
B_HEADS = 4
B_HEAD_DIM = 128
B_WIDTH = B_HEADS * B_HEAD_DIM
IDX_HEADS = 4
IDX_DIM = 64
TOPK_MAX = 256
C_HEADS = 4
C_HEAD_DIM = 128
C_WIDTH = C_HEADS * C_HEAD_DIM
MIX_WIDTH = A_WIDTH + B_WIDTH + C_WIDTH
N_BUCKETS = 32
MAX_DISTANCE = 1024
KV_A = A_KV_HEADS * A_HEAD_DIM
SPLIT_SIZES = (A_WIDTH, KV_A, KV_A, B_WIDTH, B_WIDTH, B_WIDTH,
               IDX_HEADS * IDX_DIM, IDX_DIM, IDX_HEADS, C_WIDTH, MIX_WIDTH)
IN_WIDTH = sum(SPLIT_SIZES)

F32 = jnp.float32
BF16 = jnp.bfloat16
LOG2E = math.log2(math.e)
NEG = -1e30
INT_MIN = -(2 ** 31)
LANES = 128
SUBLANES = 8
PACKED = 16
HALF = 1 << 15
KEY_POS_INF = 0x7F800000
KEY_NEG_INF = -KEY_POS_INF
FINE_PASSES = 17
PRE_PASSES = 2
F32_MANT_BITS = 23
F32_EXP_BIAS = 127

(SRC_AQ, SRC_AK, SRC_AV, SRC_BQ, SRC_BK, SRC_BV,
 SRC_IQ, SRC_IK, SRC_IW, SRC_CQ, SRC_GATE) = (int(c) for c in np.cumsum((0,) + SPLIT_SIZES)[:-1])
COL_GATE = 0
COL_AQ = COL_GATE + MIX_WIDTH
COL_BQ = COL_AQ + A_WIDTH
COL_BK = COL_BQ + B_WIDTH
COL_BV = COL_BK + B_WIDTH
COL_CQ = COL_BV + B_WIDTH
COL_AKV = COL_CQ + C_WIDTH
COL_IQ = COL_AKV + 2 * KV_A
COL_IKW = COL_IQ + IDX_HEADS * IDX_DIM
PROJ_W = 6144
SEGMENTS = ((COL_GATE, SRC_GATE, MIX_WIDTH), (COL_AQ, SRC_AQ, A_WIDTH), (COL_BQ, SRC_BQ, B_WIDTH),
            (COL_BK, SRC_BK, B_WIDTH), (COL_BV, SRC_BV, B_WIDTH), (COL_CQ, SRC_CQ, C_WIDTH),
            (COL_AKV, SRC_AK, 2 * KV_A), (COL_IQ, SRC_IQ, IDX_HEADS * IDX_DIM), (COL_IKW, SRC_IK, 256))
KB_IN = D_MODEL // LANES

TC_PREP = 256
TM_IN, TN_IN = 1024, 2048
SUB_IN = 256
TQ_A = 128
RB_A = 1024
TQ_B = 256
TK_B = 256
NEAR_B = 4
TM_OUT = 512
SUB_OUT = 256
V7X_VMEM_BYTES = 64 * 1024 * 1024
VMEM_REQUEST = V7X_VMEM_BYTES * 7 // 8


def _vmem_limit(in_specs, operands, out_spec, out_shape, scratch_shapes=()):
    def buffers(spec, dtype):
        if spec.block_shape is None:
            return 0
        elems = math.prod(int(getattr(d, "block_size", d)) for d in spec.block_shape)
        count = spec.pipeline_mode.buffer_count if spec.pipeline_mode is not None else 2
        return elems * jnp.dtype(dtype).itemsize * count

    total = sum(buffers(s, a.dtype) for s, a in zip(in_specs, operands)) + buffers(out_spec, out_shape.dtype)
    total += sum(math.prod(s.shape) * jnp.dtype(s.dtype).itemsize for s in scratch_shapes)
    assert total <= VMEM_REQUEST, (total, VMEM_REQUEST)
    return VMEM_REQUEST


def _call(kernel, *, name, grid, semantics, in_specs, operands, out_spec, out_shape, scratch_shapes=()):
    return pl.pallas_call(
        kernel,
        grid=grid,
        in_specs=in_specs,
        out_specs=out_spec,
        out_shape=out_shape,
        scratch_shapes=list(scratch_shapes),
        compiler_params=pltpu.CompilerParams(
            dimension_semantics=semantics,
            vmem_limit_bytes=_vmem_limit(in_specs, operands, out_spec, out_shape, scratch_shapes)),
        name=name,
    )(*operands)


def _t5_bucket_np(rel):
    nb = N_BUCKETS // 2
    max_exact = nb // 2
    side = np.where(rel > 0, nb, 0)
    n = np.abs(rel)
    nf = np.maximum(n, max_exact).astype(np.float32)
    large = max_exact + (np.log(nf / max_exact) / math.log(MAX_DISTANCE / max_exact)
                         * (nb - max_exact)).astype(np.int32)
    large = np.minimum(large, nb - 1)
    return (side + np.where(n < max_exact, n, large)).astype(np.int32)


def _nt_dot(a, b):
    return lax.dot_general(a, b, (((1,), (1,)), ((), ())), preferred_element_type=F32)


def _bias_tables(bucket, present, rb_ref, cols, sub_row=None):
    accs = [jnp.zeros(bucket.shape, F32) for _ in cols]
    for b in present:
        hit = bucket == b
        for n, col in enumerate(cols):
            val = rb_ref[b, col]
            if sub_row is not None:
                val = val - rb_ref[sub_row, col]
            accs[n] = jnp.where(hit, val * LOG2E, accs[n])
    return accs


def _wprep_kernel(src_ref, w_ref, o_ref):
    t = pl.program_id(0)

    @pl.when(src_ref[t] >= 0)
    def _():
        for kb in range(KB_IN):
            o_ref[:, kb * LANES:(kb + 1) * LANES] = w_ref[pl.ds(kb, TC_PREP, stride=KB_IN), :].astype(BF16)

    @pl.when(src_ref[t] < 0)
    def _():
        o_ref[...] = jnp.zeros(o_ref.shape, BF16)


def _wprep(w_t):
    src = np.full((PROJ_W // TC_PREP,), -1, np.int32)
    for dst_col, src_col, width in SEGMENTS:
        for n in range(width // TC_PREP):
            src[dst_col // TC_PREP + n] = src_col + n * TC_PREP
    in_specs = [pl.BlockSpec((pl.Element(TC_PREP * KB_IN), pl.Element(LANES)),
                             lambda t, src: (jnp.maximum(src[t], 0) * KB_IN, 0))]
    out_spec = pl.BlockSpec((TC_PREP, D_MODEL), lambda t, src: (t, 0))
    out_shape = jax.ShapeDtypeStruct((PROJ_W, D_MODEL), BF16)
    return pl.pallas_call(
        _wprep_kernel,
        grid_spec=pltpu.PrefetchScalarGridSpec(
            num_scalar_prefetch=1, grid=(PROJ_W // TC_PREP,), in_specs=in_specs, out_specs=out_spec),
        out_shape=out_shape,
        compiler_params=pltpu.CompilerParams(
            dimension_semantics=("arbitrary",),
            vmem_limit_bytes=_vmem_limit(in_specs, [w_t], out_spec, out_shape)),
        name="wprep",
    )(jnp.asarray(src), w_t)


def _inproj_kernel(x_ref, g_ref, wt_ref, cs_ref, o_ref, hn_ref):
    assert COL_GATE == 0 and MIX_WIDTH == TN_IN
    j = pl.program_id(1)

    @pl.when(j == 0)
    def _():
        for r in range(TM_IN // SUB_IN):
            rows = slice(r * SUB_IN, (r + 1) * SUB_IN)
            x = x_ref[rows, :]
            ms = jnp.mean(x * x, axis=-1, keepdims=True)
            hn = (x * lax.rsqrt(ms + EPS) * g_ref[...]).astype(BF16)
            hn_ref[rows, :] = hn
            gate = _nt_dot(hn, wt_ref[...])
            o_ref[rows, :] = (gate / (1.0 + jnp.exp(-gate))).astype(BF16)

    @pl.when(j > 0)
    def _():
        o_ref[...] = (_nt_dot(hn_ref[...], wt_ref[...]) * cs_ref[...]).astype(BF16)


def _inproj(x2d, g, wt, cs):
    m = x2d.shape[0]
    return _call(
        _inproj_kernel,
        grid=(m // TM_IN, PROJ_W // TN_IN),
        in_specs=[
            pl.BlockSpec((TM_IN, D_MODEL), lambda i, j: (i, 0)),
            pl.BlockSpec((1, D_MODEL), lambda i, j: (0, 0)),
            pl.BlockSpec((TN_IN, D_MODEL), lambda i, j: (j, 0)),
            pl.BlockSpec((1, TN_IN), lambda i, j: (0, j)),
        ],
        out_spec=pl.BlockSpec((TM_IN, TN_IN), lambda i, j: (i, j)),
        out_shape=jax.ShapeDtypeStruct((m, PROJ_W), BF16),
        scratch_shapes=[pltpu.VMEM((TM_IN, D_MODEL), BF16)],
        semantics=("parallel", "arbitrary"),
        name="inproj",
        operands=[x2d, g, wt, cs],
    )


def _memkv_kernel(m_ref, g_ref, w_ref, o_ref):
    x = m_ref[...]
    ms = jnp.mean(x * x, axis=-1, keepdims=True)
    hn = (x * lax.rsqrt(ms + EPS) * g_ref[...]).astype(BF16)
    o_ref[...] = jnp.dot(hn, w_ref[...].astype(BF16), preferred_element_type=F32).astype(BF16)


def _memkv(mem2d, g, w):
    m = mem2d.shape[0]
    return _call(
        _memkv_kernel,
        grid=(m // N_MEM,),
        in_specs=[
            pl.BlockSpec((N_MEM, D_MODEL), lambda i: (i, 0)),
            pl.BlockSpec((1, D_MODEL), lambda i: (0, 0)),
            pl.BlockSpec((D_MODEL, 2 * C_WIDTH), lambda i: (0, 0)),
        ],
        out_spec=pl.BlockSpec((N_MEM, 2 * C_WIDTH), lambda i: (i, 0)),
        out_shape=jax.ShapeDtypeStruct((m, 2 * C_WIDTH), BF16),
        semantics=("arbitrary",),
        name="memkv",
        operands=[mem2d, g, w],
    )


GRP_A = A_HEADS // A_KV_HEADS
PAIRS_A = GRP_A // 2
COLS_A = GRP_A * TQ_A
KEYS_A = 2 * TQ_A
CHUNK_PAIRS_A = 4


def _swa_kernel(q_ref, kvp_ref, kvc_ref, bkt_ref, rb_ref, sink_ref, o_ref,
                tab_ref, sinkv_ref, kp_ref, vt_ref, s_ref, m_ref, ot_ref, *, present):
    i = pl.program_id(1)

    @pl.when((pl.program_id(0) == 0) & (i == 0))
    def _():
        bkt = bkt_ref[...]
        kchunk = lax.broadcasted_iota(jnp.int32, bkt.shape, 0) // CHUNK
        qchunk = lax.broadcasted_iota(jnp.int32, bkt.shape, 1) // CHUNK
        allowed = (kchunk >= qchunk) & (kchunk <= qchunk + WINDOW_CHUNKS)
        has_prev = lax.broadcasted_iota(jnp.int32, bkt.shape, 0) >= TQ_A
        for g in range(A_KV_HEADS):
            for c0 in range(0, GRP_A, PAIRS_A):
                heads = [g * GRP_A + 2 * (col % PAIRS_A) + col // PAIRS_A for col in range(c0, c0 + PAIRS_A)]
                for n, t in enumerate(_bias_tables(bkt, present, rb_ref, heads)):
                    cols = slice((c0 + n) * TQ_A, (c0 + n + 1) * TQ_A)
                    t = jnp.where(allowed, t, NEG)
                    tab_ref[0, g, :, cols] = t
                    tab_ref[1, g, :, cols] = jnp.where(has_prev, t, NEG)
                    sinkv_ref[g, :, cols] = jnp.full((SUBLANES, TQ_A), sink_ref[heads[n]] * LOG2E, F32)

    width = CHUNK_PAIRS_A * TQ_A
    tasks = [(t, g, p0, parity) for t in range(RB_A // TQ_A) for g in range(A_KV_HEADS)
             for p0 in range(0, PAIRS_A, CHUNK_PAIRS_A) for parity in range(2)]

    def prepare(t, g):
        slot = (t * A_KV_HEADS + g) % 2
        if t == 0:
            kv_t = jnp.concatenate([kvp_ref[...], kvc_ref[0:TQ_A, :]], axis=0)
        else:
            kv_t = kvc_ref[(t - 1) * TQ_A:(t + 1) * TQ_A, :]
        kg = kv_t[:, g * A_HEAD_DIM:(g + 1) * A_HEAD_DIM]
        zeros = jnp.zeros((KEYS_A, A_HEAD_DIM), BF16)
        kp_ref[slot, 0] = jnp.concatenate([kg, zeros], axis=1)
        kp_ref[slot, 1] = jnp.concatenate([zeros, kg], axis=1)
        r = lax.broadcasted_iota(jnp.int32, (A_HEAD_DIM, KV_A), 0)
        c = lax.broadcasted_iota(jnp.int32, (A_HEAD_DIM, KV_A), 1)
        pick = jnp.where(c == g * A_HEAD_DIM + r, 1.0, 0.0).astype(BF16)
        vt = _nt_dot(pick, kv_t[:, KV_A:]).astype(BF16)
        vt_ref[slot] = jnp.concatenate([vt, jnp.ones((PACKED, KEYS_A), BF16)], axis=0)

    def chunk_cols(p0, parity):
        lo = (parity * PAIRS_A + p0) * TQ_A
        return slice(lo, lo + width)

    def logits_stage(n):
        t, g, p0, parity = tasks[n]
        if (p0, parity) == (0, 0):
            prepare(t, g)
        slot, buf, cols = (t * A_KV_HEADS + g) % 2, n % 2, chunk_cols(p0, parity)
        q_pairs = jnp.concatenate(
            [q_ref[t * TQ_A:(t + 1) * TQ_A, (g * PAIRS_A + p) * LANES:(g * PAIRS_A + p + 1) * LANES]
             for p in range(p0, p0 + CHUNK_PAIRS_A)], axis=0)
        variant = jnp.where(i == 0, 1, 0) if t == 0 else 0
        s = _nt_dot(kp_ref[slot, parity], q_pairs) + tab_ref[variant, g, :, cols]
        s_ref[buf] = s
        m = jnp.max(s.reshape(KEYS_A // SUBLANES, SUBLANES, width), axis=0)
        m_ref[buf] = jnp.maximum(jnp.broadcast_to(jnp.max(m, axis=0, keepdims=True), m.shape),
                                 sinkv_ref[g, :, cols])

    def update_stage(n):
        t, g, p0, parity = tasks[n]
        slot, buf, cols = (t * A_KV_HEADS + g) % 2, n % 2, chunk_cols(p0, parity)
        m = m_ref[buf]
        s = s_ref[buf].reshape(KEYS_A // SUBLANES, SUBLANES, width)
        p = jnp.exp2(s - m[None]).reshape(KEYS_A, width).astype(BF16)
        pv = jnp.dot(vt_ref[slot], p, preferred_element_type=F32)
        den = pv[A_HEAD_DIM:A_HEAD_DIM + SUBLANES] + jnp.exp2(sinkv_ref[g, :, cols] - m)
        out_t = pv[:A_HEAD_DIM].reshape(A_HEAD_DIM // SUBLANES, SUBLANES, width) / den[None]
        ot_ref[parity] = out_t.reshape(A_HEAD_DIM, width)
        if parity == 1:
            for k in range(CHUNK_PAIRS_A):
                blk = jnp.concatenate([ot_ref[0, :, k * TQ_A:(k + 1) * TQ_A],
                                       ot_ref[1, :, k * TQ_A:(k + 1) * TQ_A]], axis=0)
                lanes = slice((g * PAIRS_A + p0 + k) * LANES, (g * PAIRS_A + p0 + k + 1) * LANES)
                o_ref[t * TQ_A:(t + 1) * TQ_A, lanes] = blk.T.astype(BF16)

    logits_stage(0)
    for n in range(len(tasks)):
        if n + 1 < len(tasks):
            logits_stage(n + 1)
        update_stage(n)


def _swa(proj, bkt, rel_bias, sinks, batch, seq):
    nt = seq // RB_A
    sub = RB_A // TQ_A
    kv_blk = COL_AKV // (2 * KV_A)
    return _call(
        functools.partial(_swa_kernel, present=tuple(int(b) for b in np.unique(bkt))),
        grid=(batch, nt),
        in_specs=[
            pl.BlockSpec((RB_A, A_WIDTH), lambda b, i: (b * nt + i, COL_AQ // A_WIDTH)),
            pl.BlockSpec((TQ_A, 2 * KV_A), lambda b, i: (jnp.maximum((b * nt + i) * sub - 1, 0), kv_blk)),
            pl.BlockSpec((RB_A, 2 * KV_A), lambda b, i: (b * nt + i, kv_blk)),
            pl.BlockSpec((KEYS_A, TQ_A), lambda b, i: (0, 0)),
            pl.BlockSpec(memory_space=pltpu.SMEM),
            pl.BlockSpec(memory_space=pltpu.SMEM),
        ],
        out_spec=pl.BlockSpec((RB_A, A_WIDTH), lambda b, i: (b * nt + i, 0)),
        out_shape=jax.ShapeDtypeStruct((batch * seq, A_WIDTH), BF16),
        scratch_shapes=[
            pltpu.VMEM((2, A_KV_HEADS, KEYS_A, COLS_A), F32),
            pltpu.VMEM((A_KV_HEADS, SUBLANES, COLS_A), F32),
            pltpu.VMEM((2, 2, KEYS_A, LANES), BF16),
            pltpu.VMEM((2, A_HEAD_DIM + PACKED, KEYS_A), BF16),
            pltpu.VMEM((2, KEYS_A, CHUNK_PAIRS_A * TQ_A), F32),
            pltpu.VMEM((2, SUBLANES, CHUNK_PAIRS_A * TQ_A), F32),
            pltpu.VMEM((2, A_HEAD_DIM, CHUNK_PAIRS_A * TQ_A), F32),
        ],
        semantics=("arbitrary", "arbitrary"),
        name="swa",
        operands=[proj, proj, proj, jnp.asarray(bkt), rel_bias, sinks],
    )


def _dsa_kernel(iq_ref, iwq_ref, ik_ref, q_ref, k_ref, v_ref, bkt_ref, rb_ref, o_ref,
                sc_ref, scb_ref, d16_ref, mb_ref, tab_ref, tri_ref, vt_ref, acc_ref, m_ref,
                sa_ref, smaxa_ref, sb_ref, smaxb_ref, *, topk, seq, present):
    i = pl.program_id(1)
    nkt = i + 1
    int32, int16 = jnp.int32, jnp.int16
    grp = TK_B // SUBLANES

    def hcols(h):
        return slice(h * B_HEAD_DIM, (h + 1) * B_HEAD_DIM)

    def ktile(kt):
        return pl.ds(pl.multiple_of(kt * TK_B, TK_B), TK_B)

    def rows3(a):
        return a.reshape(a.shape[0] // SUBLANES, SUBLANES, TQ_B)

    def all_rows(a, op):
        return jnp.broadcast_to(op(a, axis=0, keepdims=True), a.shape)

    @pl.when((pl.program_id(0) == 0) & (i == 0))
    def _():
        far_bucket = N_BUCKETS // 2 - 1
        for d in range(NEAR_B):
            for h0 in range(0, B_HEADS, 2):
                tabs = _bias_tables(bkt_ref[d], present[d], rb_ref, [A_HEADS + h0, A_HEADS + h0 + 1],
                                    sub_row=far_bucket)
                tab_ref[d, h0] = tabs[0]
                tab_ref[d, h0 + 1] = tabs[1]
        tab_ref[NEAR_B] = jnp.zeros(tab_ref.shape[1:], F32)
        r = lax.broadcasted_iota(int32, (TK_B, TK_B), 0)
        c = lax.broadcasted_iota(int32, (TK_B, TK_B), 1)
        tri_ref[...] = jnp.where(c < r, 1.0, 0.0).astype(BF16)

    @pl.when(i == 0)
    def _():
        r = lax.broadcasted_iota(int32, (B_HEAD_DIM, B_HEAD_DIM), 0)
        c = lax.broadcasted_iota(int32, (B_HEAD_DIM, B_HEAD_DIM), 1)
        eye = jnp.where(r == c, 1.0, 0.0).astype(BF16)

        def body(kt, carry):
            for h in range(B_HEADS):
                vt_ref[kt, h] = _nt_dot(eye, v_ref[ktile(kt), hcols(h)]).astype(BF16)
            return carry

        lax.fori_loop(0, seq // TK_B, body, 0)

    key_chunk = lax.broadcasted_iota(int32, (TK_B, TQ_B), 0) // CHUNK
    qry_chunk = lax.broadcasted_iota(int32, (TK_B, TQ_B), 1) // CHUNK
    adm_diag = key_chunk <= qry_chunk

    r = lax.broadcasted_iota(int32, (IDX_HEADS * SUBLANES, LANES), 0)
    c = lax.broadcasted_iota(int32, (IDX_HEADS * SUBLANES, LANES), 1)
    pick_w = jnp.where(c == IDX_DIM + r // SUBLANES, 1.0, 0.0).astype(BF16)
    w_all = _nt_dot(pick_w, iwq_ref[...]) * (IDX_HEADS ** -0.5 * IDX_DIM ** -0.5)

    def for_tiles(n, body, unroll=2):
        def group(j, carry):
            for k in range(unroll):
                body(unroll * j + k)
            return carry

        def single(kt, carry):
            body(kt)
            return carry

        lax.fori_loop(0, n // unroll, group, 0)
        lax.fori_loop(n - n % unroll, n, single, 0)

    def score_tile(kt):
        ikt = ik_ref[ktile(kt), 0:IDX_DIM]
        sc = jnp.zeros((grp, SUBLANES, TQ_B), F32)
        for h in range(IDX_HEADS):
            x = _nt_dot(ikt, iq_ref[:, h * IDX_DIM:(h + 1) * IDX_DIM])
            sc = sc + w_all[h * SUBLANES:(h + 1) * SUBLANES][None] * jnp.maximum(rows3(x), 0.0)
        store_score(kt, sc.reshape(TK_B, TQ_B))

    def store_score(kt, sc):
        sc_ref[kt] = sc
        scb_ref[kt] = sc.astype(BF16)

    for_tiles(nkt, score_tile, unroll=4)
    store_score(i, jnp.where(adm_diag, sc_ref[i], -jnp.inf))

    def key_to_f32(key):
        return pltpu.bitcast(jnp.where(key >= 0, key, INT_MIN - key), F32)

    def over_tiles(tile_count, zero):
        def pair(j, acc):
            return acc + (tile_count(2 * j) + tile_count(2 * j + 1))
        acc = lax.fori_loop(0, nkt // 2, pair, zero)
        return lax.cond(nkt % 2 == 1, lambda a: a + tile_count(nkt - 1), lambda a: a, acc)

    def tree_sum(parts):
        while len(parts) > 1:
            parts = [parts[n] + parts[n + 1] for n in range(0, len(parts), 2)]
        return parts[0]

    def count_bf16(cand):
        def tile_count(kt):
            blk = scb_ref[kt].reshape(TK_B // PACKED, PACKED, TQ_B)
            return tree_sum([jnp.where(blk[g] >= cand, jnp.ones((), int16), jnp.zeros((), int16))
                             for g in range(TK_B // PACKED)])
        acc = over_tiles(tile_count, jnp.zeros((PACKED, TQ_B), int16))
        return all_rows(acc.astype(int32), jnp.sum)

    def count_f32(pred):
        def tile_count(kt):
            blk = rows3(sc_ref[kt])
            return tree_sum([jnp.where(pred(blk[g]), 1, 0) for g in range(grp)])
        return all_rows(over_tiles(tile_count, jnp.zeros((SUBLANES, TQ_B), int32)), jnp.sum)

    def coarse_pass(p, u):
        bit = jnp.left_shift(jnp.int32(1), 15 - p)
        cand = key_to_f32(((u | bit) - HALF) << 16).astype(BF16)
        return jnp.where(count_bf16(cand) >= topk, u | bit, u)

    u = lax.fori_loop(0, 16, coarse_pass, jnp.zeros((PACKED, TQ_B), int32))
    coarse_key = ((u - HALF) << 16)[:SUBLANES]

    lo = jnp.maximum(coarse_key - HALF, KEY_NEG_INF)
    hi = jnp.minimum(coarse_key, KEY_POS_INF - 2 * HALF) + 2 * HALF

    def fine_pass(p, lohi):
        lo, hi = lohi
        mid = lo + ((hi - lo) >> 1)
        cand = key_to_f32(mid)
        ok = count_f32(lambda blk: blk >= cand) >= topk
        return jnp.where(ok, mid, lo), jnp.where(ok, hi, mid)

    lo, hi = lax.fori_loop(0, PRE_PASSES, fine_pass, (lo, hi))

    mag = jnp.minimum(jnp.abs(lo), jnp.abs(hi))
    around_zero = ((lo <= 0) & (hi >= 0)) | (mag >> F32_MANT_BITS == 0)
    exp_field = jnp.where(around_zero, F32_MANT_BITS + 1, mag >> F32_MANT_BITS)
    any_irregular = jnp.max(jnp.where(exp_field <= F32_MANT_BITS, 1, 0)) > 0

    def finish_f32(lohi):
        lo, hi = lax.fori_loop(PRE_PASSES, FINE_PASSES, fine_pass, lohi)
        tau_f = key_to_f32(lo)
        return tau_f, count_f32(lambda blk: blk > tau_f)

    def finish_int16(lohi):
        lo, hi = lohi
        centre = jnp.where(around_zero, 0.0, key_to_f32(lo + ((hi - lo) >> 1)))
        ulp = pltpu.bitcast((exp_field - F32_MANT_BITS) << F32_MANT_BITS, F32)
        inv_ulp = pltpu.bitcast((2 * F32_EXP_BIAS + F32_MANT_BITS - exp_field) << F32_MANT_BITS, F32)

        def build(kt):
            d = (rows3(sc_ref[kt]) - centre[None]) * inv_ulp[None]
            d = jnp.clip(d, -HALF, HALF - 1).reshape(TK_B, TQ_B)
            d16_ref[kt] = d.astype(int32).astype(int16)

        for_tiles(nkt, build)

        def count_d16(cand):
            def tile_count(kt):
                blk = d16_ref[kt].reshape(TK_B // PACKED, PACKED, TQ_B)
                return tree_sum([jnp.where(blk[g] >= cand, jnp.ones((), int16), jnp.zeros((), int16))
                                 for g in range(TK_B // PACKED)])
            acc = over_tiles(tile_count, jnp.zeros((PACKED, TQ_B), int16))
            return all_rows(acc.astype(int32), jnp.sum)

        def d_pass(p, u):
            bit = jnp.left_shift(jnp.int32(1), 15 - p)
            cand = ((u | bit) - HALF).astype(int16)
            return jnp.where(count_d16(cand) >= topk, u | bit, u)

        t = lax.fori_loop(0, 16, d_pass, jnp.zeros((PACKED, TQ_B), int32)) - HALF
        above = count_d16((t + 1).astype(int16))
        return centre + t[:SUBLANES].astype(F32) * ulp, above[:SUBLANES]

    tau2d, above = lax.cond(any_irregular, finish_f32, finish_int16, (lo, hi))
    tau = tau2d[None]

    need = (topk - above).astype(F32)[None]
    ones_l = jnp.ones((2 * SUBLANES, TK_B), BF16)

    def mask_tile(kt, run):
        blk = rows3(sc_ref[kt])
        eq = blk == tau
        eqf = jnp.where(eq, 1.0, 0.0).reshape(TK_B, TQ_B).astype(BF16)
        rank = rows3(jnp.dot(tri_ref[...], eqf, preferred_element_type=F32)) + run[None]
        sel = (blk > tau) | (eq & (rank < need))
        mb_ref[kt] = jnp.where(sel, 0.0, NEG).reshape(TK_B, TQ_B)
        return run + jnp.dot(ones_l, eqf, preferred_element_type=F32)[:SUBLANES]

    def mask_group(j, run):
        for k in range(4):
            run = mask_tile(4 * j + k, run)
        return run

    run = lax.fori_loop(0, nkt // 4, mask_group, jnp.zeros((SUBLANES, TQ_B), F32))
    lax.fori_loop(nkt - nkt % 4, nkt, mask_tile, run)

    mb_ref[i] = jnp.where(adm_diag, mb_ref[i], NEG)

    m_ref[...] = jnp.full(m_ref.shape, NEG, F32)
    acc_ref[...] = jnp.zeros(acc_ref.shape, F32)
    ones_rows = jnp.ones((PACKED, TK_B), BF16)

    buf_a, buf_b = (sa_ref, smaxa_ref), (sb_ref, smaxb_ref)

    def logits_stage(kt, buf, near):
        s_ref, smax_ref = buf
        mb = mb_ref[kt]
        for h in range(B_HEADS):
            s = _nt_dot(k_ref[ktile(kt), hcols(h)], q_ref[:, hcols(h)]) + mb
            if near:
                s = s + tab_ref[jnp.minimum(i - kt, NEAR_B), h]
            s_ref[h] = s
            smax_ref[h] = all_rows(jnp.max(rows3(s), axis=0), jnp.max)

    def update_stage(kt, buf):
        s_ref, smax_ref = buf
        for h in range(B_HEADS):
            m_old = m_ref[h]
            m_new = jnp.maximum(m_old, smax_ref[h])
            alpha = jnp.exp2(m_old - m_new)
            p = jnp.exp2(rows3(s_ref[h]) - m_new[None]).reshape(TK_B, TQ_B).astype(BF16)
            vaug = jnp.concatenate([vt_ref[kt, h], ones_rows], axis=0)
            pv = jnp.dot(vaug, p, preferred_element_type=F32)
            acc_ref[h] = (rows3(acc_ref[h]) * alpha[None] + rows3(pv)).reshape(acc_ref.shape[1:])
            m_ref[h] = m_new

    def pair_body(j, carry, near):
        logits_stage(2 * j + 1, buf_b, near)
        update_stage(2 * j, buf_a)
        logits_stage(2 * j + 2, buf_a, near)
        update_stage(2 * j + 1, buf_b)
        return carry

    near_lo = jnp.maximum(i - (NEAR_B - 1), 0)
    far_pairs = jnp.maximum(near_lo - 1, 0) // 2
    logits_stage(0, buf_a, True)
    lax.fori_loop(0, far_pairs, functools.partial(pair_body, near=False), 0)
    lax.fori_loop(far_pairs, (nkt - 1) // 2, functools.partial(pair_body, near=True), 0)

    @pl.when(nkt % 2 == 0)
    def _():
        logits_stage(nkt - 1, buf_b, True)
        update_stage(nkt - 2, buf_a)
        update_stage(nkt - 1, buf_b)

    @pl.when(nkt % 2 == 1)
    def _():
        update_stage(nkt - 1, buf_a)

    for h in range(B_HEADS):
        num = rows3(acc_ref[h, 0:B_HEAD_DIM, :])
        den = acc_ref[h, B_HEAD_DIM:B_HEAD_DIM + SUBLANES, :]
        out_t = (num / den[None]).reshape(B_HEAD_DIM, TQ_B)
        o_ref[:, h * B_HEAD_DIM:(h + 1) * B_HEAD_DIM] = out_t.T.astype(BF16)


def _dsa(proj, bkt, rel_bias, batch, seq):
    nt = seq // TQ_B
    topk = min(TOPK_MAX, seq // 4)

    def q_spec(width, col):
        return pl.BlockSpec((TQ_B, width), lambda b, i: (b * nt + i, col // width))

    def seq_spec(width, col):
        return pl.BlockSpec((seq, width), lambda b, i: (b, col // width))

    return _call(
        functools.partial(_dsa_kernel, topk=topk, seq=seq,
                          present=tuple(tuple(int(b) for b in np.unique(t)) for t in bkt)),
        grid=(batch, nt),
        in_specs=[
            q_spec(IDX_HEADS * IDX_DIM, COL_IQ),
            q_spec(LANES, COL_IKW),
            seq_spec(LANES, COL_IKW),
            q_spec(B_WIDTH, COL_BQ), seq_spec(B_WIDTH, COL_BK), seq_spec(B_WIDTH, COL_BV),
            pl.BlockSpec((NEAR_B, TK_B, TQ_B), lambda b, i: (0, 0, 0)),
            pl.BlockSpec(memory_space=pltpu.SMEM),
        ],
        out_spec=pl.BlockSpec((TQ_B, B_WIDTH), lambda b, i: (b * nt + i, 0)),
        out_shape=jax.ShapeDtypeStruct((batch * seq, B_WIDTH), BF16),
        scratch_shapes=[
            pltpu.VMEM((nt, TK_B, TQ_B), F32),
            pltpu.VMEM((nt, TK_B, TQ_B), BF16),
            pltpu.VMEM((nt, TK_B, TQ_B), jnp.int16),
            pltpu.VMEM((nt, TK_B, TQ_B), F32),
            pltpu.VMEM((NEAR_B + 1, B_HEADS, TK_B, TQ_B), F32),
            pltpu.VMEM((TK_B, TK_B), BF16),
            pltpu.VMEM((seq // TK_B, B_HEADS, B_HEAD_DIM, TK_B), BF16),
            pltpu.VMEM((B_HEADS, B_HEAD_DIM + PACKED, TQ_B), F32),
            pltpu.VMEM((B_HEADS, SUBLANES, TQ_B), F32),
            pltpu.VMEM((B_HEADS, TK_B, TQ_B), F32),
            pltpu.VMEM((B_HEADS, SUBLANES, TQ_B), F32),
            pltpu.VMEM((B_HEADS, TK_B, TQ_B), F32),
            pltpu.VMEM((B_HEADS, SUBLANES, TQ_B), F32),
        ],
        semantics=("arbitrary", "arbitrary"),
        name="dsa",
        operands=[proj, proj, proj, proj, proj, proj, jnp.asarray(bkt), rel_bias],
    )


def _outproj_kernel(oa_ref, ob_ref, cq_ref, mkv_ref, gate_ref, x_ref, w32_ref, g_ref, o_ref, w_ref):
    @pl.when(pl.program_id(0) == 0)
    def _():
        w_ref[...] = w32_ref[...].astype(BF16)

    ones_m = jnp.ones((N_MEM, C_HEAD_DIM), BF16)
    for r in range(TM_OUT // SUB_OUT):
        rows = slice(r * SUB_OUT, (r + 1) * SUB_OUT)
        att = [oa_ref[rows, :].astype(F32), ob_ref[rows, :].astype(F32)]
        for h in range(C_HEADS):
            hs = slice(h * C_HEAD_DIM, (h + 1) * C_HEAD_DIM)
            s = _nt_dot(cq_ref[rows, hs], mkv_ref[:, hs])
            p = jnp.exp2(s - jnp.max(s, axis=1, keepdims=True)).astype(BF16)
            vaug = jnp.concatenate([mkv_ref[:, C_WIDTH + h * C_HEAD_DIM:C_WIDTH + (h + 1) * C_HEAD_DIM], ones_m],
                                   axis=1)
            pv = jnp.dot(p, vaug, preferred_element_type=F32)
            att.append(pv[:, :C_HEAD_DIM] / pv[:, C_HEAD_DIM:])
        y = (jnp.concatenate(att, axis=1) * gate_ref[rows, :].astype(F32)).astype(BF16)
        h_new = x_ref[rows, :] + jnp.dot(y, w_ref[...], preferred_element_type=F32)
        ms = jnp.mean(h_new * h_new, axis=-1, keepdims=True)
        o_ref[rows, :] = h_new * lax.rsqrt(ms + EPS) * g_ref[...]


def _outproj(oa, ob, proj, mkv, x2d, w, g, seq):
    m = x2d.shape[0]
    steps_per_seq = seq // TM_OUT
    return _call(
        _outproj_kernel,
        grid=(m // TM_OUT,),
        in_specs=[
            pl.BlockSpec((TM_OUT, A_WIDTH), lambda i: (i, 0)),
            pl.BlockSpec((TM_OUT, B_WIDTH), lambda i: (i, 0)),
            pl.BlockSpec((TM_OUT, C_WIDTH), lambda i: (i, COL_CQ // C_WIDTH)),
            pl.BlockSpec((N_MEM, 2 * C_WIDTH), lambda i: (i // steps_per_seq, 0)),
            pl.BlockSpec((TM_OUT, MIX_WIDTH), lambda i: (i, COL_GATE // MIX_WIDTH)),
            pl.BlockSpec((TM_OUT, D_MODEL), lambda i: (i, 0)),
            pl.BlockSpec((MIX_WIDTH, D_MODEL), lambda i: (0, 0), pipeline_mode=pl.Buffered(1)),
            pl.BlockSpec((1, D_MODEL), lambda i: (0, 0)),
        ],
        out_spec=pl.BlockSpec((TM_OUT, D_MODEL), lambda i: (i, 0)),
        out_shape=jax.ShapeDtypeStruct((m, D_MODEL), F32),
        scratch_shapes=[pltpu.VMEM((MIX_WIDTH, D_MODEL), BF16)],
        semantics=("arbitrary",),
        name="outproj",
        operands=[oa, ob, proj, mkv, proj, x2d, w, g],
    )


def _col_scale():
    cs = np.ones((1, PROJ_W), np.float32)
    cs[0, COL_AQ:COL_AQ + A_WIDTH] = A_HEAD_DIM ** -0.5 * LOG2E
    cs[0, COL_BQ:COL_BQ + B_WIDTH] = B_HEAD_DIM ** -0.5 * LOG2E
    cs[0, COL_CQ:COL_CQ + C_WIDTH] = C_HEAD_DIM ** -0.5 * LOG2E
    return jnp.asarray(cs)


def kernel(x, mem, g_norm, w_in, sinks, rel_bias, g_mem, w_mem_kv, w_out, g_final):
    batch, seq, _ = x.shape
    assert w_in.shape[0] == 1, "the out-projection kernel fuses the final norm of a single-layer trunk"
    kk = np.arange(2 * TQ_A)[:, None]
    qq = np.arange(TQ_A)[None, :]
    bkt_a = _t5_bucket_np(kk - TQ_A - qq)
    kk = np.arange(TK_B)[None, :, None]
    qq = np.arange(TQ_B)[None, None, :]
    d = np.arange(NEAR_B)[:, None, None]
    bkt_b = _t5_bucket_np(kk - qq - TK_B * d)

    h = x.reshape(batch * seq, D_MODEL)
    mem2d = mem.reshape(batch * N_MEM, D_MODEL)
    w_t = jnp.transpose(w_in, (2, 0, 1)).reshape(IN_WIDTH * KB_IN, LANES)
    proj = _inproj(h, g_norm[0].reshape(1, D_MODEL), _wprep(w_t), _col_scale())
    mkv = _memkv(mem2d, g_mem[0].reshape(1, D_MODEL), w_mem_kv[0])
    oa = _swa(proj, bkt_a, rel_bias, sinks[0], batch, seq)
    ob = _dsa(proj, bkt_b, rel_bias, batch, seq)
    out = _outproj(oa, ob, proj, mkv, h, w_out[0], g_final.reshape(1, D_MODEL), seq)
    return out.reshape(batch, seq, D_MODEL)
```

```python
import functools
import math

import numpy as np
import jax
import jax.numpy as jnp
from jax import lax
from jax.experimental import pallas as pl
from jax.experimental.pallas import tpu as pltpu

D_MODEL = 2048
CHUNK = 64
N_MEM = 256
EPS = 1e-6
A_HEADS = 16
A_KV_HEADS = 2
A_HEAD_DIM = 64
WINDOW_CHUNKS = 2
A_WIDTH = A_HEADS * A_HEAD_DIM
B_HEADS = 4
B_HEAD_DIM = 128
B_WIDTH = B_HEADS * B_HEAD_DIM
IDX_HEADS = 4
IDX_DIM = 64
TOPK_MAX = 256
C_HEADS = 4
C_HEAD_DIM = 128
C_WIDTH = C_HEADS * C_HEAD_DIM
MIX_WIDTH = A_WIDTH + B_WIDTH + C_WIDTH
N_BUCKETS = 32
MAX_DISTANCE = 1024
KV_A = A_KV_HEADS * A_HEAD_DIM
SPLIT_SIZES = (A_WIDTH, KV_A, KV_A, B_WIDTH, B_WIDTH, B_WIDTH,
               IDX_HEADS * IDX_DIM, IDX_DIM, IDX_HEADS, C_WIDTH, MIX_WIDTH)
IN_WIDTH = sum(SPLIT_SIZES)

F32 = jnp.float32
BF16 = jnp.bfloat16
LOG2E = math.log2(math.e)
NEG = -1e30
INT_MIN = -(2 ** 31)
LANES = 128
SUBLANES = 8
PACKED = 16
HALF = 1 << 15
KEY_POS_INF = 0x7F800000
KEY_NEG_INF = -KEY_POS_INF
FINE_PASSES = 17
PRE_PASSES = 2
F32_MANT_BITS = 23
F32_EXP_BIAS = 127

(SRC_AQ, SRC_AK, SRC_AV, SRC_BQ, SRC_BK, SRC_BV,
 SRC_IQ, SRC_IK, SRC_IW, SRC_CQ, SRC_GATE) = (int(c) for c in np.cumsum((0,) + SPLIT_SIZES)[:-1])
COL_GATE = 0
COL_AQ = COL_GATE + MIX_WIDTH
COL_BQ = COL_AQ + A_WIDTH
COL_BK = COL_BQ + B_WIDTH
COL_BV = COL_BK + B_WIDTH
COL_CQ = COL_BV + B_WIDTH
COL_AKV = COL_CQ + C_WIDTH
COL_IQ = COL_AKV + 2 * KV_A
COL_IKW = COL_IQ + IDX_HEADS * IDX_DIM
PROJ_W = 6144
SEGMENTS = ((COL_GATE, SRC_GATE, MIX_WIDTH), (COL_AQ, SRC_AQ, A_WIDTH), (COL_BQ, SRC_BQ, B_WIDTH),
            (COL_BK, SRC_BK, B_WIDTH), (COL_BV, SRC_BV, B_WIDTH), (COL_CQ, SRC_CQ, C_WIDTH),
            (COL_AKV, SRC_AK, 2 * KV_A), (COL_IQ, SRC_IQ, IDX_HEADS * IDX_DIM), (COL_IKW, SRC_IK, 256))
KB_IN = D_MODEL // LANES

TC_PREP = 256
PREFETCH_PREP = 4
TM_IN, TN_IN = 1024, 2048
SUB_IN = 256
TQ_A = 128
RB_A = 1024
TQ_B = 256
TK_B = 256
NEAR_B = 4
TM_OUT = 512
SUB_OUT = 256
V7X_VMEM_BYTES = 64 * 1024 * 1024
VMEM_REQUEST = V7X_VMEM_BYTES * 7 // 8


def _vmem_limit(in_specs, operands, out_spec, out_shape, scratch_shapes=()):
    def buffers(spec, dtype):
        if spec.block_shape is None:
            return 0
        elems = math.prod(int(getattr(d, "block_size", d)) for d in spec.block_shape)
        count = spec.pipeline_mode.buffer_count if spec.pipeline_mode is not None else 2
        return elems * jnp.dtype(dtype).itemsize * count

    total = sum(buffers(s, a.dtype) for s, a in zip(in_specs, operands)) + buffers(out_spec, out_shape.dtype)
    total += sum(math.prod(s.shape) * jnp.dtype(s.dtype).itemsize for s in scratch_shapes)
    assert total <= VMEM_REQUEST, (total, VMEM_REQUEST)
    return VMEM_REQUEST


def _call(kernel, *, name, grid, semantics, in_specs, operands, out_spec, out_shape, scratch_shapes=()):
    return pl.pallas_call(
        kernel,
        grid=grid,
        in_specs=in_specs,
        out_specs=out_spec,
        out_shape=out_shape,
        scratch_shapes=list(scratch_shapes),
        compiler_params=pltpu.CompilerParams(
            dimension_semantics=semantics,
            vmem_limit_bytes=_vmem_limit(in_specs, operands, out_spec, out_shape, scratch_shapes)),
        name=name,
    )(*operands)


def _t5_bucket_np(rel):
    nb = N_BUCKETS // 2
    max_exact = nb // 2
    side = np.where(rel > 0, nb, 0)
    n = np.abs(rel)
    nf = np.maximum(n, max_exact).astype(np.float32)
    large = max_exact + (np.log(nf / max_exact) / math.log(MAX_DISTANCE / max_exact)
                         * (nb - max_exact)).astype(np.int32)
    large = np.minimum(large, nb - 1)
    return (side + np.where(n < max_exact, n, large)).astype(np.int32)


def _nt_dot(a, b):
    return lax.dot_general(a, b, (((1,), (1,)), ((), ())), preferred_element_type=F32)


def _bias_tables(bucket, present, rb_ref, cols, sub_row=None):
    accs = [jnp.zeros(bucket.shape, F32) for _ in cols]
    for b in present:
        hit = bucket == b
        for n, col in enumerate(cols):
            val = rb_ref[b, col]
            if sub_row is not None:
                val = val - rb_ref[sub_row, col]
            accs[n] = jnp.where(hit, val * LOG2E, accs[n])
    return accs


def _wprep_kernel(src_ref, w_hbm, o_ref, buf_ref, sem_ref):
    t = pl.program_id(0)
    steps = pl.num_programs(0)

    def fetch(step):
        row = pl.multiple_of(jnp.maximum(src_ref[step], 0) * KB_IN, SUBLANES)
        slot = step % PREFETCH_PREP
        return pltpu.make_async_copy(w_hbm.at[pl.ds(row, TC_PREP * KB_IN), :], buf_ref.at[slot], sem_ref.at[slot])

    @pl.when(t == 0)
    def _():
        for k in range(PREFETCH_PREP - 1):
            fetch(k).start()

    @pl.when(t + PREFETCH_PREP - 1 < steps)
    def _():
        fetch(t + PREFETCH_PREP - 1).start()

    fetch(t).wait()
    w_ref = buf_ref.at[t % PREFETCH_PREP]

    @pl.when(src_ref[t] >= 0)
    def _():
        for kb in range(KB_IN):
            o_ref[:, kb * LANES:(kb + 1) * LANES] = w_ref[pl.ds(kb, TC_PREP, stride=KB_IN), :].astype(BF16)

    @pl.when(src_ref[t] < 0)
    def _():
        o_ref[...] = jnp.zeros(o_ref.shape, BF16)


def _wprep(w_t):
    src = np.full((PROJ_W // TC_PREP,), -1, np.int32)
    for dst_col, src_col, width in SEGMENTS:
        for n in range(width // TC_PREP):
            src[dst_col // TC_PREP + n] = src_col + n * TC_PREP
    steps = PROJ_W // TC_PREP
    assert steps >= PREFETCH_PREP
    in_specs = [pl.BlockSpec(memory_space=pl.ANY)]
    out_spec = pl.BlockSpec((TC_PREP, D_MODEL), lambda t, src: (t, 0))
    out_shape = jax.ShapeDtypeStruct((PROJ_W, D_MODEL), BF16)
    ring = pltpu.VMEM((PREFETCH_PREP, TC_PREP * KB_IN, LANES), F32)
    return pl.pallas_call(
        _wprep_kernel,
        grid_spec=pltpu.PrefetchScalarGridSpec(
            num_scalar_prefetch=1, grid=(steps,), in_specs=in_specs, out_specs=out_spec,
            scratch_shapes=[ring, pltpu.SemaphoreType.DMA((PREFETCH_PREP,))]),
        out_shape=out_shape,
        compiler_params=pltpu.CompilerParams(
            dimension_semantics=("arbitrary",),
            vmem_limit_bytes=_vmem_limit(in_specs, [w_t], out_spec, out_shape, [ring])),
        name="wprep",
    )(jnp.asarray(src), w_t)


def _inproj_kernel(x_ref, g_ref, wt_ref, cs_ref, o_ref, hn_ref):
    assert COL_GATE == 0 and MIX_WIDTH == TN_IN
    j = pl.program_id(1)

    @pl.when(j == 0)
    def _():
        for r in range(TM_IN // SUB_IN):
            rows = slice(r * SUB_IN, (r + 1) * SUB_IN)
            x = x_ref[rows, :]
            ms = jnp.mean(x * x, axis=-1, keepdims=True)
            hn = (x * lax.rsqrt(ms + EPS) * g_ref[...]).astype(BF16)
            hn_ref[rows, :] = hn
            gate = _nt_dot(hn, wt_ref[...])
            o_ref[rows, :] = (gate / (1.0 + jnp.exp(-gate))).astype(BF16)

    @pl.when(j > 0)
    def _():
        o_ref[...] = (_nt_dot(hn_ref[...], wt_ref[...]) * cs_ref[...]).astype(BF16)


def _inproj(x2d, g, wt, cs):
    m = x2d.shape[0]
    return _call(
        _inproj_kernel,
        grid=(m // TM_IN, PROJ_W // TN_IN),
        in_specs=[
            pl.BlockSpec((TM_IN, D_MODEL), lambda i, j: (i, 0)),
            pl.BlockSpec((1, D_MODEL), lambda i, j: (0, 0)),
            pl.BlockSpec((TN_IN, D_MODEL), lambda i, j: (j, 0)),
            pl.BlockSpec((1, TN_IN), lambda i, j: (0, j)),
        ],
        out_spec=pl.BlockSpec((TM_IN, TN_IN), lambda i, j: (i, j)),
        out_shape=jax.ShapeDtypeStruct((m, PROJ_W), BF16),
        scratch_shapes=[pltpu.VMEM((TM_IN, D_MODEL), BF16)],
        semantics=("parallel", "arbitrary"),
        name="inproj",
        operands=[x2d, g, wt, cs],
    )


def _memkv_kernel(m_ref, g_ref, w_ref, o_ref):
    x = m_ref[...]
    ms = jnp.mean(x * x, axis=-1, keepdims=True)
    hn = (x * lax.rsqrt(ms + EPS) * g_ref[...]).astype(BF16)
    o_ref[...] = jnp.dot(hn, w_ref[...].astype(BF16), preferred_element_type=F32).astype(BF16)


def _memkv(mem2d, g, w):
    m = mem2d.shape[0]
    return _call(
        _memkv_kernel,
        grid=(m // N_MEM,),
        in_specs=[
            pl.BlockSpec((N_MEM, D_MODEL), lambda i: (i, 0)),
            pl.BlockSpec((1, D_MODEL), lambda i: (0, 0)),
            pl.BlockSpec((D_MODEL, 2 * C_WIDTH), lambda i: (0, 0)),
        ],
        out_spec=pl.BlockSpec((N_MEM, 2 * C_WIDTH), lambda i: (i, 0)),
        out_shape=jax.ShapeDtypeStruct((m, 2 * C_WIDTH), BF16),
        semantics=("arbitrary",),
        name="memkv",
        operands=[mem2d, g, w],
    )


GRP_A = A_HEADS // A_KV_HEADS
PAIRS_A = GRP_A // 2
COLS_A = GRP_A * TQ_A
KEYS_A = 2 * TQ_A
CHUNK_PAIRS_A = 4


def _swa_kernel(q_ref, kvp_ref, kvc_ref, bkt_ref, rb_ref, sink_ref, o_ref,
                tab_ref, sinkv_ref, kp_ref, vt_ref, s_ref, m_ref, ot_ref, *, present):
    i = pl.program_id(1)

    @pl.when((pl.program_id(0) == 0) & (i == 0))
    def _():
        bkt = bkt_ref[...]
        kchunk = lax.broadcasted_iota(jnp.int32, bkt.shape, 0) // CHUNK
        qchunk = lax.broadcasted_iota(jnp.int32, bkt.shape, 1) // CHUNK
        allowed = (kchunk >= qchunk) & (kchunk <= qchunk + WINDOW_CHUNKS)
        has_prev = lax.broadcasted_iota(jnp.int32, bkt.shape, 0) >= TQ_A
        for g in range(A_KV_HEADS):
            for c0 in range(0, GRP_A, PAIRS_A):
                heads = [g * GRP_A + 2 * (col % PAIRS_A) + col // PAIRS_A for col in range(c0, c0 + PAIRS_A)]
                for n, t in enumerate(_bias_tables(bkt, present, rb_ref, heads)):
                    cols = slice((c0 + n) * TQ_A, (c0 + n + 1) * TQ_A)
                    t = jnp.where(allowed, t, NEG)
                    tab_ref[0, g, :, cols] = t
                    tab_ref[1, g, :, cols] = jnp.where(has_prev, t, NEG)
                    sinkv_ref[g, :, cols] = jnp.full((SUBLANES, TQ_A), sink_ref[heads[n]] * LOG2E, F32)

    width = CHUNK_PAIRS_A * TQ_A
    tasks = [(t, g, p0, parity) for t in range(RB_A // TQ_A) for g in range(A_KV_HEADS)
             for p0 in range(0, PAIRS_A, CHUNK_PAIRS_A) for parity in range(2)]

    def prepare(t, g):
        slot = (t * A_KV_HEADS + g) % 2
        if t == 0:
            kv_t = jnp.concatenate([kvp_ref[...], kvc_ref[0:TQ_A, :]], axis=0)
        else:
            kv_t = kvc_ref[(t - 1) * TQ_A:(t + 1) * TQ_A, :]
        kg = kv_t[:, g * A_HEAD_DIM:(g + 1) * A_HEAD_DIM]
        zeros = jnp.zeros((KEYS_A, A_HEAD_DIM), BF16)
        kp_ref[slot, 0] = jnp.concatenate([kg, zeros], axis=1)
        kp_ref[slot, 1] = jnp.concatenate([zeros, kg], axis=1)
        r = lax.broadcasted_iota(jnp.int32, (A_HEAD_DIM, KV_A), 0)
        c = lax.broadcasted_iota(jnp.int32, (A_HEAD_DIM, KV_A), 1)
        pick = jnp.where(c == g * A_HEAD_DIM + r, 1.0, 0.0).astype(BF16)
        vt = _nt_dot(pick, kv_t[:, KV_A:]).astype(BF16)
        vt_ref[slot] = jnp.concatenate([vt, jnp.ones((PACKED, KEYS_A), BF16)], axis=0)

    def chunk_cols(p0, parity):
        lo = (parity * PAIRS_A + p0) * TQ_A
        return slice(lo, lo + width)

    def logits_stage(n):
        t, g, p0, parity = tasks[n]
        if (p0, parity) == (0, 0):
            prepare(t, g)
        slot, buf, cols = (t * A_KV_HEADS + g) % 2, n % 2, chunk_cols(p0, parity)
        q_pairs = jnp.concatenate(
            [q_ref[t * TQ_A:(t + 1) * TQ_A, (g * PAIRS_A + p) * LANES:(g * PAIRS_A + p + 1) * LANES]
             for p in range(p0, p0 + CHUNK_PAIRS_A)], axis=0)
        variant = jnp.where(i == 0, 1, 0) if t == 0 else 0
        s = _nt_dot(kp_ref[slot, parity], q_pairs) + tab_ref[variant, g, :, cols]
        s_ref[buf] = s
        m = jnp.max(s.reshape(KEYS_A // SUBLANES, SUBLANES, width), axis=0)
        m_ref[buf] = jnp.maximum(jnp.broadcast_to(jnp.max(m, axis=0, keepdims=True), m.shape),
                                 sinkv_ref[g, :, cols])

    def update_stage(n):
        t, g, p0, parity = tasks[n]
        slot, buf, cols = (t * A_KV_HEADS + g) % 2, n % 2, chunk_cols(p0, parity)
        m = m_ref[buf]
        s = s_ref[buf].reshape(KEYS_A // SUBLANES, SUBLANES, width)
        p = jnp.exp2(s - m[None]).reshape(KEYS_A, width).astype(BF16)
        pv = jnp.dot(vt_ref[slot], p, preferred_element_type=F32)
        den = pv[A_HEAD_DIM:A_HEAD_DIM + SUBLANES] + jnp.exp2(sinkv_ref[g, :, cols] - m)
        out_t = pv[:A_HEAD_DIM].reshape(A_HEAD_DIM // SUBLANES, SUBLANES, width) / den[None]
        ot_ref[parity] = out_t.reshape(A_HEAD_DIM, width)
        if parity == 1:
            for k in range(CHUNK_PAIRS_A):
                blk = jnp.concatenate([ot_ref[0, :, k * TQ_A:(k + 1) * TQ_A],
                                       ot_ref[1, :, k * TQ_A:(k + 1) * TQ_A]], axis=0)
                lanes = slice((g * PAIRS_A + p0 + k) * LANES, (g * PAIRS_A + p0 + k + 1) * LANES)
                o_ref[t * TQ_A:(t + 1) * TQ_A, lanes] = blk.T.astype(BF16)

    logits_stage(0)
    for n in range(len(tasks)):
        if n + 1 < len(tasks):
            logits_stage(n + 1)
        update_stage(n)


def _swa(proj, bkt, rel_bias, sinks, batch, seq):
    nt = seq // RB_A
    sub = RB_A // TQ_A
    kv_blk = COL_AKV // (2 * KV_A)
    return _call(
        functools.partial(_swa_kernel, present=tuple(int(b) for b in np.unique(bkt))),
        grid=(batch, nt),
        in_specs=[
            pl.BlockSpec((RB_A, A_WIDTH), lambda b, i: (b * nt + i, COL_AQ // A_WIDTH)),
            pl.BlockSpec((TQ_A, 2 * KV_A), lambda b, i: (jnp.maximum((b * nt + i) * sub - 1, 0), kv_blk)),
            pl.BlockSpec((RB_A, 2 * KV_A), lambda b, i: (b * nt + i, kv_blk)),
            pl.BlockSpec((KEYS_A, TQ_A), lambda b, i: (0, 0)),
            pl.BlockSpec(memory_space=pltpu.SMEM),
            pl.BlockSpec(memory_space=pltpu.SMEM),
        ],
        out_spec=pl.BlockSpec((RB_A, A_WIDTH), lambda b, i: (b * nt + i, 0)),
        out_shape=jax.ShapeDtypeStruct((batch * seq, A_WIDTH), BF16),
        scratch_shapes=[
            pltpu.VMEM((2, A_KV_HEADS, KEYS_A, COLS_A), F32),
            pltpu.VMEM((A_KV_HEADS, SUBLANES, COLS_A), F32),
            pltpu.VMEM((2, 2, KEYS_A, LANES), BF16),
            pltpu.VMEM((2, A_HEAD_DIM + PACKED, KEYS_A), BF16),
            pltpu.VMEM((2, KEYS_A, CHUNK_PAIRS_A * TQ_A), F32),
            pltpu.VMEM((2, SUBLANES, CHUNK_PAIRS_A * TQ_A), F32),
            pltpu.VMEM((2, A_HEAD_DIM, CHUNK_PAIRS_A * TQ_A), F32),
        ],
        semantics=("arbitrary", "arbitrary"),
        name="swa",
        operands=[proj, proj, proj, jnp.asarray(bkt), rel_bias, sinks],
    )


def _dsa_kernel(iq_ref, iwq_ref, ik_ref, q_ref, k_ref, v_ref, bkt_ref, rb_ref, o_ref,
                sc_ref, scb_ref, d16_ref, mb_ref, tab_ref, tri_ref, vt_ref, acc_ref, m_ref,
                sa_ref, smaxa_ref, sb_ref, smaxb_ref, *, topk, seq, present):
    i = pl.program_id(1)
    nkt = i + 1
    int32, int16 = jnp.int32, jnp.int16
    grp = TK_B // SUBLANES

    def hcols(h):
        return slice(h * B_HEAD_DIM, (h + 1) * B_HEAD_DIM)

    def ktile(kt):
        return pl.ds(pl.multiple_of(kt * TK_B, TK_B), TK_B)

    def rows3(a):
        return a.reshape(a.shape[0] // SUBLANES, SUBLANES, TQ_B)

    def all_rows(a, op):
        return jnp.broadcast_to(op(a, axis=0, keepdims=True), a.shape)

    @pl.when((pl.program_id(0) == 0) & (i == 0))
    def _():
        far_bucket = N_BUCKETS // 2 - 1
        for d in range(NEAR_B):
            for h0 in range(0, B_HEADS, 2):
                tabs = _bias_tables(bkt_ref[d], present[d], rb_ref, [A_HEADS + h0, A_HEADS + h0 + 1],
                                    sub_row=far_bucket)
                tab_ref[d, h0] = tabs[0]
                tab_ref[d, h0 + 1] = tabs[1]
        tab_ref[NEAR_B] = jnp.zeros(tab_ref.shape[1:], F32)
        r = lax.broadcasted_iota(int32, (TK_B, TK_B), 0)
        c = lax.broadcasted_iota(int32, (TK_B, TK_B), 1)
        tri_ref[...] = jnp.where(c < r, 1.0, 0.0).astype(BF16)

    @pl.when(i == 0)
    def _():
        r = lax.broadcasted_iota(int32, (B_HEAD_DIM, B_HEAD_DIM), 0)
        c = lax.broadcasted_iota(int32, (B_HEAD_DIM, B_HEAD_DIM), 1)
        eye = jnp.where(r == c, 1.0, 0.0).astype(BF16)

        def body(kt, carry):
            for h in range(B_HEADS):
                vt_ref[kt, h] = _nt_dot(eye, v_ref[ktile(kt), hcols(h)]).astype(BF16)
            return carry

        lax.fori_loop(0, seq // TK_B, body, 0)

    key_chunk = lax.broadcasted_iota(int32, (TK_B, TQ_B), 0) // CHUNK
    qry_chunk = lax.broadcasted_iota(int32, (TK_B, TQ_B), 1) // CHUNK
    adm_diag = key_chunk <= qry_chunk

    r = lax.broadcasted_iota(int32, (IDX_HEADS * SUBLANES, LANES), 0)
    c = lax.broadcasted_iota(int32, (IDX_HEADS * SUBLANES, LANES), 1)
    pick_w = jnp.where(c == IDX_DIM + r // SUBLANES, 1.0, 0.0).astype(BF16)
    w_all = _nt_dot(pick_w, iwq_ref[...]) * (IDX_HEADS ** -0.5 * IDX_DIM ** -0.5)

    def for_tiles(n, body, unroll=2):
        def group(j, carry):
            for k in range(unroll):
                body(unroll * j + k)
            return carry

        def single(kt, carry):
            body(kt)
            return carry

        lax.fori_loop(0, n // unroll, group, 0)
        lax.fori_loop(n - n % unroll, n, single, 0)

    def score_tile(kt):
        ikt = ik_ref[ktile(kt), 0:IDX_DIM]
        sc = jnp.zeros((grp, SUBLANES, TQ_B), F32)
        for h in range(IDX_HEADS):
            x = _nt_dot(ikt, iq_ref[:, h * IDX_DIM:(h + 1) * IDX_DIM])
            sc = sc + w_all[h * SUBLANES:(h + 1) * SUBLANES][None] * jnp.maximum(rows3(x), 0.0)
        store_score(kt, sc.reshape(TK_B, TQ_B))

    def store_score(kt, sc):
        sc_ref[kt] = sc
        scb_ref[kt] = sc.astype(BF16)

    for_tiles(nkt, score_tile, unroll=4)
    store_score(i, jnp.where(adm_diag, sc_ref[i], -jnp.inf))

    def key_to_f32(key):
        return pltpu.bitcast(jnp.where(key >= 0, key, INT_MIN - key), F32)

    def over_tiles(tile_count, zero):
        def pair(j, acc):
            return acc + (tile_count(2 * j) + tile_count(2 * j + 1))
        acc = lax.fori_loop(0, nkt // 2, pair, zero)
        return lax.cond(nkt % 2 == 1, lambda a: a + tile_count(nkt - 1), lambda a: a, acc)

    def tree_sum(parts):
        while len(parts) > 1:
            parts = [parts[n] + parts[n + 1] for n in range(0, len(parts), 2)]
        return parts[0]

    def count_bf16(cand):
        def tile_count(kt):
            blk = scb_ref[kt].reshape(TK_B // PACKED, PACKED, TQ_B)
            return tree_sum([jnp.where(blk[g] >= cand, jnp.ones((), int16), jnp.zeros((), int16))
                             for g in range(TK_B // PACKED)])
        acc = over_tiles(tile_count, jnp.zeros((PACKED, TQ_B), int16))
        return all_rows(acc.astype(int32), jnp.sum)

    def count_f32(pred):
        def tile_count(kt):
            blk = rows3(sc_ref[kt])
            return tree_sum([jnp.where(pred(blk[g]), 1, 0) for g in range(grp)])
        return all_rows(over_tiles(tile_count, jnp.zeros((SUBLANES, TQ_B), int32)), jnp.sum)

    def coarse_pass(p, u):
        bit = jnp.left_shift(jnp.int32(1), 15 - p)
        cand = key_to_f32(((u | bit) - HALF) << 16).astype(BF16)
        return jnp.where(count_bf16(cand) >= topk, u | bit, u)

    u = lax.fori_loop(0, 16, coarse_pass, jnp.zeros((PACKED, TQ_B), int32))
    coarse_key = ((u - HALF) << 16)[:SUBLANES]

    lo = jnp.maximum(coarse_key - HALF, KEY_NEG_INF)
    hi = jnp.minimum(coarse_key, KEY_POS_INF - 2 * HALF) + 2 * HALF

    def fine_pass(p, lohi):
        lo, hi = lohi
        mid = lo + ((hi - lo) >> 1)
        cand = key_to_f32(mid)
        ok = count_f32(lambda blk: blk >= cand) >= topk
        return jnp.where(ok, mid, lo), jnp.where(ok, hi, mid)

    lo, hi = lax.fori_loop(0, PRE_PASSES, fine_pass, (lo, hi))

    mag = jnp.minimum(jnp.abs(lo), jnp.abs(hi))
    around_zero = ((lo <= 0) & (hi >= 0)) | (mag >> F32_MANT_BITS == 0)
    exp_field = jnp.where(around_zero, F32_MANT_BITS + 1, mag >> F32_MANT_BITS)
    any_irregular = jnp.max(jnp.where(exp_field <= F32_MANT_BITS, 1, 0)) > 0

    def finish_f32(lohi):
        lo, hi = lax.fori_loop(PRE_PASSES, FINE_PASSES, fine_pass, lohi)
        tau_f = key_to_f32(lo)
        return tau_f, count_f32(lambda blk: blk > tau_f)

    def finish_int16(lohi):
        lo, hi = lohi
        centre = jnp.where(around_zero, 0.0, key_to_f32(lo + ((hi - lo) >> 1)))
        ulp = pltpu.bitcast((exp_field - F32_MANT_BITS) << F32_MANT_BITS, F32)
        inv_ulp = pltpu.bitcast((2 * F32_EXP_BIAS + F32_MANT_BITS - exp_field) << F32_MANT_BITS, F32)

        def build(kt):
            d = (rows3(sc_ref[kt]) - centre[None]) * inv_ulp[None]
            d = jnp.clip(d, -HALF, HALF - 1).reshape(TK_B, TQ_B)
            d16_ref[kt] = d.astype(int32).astype(int16)

        for_tiles(nkt, build)

        def count_d16(cand):
            def tile_count(kt):
                blk = d16_ref[kt].reshape(TK_B // PACKED, PACKED, TQ_B)
                return tree_sum([jnp.where(blk[g] >= cand, jnp.ones((), int16), jnp.zeros((), int16))
                                 for g in range(TK_B // PACKED)])
            acc = over_tiles(tile_count, jnp.zeros((PACKED, TQ_B), int16))
            return all_rows(acc.astype(int32), jnp.sum)

        def d_pass(p, u):
            bit = jnp.left_shift(jnp.int32(1), 15 - p)
            cand = ((u | bit) - HALF).astype(int16)
            return jnp.where(count_d16(cand) >= topk, u | bit, u)

        t = lax.fori_loop(0, 16, d_pass, jnp.zeros((PACKED, TQ_B), int32)) - HALF
        above = count_d16((t + 1).astype(int16))
        return centre + t[:SUBLANES].astype(F32) * ulp, above[:SUBLANES]

    tau2d, above = lax.cond(any_irregular, finish_f32, finish_int16, (lo, hi))
    tau = tau2d[None]

    need = (topk - above).astype(F32)[None]
    ones_l = jnp.ones((2 * SUBLANES, TK_B), BF16)

    def mask_tile(kt, run):
        blk = rows3(sc_ref[kt])
        eq = blk == tau
        eqf = jnp.where(eq, 1.0, 0.0).reshape(TK_B, TQ_B).astype(BF16)
        rank = rows3(jnp.dot(tri_ref[...], eqf, preferred_element_type=F32)) + run[None]
        sel = (blk > tau) | (eq & (rank < need))
        mb_ref[kt] = jnp.where(sel, 0.0, NEG).reshape(TK_B, TQ_B)
        return run + jnp.dot(ones_l, eqf, preferred_element_type=F32)[:SUBLANES]

    def mask_group(j, run):
        for k in range(4):
            run = mask_tile(4 * j + k, run)
        return run

    run = lax.fori_loop(0, nkt // 4, mask_group, jnp.zeros((SUBLANES, TQ_B), F32))
    lax.fori_loop(nkt - nkt % 4, nkt, mask_tile, run)

    mb_ref[i] = jnp.where(adm_diag, mb_ref[i], NEG)

    m_ref[...] = jnp.full(m_ref.shape, NEG, F32)
    acc_ref[...] = jnp.zeros(acc_ref.shape, F32)
    ones_rows = jnp.ones((PACKED, TK_B), BF16)

    buf_a, buf_b = (sa_ref, smaxa_ref), (sb_ref, smaxb_ref)

    def logits_stage(kt, buf, near):
        s_ref, smax_ref = buf
        mb = mb_ref[kt]
        for h in range(B_HEADS):
            s = _nt_dot(k_ref[ktile(kt), hcols(h)], q_ref[:, hcols(h)]) + mb
            if near:
                s = s + tab_ref[jnp.minimum(i - kt, NEAR_B), h]
            s_ref[h] = s
            smax_ref[h] = all_rows(jnp.max(rows3(s), axis=0), jnp.max)

    def update_stage(kt, buf):
        s_ref, smax_ref = buf
        for h in range(B_HEADS):
            m_old = m_ref[h]
            m_new = jnp.maximum(m_old, smax_ref[h])
            alpha = jnp.exp2(m_old - m_new)
            p = jnp.exp2(rows3(s_ref[h]) - m_new[None]).reshape(TK_B, TQ_B).astype(BF16)
            vaug = jnp.concatenate([vt_ref[kt, h], ones_rows], axis=0)
            pv = jnp.dot(vaug, p, preferred_element_type=F32)
            acc_ref[h] = (rows3(acc_ref[h]) * alpha[None] + rows3(pv)).reshape(acc_ref.shape[1:])
            m_ref[h] = m_new

    def pair_body(j, carry, near):
        logits_stage(2 * j + 1, buf_b, near)
        update_stage(2 * j, buf_a)
        logits_stage(2 * j + 2, buf_a, near)
        update_stage(2 * j + 1, buf_b)
        return carry

    near_lo = jnp.maximum(i - (NEAR_B - 1), 0)
    far_pairs = jnp.maximum(near_lo - 1, 0) // 2
    logits_stage(0, buf_a, True)
    lax.fori_loop(0, far_pairs, functools.partial(pair_body, near=False), 0)
    lax.fori_loop(far_pairs, (nkt - 1) // 2, functools.partial(pair_body, near=True), 0)

    @pl.when(nkt % 2 == 0)
    def _():
        logits_stage(nkt - 1, buf_b, True)
        update_stage(nkt - 2, buf_a)
        update_stage(nkt - 1, buf_b)

    @pl.when(nkt % 2 == 1)
    def _():
        update_stage(nkt - 1, buf_a)

    for h in range(B_HEADS):
        num = rows3(acc_ref[h, 0:B_HEAD_DIM, :])
        den = acc_ref[h, B_HEAD_DIM:B_HEAD_DIM + SUBLANES, :]
        out_t = (num / den[None]).reshape(B_HEAD_DIM, TQ_B)
        o_ref[:, h * B_HEAD_DIM:(h + 1) * B_HEAD_DIM] = out_t.T.astype(BF16)


def _dsa(proj, bkt, rel_bias, batch, seq):
    nt = seq // TQ_B
    topk = min(TOPK_MAX, seq // 4)

    def q_spec(width, col):
        return pl.BlockSpec((TQ_B, width), lambda b, i: (b * nt + i, col // width))

    def seq_spec(width, col):
        return pl.BlockSpec((seq, width), lambda b, i: (b, col // width))

    return _call(
        functools.partial(_dsa_kernel, topk=topk, seq=seq,
                          present=tuple(tuple(int(b) for b in np.unique(t)) for t in bkt)),
        grid=(batch, nt),
        in_specs=[
            q_spec(IDX_HEADS * IDX_DIM, COL_IQ),
            q_spec(LANES, COL_IKW),
            seq_spec(LANES, COL_IKW),
            q_spec(B_WIDTH, COL_BQ), seq_spec(B_WIDTH, COL_BK), seq_spec(B_WIDTH, COL_BV),
            pl.BlockSpec((NEAR_B, TK_B, TQ_B), lambda b, i: (0, 0, 0)),
            pl.BlockSpec(memory_space=pltpu.SMEM),
        ],
        out_spec=pl.BlockSpec((TQ_B, B_WIDTH), lambda b, i: (b * nt + i, 0)),
        out_shape=jax.ShapeDtypeStruct((batch * seq, B_WIDTH), BF16),
        scratch_shapes=[
            pltpu.VMEM((nt, TK_B, TQ_B), F32),
            pltpu.VMEM((nt, TK_B, TQ_B), BF16),
            pltpu.VMEM((nt, TK_B, TQ_B), jnp.int16),
            pltpu.VMEM((nt, TK_B, TQ_B), F32),
            pltpu.VMEM((NEAR_B + 1, B_HEADS, TK_B, TQ_B), F32),
            pltpu.VMEM((TK_B, TK_B), BF16),
            pltpu.VMEM((seq // TK_B, B_HEADS, B_HEAD_DIM, TK_B), BF16),
            pltpu.VMEM((B_HEADS, B_HEAD_DIM + PACKED, TQ_B), F32),
            pltpu.VMEM((B_HEADS, SUBLANES, TQ_B), F32),
            pltpu.VMEM((B_HEADS, TK_B, TQ_B), F32),
            pltpu.VMEM((B_HEADS, SUBLANES, TQ_B), F32),
            pltpu.VMEM((B_HEADS, TK_B, TQ_B), F32),
            pltpu.VMEM((B_HEADS, SUBLANES, TQ_B), F32),
        ],
        semantics=("arbitrary", "arbitrary"),
        name="dsa",
        operands=[proj, proj, proj, proj, proj, proj, jnp.asarray(bkt), rel_bias],
    )


def _outproj_kernel(oa_ref, ob_ref, cq_ref, mkv_ref, gate_ref, x_ref, w32_ref, g_ref, o_ref, w_ref):
    @pl.when(pl.program_id(0) == 0)
    def _():
        w_ref[...] = w32_ref[...].astype(BF16)

    ones_m = jnp.ones((N_MEM, C_HEAD_DIM), BF16)
    for r in range(TM_OUT // SUB_OUT):
        rows = slice(r * SUB_OUT, (r + 1) * SUB_OUT)
        att = [oa_ref[rows, :].astype(F32), ob_ref[rows, :].astype(F32)]
        for h in range(C_HEADS):
            hs = slice(h * C_HEAD_DIM, (h + 1) * C_HEAD_DIM)
            s = _nt_dot(cq_ref[rows, hs], mkv_ref[:, hs])
            p = jnp.exp2(s - jnp.max(s, axis=1, keepdims=True)).astype(BF16)
            vaug = jnp.concatenate([mkv_ref[:, C_WIDTH + h * C_HEAD_DIM:C_WIDTH + (h + 1) * C_HEAD_DIM], ones_m],
                                   axis=1)
            pv = jnp.dot(p, vaug, preferred_element_type=F32)
            att.append(pv[:, :C_HEAD_DIM] / pv[:, C_HEAD_DIM:])
        y = (jnp.concatenate(att, axis=1) * gate_ref[rows, :].astype(F32)).astype(BF16)
        h_new = x_ref[rows, :] + jnp.dot(y, w_ref[...], preferred_element_type=F32)
        ms = jnp.mean(h_new * h_new, axis=-1, keepdims=True)
        o_ref[rows, :] = h_new * lax.rsqrt(ms + EPS) * g_ref[...]


def _outproj(oa, ob, proj, mkv, x2d, w, g, seq):
    m = x2d.shape[0]
    steps_per_seq = seq // TM_OUT
    return _call(
        _outproj_kernel,
        grid=(m // TM_OUT,),
        in_specs=[
            pl.BlockSpec((TM_OUT, A_WIDTH), lambda i: (i, 0)),
            pl.BlockSpec((TM_OUT, B_WIDTH), lambda i: (i, 0)),
            pl.BlockSpec((TM_OUT, C_WIDTH), lambda i: (i, COL_CQ // C_WIDTH)),
            pl.BlockSpec((N_MEM, 2 * C_WIDTH), lambda i: (i // steps_per_seq, 0)),
            pl.BlockSpec((TM_OUT, MIX_WIDTH), lambda i: (i, COL_GATE // MIX_WIDTH)),
            pl.BlockSpec((TM_OUT, D_MODEL), lambda i: (i, 0)),
            pl.BlockSpec((MIX_WIDTH, D_MODEL), lambda i: (0, 0), pipeline_mode=pl.Buffered(1)),
            pl.BlockSpec((1, D_MODEL), lambda i: (0, 0)),
        ],
        out_spec=pl.BlockSpec((TM_OUT, D_MODEL), lambda i: (i, 0)),
        out_shape=jax.ShapeDtypeStruct((m, D_MODEL), F32),
        scratch_shapes=[pltpu.VMEM((MIX_WIDTH, D_MODEL), BF16)],
        semantics=("arbitrary",),
        name="outproj",
        operands=[oa, ob, proj, mkv, proj, x2d, w, g],
    )


def _col_scale():
    cs = np.ones((1, PROJ_W), np.float32)
    cs[0, COL_AQ:COL_AQ + A_WIDTH] = A_HEAD_DIM ** -0.5 * LOG2E
    cs[0, COL_BQ:COL_BQ + B_WIDTH] = B_HEAD_DIM ** -0.5 * LOG2E
    cs[0, COL_CQ:COL_CQ + C_WIDTH] = C_HEAD_DIM ** -0.5 * LOG2E
    return jnp.asarray(cs)


def kernel(x, mem, g_norm, w_in, sinks, rel_bias, g_mem, w_mem_kv, w_out, g_final):
    batch, seq, _ = x.shape
    assert w_in.shape[0] == 1, "the out-projection kernel fuses the final norm of a single-layer trunk"
    kk = np.arange(2 * TQ_A)[:, None]
    qq = np.arange(TQ_A)[None, :]
    bkt_a = _t5_bucket_np(kk - TQ_A - qq)
    kk = np.arange(TK_B)[None, :, None]
    qq = np.arange(TQ_B)[None, None, :]
    d = np.arange(NEAR_B)[:, None, None]
    bkt_b = _t5_bucket_np(kk - qq - TK_B * d)

    h = x.reshape(batch * seq, D_MODEL)
    mem2d = mem.reshape(batch * N_MEM, D_MODEL)
    w_t = jnp.transpose(w_in, (2, 0, 1)).reshape(IN_WIDTH * KB_IN, LANES)
    proj = _inproj(h, g_norm[0].reshape(1, D_MODEL), _wprep(w_t), _col_scale())
    mkv = _memkv(mem2d, g_mem[0].reshape(1, D_MODEL), w_mem_kv[0])
    oa = _swa(proj, bkt_a, rel_bias, sinks[0], batch, seq)
    ob = _dsa(proj, bkt_b, rel_bias, batch, seq)
    out = _outproj(oa, ob, proj, mkv, h, w_out[0], g_final.reshape(1, D_MODEL), seq)
    return out.reshape(batch, seq, D_MODEL)
```

```python
import functools
import math

import numpy as np
import jax
import jax.numpy as jnp
from jax import lax
from jax.experimental import pallas as pl
from jax.experimental.pallas import tpu as pltpu

D_MODEL = 2048
CHUNK = 64
N_MEM = 256
EPS = 1e-6
A_HEADS = 16
A_KV_HEADS = 2
A_HEAD_DIM = 64
WINDOW_CHUNKS = 2
A_WIDTH = A_HEADS * A_HEAD_DIM
B_HEADS = 4
B_HEAD_DIM = 128
B_WIDTH = B_HEADS * B_HEAD_DIM
IDX_HEADS = 4
IDX_DIM = 64
TOPK_MAX = 256
C_HEADS = 4
C_HEAD_DIM = 128
C_WIDTH = C_HEADS * C_HEAD_DIM
MIX_WIDTH = A_WIDTH + B_WIDTH + C_WIDTH
N_BUCKETS = 32
MAX_DISTANCE = 1024
KV_A = A_KV_HEADS * A_HEAD_DIM
SPLIT_SIZES = (A_WIDTH, KV_A, KV_A, B_WIDTH, B_WIDTH, B_WIDTH,
               IDX_HEADS * IDX_DIM, IDX_DIM, IDX_HEADS, C_WIDTH, MIX_WIDTH)
IN_WIDTH = sum(SPLIT_SIZES)

F32 = jnp.float32
BF16 = jnp.bfloat16
LOG2E = math.log2(math.e)
NEG = -1e30
INT_MIN = -(2 ** 31)
LANES = 128
SUBLANES = 8
PACKED = 16
HALF = 1 << 15
KEY_POS_INF = 0x7F800000
KEY_NEG_INF = -KEY_POS_INF
FINE_PASSES = 17
PRE_PASSES = 2
F32_MANT_BITS = 23
F32_EXP_BIAS = 127

(SRC_AQ, SRC_AK, SRC_AV, SRC_BQ, SRC_BK, SRC_BV,
 SRC_IQ, SRC_IK, SRC_IW, SRC_CQ, SRC_GATE) = (int(c) for c in np.cumsum((0,) + SPLIT_SIZES)[:-1])
COL_GATE = 0
COL_AQ = COL_GATE + MIX_WIDTH
COL_BQ = COL_AQ + A_WIDTH
COL_BK = COL_BQ + B_WIDTH
COL_BV = COL_BK + B_WIDTH
COL_CQ = COL_BV + B_WIDTH
COL_AKV = COL_CQ + C_WIDTH
COL_IQ = COL_AKV + 2 * KV_A
COL_IKW = COL_IQ + IDX_HEADS * IDX_DIM
PROJ_W = 6144
SEGMENTS = ((COL_GATE, SRC_GATE, MIX_WIDTH), (COL_AQ, SRC_AQ, A_WIDTH), (COL_BQ, SRC_BQ, B_WIDTH),
            (COL_BK, SRC_BK, B_WIDTH), (COL_BV, SRC_BV, B_WIDTH), (COL_CQ, SRC_CQ, C_WIDTH),
            (COL_AKV, SRC_AK, 2 * KV_A), (COL_IQ, SRC_IQ, IDX_HEADS * IDX_DIM), (COL_IKW, SRC_IK, 256))
KB_IN = D_MODEL // LANES

TC_PREP = 256
PREFETCH_PREP = 4
TM_IN, TN_IN = 1024, 2048
SUB_IN = 256
TQ_A = 128
RB_A = 1024
TQ_B = 256
TK_B = 256
NEAR_B = 4
TM_OUT = 512
SUB_OUT = 256
V7X_VMEM_BYTES = 64 * 1024 * 1024
VMEM_REQUEST = V7X_VMEM_BYTES * 7 // 8


def _vmem_limit(in_specs, operands, out_spec, out_shape, scratch_shapes=()):
    def buffers(spec, dtype):
        if spec.block_shape is None:
            return 0
        elems = math.prod(int(getattr(d, "block_size", d)) for d in spec.block_shape)
        count = spec.pipeline_mode.buffer_count if spec.pipeline_mode is not None else 2
        return elems * jnp.dtype(dtype).itemsize * count

    total = sum(buffers(s, a.dtype) for s, a in zip(in_specs, operands)) + buffers(out_spec, out_shape.dtype)
    total += sum(math.prod(s.shape) * jnp.dtype(s.dtype).itemsize for s in scratch_shapes)
    assert total <= VMEM_REQUEST, (total, VMEM_REQUEST)
    return VMEM_REQUEST


def _call(kernel, *, name, grid, semantics, in_specs, operands, out_spec, out_shape, scratch_shapes=()):
    return pl.pallas_call(
        kernel,
        grid=grid,
        in_specs=in_specs,
        out_specs=out_spec,
        out_shape=out_shape,
        scratch_shapes=list(scratch_shapes),
        compiler_params=pltpu.CompilerParams(
            dimension_semantics=semantics,
            vmem_limit_bytes=_vmem_limit(in_specs, operands, out_spec, out_shape, scratch_shapes)),
        name=name,
    )(*operands)


def _t5_bucket_np(rel):
    nb = N_BUCKETS // 2
    max_exact = nb // 2
    side = np.where(rel > 0, nb, 0)
    n = np.abs(rel)
    nf = np.maximum(n, max_exact).astype(np.float32)
    large = max_exact + (np.log(nf / max_exact) / math.log(MAX_DISTANCE / max_exact)
                         * (nb - max_exact)).astype(np.int32)
    large = np.minimum(large, nb - 1)
    return (side + np.where(n < max_exact, n, large)).astype(np.int32)


def _nt_dot(a, b):
    return lax.dot_general(a, b, (((1,), (1,)), ((), ())), preferred_element_type=F32)


def _bias_tables(bucket, present, rb_ref, cols, sub_row=None):
    accs = [jnp.zeros(bucket.shape, F32) for _ in cols]
    for b in present:
        hit = bucket == b
        for n, col in enumerate(cols):
            val = rb_ref[b, col]
            if sub_row is not None:
                val = val - rb_ref[sub_row, col]
            accs[n] = jnp.where(hit, val * LOG2E, accs[n])
    return accs


def _wprep_kernel(src_ref, w_hbm, o_ref, buf_ref, sem_ref):
    t = pl.program_id(0)
    steps = pl.num_programs(0)

    def fetch(step, kb):
        col = jnp.maximum(src_ref[step], 0)
        slot = step % PREFETCH_PREP
        return pltpu.make_async_copy(w_hbm.at[pl.ds(col, TC_PREP), kb, :], buf_ref.at[slot, kb], sem_ref.at[slot])

    def start(step):
        for kb in range(KB_IN):
            fetch(step, kb).start()

    @pl.when(t == 0)
    def _():
        for k in range(PREFETCH_PREP - 1):
            start(k)

    @pl.when(t + PREFETCH_PREP - 1 < steps)
    def _():
        start(t + PREFETCH_PREP - 1)

    for kb in range(KB_IN):
        fetch(t, kb).wait()
    w_ref = buf_ref.at[t % PREFETCH_PREP]

    @pl.when(src_ref[t] >= 0)
    def _():
        for kb in range(KB_IN):
            o_ref[:, kb * LANES:(kb + 1) * LANES] = w_ref[kb].astype(BF16)

    @pl.when(src_ref[t] < 0)
    def _():
        o_ref[...] = jnp.zeros(o_ref.shape, BF16)


def _wprep(w_t):
    src = np.full((PROJ_W // TC_PREP,), -1, np.int32)
    for dst_col, src_col, width in SEGMENTS:
        for n in range(width // TC_PREP):
            src[dst_col // TC_PREP + n] = src_col + n * TC_PREP
    steps = PROJ_W // TC_PREP
    assert steps >= PREFETCH_PREP
    in_specs = [pl.BlockSpec(memory_space=pl.ANY)]
    out_spec = pl.BlockSpec((TC_PREP, D_MODEL), lambda t, src: (t, 0))
    out_shape = jax.ShapeDtypeStruct((PROJ_W, D_MODEL), BF16)
    ring = pltpu.VMEM((PREFETCH_PREP, KB_IN, TC_PREP, LANES), F32)
    return pl.pallas_call(
        _wprep_kernel,
        grid_spec=pltpu.PrefetchScalarGridSpec(
            num_scalar_prefetch=1, grid=(steps,), in_specs=in_specs, out_specs=out_spec,
            scratch_shapes=[ring, pltpu.SemaphoreType.DMA((PREFETCH_PREP,))]),
        out_shape=out_shape,
        compiler_params=pltpu.CompilerParams(
            dimension_semantics=("arbitrary",),
            vmem_limit_bytes=_vmem_limit(in_specs, [w_t], out_spec, out_shape, [ring])),
        name="wprep",
    )(jnp.asarray(src), w_t)


def _inproj_kernel(x_ref, g_ref, wt_ref, cs_ref, o_ref, hn_ref):
    assert COL_GATE == 0 and MIX_WIDTH == TN_IN
    j = pl.program_id(1)

    @pl.when(j == 0)
    def _():
        for r in range(TM_IN // SUB_IN):
            rows = slice(r * SUB_IN, (r + 1) * SUB_IN)
            x = x_ref[rows, :]
            ms = jnp.mean(x * x, axis=-1, keepdims=True)
            hn = (x * lax.rsqrt(ms + EPS) * g_ref[...]).astype(BF16)
            hn_ref[rows, :] = hn
            gate = _nt_dot(hn, wt_ref[...])
            o_ref[rows, :] = (gate / (1.0 + jnp.exp(-gate))).astype(BF16)

    @pl.when(j > 0)
    def _():
        o_ref[...] = (_nt_dot(hn_ref[...], wt_ref[...]) * cs_ref[...]).astype(BF16)


def _inproj(x2d, g, wt, cs):
    m = x2d.shape[0]
    return _call(
        _inproj_kernel,
        grid=(m // TM_IN, PROJ_W // TN_IN),
        in_specs=[
            pl.BlockSpec((TM_IN, D_MODEL), lambda i, j: (i, 0)),
            pl.BlockSpec((1, D_MODEL), lambda i, j: (0, 0)),
            pl.BlockSpec((TN_IN, D_MODEL), lambda i, j: (j, 0)),
            pl.BlockSpec((1, TN_IN), lambda i, j: (0, j)),
        ],
        out_spec=pl.BlockSpec((TM_IN, TN_IN), lambda i, j: (i, j)),
        out_shape=jax.ShapeDtypeStruct((m, PROJ_W), BF16),
        scratch_shapes=[pltpu.VMEM((TM_IN, D_MODEL), BF16)],
        semantics=("parallel", "arbitrary"),
        name="inproj",
        operands=[x2d, g, wt, cs],
    )


def _memkv_kernel(m_ref, g_ref, w_ref, o_ref):
    x = m_ref[...]
    ms = jnp.mean(x * x, axis=-1, keepdims=True)
    hn = (x * lax.rsqrt(ms + EPS) * g_ref[...]).astype(BF16)
    o_ref[...] = jnp.dot(hn, w_ref[...].astype(BF16), preferred_element_type=F32).astype(BF16)


def _memkv(mem2d, g, w):
    m = mem2d.shape[0]
    return _call(
        _memkv_kernel,
        grid=(m // N_MEM,),
        in_specs=[
            pl.BlockSpec((N_MEM, D_MODEL), lambda i: (i, 0)),
            pl.BlockSpec((1, D_MODEL), lambda i: (0, 0)),
            pl.BlockSpec((D_MODEL, 2 * C_WIDTH), lambda i: (0, 0)),
        ],
        out_spec=pl.BlockSpec((N_MEM, 2 * C_WIDTH), lambda i: (i, 0)),
        out_shape=jax.ShapeDtypeStruct((m, 2 * C_WIDTH), BF16),
        semantics=("arbitrary",),
        name="memkv",
        operands=[mem2d, g, w],
    )


GRP_A = A_HEADS // A_KV_HEADS
PAIRS_A = GRP_A // 2
COLS_A = GRP_A * TQ_A
KEYS_A = 2 * TQ_A
CHUNK_PAIRS_A = 4


def _swa_kernel(q_ref, kvp_ref, kvc_ref, bkt_ref, rb_ref, sink_ref, o_ref,
                tab_ref, sinkv_ref, kp_ref, vt_ref, s_ref, m_ref, ot_ref, *, present):
    i = pl.program_id(1)

    @pl.when((pl.program_id(0) == 0) & (i == 0))
    def _():
        bkt = bkt_ref[...]
        kchunk = lax.broadcasted_iota(jnp.int32, bkt.shape, 0) // CHUNK
        qchunk = lax.broadcasted_iota(jnp.int32, bkt.shape, 1) // CHUNK
        allowed = (kchunk >= qchunk) & (kchunk <= qchunk + WINDOW_CHUNKS)
        has_prev = lax.broadcasted_iota(jnp.int32, bkt.shape, 0) >= TQ_A
        for g in range(A_KV_HEADS):
            for c0 in range(0, GRP_A, PAIRS_A):
                heads = [g * GRP_A + 2 * (col % PAIRS_A) + col // PAIRS_A for col in range(c0, c0 + PAIRS_A)]
                for n, t in enumerate(_bias_tables(bkt, present, rb_ref, heads)):
                    cols = slice((c0 + n) * TQ_A, (c0 + n + 1) * TQ_A)
                    t = jnp.where(allowed, t, NEG)
                    tab_ref[0, g, :, cols] = t
                    tab_ref[1, g, :, cols] = jnp.where(has_prev, t, NEG)
                    sinkv_ref[g, :, cols] = jnp.full((SUBLANES, TQ_A), sink_ref[heads[n]] * LOG2E, F32)

    width = CHUNK_PAIRS_A * TQ_A
    tasks = [(t, g, p0, parity) for t in range(RB_A // TQ_A) for g in range(A_KV_HEADS)
             for p0 in range(0, PAIRS_A, CHUNK_PAIRS_A) for parity in range(2)]

    def prepare(t, g):
        slot = (t * A_KV_HEADS + g) % 2
        if t == 0:
            kv_t = jnp.concatenate([kvp_ref[...], kvc_ref[0:TQ_A, :]], axis=0)
        else:
            kv_t = kvc_ref[(t - 1) * TQ_A:(t + 1) * TQ_A, :]
        kg = kv_t[:, g * A_HEAD_DIM:(g + 1) * A_HEAD_DIM]
        zeros = jnp.zeros((KEYS_A, A_HEAD_DIM), BF16)
        kp_ref[slot, 0] = jnp.concatenate([kg, zeros], axis=1)
        kp_ref[slot, 1] = jnp.concatenate([zeros, kg], axis=1)
        r = lax.broadcasted_iota(jnp.int32, (A_HEAD_DIM, KV_A), 0)
        c = lax.broadcasted_iota(jnp.int32, (A_HEAD_DIM, KV_A), 1)
        pick = jnp.where(c == g * A_HEAD_DIM + r, 1.0, 0.0).astype(BF16)
        vt = _nt_dot(pick, kv_t[:, KV_A:]).astype(BF16)
        vt_ref[slot] = jnp.concatenate([vt, jnp.ones((PACKED, KEYS_A), BF16)], axis=0)

    def chunk_cols(p0, parity):
        lo = (parity * PAIRS_A + p0) * TQ_A
        return slice(lo, lo + width)

    def logits_stage(n):
        t, g, p0, parity = tasks[n]
        if (p0, parity) == (0, 0):
            prepare(t, g)
        slot, buf, cols = (t * A_KV_HEADS + g) % 2, n % 2, chunk_cols(p0, parity)
        q_pairs = jnp.concatenate(
            [q_ref[t * TQ_A:(t + 1) * TQ_A, (g * PAIRS_A + p) * LANES:(g * PAIRS_A + p + 1) * LANES]
             for p in range(p0, p0 + CHUNK_PAIRS_A)], axis=0)
        variant = jnp.where(i == 0, 1, 0) if t == 0 else 0
        s = _nt_dot(kp_ref[slot, parity], q_pairs) + tab_ref[variant, g, :, cols]
        s_ref[buf] = s
        m = jnp.max(s.reshape(KEYS_A // SUBLANES, SUBLANES, width), axis=0)
        m_ref[buf] = jnp.maximum(jnp.broadcast_to(jnp.max(m, axis=0, keepdims=True), m.shape),
                                 sinkv_ref[g, :, cols])

    def update_stage(n):
        t, g, p0, parity = tasks[n]
        slot, buf, cols = (t * A_KV_HEADS + g) % 2, n % 2, chunk_cols(p0, parity)
        m = m_ref[buf]
        s = s_ref[buf].reshape(KEYS_A // SUBLANES, SUBLANES, width)
        p = jnp.exp2(s - m[None]).reshape(KEYS_A, width).astype(BF16)
        pv = jnp.dot(vt_ref[slot], p, preferred_element_type=F32)
        den = pv[A_HEAD_DIM:A_HEAD_DIM + SUBLANES] + jnp.exp2(sinkv_ref[g, :, cols] - m)
        out_t = pv[:A_HEAD_DIM].reshape(A_HEAD_DIM // SUBLANES, SUBLANES, width) / den[None]
        ot_ref[parity] = out_t.reshape(A_HEAD_DIM, width)
        if parity == 1:
            for k in range(CHUNK_PAIRS_A):
                blk = jnp.concatenate([ot_ref[0, :, k * TQ_A:(k + 1) * TQ_A],
                                       ot_ref[1, :, k * TQ_A:(k + 1) * TQ_A]], axis=0)
                lanes = slice((g * PAIRS_A + p0 + k) * LANES, (g * PAIRS_A + p0 + k + 1) * LANES)
                o_ref[t * TQ_A:(t + 1) * TQ_A, lanes] = blk.T.astype(BF16)

    logits_stage(0)
    for n in range(len(tasks)):
        if n + 1 < len(tasks):
            logits_stage(n + 1)
        update_stage(n)


def _swa(proj, bkt, rel_bias, sinks, batch, seq):
    nt = seq // RB_A
    sub = RB_A // TQ_A
    kv_blk = COL_AKV // (2 * KV_A)
    return _call(
        functools.partial(_swa_kernel, present=tuple(int(b) for b in np.unique(bkt))),
        grid=(batch, nt),
        in_specs=[
            pl.BlockSpec((RB_A, A_WIDTH), lambda b, i: (b * nt + i, COL_AQ // A_WIDTH)),
            pl.BlockSpec((TQ_A, 2 * KV_A), lambda b, i: (jnp.maximum((b * nt + i) * sub - 1, 0), kv_blk)),
            pl.BlockSpec((RB_A, 2 * KV_A), lambda b, i: (b * nt + i, kv_blk)),
            pl.BlockSpec((KEYS_A, TQ_A), lambda b, i: (0, 0)),
            pl.BlockSpec(memory_space=pltpu.SMEM),
            pl.BlockSpec(memory_space=pltpu.SMEM),
        ],
        out_spec=pl.BlockSpec((RB_A, A_WIDTH), lambda b, i: (b * nt + i, 0)),
        out_shape=jax.ShapeDtypeStruct((batch * seq, A_WIDTH), BF16),
        scratch_shapes=[
            pltpu.VMEM((2, A_KV_HEADS, KEYS_A, COLS_A), F32),
            pltpu.VMEM((A_KV_HEADS, SUBLANES, COLS_A), F32),
            pltpu.VMEM((2, 2, KEYS_A, LANES), BF16),
            pltpu.VMEM((2, A_HEAD_DIM + PACKED, KEYS_A), BF16),
            pltpu.VMEM((2, KEYS_A, CHUNK_PAIRS_A * TQ_A), F32),
            pltpu.VMEM((2, SUBLANES, CHUNK_PAIRS_A * TQ_A), F32),
            pltpu.VMEM((2, A_HEAD_DIM, CHUNK_PAIRS_A * TQ_A), F32),
        ],
        semantics=("arbitrary", "arbitrary"),
        name="swa",
        operands=[proj, proj, proj, jnp.asarray(bkt), rel_bias, sinks],
    )


def _dsa_kernel(iq_ref, iwq_ref, ik_ref, q_ref, k_ref, v_ref, bkt_ref, rb_ref, o_ref,
                sc_ref, scb_ref, d16_ref, mb_ref, tab_ref, tri_ref, vt_ref, acc_ref, m_ref,
                sa_ref, smaxa_ref, sb_ref, smaxb_ref, *, topk, seq, present):
    i = pl.program_id(1)
    nkt = i + 1
    int32, int16 = jnp.int32, jnp.int16
    grp = TK_B // SUBLANES

    def hcols(h):
        return slice(h * B_HEAD_DIM, (h + 1) * B_HEAD_DIM)

    def ktile(kt):
        return pl.ds(pl.multiple_of(kt * TK_B, TK_B), TK_B)

    def rows3(a):
        return a.reshape(a.shape[0] // SUBLANES, SUBLANES, TQ_B)

    def all_rows(a, op):
        return jnp.broadcast_to(op(a, axis=0, keepdims=True), a.shape)

    @pl.when((pl.program_id(0) == 0) & (i == 0))
    def _():
        far_bucket = N_BUCKETS // 2 - 1
        for d in range(NEAR_B):
            for h0 in range(0, B_HEADS, 2):
                tabs = _bias_tables(bkt_ref[d], present[d], rb_ref, [A_HEADS + h0, A_HEADS + h0 + 1],
                                    sub_row=far_bucket)
                tab_ref[d, h0] = tabs[0]
                tab_ref[d, h0 + 1] = tabs[1]
        tab_ref[NEAR_B] = jnp.zeros(tab_ref.shape[1:], F32)
        r = lax.broadcasted_iota(int32, (TK_B, TK_B), 0)
        c = lax.broadcasted_iota(int32, (TK_B, TK_B), 1)
        tri_ref[...] = jnp.where(c < r, 1.0, 0.0).astype(BF16)

    @pl.when(i == 0)
    def _():
        r = lax.broadcasted_iota(int32, (B_HEAD_DIM, B_HEAD_DIM), 0)
        c = lax.broadcasted_iota(int32, (B_HEAD_DIM, B_HEAD_DIM), 1)
        eye = jnp.where(r == c, 1.0, 0.0).astype(BF16)

        def body(kt, carry):
            for h in range(B_HEADS):
                vt_ref[kt, h] = _nt_dot(eye, v_ref[ktile(kt), hcols(h)]).astype(BF16)
            return carry

        lax.fori_loop(0, seq // TK_B, body, 0)

    key_chunk = lax.broadcasted_iota(int32, (TK_B, TQ_B), 0) // CHUNK
    qry_chunk = lax.broadcasted_iota(int32, (TK_B, TQ_B), 1) // CHUNK
    adm_diag = key_chunk <= qry_chunk

    r = lax.broadcasted_iota(int32, (IDX_HEADS * SUBLANES, LANES), 0)
    c = lax.broadcasted_iota(int32, (IDX_HEADS * SUBLANES, LANES), 1)
    pick_w = jnp.where(c == IDX_DIM + r // SUBLANES, 1.0, 0.0).astype(BF16)
    w_all = _nt_dot(pick_w, iwq_ref[...]) * (IDX_HEADS ** -0.5 * IDX_DIM ** -0.5)

    def for_tiles(n, body, unroll=2):
        def group(j, carry):
            for k in range(unroll):
                body(unroll * j + k)
            return carry

        def single(kt, carry):
            body(kt)
            return carry

        lax.fori_loop(0, n // unroll, group, 0)
        lax.fori_loop(n - n % unroll, n, single, 0)

    def score_tile(kt):
        ikt = ik_ref[ktile(kt), 0:IDX_DIM]
        sc = jnp.zeros((grp, SUBLANES, TQ_B), F32)
        for h in range(IDX_HEADS):
            x = _nt_dot(ikt, iq_ref[:, h * IDX_DIM:(h + 1) * IDX_DIM])
            sc = sc + w_all[h * SUBLANES:(h + 1) * SUBLANES][None] * jnp.maximum(rows3(x), 0.0)
        store_score(kt, sc.reshape(TK_B, TQ_B))

    def store_score(kt, sc):
        sc_ref[kt] = sc
        scb_ref[kt] = sc.astype(BF16)

    for_tiles(nkt, score_tile, unroll=4)
    store_score(i, jnp.where(adm_diag, sc_ref[i], -jnp.inf))

    def key_to_f32(key):
        return pltpu.bitcast(jnp.where(key >= 0, key, INT_MIN - key), F32)

    def over_tiles(tile_count, zero):
        def pair(j, acc):
            return acc + (tile_count(2 * j) + tile_count(2 * j + 1))
        acc = lax.fori_loop(0, nkt // 2, pair, zero)
        return lax.cond(nkt % 2 == 1, lambda a: a + tile_count(nkt - 1), lambda a: a, acc)

    def tree_sum(parts):
        while len(parts) > 1:
            parts = [parts[n] + parts[n + 1] for n in range(0, len(parts), 2)]
        return parts[0]

    def count_bf16(cand):
        def tile_count(kt):
            blk = scb_ref[kt].reshape(TK_B // PACKED, PACKED, TQ_B)
            return tree_sum([jnp.where(blk[g] >= cand, jnp.ones((), int16), jnp.zeros((), int16))
                             for g in range(TK_B // PACKED)])
        acc = over_tiles(tile_count, jnp.zeros((PACKED, TQ_B), int16))
        return all_rows(acc.astype(int32), jnp.sum)

    def count_f32(pred):
        def tile_count(kt):
            blk = rows3(sc_ref[kt])
            return tree_sum([jnp.where(pred(blk[g]), 1, 0) for g in range(grp)])
        return all_rows(over_tiles(tile_count, jnp.zeros((SUBLANES, TQ_B), int32)), jnp.sum)

    def coarse_pass(p, u):
        bit = jnp.left_shift(jnp.int32(1), 15 - p)
        cand = key_to_f32(((u | bit) - HALF) << 16).astype(BF16)
        return jnp.where(count_bf16(cand) >= topk, u | bit, u)

    u = lax.fori_loop(0, 16, coarse_pass, jnp.zeros((PACKED, TQ_B), int32))
    coarse_key = ((u - HALF) << 16)[:SUBLANES]

    lo = jnp.maximum(coarse_key - HALF, KEY_NEG_INF)
    hi = jnp.minimum(coarse_key, KEY_POS_INF - 2 * HALF) + 2 * HALF

    def fine_pass(p, lohi):
        lo, hi = lohi
        mid = lo + ((hi - lo) >> 1)
        cand = key_to_f32(mid)
        ok = count_f32(lambda blk: blk >= cand) >= topk
        return jnp.where(ok, mid, lo), jnp.where(ok, hi, mid)

    lo, hi = lax.fori_loop(0, PRE_PASSES, fine_pass, (lo, hi))

    mag = jnp.minimum(jnp.abs(lo), jnp.abs(hi))
    around_zero = ((lo <= 0) & (hi >= 0)) | (mag >> F32_MANT_BITS == 0)
    exp_field = jnp.where(around_zero, F32_MANT_BITS + 1, mag >> F32_MANT_BITS)
    any_irregular = jnp.max(jnp.where(exp_field <= F32_MANT_BITS, 1, 0)) > 0

    def finish_f32(lohi):
        lo, hi = lax.fori_loop(PRE_PASSES, FINE_PASSES, fine_pass, lohi)
        tau_f = key_to_f32(lo)
        return tau_f, count_f32(lambda blk: blk > tau_f)

    def finish_int16(lohi):
        lo, hi = lohi
        centre = jnp.where(around_zero, 0.0, key_to_f32(lo + ((hi - lo) >> 1)))
        ulp = pltpu.bitcast((exp_field - F32_MANT_BITS) << F32_MANT_BITS, F32)
        inv_ulp = pltpu.bitcast((2 * F32_EXP_BIAS + F32_MANT_BITS - exp_field) << F32_MANT_BITS, F32)

        def build(kt):
            d = (rows3(sc_ref[kt]) - centre[None]) * inv_ulp[None]
            d = jnp.clip(d, -HALF, HALF - 1).reshape(TK_B, TQ_B)
            d16_ref[kt] = d.astype(int32).astype(int16)

        for_tiles(nkt, build)

        def count_d16(cand):
            def tile_count(kt):
                blk = d16_ref[kt].reshape(TK_B // PACKED, PACKED, TQ_B)
                return tree_sum([jnp.where(blk[g] >= cand, jnp.ones((), int16), jnp.zeros((), int16))
                                 for g in range(TK_B // PACKED)])
            acc = over_tiles(tile_count, jnp.zeros((PACKED, TQ_B), int16))
            return all_rows(acc.astype(int32), jnp.sum)

        def d_pass(p, u):
            bit = jnp.left_shift(jnp.int32(1), 15 - p)
            cand = ((u | bit) - HALF).astype(int16)
            return jnp.where(count_d16(cand) >= topk, u | bit, u)

        t = lax.fori_loop(0, 16, d_pass, jnp.zeros((PACKED, TQ_B), int32)) - HALF
        above = count_d16((t + 1).astype(int16))
        return centre + t[:SUBLANES].astype(F32) * ulp, above[:SUBLANES]

    tau2d, above = lax.cond(any_irregular, finish_f32, finish_int16, (lo, hi))
    tau = tau2d[None]

    need = (topk - above).astype(F32)[None]
    ones_l = jnp.ones((2 * SUBLANES, TK_B), BF16)

    def mask_tile(kt, run):
        blk = rows3(sc_ref[kt])
        eq = blk == tau
        eqf = jnp.where(eq, 1.0, 0.0).reshape(TK_B, TQ_B).astype(BF16)
        rank = rows3(jnp.dot(tri_ref[...], eqf, preferred_element_type=F32)) + run[None]
        sel = (blk > tau) | (eq & (rank < need))
        mb_ref[kt] = jnp.where(sel, 0.0, NEG).reshape(TK_B, TQ_B)
        return run + jnp.dot(ones_l, eqf, preferred_element_type=F32)[:SUBLANES]

    def mask_group(j, run):
        for k in range(4):
            run = mask_tile(4 * j + k, run)
        return run

    run = lax.fori_loop(0, nkt // 4, mask_group, jnp.zeros((SUBLANES, TQ_B), F32))
    lax.fori_loop(nkt - nkt % 4, nkt, mask_tile, run)

    mb_ref[i] = jnp.where(adm_diag, mb_ref[i], NEG)

    m_ref[...] = jnp.full(m_ref.shape, NEG, F32)
    acc_ref[...] = jnp.zeros(acc_ref.shape, F32)
    ones_rows = jnp.ones((PACKED, TK_B), BF16)

    buf_a, buf_b = (sa_ref, smaxa_ref), (sb_ref, smaxb_ref)

    def logits_stage(kt, buf, near):
        s_ref, smax_ref = buf
        mb = mb_ref[kt]
        for h in range(B_HEADS):
            s = _nt_dot(k_ref[ktile(kt), hcols(h)], q_ref[:, hcols(h)]) + mb
            if near:
                s = s + tab_ref[jnp.minimum(i - kt, NEAR_B), h]
            s_ref[h] = s
            smax_ref[h] = all_rows(jnp.max(rows3(s), axis=0), jnp.max)

    def update_stage(kt, buf):
        s_ref, smax_ref = buf
        for h in range(B_HEADS):
            m_old = m_ref[h]
            m_new = jnp.maximum(m_old, smax_ref[h])
            alpha = jnp.exp2(m_old - m_new)
            p = jnp.exp2(rows3(s_ref[h]) - m_new[None]).reshape(TK_B, TQ_B).astype(BF16)
            vaug = jnp.concatenate([vt_ref[kt, h], ones_rows], axis=0)
            pv = jnp.dot(vaug, p, preferred_element_type=F32)
            acc_ref[h] = (rows3(acc_ref[h]) * alpha[None] + rows3(pv)).reshape(acc_ref.shape[1:])
            m_ref[h] = m_new

    def pair_body(j, carry, near):
        logits_stage(2 * j + 1, buf_b, near)
        update_stage(2 * j, buf_a)
        logits_stage(2 * j + 2, buf_a, near)
        update_stage(2 * j + 1, buf_b)
        return carry

    near_lo = jnp.maximum(i - (NEAR_B - 1), 0)
    far_pairs = jnp.maximum(near_lo - 1, 0) // 2
    logits_stage(0, buf_a, True)
    lax.fori_loop(0, far_pairs, functools.partial(pair_body, near=False), 0)
    lax.fori_loop(far_pairs, (nkt - 1) // 2, functools.partial(pair_body, near=True), 0)

    @pl.when(nkt % 2 == 0)
    def _():
        logits_stage(nkt - 1, buf_b, True)
        update_stage(nkt - 2, buf_a)
        update_stage(nkt - 1, buf_b)

    @pl.when(nkt % 2 == 1)
    def _():
        update_stage(nkt - 1, buf_a)

    for h in range(B_HEADS):
        num = rows3(acc_ref[h, 0:B_HEAD_DIM, :])
        den = acc_ref[h, B_HEAD_DIM:B_HEAD_DIM + SUBLANES, :]
        out_t = (num / den[None]).reshape(B_HEAD_DIM, TQ_B)
        o_ref[:, h * B_HEAD_DIM:(h + 1) * B_HEAD_DIM] = out_t.T.astype(BF16)


def _dsa(proj, bkt, rel_bias, batch, seq):
    nt = seq // TQ_B
    topk = min(TOPK_MAX, seq // 4)

    def q_spec(width, col):
        return pl.BlockSpec((TQ_B, width), lambda b, i: (b * nt + i, col // width))

    def seq_spec(width, col):
        return pl.BlockSpec((seq, width), lambda b, i: (b, col // width))

    return _call(
        functools.partial(_dsa_kernel, topk=topk, seq=seq,
                          present=tuple(tuple(int(b) for b in np.unique(t)) for t in bkt)),
        grid=(batch, nt),
        in_specs=[
            q_spec(IDX_HEADS * IDX_DIM, COL_IQ),
            q_spec(LANES, COL_IKW),
            seq_spec(LANES, COL_IKW),
            q_spec(B_WIDTH, COL_BQ), seq_spec(B_WIDTH, COL_BK), seq_spec(B_WIDTH, COL_BV),
            pl.BlockSpec((NEAR_B, TK_B, TQ_B), lambda b, i: (0, 0, 0)),
            pl.BlockSpec(memory_space=pltpu.SMEM),
        ],
        out_spec=pl.BlockSpec((TQ_B, B_WIDTH), lambda b, i: (b * nt + i, 0)),
        out_shape=jax.ShapeDtypeStruct((batch * seq, B_WIDTH), BF16),
        scratch_shapes=[
            pltpu.VMEM((nt, TK_B, TQ_B), F32),
            pltpu.VMEM((nt, TK_B, TQ_B), BF16),
            pltpu.VMEM((nt, TK_B, TQ_B), jnp.int16),
            pltpu.VMEM((nt, TK_B, TQ_B), F32),
            pltpu.VMEM((NEAR_B + 1, B_HEADS, TK_B, TQ_B), F32),
            pltpu.VMEM((TK_B, TK_B), BF16),
            pltpu.VMEM((seq // TK_B, B_HEADS, B_HEAD_DIM, TK_B), BF16),
            pltpu.VMEM((B_HEADS, B_HEAD_DIM + PACKED, TQ_B), F32),
            pltpu.VMEM((B_HEADS, SUBLANES, TQ_B), F32),
            pltpu.VMEM((B_HEADS, TK_B, TQ_B), F32),
            pltpu.VMEM((B_HEADS, SUBLANES, TQ_B), F32),
            pltpu.VMEM((B_HEADS, TK_B, TQ_B), F32),
            pltpu.VMEM((B_HEADS, SUBLANES, TQ_B), F32),
        ],
        semantics=("arbitrary", "arbitrary"),
        name="dsa",
        operands=[proj, proj, proj, proj, proj, proj, jnp.asarray(bkt), rel_bias],
    )


def _outproj_kernel(oa_ref, ob_ref, cq_ref, mkv_ref, gate_ref, x_ref, w32_ref, g_ref, o_ref, w_ref):
    @pl.when(pl.program_id(0) == 0)
    def _():
        w_ref[...] = w32_ref[...].astype(BF16)

    ones_m = jnp.ones((N_MEM, C_HEAD_DIM), BF16)
    for r in range(TM_OUT // SUB_OUT):
        rows = slice(r * SUB_OUT, (r + 1) * SUB_OUT)
        att = [oa_ref[rows, :].astype(F32), ob_ref[rows, :].astype(F32)]
        for h in range(C_HEADS):
            hs = slice(h * C_HEAD_DIM, (h + 1) * C_HEAD_DIM)
            s = _nt_dot(cq_ref[rows, hs], mkv_ref[:, hs])
            p = jnp.exp2(s - jnp.max(s, axis=1, keepdims=True)).astype(BF16)
            vaug = jnp.concatenate([mkv_ref[:, C_WIDTH + h * C_HEAD_DIM:C_WIDTH + (h + 1) * C_HEAD_DIM], ones_m],
                                   axis=1)
            pv = jnp.dot(p, vaug, preferred_element_type=F32)
            att.append(pv[:, :C_HEAD_DIM] / pv[:, C_HEAD_DIM:])
        y = (jnp.concatenate(att, axis=1) * gate_ref[rows, :].astype(F32)).astype(BF16)
        h_new = x_ref[rows, :] + jnp.dot(y, w_ref[...], preferred_element_type=F32)
        ms = jnp.mean(h_new * h_new, axis=-1, keepdims=True)
        o_ref[rows, :] = h_new * lax.rsqrt(ms + EPS) * g_ref[...]


def _outproj(oa, ob, proj, mkv, x2d, w, g, seq):
    m = x2d.shape[0]
    steps_per_seq = seq // TM_OUT
    return _call(
        _outproj_kernel,
        grid=(m // TM_OUT,),
        in_specs=[
            pl.BlockSpec((TM_OUT, A_WIDTH), lambda i: (i, 0)),
            pl.BlockSpec((TM_OUT, B_WIDTH), lambda i: (i, 0)),
            pl.BlockSpec((TM_OUT, C_WIDTH), lambda i: (i, COL_CQ // C_WIDTH)),
            pl.BlockSpec((N_MEM, 2 * C_WIDTH), lambda i: (i // steps_per_seq, 0)),
            pl.BlockSpec((TM_OUT, MIX_WIDTH), lambda i: (i, COL_GATE // MIX_WIDTH)),
            pl.BlockSpec((TM_OUT, D_MODEL), lambda i: (i, 0)),
            pl.BlockSpec((MIX_WIDTH, D_MODEL), lambda i: (0, 0), pipeline_mode=pl.Buffered(1)),
            pl.BlockSpec((1, D_MODEL), lambda i: (0, 0)),
        ],
        out_spec=pl.BlockSpec((TM_OUT, D_MODEL), lambda i: (i, 0)),
        out_shape=jax.ShapeDtypeStruct((m, D_MODEL), F32),
        scratch_shapes=[pltpu.VMEM((MIX_WIDTH, D_MODEL), BF16)],
        semantics=("arbitrary",),
        name="outproj",
        operands=[oa, ob, proj, mkv, proj, x2d, w, g],
    )


def _col_scale():
    cs = np.ones((1, PROJ_W), np.float32)
    cs[0, COL_AQ:COL_AQ + A_WIDTH] = A_HEAD_DIM ** -0.5 * LOG2E
    cs[0, COL_BQ:COL_BQ + B_WIDTH] = B_HEAD_DIM ** -0.5 * LOG2E
    cs[0, COL_CQ:COL_CQ + C_WIDTH] = C_HEAD_DIM ** -0.5 * LOG2E
    return jnp.asarray(cs)


def kernel(x, mem, g_norm, w_in, sinks, rel_bias, g_mem, w_mem_kv, w_out, g_final):
    batch, seq, _ = x.shape
    assert w_in.shape[0] == 1, "the out-projection kernel fuses the final norm of a single-layer trunk"
    kk = np.arange(2 * TQ_A)[:, None]
    qq = np.arange(TQ_A)[None, :]
    bkt_a = _t5_bucket_np(kk - TQ_A - qq)
    kk = np.arange(TK_B)[None, :, None]
    qq = np.arange(TQ_B)[None, None, :]
    d = np.arange(NEAR_B)[:, None, None]
    bkt_b = _t5_bucket_np(kk - qq - TK_B * d)

    h = x.reshape(batch * seq, D_MODEL)
    mem2d = mem.reshape(batch * N_MEM, D_MODEL)
    w_t = jnp.transpose(w_in, (2, 0, 1)).reshape(IN_WIDTH, KB_IN, LANES)
    proj = _inproj(h, g_norm[0].reshape(1, D_MODEL), _wprep(w_t), _col_scale())
    mkv = _memkv(mem2d, g_mem[0].reshape(1, D_MODEL), w_mem_kv[0])
    oa = _swa(proj, bkt_a, rel_bias, sinks[0], batch, seq)
    ob = _dsa(proj, bkt_b, rel_bias, batch, seq)
    out = _outproj(oa, ob, proj, mkv, h, w_out[0], g_final.reshape(1, D_MODEL), seq)
    return out.reshape(batch, seq, D_MODEL)
```
